```python
import math
import jax, jax.numpy as jnp
from jax import lax
import numpy as np

D_MODEL = 1024
BATCH = 32
SEQ = 256
DEPTH = 2
DEC_BATCH = 2
DEC_SEQ = 2048
PAST_LEN = 256

GRID_W = 64
EPS = 1e-6
N_EVEN = (DEPTH + 1) // 2
N_ODD = DEPTH // 2
ML_HEADS = 4
ML_DK = 128
ML_DV = 128
ML_CHUNK = 64
MLA_HEADS = 8
MLA_Q_RANK = 384
MLA_KV_RANK = 256
MLA_NOPE = 64
MLA_ROPE = 32
MLA_V = 64
ATTN_BLOCK = 128
ROPE_BASE = 10000.0
SSD_HEADS = 8
SSD_P = 64
SSD_GROUPS = 2
SSD_N = 128
SSD_CHUNK = 64
GDN_HEADS = 4
GDN_DK = 128
GDN_DV = 128
GDN_CHUNK = 64
CONV_W = 5
MOE_GROUPS = 4
MOE_PER_GROUP = 8
MOE_EXPERTS = MOE_GROUPS * MOE_PER_GROUP
MOE_TOPK = 2
MOE_FF = 256

ML_W = ML_HEADS * ML_DK
MLA_OUT = MLA_HEADS * MLA_V
EVEN_SPLIT = [ML_W, ML_W, ML_HEADS * ML_DV, ML_HEADS * ML_DV, 2 * ML_HEADS, 2 * ML_HEADS,
              MLA_Q_RANK, MLA_KV_RANK, MLA_ROPE]
IN_EVEN = sum(EVEN_SPLIT)
OUT_EVEN = ML_HEADS * ML_DV + MLA_OUT
SSD_W = SSD_HEADS * SSD_P
SSD_BC = SSD_GROUPS * SSD_N
SSD_CONV_CH = SSD_W + 2 * SSD_BC
GDN_W = GDN_HEADS * GDN_DK
GDN_CONV_CH = 2 * GDN_W + GDN_HEADS * GDN_DV
ODD_SPLIT = [SSD_W, SSD_CONV_CH, 2 * SSD_HEADS, GDN_CONV_CH, GDN_HEADS * GDN_DV, 2 * GDN_HEADS, 2 * GDN_HEADS]
IN_ODD = sum(ODD_SPLIT)
OUT_ODD = SSD_W + GDN_HEADS * GDN_DV

kernel_name = 'hybrid_mlstm_mla_ssd_gdn_hmoe_prefix_dit'


def rms_norm(x, g):
    xf = x.astype(jnp.float32)
    y = xf * lax.rsqrt(jnp.mean(xf * xf, axis=-1, keepdims=True) + EPS)
    return (y * g.astype(jnp.float32)).astype(x.dtype)


def l2_norm(x):
    return x * lax.rsqrt(jnp.sum(x * x, axis=-1, keepdims=True) + EPS)


def split_cols(y, sizes):
    idx = [int(s) for s in np.cumsum(sizes)[:-1]]
    return jnp.split(y, idx, axis=-1)


def heads_first(x, n_heads):
    B, T, _ = x.shape
    return x.reshape(B, T, n_heads, -1).transpose(0, 2, 1, 3)


def flip_t(x):
    return jnp.flip(x, axis=2)


def to_chunks(x, L):
    B, H, T = x.shape[:3]
    return jnp.moveaxis(x.reshape(B, H, T // L, L, *x.shape[3:]), 2, 0)


def from_chunks(y):
    nc, B, H, L = y.shape[:4]
    return jnp.moveaxis(y, 0, 2).reshape(B, H, nc * L, *y.shape[4:])


def centred_dwconv(x, w, b):
    pad = CONV_W // 2
    y = lax.conv_general_dilated(x, w[:, None, :].astype(x.dtype), window_strides=(1,),
                                 padding=[(pad, pad)], dimension_numbers=('NWC', 'WIO', 'NWC'),
                                 feature_group_count=x.shape[-1])
    return jax.nn.silu(y + b.astype(x.dtype))


def axial_rope(x):
    T = x.shape[1]
    rows = T // GRID_W
    row, col = jnp.meshgrid(jnp.arange(rows), jnp.arange(GRID_W), indexing='ij')
    row = row.reshape(-1).astype(jnp.float32)
    col = col.reshape(-1).astype(jnp.float32)
    half = MLA_ROPE // 2
    nf = half // 2
    inv = ROPE_BASE ** (-jnp.arange(nf, dtype=jnp.float32) / nf)

    def rot(xa, pos):
        ang = pos[:, None] * inv
        cos = jnp.cos(ang)[None, :, None, :]
        sin = jnp.sin(ang)[None, :, None, :]
        x1, x2 = xa[..., :nf], xa[..., nf:]
        return jnp.concatenate([x1 * cos - x2 * sin, x1 * sin + x2 * cos], axis=-1)

    xf = x.astype(jnp.float32)
    return jnp.concatenate([rot(xf[..., :half], row), rot(xf[..., half:], col)], axis=-1).astype(x.dtype)


def mla_attend(q, k, v):
    B, Tq, H, dq = q.shape
    nb = Tq // ATTN_BLOCK
    qb = q.reshape(B, nb, ATTN_BLOCK, H, dq).transpose(1, 0, 2, 3, 4)
    scale = dq ** -0.5

    def one_block(qi):
        s = jnp.einsum('bqhd,bkhd->bhqk', qi, k).astype(jnp.float32) * scale
        p = jax.nn.softmax(s, axis=-1).astype(v.dtype)
        return jnp.einsum('bhqk,bkhd->bqhd', p, v)

    o = lax.map(one_block, qb)
    return o.transpose(1, 0, 2, 3, 4).reshape(B, Tq, H, -1)


def mlstm_scan(q, k, v, logi, logf, C0, n0, m0):
    L = ML_CHUNK
    tril = jnp.tril(jnp.ones((L, L), dtype=bool))

    def step(carry, xs):
        C, n, m = carry
        qc, kc, vc, li, lf = xs
        b = jnp.cumsum(lf, axis=-1)
        dmat = jnp.where(tril, b[..., :, None] - b[..., None, :] + li[..., None, :], -jnp.inf)
        inter = b + m[..., None]
        mq = jnp.maximum(inter, jnp.max(dmat, axis=-1))
        s = jnp.einsum('bhtd,bhsd->bhts', qc, kc) * jnp.exp(dmat - mq[..., None])
        a = jnp.exp(inter - mq)
        num = jnp.einsum('bhts,bhsv->bhtv', s, vc) + a[..., None] * jnp.einsum('bhtk,bhkv->bhtv', qc, C)
        den = jnp.sum(s, axis=-1) + a * jnp.einsum('bhtk,bhk->bht', qc, n)
        h = num / jnp.maximum(jnp.abs(den), jnp.exp(-mq))[..., None]
        bl = b[..., -1]
        g = bl[..., None] - b + li
        m_new = jnp.maximum(bl + m, jnp.max(g, axis=-1))
        wk = jnp.exp(g - m_new[..., None])
        dec = jnp.exp(bl + m - m_new)
        C = dec[..., None, None] * C + jnp.einsum('bhs,bhsk,bhsv->bhkv', wk, kc, vc)
        n = dec[..., None] * n + jnp.einsum('bhs,bhsk->bhk', wk, kc)
        return (C, n, m_new), h

    xs = tuple(to_chunks(a, L) for a in (q, k, v, logi, logf))
    (C, n, m), h = lax.scan(step, (C0, n0, m0), xs)
    return from_chunks(h), C, n, m


def ssd_scan(x, dt, A, Bh, Ch, h0):
    L = SSD_CHUNK
    tril = jnp.tril(jnp.ones((L, L), dtype=bool))

    def step(hs, xs):
        xc, dtc, Bc, Cc = xs
        a = jnp.cumsum(dtc * A[None, :, None], axis=-1)
        seg = jnp.exp(jnp.where(tril, a[..., :, None] - a[..., None, :], -jnp.inf))
        gmat = jnp.einsum('bhtn,bhsn->bhts', Cc, Bc) * seg * dtc[..., None, :]
        y = jnp.einsum('bhts,bhsp->bhtp', gmat, xc) + jnp.exp(a)[..., None] * jnp.einsum('bhtn,bhpn->bhtp', Cc, hs)
        al = a[..., -1]
        wgt = jnp.exp(al[..., None] - a) * dtc
        hs = jnp.exp(al)[..., None, None] * hs + jnp.einsum('bhs,bhsp,bhsn->bhpn', wgt, xc, Bc)
        return hs, y

    hs, y = lax.scan(step, h0, tuple(to_chunks(a, L) for a in (x, dt, Bh, Ch)))
    return from_chunks(y), hs


def gdn_scan(q, k, v, beta, g, S0):
    L = GDN_CHUNK
    qc, kc, vc, bc, gc = (to_chunks(a, L) for a in (q, k, v, beta, g))
    gc = jnp.cumsum(gc, axis=-1)
    diff = gc[..., :, None] - gc[..., None, :]
    incl = jnp.tril(jnp.ones((L, L), dtype=bool))
    strict = jnp.tril(jnp.ones((L, L), dtype=bool), -1)
    dec_strict = jnp.exp(jnp.where(strict, diff, -jnp.inf))
    dec_incl = jnp.exp(jnp.where(incl, diff, -jnp.inf))
    amat = jnp.eye(L, dtype=q.dtype) + dec_strict * jnp.einsum('...tk,...sk->...ts', kc, kc) * bc[..., None, :]
    u0 = lax.linalg.triangular_solve(amat, vc, left_side=True, lower=True, unit_diagonal=True)
    w = lax.linalg.triangular_solve(amat, jnp.exp(gc)[..., None] * kc, left_side=True, lower=True,
                                    unit_diagonal=True)
    pmat = dec_incl * jnp.einsum('...tk,...sk->...ts', qc, kc) * bc[..., None, :]

    def step(S, xs):
        qi, ki, u0i, wi, pi, gi, bi = xs
        u = u0i - jnp.einsum('bhtk,bhkv->bhtv', wi, S)
        o = jnp.exp(gi)[..., None] * jnp.einsum('bhtk,bhkv->bhtv', qi, S) + jnp.einsum('bhts,bhsv->bhtv', pi, u)
        gl = gi[..., -1]
        S = jnp.exp(gl)[..., None, None] * S + jnp.einsum('bhs,bhsk,bhsv->bhkv', jnp.exp(gl[..., None] - gi) * bi, ki, u)
        return S, o

    S, o = lax.scan(step, S0, (qc, kc, u0, w, pmat, gc, bc))
    return from_chunks(o), S


def even_mixer(h, P, e, ctx):
    B, T, _ = h.shape
    f32 = jnp.float32
    y = h @ P['w_in_even'][e]
    q, k, v, o, ig, fg, cq, ckv, kpe = split_cols(y, EVEN_SPLIT)
    qh = heads_first(q, ML_HEADS).astype(f32) * ML_DK ** -0.5
    kh = heads_first(k, ML_HEADS).astype(f32)
    vh = heads_first(v, ML_HEADS).astype(f32)
    logi = (ig.astype(f32).reshape(B, T, 2, ML_HEADS) + P['ml_i_bias'][e]).transpose(2, 0, 3, 1)
    logf = jax.nn.log_sigmoid(fg.astype(f32).reshape(B, T, 2, ML_HEADS) + P['ml_f_bias'][e]).transpose(2, 0, 3, 1)
    if ctx is None:
        C0 = jnp.zeros((B, 2, ML_HEADS, ML_DK, ML_DV), f32)
        n0 = jnp.zeros((B, 2, ML_HEADS, ML_DK), f32)
        m0 = jnp.zeros((B, 2, ML_HEADS), f32)
    else:
        C0, n0, m0 = ctx[2].astype(f32), ctx[3].astype(f32), ctx[4].astype(f32)
    hf, Cf, nf, mf = mlstm_scan(qh, kh, vh, logi[0], logf[0], C0[:, 0], n0[:, 0], m0[:, 0])
    hb, Cb, nb, mb = mlstm_scan(flip_t(qh), flip_t(kh), flip_t(vh), flip_t(logi[1]), flip_t(logf[1]),
                                C0[:, 1], n0[:, 1], m0[:, 1])
    hm = (hf + flip_t(hb)).transpose(0, 2, 1, 3)
    hm = rms_norm(hm, P['ml_norm'][e]).reshape(B, T, ML_HEADS * ML_DV) * jax.nn.sigmoid(o.astype(f32))
    cq = rms_norm(cq, P['mla_q_norm'][e])
    qf = (cq @ P['mla_w_uq'][e]).reshape(B, T, MLA_HEADS, MLA_NOPE + MLA_ROPE)
    q_nope, q_pe = qf[..., :MLA_NOPE], qf[..., MLA_NOPE:]
    ckv = rms_norm(ckv, P['mla_kv_norm'][e])
    if ctx is None:
        kv_all, kpe_all = ckv, kpe
    else:
        q_pe = axial_rope(q_pe)
        kv_all = jnp.concatenate([ctx[0].astype(ckv.dtype), ckv], axis=1)
        kpe_all = jnp.concatenate([ctx[1].astype(kpe.dtype), axial_rope(kpe[:, :, None, :])[:, :, 0]], axis=1)
    Tk = kv_all.shape[1]
    kvh = (kv_all @ P['mla_w_ukv'][e]).reshape(B, Tk, MLA_HEADS, MLA_NOPE + MLA_V)
    k_full = jnp.concatenate([kvh[..., :MLA_NOPE],
                              jnp.broadcast_to(kpe_all[:, :, None, :], (B, Tk, MLA_HEADS, MLA_ROPE))], axis=-1)
    q_full = jnp.concatenate([q_nope, q_pe], axis=-1)
    att = mla_attend(q_full, k_full, kvh[..., MLA_NOPE:]).reshape(B, T, MLA_OUT)
    out = jnp.concatenate([hm.astype(h.dtype), att.astype(h.dtype)], axis=-1) @ P['w_out_even'][e]
    if ctx is None:
        return out, (ckv, kpe, jnp.stack([Cf, Cb], axis=1), jnp.stack([nf, nb], axis=1), jnp.stack([mf, mb], axis=1))
    return out, None


def odd_mixer(h, P, oi, ctx):
    B, T, _ = h.shape
    f32 = jnp.float32
    y = h @ P['w_in_odd'][oi]
    z_s, xbc, dt, qkv, z_g, ga, gb = split_cols(y, ODD_SPLIT)
    if ctx is None:
        h0 = jnp.zeros((B, 2, SSD_HEADS, SSD_P, SSD_N), f32)
        S0 = jnp.zeros((B, 2, GDN_HEADS, GDN_DK, GDN_DV), f32)
    else:
        h0, S0 = ctx[0].astype(f32), ctx[1].astype(f32)
    xbc = centred_dwconv(xbc, P['ssd_conv_w'][oi], P['ssd_conv_b'][oi]).astype(f32)
    xs_, Bm, Cm = split_cols(xbc, [SSD_W, SSD_BC, SSD_BC])
    xh = heads_first(xs_, SSD_HEADS)
    rep = SSD_HEADS // SSD_GROUPS
    Bh = jnp.repeat(heads_first(Bm, SSD_GROUPS), rep, axis=1)
    Ch = jnp.repeat(heads_first(Cm, SSD_GROUPS), rep, axis=1)
    dts = jax.nn.softplus(dt.astype(f32).reshape(B, T, 2, SSD_HEADS) + P['ssd_dt_bias'][oi]).transpose(2, 0, 3, 1)
    A = -jnp.exp(P['ssd_A_log'][oi].astype(f32))
    yf, sf = ssd_scan(xh, dts[0], A[0], Bh, Ch, h0[:, 0])
    yb, sb = ssd_scan(flip_t(xh), flip_t(dts[1]), A[1], flip_t(Bh), flip_t(Ch), h0[:, 1])
    ys = yf + flip_t(yb) + P['ssd_D'][oi][None, :, None, None] * xh
    ys = ys.transpose(0, 2, 1, 3).reshape(B, T, SSD_W)
    ys = rms_norm(ys * jax.nn.silu(z_s.astype(f32)), P['ssd_norm'][oi])
    qkv = centred_dwconv(qkv, P['gdn_conv_w'][oi], P['gdn_conv_b'][oi]).astype(f32)
    qg, kg, vg = split_cols(qkv, [GDN_W, GDN_W, GDN_HEADS * GDN_DV])
    qg = l2_norm(heads_first(qg, GDN_HEADS)) * GDN_DK ** -0.5
    kg = l2_norm(heads_first(kg, GDN_HEADS))
    vg = heads_first(vg, GDN_HEADS)
    beta = jax.nn.sigmoid(gb.astype(f32).reshape(B, T, 2, GDN_HEADS)).transpose(2, 0, 3, 1)
    gdec = -jnp.exp(P['gdn_A_log'][oi].astype(f32))[:, None, :, None] * jax.nn.softplus(
        ga.astype(f32).reshape(B, T, 2, GDN_HEADS) + P['gdn_dt_bias'][oi]).transpose(2, 0, 3, 1)
    of, Sf = gdn_scan(qg, kg, vg, beta[0], gdec[0], S0[:, 0])
    ob, Sb = gdn_scan(flip_t(qg), flip_t(kg), flip_t(vg), flip_t(beta[1]), flip_t(gdec[1]), S0[:, 1])
    og = (of + flip_t(ob)).transpose(0, 2, 1, 3)
    og = rms_norm(og, P['gdn_norm'][oi]) * jax.nn.silu(z_g.astype(f32).reshape(B, T, GDN_HEADS, GDN_DV))
    og = og.reshape(B, T, GDN_HEADS * GDN_DV)
    out = jnp.concatenate([ys, og], axis=-1).astype(h.dtype) @ P['w_out_odd'][oi]
    if ctx is None:
        return out, (jnp.stack([sf, sb], axis=1), jnp.stack([Sf, Sb], axis=1))
    return out, None


def hier_moe(h, P, l):
    B, T, D = h.shape
    f32 = jnp.float32
    x = h.reshape(B * T, D)
    glog = (x @ P['moe_w_group'][l] + P['moe_b_group'][l]).astype(f32)
    gprob = jax.nn.softmax(glog, axis=-1)
    gsel = jnp.argmax(glog, axis=-1)
    gw = jnp.take_along_axis(gprob, gsel[:, None], axis=-1)
    elog = (x @ P['moe_w_expert'][l] + P['moe_b_expert'][l]).astype(f32).reshape(-1, MOE_GROUPS, MOE_PER_GROUP)
    elog = jnp.take_along_axis(elog, gsel[:, None, None], axis=1)[:, 0]
    tv, ti = lax.top_k(elog, MOE_TOPK)
    tw = jax.nn.softmax(tv, axis=-1) * gw
    eid = gsel[:, None] * MOE_PER_GROUP + ti
    combine = jnp.einsum('nk,nke->ne', tw, jax.nn.one_hot(eid, MOE_EXPERTS, dtype=f32))
    hg = jnp.einsum('nd,edf->nef', x, P['moe_w_gate'][l])
    hu = jnp.einsum('nd,edf->nef', x, P['moe_w_up'][l])
    act = jax.nn.silu(hg) * hu * combine[..., None].astype(x.dtype)
    y = jnp.einsum('nef,efd->nd', act, P['moe_w_down'][l])
    return y.reshape(B, T, D)


def trunk_layer(x, mod, P, l, ctx):
    sh1, sc1, g1, sh2, sc2, g2 = jnp.split(mod, 6, axis=-1)
    h = rms_norm(x, P['norm_mix'][l]) * (1 + sc1) + sh1
    if l % 2 == 0:
        out, st = even_mixer(h, P, l // 2, ctx)
    else:
        out, st = odd_mixer(h, P, l // 2, ctx)
    x = x + g1 * out
    h = rms_norm(x, P['norm_ffn'][l]) * (1 + sc2) + sh2
    x = x + g2 * hier_moe(h, P, l)
    return x, st


def setup_inputs(seed: int = 0) -> dict:
    key = jax.random.key(seed)
    ks = iter(jax.random.split(key, 64))
    f32 = jnp.float32
    D = D_MODEL

    def nrm(shape, scale):
        return jax.random.normal(next(ks), shape, f32) * scale

    def gain(shape):
        return 1.0 + nrm(shape, 0.05)

    def dt_bias(shape):
        dt = jnp.exp(jax.random.uniform(next(ks), shape, f32, math.log(1e-3), math.log(1e-1)))
        return dt + jnp.log(-jnp.expm1(-dt))

    def a_log(shape):
        return jnp.log(jax.random.uniform(next(ks), shape, f32, 1.0, 16.0))

    return {
        'x_prompt': nrm((BATCH, SEQ, D), 1.0),
        'x_sample': nrm((DEC_BATCH, DEC_SEQ, D), 1.0),
        'c': nrm((DEC_BATCH, D), 1.0),
        'cache_mla_kv': nrm((DEC_BATCH, N_EVEN, PAST_LEN, MLA_KV_RANK), 1.0),
        'cache_mla_krope': nrm((DEC_BATCH, N_EVEN, PAST_LEN, MLA_ROPE), 1.0),
        'state_mlstm_C': nrm((DEC_BATCH, N_EVEN, 2, ML_HEADS, ML_DK, ML_DV), 1.0),
        'state_mlstm_n': nrm((DEC_BATCH, N_EVEN, 2, ML_HEADS, ML_DK), 1.0),
        'state_mlstm_m': nrm((DEC_BATCH, N_EVEN, 2, ML_HEADS), 1.0),
        'state_ssd': nrm((DEC_BATCH, N_ODD, 2, SSD_HEADS, SSD_P, SSD_N), 0.1),
        'state_gdn': nrm((DEC_BATCH, N_ODD, 2, GDN_HEADS, GDN_DK, GDN_DV), 0.3),
        'c_ctx': nrm((D,), 1.0),
        'ada_w': nrm((DEPTH, D, 6 * D), D ** -0.5),
        'ada_b': nrm((DEPTH, 6 * D), 0.02),
        'norm_mix': gain((DEPTH, D)),
        'norm_ffn': gain((DEPTH, D)),
        'w_in_even': nrm((N_EVEN, D, IN_EVEN), D ** -0.5),
        'ml_i_bias': nrm((N_EVEN, 2, ML_HEADS), 0.1),
        'ml_f_bias': 3.0 + nrm((N_EVEN, 2, ML_HEADS), 0.1),
        'ml_norm': gain((N_EVEN, ML_HEADS, ML_DV)),
        'mla_q_norm': gain((N_EVEN, MLA_Q_RANK)),
        'mla_w_uq': nrm((N_EVEN, MLA_Q_RANK, MLA_HEADS * (MLA_NOPE + MLA_ROPE)), MLA_Q_RANK ** -0.5),
        'mla_kv_norm': gain((N_EVEN, MLA_KV_RANK)),
        'mla_w_ukv': nrm((N_EVEN, MLA_KV_RANK, MLA_HEADS * (MLA_NOPE + MLA_V)), MLA_KV_RANK ** -0.5),
        'w_out_even': nrm((N_EVEN, OUT_EVEN, D), OUT_EVEN ** -0.5),
        'w_in_odd': nrm((N_ODD, D, IN_ODD), D ** -0.5),
        'ssd_conv_w': nrm((N_ODD, CONV_W, SSD_CONV_CH), CONV_W ** -0.5),
        'ssd_conv_b': nrm((N_ODD, SSD_CONV_CH), 0.02),
        'ssd_dt_bias': dt_bias((N_ODD, 2, SSD_HEADS)),
        'ssd_A_log': a_log((N_ODD, 2, SSD_HEADS)),
        'ssd_D': 1.0 + nrm((N_ODD, SSD_HEADS), 0.1),
        'ssd_norm': gain((N_ODD, SSD_W)),
        'gdn_conv_w': nrm((N_ODD, CONV_W, GDN_CONV_CH), CONV_W ** -0.5),
        'gdn_conv_b': nrm((N_ODD, GDN_CONV_CH), 0.02),
        'gdn_dt_bias': dt_bias((N_ODD, 2, GDN_HEADS)),
        'gdn_A_log': a_log((N_ODD, 2, GDN_HEADS)),
        'gdn_norm': gain((N_ODD, GDN_DV)),
        'w_out_odd': nrm((N_ODD, OUT_ODD, D), OUT_ODD ** -0.5),
        'moe_w_group': nrm((DEPTH, D, MOE_GROUPS), D ** -0.5),
        'moe_b_group': nrm((DEPTH, MOE_GROUPS), 0.01),
        'moe_w_expert': nrm((DEPTH, D, MOE_EXPERTS), D ** -0.5),
        'moe_b_expert': nrm((DEPTH, MOE_EXPERTS), 0.01),
        'moe_w_gate': nrm((DEPTH, MOE_EXPERTS, D, MOE_FF), D ** -0.5),
        'moe_w_up': nrm((DEPTH, MOE_EXPERTS, D, MOE_FF), D ** -0.5),
        'moe_w_down': nrm((DEPTH, MOE_EXPERTS, MOE_FF, D), MOE_FF ** -0.5),
        'norm_final': gain((D,)),
    }


def reference(x_prompt, x_sample, c, cache_mla_kv, cache_mla_krope, state_mlstm_C, state_mlstm_n,
              state_mlstm_m, state_ssd, state_gdn, c_ctx, ada_w, ada_b, norm_mix, norm_ffn,
              w_in_even, ml_i_bias, ml_f_bias, ml_norm, mla_q_norm, mla_w_uq, mla_kv_norm, mla_w_ukv,
              w_out_even, w_in_odd, ssd_conv_w, ssd_conv_b, ssd_dt_bias, ssd_A_log, ssd_D, ssd_norm,
              gdn_conv_w, gdn_conv_b, gdn_dt_bias, gdn_A_log, gdn_norm, w_out_odd,
              moe_w_group, moe_b_group, moe_w_expert, moe_b_expert, moe_w_gate, moe_w_up, moe_w_down,
              norm_final):
    P = dict(norm_mix=norm_mix, norm_ffn=norm_ffn, w_in_even=w_in_even, ml_i_bias=ml_i_bias,
             ml_f_bias=ml_f_bias, ml_norm=ml_norm, mla_q_norm=mla_q_norm, mla_w_uq=mla_w_uq,
             mla_kv_norm=mla_kv_norm, mla_w_ukv=mla_w_ukv, w_out_even=w_out_even, w_in_odd=w_in_odd,
             ssd_conv_w=ssd_conv_w, ssd_conv_b=ssd_conv_b, ssd_dt_bias=ssd_dt_bias, ssd_A_log=ssd_A_log,
             ssd_D=ssd_D, ssd_norm=ssd_norm, gdn_conv_w=gdn_conv_w, gdn_conv_b=gdn_conv_b,
             gdn_dt_bias=gdn_dt_bias, gdn_A_log=gdn_A_log, gdn_norm=gdn_norm, w_out_odd=w_out_odd,
             moe_w_group=moe_w_group, moe_b_group=moe_b_group, moe_w_expert=moe_w_expert,
             moe_b_expert=moe_b_expert, moe_w_gate=moe_w_gate, moe_w_up=moe_w_up, moe_w_down=moe_w_down)
    x = x_prompt
    even_st, odd_st = [], []
    for l in range(DEPTH):
        mod = (jax.nn.silu(c_ctx) @ ada_w[l] + ada_b[l])[None, None, :]
        x, st = trunk_layer(x, mod, P, l, None)
        if l % 2 == 0:
            even_st.append(st)
        else:
            odd_st.append(st)
    y_prompt = rms_norm(x, norm_final)
    new_mla_kv = jnp.stack([s[0] for s in even_st], axis=1)
    new_mla_krope = jnp.stack([s[1] for s in even_st], axis=1)
    new_mlstm_C = jnp.stack([s[2] for s in even_st], axis=1)
    new_mlstm_n = jnp.stack([s[3] for s in even_st], axis=1)
    new_mlstm_m = jnp.stack([s[4] for s in even_st], axis=1)
    new_ssd = jnp.stack([s[0] for s in odd_st], axis=1)
    new_gdn = jnp.stack([s[1] for s in odd_st], axis=1)
    x = x_sample
    for l in range(DEPTH):
        mod = (jax.nn.silu(c) @ ada_w[l] + ada_b[l])[:, None, :]
        if l % 2 == 0:
            e = l // 2
            ctx = (cache_mla_kv[:, e], cache_mla_krope[:, e], state_mlstm_C[:, e], state_mlstm_n[:, e],
                   state_mlstm_m[:, e])
        else:
            oi = l // 2
            ctx = (state_ssd[:, oi], state_gdn[:, oi])
        x, _ = trunk_layer(x, mod, P, l, ctx)
    y_sample = rms_norm(x, norm_final)
    return (y_prompt, y_sample, new_mla_kv, new_mla_krope, new_mlstm_C, new_mlstm_n, new_mlstm_m, new_ssd, new_gdn)
```

```python
import functools

import numpy as np
import jax
import jax.numpy as jnp
from jax import lax
from jax.experimental import pallas as pl
from jax.experimental.pallas import tpu as pltpu

F32 = jnp.float32
BF16 = jnp.bfloat16

D_MODEL = 1024
BATCH = 32
SEQ = 256
DEPTH = 2
DEC_BATCH = 2
DEC_SEQ = 2048
PAST_LEN = 256
GRID_W = 64
EPS = 1e-6
ML_HEADS = 4
ML_DK = 128
ML_DV = 128
MLA_HEADS = 8
MLA_Q_RANK = 384
MLA_KV_RANK = 256
MLA_NOPE = 64
MLA_ROPE = 32
MLA_V = 64
ROPE_BASE = 10000.0
SSD_HEADS = 8
SSD_P = 64
SSD_GROUPS = 2
SSD_N = 128
GDN_HEADS = 4
GDN_DK = 128
GDN_DV = 128
CONV_W = 5
MOE_GROUPS = 4
MOE_PER_GROUP = 8
MOE_EXPERTS = 32
MOE_FF = 256

N_PROMPT = BATCH * SEQ
N_SAMPLE = DEC_BATCH * DEC_SEQ
N_TOK = N_PROMPT + N_SAMPLE
N_COND = 8

LANES = 128
CHUNK = 64
ROW_TILE = 512
CONV_HALO = 8
CONV_ROWS = 256
VMEM_LIMIT = 56 * 1024 * 1024

NEG_INF = float("-inf")


def _cparams(*sem):
    return pltpu.CompilerParams(dimension_semantics=sem, vmem_limit_bytes=VMEM_LIMIT)


def _bdot(a, b):
    return jnp.dot(a.astype(BF16), b.astype(BF16), preferred_element_type=F32)


def _bdot_nt(a, b):
    return lax.dot_general(a.astype(BF16), b.astype(BF16), (((1,), (1,)), ((), ())),
                           preferred_element_type=F32)


def _bdot_tn(a, b):
    return lax.dot_general(a.astype(BF16), b.astype(BF16), (((0,), (0,)), ((), ())),
                           preferred_element_type=F32)


def _rms(x, g):
    return x * lax.rsqrt(jnp.mean(x * x, axis=-1, keepdims=True) + EPS) * g


def _softplus(x):
    return jnp.maximum(x, 0.0) + jnp.log1p(jnp.exp(-jnp.abs(x)))


def _silu(x):
    return x * jax.nn.sigmoid(x)


def _colsel(x, j):
    lane = lax.broadcasted_iota(jnp.int32, x.shape, 1)
    return jnp.sum(jnp.where(lane == j, x, 0.0), axis=1, keepdims=True)


def _chunk_masks(rev):
    t = lax.broadcasted_iota(jnp.int32, (CHUNK, CHUNK), 0)
    s = lax.broadcasted_iota(jnp.int32, (CHUNK, CHUNK), 1)
    if rev:
        return s >= t, t >= s, s > t, s == t
    return s <= t, t <= s, s < t, s == t


def _to_row(col, eye):
    return jnp.sum(jnp.where(eye, col, 0.0), axis=0, keepdims=True)


def _cum(col, row, m_incl, m_incl_t):
    c_col = jnp.sum(jnp.where(m_incl, row, 0.0), axis=1, keepdims=True)
    c_row = jnp.sum(jnp.where(m_incl_t, col, 0.0), axis=0, keepdims=True)
    return c_col, c_row


def _mod_index(i, rows_per_tile):
    p_tiles = N_PROMPT // rows_per_tile
    s_tiles = DEC_SEQ // rows_per_tile
    return jnp.where(i < p_tiles, 0, 1 + (i - p_tiles) // s_tiles)


def _ada_kernel(c_ref, w_ref, b_ref, o_ref):
    c = c_ref[...]
    o_ref[...] = _bdot(_silu(c), w_ref[...]) + b_ref[...]


def _ada(cond, ada_w, ada_b):
    nb = 6
    return pl.pallas_call(
        _ada_kernel,
        grid=(DEPTH, nb),
        in_specs=[pl.BlockSpec((N_COND, D_MODEL), lambda l, j: (0, 0)),
                  pl.BlockSpec((None, D_MODEL, D_MODEL), lambda l, j: (l, 0, j)),
                  pl.BlockSpec((None, 1, D_MODEL), lambda l, j: (l, 0, j))],
        out_specs=pl.BlockSpec((None, N_COND, D_MODEL), lambda l, j: (l, 0, j)),
        out_shape=jax.ShapeDtypeStruct((DEPTH, N_COND, 6 * D_MODEL), F32),
        compiler_params=_cparams("arbitrary", "arbitrary"),
        name="ada",
    )(cond, ada_w, ada_b.reshape(DEPTH, 1, 6 * D_MODEL))


def _inproj_kernel(x_ref, g_ref, mod_ref, *rest, n_out):
    w_refs, o_refs = rest[:n_out], rest[n_out:]
    mod = mod_ref[...]
    h = _rms(x_ref[...], g_ref[...]) * (1.0 + mod[:, D_MODEL:2 * D_MODEL]) + mod[:, :D_MODEL]
    hb = h.astype(BF16)
    for w_ref, o_ref in zip(w_refs, o_refs):
        o_ref[...] = jnp.dot(hb, w_ref[...], preferred_element_type=F32)


def _inproj(x, gain, mod4, layer, weights):
    n_out = len(weights)
    tm = ROW_TILE
    in_specs = [pl.BlockSpec((tm, D_MODEL), lambda i: (i, 0)),
                pl.BlockSpec((1, D_MODEL), lambda i: (0, 0)),
                pl.BlockSpec((None, None, 1, 2 * D_MODEL), lambda i: (layer, _mod_index(i, tm), 0, 0))]
    in_specs += [pl.BlockSpec(w.shape, lambda i: (0, 0)) for w in weights]
    return pl.pallas_call(
        functools.partial(_inproj_kernel, n_out=n_out),
        grid=(N_TOK // tm,),
        in_specs=in_specs,
        out_specs=[pl.BlockSpec((tm, w.shape[1]), lambda i: (i, 0)) for w in weights],
        out_shape=[jax.ShapeDtypeStruct((N_TOK, w.shape[1]), F32) for w in weights],
        compiler_params=_cparams("arbitrary"),
        name="inproj",
    )(x, gain.reshape(1, D_MODEL), mod4, *weights)


def _mlstm_kernel(*refs, T, has_init, emit_state):
    it = iter(refs)
    ib_ref, fb_ref = next(it), next(it)
    m0_ref = next(it) if has_init else None
    q_ref, k_ref, v_ref, o_ref, misc_ref, gn_ref = (next(it) for _ in range(6))
    c0_ref, n0_ref = (next(it), next(it)) if has_init else (None, None)
    hm_ref = next(it)
    cout_ref, nout_ref, mout_ref = (next(it), next(it), next(it)) if emit_state else (None, None, None)
    hb_scr, c_scr, n_scr, m_scr = (next(it) for _ in range(4))

    b = pl.program_id(0)
    h = pl.program_id(1)
    nc = T // CHUNK

    for d in range(2):
        if has_init:
            c_scr[d] = c0_ref[d]
            n_scr[d] = n0_ref[d]
            m_scr[d] = jnp.full((1, 1), m0_ref[b, d, h], F32)
        else:
            c_scr[d] = jnp.zeros((ML_DK, ML_DV), F32)
            n_scr[d] = jnp.zeros((1, ML_DK), F32)
            m_scr[d] = jnp.zeros((1, 1), F32)

    def chunk(d, r0):
        m_incl, m_incl_t, _, eye = _chunk_masks(rev=(d == 1))
        rows = pl.ds(r0, CHUNK)
        qc = q_ref[rows, :] * (ML_DK ** -0.5)
        kc = k_ref[rows, :]
        vc = v_ref[rows, :]
        mi = misc_ref[rows, :]
        li = _colsel(mi, d * ML_HEADS + h) + ib_ref[d, h]
        lf = -_softplus(-(_colsel(mi, 2 * ML_HEADS + d * ML_HEADS + h) + fb_ref[d, h]))
        lf_row = _to_row(lf, eye)
        li_row = _to_row(li, eye)
        b_col, b_row = _cum(lf, lf_row, m_incl, m_incl_t)
        dm = jnp.where(m_incl, b_col - b_row + li_row, NEG_INF)
        m = m_scr[d]
        inter = b_col + m
        mq = jnp.maximum(inter, jnp.max(dm, axis=1, keepdims=True))
        s = _bdot_nt(qc, kc) * jnp.exp(dm - mq)
        a = jnp.exp(inter - mq)
        c_st = c_scr[d]
        n_st = n_scr[d]
        num = _bdot(s, vc) + a * _bdot(qc, c_st)
        den = jnp.sum(s, axis=1, keepdims=True) + a * jnp.sum(qc * n_st, axis=1, keepdims=True)
        hval = num / jnp.maximum(jnp.abs(den), jnp.exp(-mq))
        bl = jnp.sum(lf, axis=0, keepdims=True)
        g = bl - b_col + li
        m_new = jnp.maximum(bl + m, jnp.max(g, axis=0, keepdims=True))
        wk = jnp.exp(g - m_new)
        dec = jnp.exp(bl + m - m_new)
        kw = wk * kc
        c_scr[d] = dec * c_st + _bdot_tn(kw, vc)
        n_scr[d] = dec * n_st + jnp.sum(kw, axis=0, keepdims=True)
        m_scr[d] = m_new
        return rows, hval

    def body(c, carry):
        rows, hval = chunk(0, pl.multiple_of(c * CHUNK, CHUNK))
        hm_ref[rows, :] = hval
        rows, hval = chunk(1, pl.multiple_of((nc - 1 - c) * CHUNK, CHUNK))
        hb_scr[rows, :] = hval
        return carry

    lax.fori_loop(0, nc, body, 0)

    hs = hm_ref[...] + hb_scr[...]
    hm_ref[...] = _rms(hs, gn_ref[...]) * jax.nn.sigmoid(o_ref[...])
    if emit_state:
        for d in range(2):
            cout_ref[d] = c_scr[d]
            nout_ref[d] = n_scr[d]
            mout_ref[d] = m_scr[d]


def _mlstm(qkvo, misc, i_bias, f_bias, gnorm, *, prompt, init=None):
    T, B = (SEQ, BATCH) if prompt else (DEC_SEQ, DEC_BATCH)
    blk0 = 0 if prompt else N_PROMPT // T
    has_init = init is not None
    smem = pl.BlockSpec(memory_space=pltpu.SMEM)

    def col(j):
        return pl.BlockSpec((T, LANES), lambda b, h: (blk0 + b, j * ML_HEADS + h))

    in_specs = [smem, smem]
    args = [i_bias, f_bias]
    if has_init:
        in_specs.append(smem)
        args.append(init[2])
    in_specs += [col(0), col(1), col(2), col(3),
                 pl.BlockSpec((T, LANES), lambda b, h: (blk0 + b, 0)),
                 pl.BlockSpec((None, 1, ML_DV), lambda b, h: (h, 0, 0))]
    args += [qkvo, qkvo, qkvo, qkvo, misc, gnorm.reshape(ML_HEADS, 1, ML_DV)]
    if has_init:
        in_specs += [pl.BlockSpec((None, 2, None, ML_DK, ML_DV), lambda b, h: (b, 0, h, 0, 0)),
                     pl.BlockSpec((None, 2, None, 1, ML_DK), lambda b, h: (b, 0, h, 0, 0))]
        args += [init[0], init[1].reshape(B, 2, ML_HEADS, 1, ML_DK)]
    out_specs = [pl.BlockSpec((T, LANES), lambda b, h: (b, h))]
    out_shape = [jax.ShapeDtypeStruct((B * T, ML_HEADS * ML_DV), F32)]
    if prompt:
        out_specs += [pl.BlockSpec((None, 2, None, ML_DK, ML_DV), lambda b, h: (b, 0, h, 0, 0)),
                      pl.BlockSpec((None, 2, None, 1, ML_DK), lambda b, h: (b, 0, h, 0, 0)),
                      pl.BlockSpec((None, 2, None, 1, 1), lambda b, h: (b, 0, h, 0, 0))]
        out_shape += [jax.ShapeDtypeStruct((B, 2, ML_HEADS, ML_DK, ML_DV), F32),
                      jax.ShapeDtypeStruct((B, 2, ML_HEADS, 1, ML_DK), F32),
                      jax.ShapeDtypeStruct((B, 2, ML_HEADS, 1, 1), F32)]
    return pl.pallas_call(
        functools.partial(_mlstm_kernel, T=T, has_init=has_init, emit_state=prompt),
        grid=(B, ML_HEADS),
        in_specs=in_specs,
        out_specs=out_specs,
        out_shape=out_shape,
        scratch_shapes=[pltpu.VMEM((T, ML_DV), F32),
                        pltpu.VMEM((2, ML_DK, ML_DV), F32),
                        pltpu.VMEM((2, 1, ML_DK), F32),
                        pltpu.VMEM((2, 1, 1), F32)],
        compiler_params=_cparams("arbitrary", "arbitrary"),
        name="mlstm_prompt" if prompt else "mlstm_sample",
    )(*args)


def _rope(x, cos, sin_signed):
    lane = lax.broadcasted_iota(jnp.int32, x.shape, 1)
    first = (lane & 15) < 8
    partner = jnp.where(first, pltpu.roll(x, LANES - 8, axis=1), pltpu.roll(x, 8, axis=1))
    return x * cos + partner * sin_signed


def _q_kernel(cq_ref, g_ref, w_ref, cos_ref, sin_ref, q_ref):
    cq = _rms(cq_ref[...], g_ref[...])
    y = _bdot(cq, w_ref[...])
    cos, sin = cos_ref[...], sin_ref[...]
    for hd in range(MLA_HEADS):
        sl = slice(hd * LANES, (hd + 1) * LANES)
        q_ref[:, sl] = _rope(y[:, sl], cos, sin).astype(BF16)


def _rope_block_index(i, tm):
    p_tiles = N_PROMPT // tm
    s_tiles = DEC_SEQ // tm
    return jnp.where(i < p_tiles, 0, 1 + (i - p_tiles) % s_tiles)


def _mla_q(cq, gain, w_uq_pad, cos_tab, sin_tab):
    tm = ROW_TILE
    tab = pl.BlockSpec((tm, LANES), lambda i: (_rope_block_index(i, tm), 0))
    return pl.pallas_call(
        _q_kernel,
        grid=(N_TOK // tm,),
        in_specs=[pl.BlockSpec((tm, MLA_Q_RANK), lambda i: (i, 0)),
                  pl.BlockSpec((1, MLA_Q_RANK), lambda i: (0, 0)),
                  pl.BlockSpec(w_uq_pad.shape, lambda i: (0, 0)),
                  tab, tab],
        out_specs=pl.BlockSpec((tm, MLA_HEADS * LANES), lambda i: (i, 0)),
        out_shape=jax.ShapeDtypeStruct((N_TOK, MLA_HEADS * LANES), BF16),
        compiler_params=_cparams("arbitrary"),
        name="mla_q",
    )(cq, gain.reshape(1, MLA_Q_RANK), w_uq_pad, cos_tab, sin_tab)


def _kv_kernel(*refs, norm):
    if norm:
        ckv_ref, kpe_ref, g_ref, wk_ref, wv_ref, cos_ref, sin_ref, ckvn_ref, k_ref, v_ref = refs
        c = _rms(ckv_ref[...], g_ref[...])
        ckvn_ref[...] = c
    else:
        ckv_ref, kpe_ref, wk_ref, wv_ref, k_ref, v_ref = refs
        c = ckv_ref[...]
    kp = kpe_ref[...]
    lane = lax.broadcasted_iota(jnp.int32, kp.shape, 1)
    kp = jnp.where((lane >= MLA_NOPE) & (lane < MLA_NOPE + MLA_ROPE), kp, 0.0)
    if norm:
        kp = _rope(kp, cos_ref[...], sin_ref[...])
    kn = _bdot(c, wk_ref[...])
    for hd in range(MLA_HEADS):
        sl = slice(hd * LANES, (hd + 1) * LANES)
        k_ref[:, sl] = (kn[:, sl] + kp).astype(BF16)
    v_ref[...] = _bdot(c, wv_ref[...]).astype(BF16)


def _mla_kv(ckv, kpe128, w_uk_pad, w_uv, gain=None, cos_tab=None, sin_tab=None):
    norm = gain is not None
    n = ckv.shape[0]
    tm = ROW_TILE
    row = lambda w: pl.BlockSpec((tm, w), lambda i: (i, 0))
    full = lambda a: pl.BlockSpec(a.shape, lambda i: (0, 0))
    in_specs = [row(MLA_KV_RANK), row(LANES)]
    args = [ckv, kpe128]
    if norm:
        in_specs.append(pl.BlockSpec((1, MLA_KV_RANK), lambda i: (0, 0)))
        args.append(gain.reshape(1, MLA_KV_RANK))
    in_specs += [full(w_uk_pad), full(w_uv)]
    args += [w_uk_pad, w_uv]
    out_specs = [row(MLA_HEADS * LANES), row(MLA_HEADS * MLA_V)]
    out_shape = [jax.ShapeDtypeStruct((n, MLA_HEADS * LANES), BF16),
                 jax.ShapeDtypeStruct((n, MLA_HEADS * MLA_V), BF16)]
    if norm:
        tab = pl.BlockSpec((tm, LANES), lambda i: (_rope_block_index(i, tm), 0))
        in_specs += [tab, tab]
        args += [cos_tab, sin_tab]
        out_specs = [row(MLA_KV_RANK)] + out_specs
        out_shape = [jax.ShapeDtypeStruct((n, MLA_KV_RANK), F32)] + out_shape
    return pl.pallas_call(
        functools.partial(_kv_kernel, norm=norm),
        grid=(n // tm,),
        in_specs=in_specs,
        out_specs=out_specs,
        out_shape=out_shape,
        compiler_params=_cparams("arbitrary"),
        name="mla_kv" if norm else "mla_kv_cache",
    )(*args)


def _attn_kernel(q_ref, k_ref, v_ref, o_ref):
    scale = (MLA_NOPE + MLA_ROPE) ** -0.5
    outs = []
    for j in range(2):
        q = q_ref[:, j * LANES:(j + 1) * LANES]
        k = k_ref[:, j * LANES:(j + 1) * LANES]
        s = lax.dot_general(q, k, (((1,), (1,)), ((), ())), preferred_element_type=F32) * scale
        p = jnp.exp(s - jnp.max(s, axis=1, keepdims=True))
        l = jnp.sum(p, axis=1, keepdims=True)
        pv = jnp.dot(p.astype(BF16), v_ref[:, j * MLA_V:(j + 1) * MLA_V], preferred_element_type=F32)
        outs.append(pv / l)
    o_ref[...] = jnp.concatenate(outs, axis=1)


def _attention(q, k, v, *, B, Tq, Tk, tq, q_row0):
    nq = Tq // tq
    blk0 = q_row0 // tq
    return pl.pallas_call(
        _attn_kernel,
        grid=(B, MLA_HEADS // 2, nq),
        in_specs=[pl.BlockSpec((tq, 2 * LANES), lambda b, hp, i: (blk0 + b * nq + i, hp)),
                  pl.BlockSpec((Tk, 2 * LANES), lambda b, hp, i: (b, hp)),
                  pl.BlockSpec((Tk, 2 * MLA_V), lambda b, hp, i: (b, hp))],
        out_specs=pl.BlockSpec((tq, 2 * MLA_V), lambda b, hp, i: (b * nq + i, hp)),
        out_shape=jax.ShapeDtypeStruct((B * Tq, MLA_HEADS * MLA_V), F32),
        compiler_params=_cparams("arbitrary", "arbitrary", "arbitrary"),
        name="attention",
    )(q, k, v)


def _outproj_kernel(*refs, odd):
    if odd:
        a1_ref, z_ref, gn_ref, a2_ref, w1_ref, w2_ref, x_ref, g1_ref, o_ref = refs
        a1 = _rms(a1_ref[...] * _silu(z_ref[...]), gn_ref[...])
    else:
        a1_ref, a2_ref, w1_ref, w2_ref, x_ref, g1_ref, o_ref = refs
        a1 = a1_ref[...]
    out = _bdot(a1, w1_ref[...]) + _bdot(a2_ref[...], w2_ref[...])
    o_ref[...] = x_ref[...] + g1_ref[...] * out


def _outproj(a1, a2, w1, w2, x, mod4, layer, z=None, gnorm=None):
    odd = z is not None
    tm = ROW_TILE
    half = a1.shape[1]
    row = lambda w: pl.BlockSpec((tm, w), lambda i: (i, 0))
    full = lambda a: pl.BlockSpec(a.shape, lambda i: (0, 0))
    in_specs, args = [row(half)], [a1]
    if odd:
        in_specs += [row(half), pl.BlockSpec((1, half), lambda i: (0, 0))]
        args += [z, gnorm.reshape(1, half)]
    in_specs += [row(half), full(w1), full(w2), row(D_MODEL),
                 pl.BlockSpec((None, None, 1, D_MODEL), lambda i: (layer, _mod_index(i, tm), 0, 2))]
    args += [a2, w1, w2, x, mod4]
    return pl.pallas_call(
        functools.partial(_outproj_kernel, odd=odd),
        grid=(N_TOK // tm,),
        in_specs=in_specs,
        out_specs=row(D_MODEL),
        out_shape=jax.ShapeDtypeStruct((N_TOK, D_MODEL), F32),
        compiler_params=_cparams("arbitrary"),
        name="outproj",
    )(*args)


def _conv_silu(src_ref, dst_ref, pad_ref, w_ref, b_ref, T):
    width = src_ref.shape[1]
    zeros = jnp.zeros((CONV_HALO, width), F32)
    pad_ref[pl.ds(0, CONV_HALO), :] = zeros
    pad_ref[pl.ds(CONV_HALO + T, CONV_HALO), :] = zeros
    pad_ref[pl.ds(CONV_HALO, T), :] = src_ref[...]
    w = w_ref[...]
    bias = b_ref[...]
    for r0 in range(0, T, CONV_ROWS):
        acc = bias
        for j in range(CONV_W):
            start = r0 + CONV_HALO + j - CONV_W // 2
            acc = acc + w[j:j + 1, :] * pad_ref[pl.ds(start, CONV_ROWS), :]
        dst_ref[pl.ds(r0, CONV_ROWS), :] = _silu(acc)


def _ssd_kernel(*refs, T, has_init, emit_state):
    it = iter(refs)
    x_ref, b_ref, c_ref = next(it), next(it), next(it)
    wx_ref, wb_ref, wc_ref = next(it), next(it), next(it)
    bx_ref, bb_ref, bc_ref = next(it), next(it), next(it)
    misc_ref, dtb_ref, alog_ref, dskip_ref = next(it), next(it), next(it), next(it)
    h0_ref = next(it) if has_init else None
    y_ref = next(it)
    hout_ref = next(it) if emit_state else None
    pad_scr, xc_scr, bc_scr, cc_scr, yb_scr, hs_scr = (next(it) for _ in range(6))

    p = pl.program_id(1)
    nc = T // CHUNK

    _conv_silu(x_ref, xc_scr, pad_scr, wx_ref, bx_ref, T)
    _conv_silu(b_ref, bc_scr, pad_scr, wb_ref, bb_ref, T)
    _conv_silu(c_ref, cc_scr, pad_scr, wc_ref, bc_ref, T)

    for d in range(2):
        if has_init:
            hs_scr[d] = h0_ref[d].reshape(2 * SSD_P, SSD_N)
        else:
            hs_scr[d] = jnp.zeros((2 * SSD_P, SSD_N), F32)

    dtb = dtb_ref[...]
    neg_a = -jnp.exp(alog_ref[...])
    lane = lax.broadcasted_iota(jnp.int32, (CHUNK, LANES), 1)
    head0 = lane < SSD_P
    prow0 = lax.broadcasted_iota(jnp.int32, (2 * SSD_P, 1), 0) < SSD_P

    def chunk(d, r0):
        m_incl, m_incl_t, _, eye = _chunk_masks(rev=(d == 1))
        rows = pl.ds(r0, CHUNK)
        xc = xc_scr[rows, :]
        bc = bc_scr[rows, :]
        cc = cc_scr[rows, :]
        dts = _softplus(misc_ref[rows, :] + dtb)
        adt = dts * neg_a
        cb = _bdot_nt(cc, bc)
        hs = hs_scr[d]
        ch = _bdot_nt(cc, hs)
        y_intra, ea, wgt, eal = [], [], [], []
        for hh in range(2):
            j = d * SSD_HEADS + 2 * p + hh
            dt_col = _colsel(dts, j)
            a_src = _colsel(adt, j)
            a_col, a_row = _cum(a_src, _to_row(a_src, eye), m_incl, m_incl_t)
            seg = jnp.exp(jnp.where(m_incl, a_col - a_row, NEG_INF))
            gmat = cb * seg * _to_row(dt_col, eye)
            y_intra.append(_bdot(gmat, xc))
            ea.append(jnp.exp(a_col))
            al = jnp.sum(a_src, axis=0, keepdims=True)
            wgt.append(jnp.exp(al - a_col) * dt_col)
            eal.append(jnp.exp(al))
        y = jnp.where(head0, y_intra[0] + ea[0] * ch, y_intra[1] + ea[1] * ch)
        xw = xc * jnp.where(head0, wgt[0], wgt[1])
        hs_scr[d] = jnp.where(prow0, eal[0], eal[1]) * hs + _bdot_tn(xw, bc)
        return rows, y

    def body(c, carry):
        rows, y = chunk(0, pl.multiple_of(c * CHUNK, CHUNK))
        y_ref[rows, :] = y
        rows, y = chunk(1, pl.multiple_of((nc - 1 - c) * CHUNK, CHUNK))
        yb_scr[rows, :] = y
        return carry

    lax.fori_loop(0, nc, body, 0)

    y_ref[...] = y_ref[...] + yb_scr[...] + dskip_ref[...] * xc_scr[...]
    if emit_state:
        for d in range(2):
            hout_ref[d] = hs_scr[d].reshape(2, SSD_P, SSD_N)


def _ssd(xbc, misc, conv_w, conv_b, dt_bias, a_log, d_skip, *, prompt, init=None):
    T, B = (SEQ, BATCH) if prompt else (DEC_SEQ, DEC_BATCH)
    blk0 = 0 if prompt else N_PROMPT // T
    has_init = init is not None
    n_pairs = SSD_HEADS // 2
    pairs_per_group = n_pairs // SSD_GROUPS
    xb = SSD_HEADS * SSD_P // LANES
    cb = xb + SSD_GROUPS * SSD_N // LANES
    colx = lambda b, p: (blk0 + b, p)
    colb = lambda b, p: (blk0 + b, xb + p // pairs_per_group)
    colc = lambda b, p: (blk0 + b, cb + p // pairs_per_group)
    wsel = lambda f: (lambda b, p: (0, f(b, p)[1]))
    row128 = pl.BlockSpec((1, LANES), lambda b, p: (0, 0))
    in_specs = [pl.BlockSpec((T, LANES), colx), pl.BlockSpec((T, LANES), colb), pl.BlockSpec((T, LANES), colc),
                pl.BlockSpec((CONV_W, LANES), wsel(colx)), pl.BlockSpec((CONV_W, LANES), wsel(colb)),
                pl.BlockSpec((CONV_W, LANES), wsel(colc)),
                pl.BlockSpec((1, LANES), wsel(colx)), pl.BlockSpec((1, LANES), wsel(colb)),
                pl.BlockSpec((1, LANES), wsel(colc)),
                pl.BlockSpec((T, LANES), lambda b, p: (blk0 + b, 0)), row128, row128,
                pl.BlockSpec((1, LANES), lambda b, p: (0, p))]
    cb2 = conv_b.reshape(1, -1)
    args = [xbc, xbc, xbc, conv_w, conv_w, conv_w, cb2, cb2, cb2, misc,
            _lane_row(dt_bias.reshape(-1)), _lane_row(a_log.reshape(-1)),
            jnp.repeat(d_skip, SSD_P).reshape(1, SSD_HEADS * SSD_P)]
    state_spec = pl.BlockSpec((None, 2, 2, SSD_P, SSD_N), lambda b, p: (b, 0, p, 0, 0))
    if has_init:
        in_specs.append(state_spec)
        args.append(init)
    out_specs = [pl.BlockSpec((T, LANES), lambda b, p: (b, p))]
    out_shape = [jax.ShapeDtypeStruct((B * T, SSD_HEADS * SSD_P), F32)]
    if prompt:
        out_specs.append(state_spec)
        out_shape.append(jax.ShapeDtypeStruct((B, 2, SSD_HEADS, SSD_P, SSD_N), F32))
    return pl.pallas_call(
        functools.partial(_ssd_kernel, T=T, has_init=has_init, emit_state=prompt),
        grid=(B, n_pairs),
        in_specs=in_specs,
        out_specs=out_specs,
        out_shape=out_shape,
        scratch_shapes=[pltpu.VMEM((T + 2 * CONV_HALO, LANES), F32)]
        + [pltpu.VMEM((T, LANES), F32) for _ in range(4)]
        + [pltpu.VMEM((2, 2 * SSD_P, SSD_N), F32)],
        compiler_params=_cparams("arbitrary", "arbitrary"),
        name="ssd_prompt" if prompt else "ssd_sample",
    )(*args)


def _lane_row(v):
    return jnp.pad(v.astype(F32), (0, LANES - v.shape[0])).reshape(1, LANES)


def _tri_inverse(nmat, eye_f):
    t = lax.broadcasted_iota(jnp.int32, (CHUNK, CHUNK), 0)
    s = lax.broadcasted_iota(jnp.int32, (CHUNK, CHUNK), 1)
    dinv = eye_f
    for level in range(1, int(np.log2(CHUNK)) + 1):
        same_big = lax.shift_right_logical(t, level) == lax.shift_right_logical(s, level)
        same_small = lax.shift_right_logical(t, level - 1) == lax.shift_right_logical(s, level - 1)
        off = jnp.where(same_big & jnp.logical_not(same_small), nmat, 0.0)
        if level == 1:
            dinv = dinv - off
        else:
            dinv = dinv - _bdot(_bdot(dinv, off), dinv)
    return dinv


def _gdn_kernel(*refs, T, has_init, emit_state):
    it = iter(refs)
    q_ref, k_ref, v_ref = next(it), next(it), next(it)
    wq_ref, wk_ref, wv_ref = next(it), next(it), next(it)
    bq_ref, bk_ref, bv_ref = next(it), next(it), next(it)
    misc_ref, dtb_ref, alog_ref, z_ref, gn_ref = (next(it) for _ in range(5))
    s0_ref = next(it) if has_init else None
    o_ref = next(it)
    sout_ref = next(it) if emit_state else None
    pad_scr, qc_scr, kc_scr, vc_scr, ob_scr, s_scr = (next(it) for _ in range(6))

    h = pl.program_id(1)
    nc = T // CHUNK

    _conv_silu(q_ref, qc_scr, pad_scr, wq_ref, bq_ref, T)
    _conv_silu(k_ref, kc_scr, pad_scr, wk_ref, bk_ref, T)
    _conv_silu(v_ref, vc_scr, pad_scr, wv_ref, bv_ref, T)
    q = qc_scr[...]
    qc_scr[...] = q * lax.rsqrt(jnp.sum(q * q, axis=-1, keepdims=True) + EPS) * (GDN_DK ** -0.5)
    k = kc_scr[...]
    kc_scr[...] = k * lax.rsqrt(jnp.sum(k * k, axis=-1, keepdims=True) + EPS)

    for d in range(2):
        if has_init:
            s_scr[d] = s0_ref[d]
        else:
            s_scr[d] = jnp.zeros((GDN_DK, GDN_DV), F32)

    dtb = dtb_ref[...]
    neg_a = -jnp.exp(alog_ref[...])

    def chunk(d, r0):
        m_incl, m_incl_t, m_strict, eye = _chunk_masks(rev=(d == 1))
        rows = pl.ds(r0, CHUNK)
        qc = qc_scr[rows, :]
        kc = kc_scr[rows, :]
        vc = vc_scr[rows, :]
        mi = misc_ref[rows, :]
        gdec = neg_a * _softplus(mi + dtb)
        j = d * GDN_HEADS + h
        g_src = _colsel(gdec, 2 * SSD_HEADS + j)
        beta = jax.nn.sigmoid(_colsel(mi, 2 * SSD_HEADS + 2 * GDN_HEADS + j))
        gc_col, gc_row = _cum(g_src, _to_row(g_src, eye), m_incl, m_incl_t)
        diff = gc_col - gc_row
        beta_row = _to_row(beta, eye)
        kk = _bdot_nt(kc, kc)
        nmat = jnp.exp(jnp.where(m_strict, diff, NEG_INF)) * kk * beta_row
        tinv = _tri_inverse(nmat, eye.astype(F32))
        egc = jnp.exp(gc_col)
        u0 = _bdot(tinv, vc)
        w = _bdot(tinv, egc * kc)
        pmat = jnp.exp(jnp.where(m_incl, diff, NEG_INF)) * _bdot_nt(qc, kc) * beta_row
        s_st = s_scr[d]
        u = u0 - _bdot(w, s_st)
        o = egc * _bdot(qc, s_st) + _bdot(pmat, u)
        gl = jnp.sum(g_src, axis=0, keepdims=True)
        coef = jnp.exp(gl - gc_col) * beta
        s_scr[d] = jnp.exp(gl) * s_st + _bdot_tn(coef * kc, u)
        return rows, o

    def body(c, carry):
        rows, o = chunk(0, pl.multiple_of(c * CHUNK, CHUNK))
        o_ref[rows, :] = o
        rows, o = chunk(1, pl.multiple_of((nc - 1 - c) * CHUNK, CHUNK))
        ob_scr[rows, :] = o
        return carry

    lax.fori_loop(0, nc, body, 0)

    og = o_ref[...] + ob_scr[...]
    o_ref[...] = _rms(og, gn_ref[...]) * _silu(z_ref[...])
    if emit_state:
        for d in range(2):
            sout_ref[d] = s_scr[d]


def _gdn(qkv, misc, zg, conv_w, conv_b, dt_row, alog_row, gnorm, *, prompt, init=None):
    T, B = (SEQ, BATCH) if prompt else (DEC_SEQ, DEC_BATCH)
    blk0 = 0 if prompt else N_PROMPT // T
    has_init = init is not None
    col = lambda j: (lambda b, h: (blk0 + b, j * GDN_HEADS + h))
    wsel = lambda j: (lambda b, h: (0, j * GDN_HEADS + h))
    row128 = pl.BlockSpec((1, LANES), lambda b, h: (0, 0))
    in_specs = [pl.BlockSpec((T, LANES), col(j)) for j in range(3)]
    in_specs += [pl.BlockSpec((CONV_W, LANES), wsel(j)) for j in range(3)]
    in_specs += [pl.BlockSpec((1, LANES), wsel(j)) for j in range(3)]
    in_specs += [pl.BlockSpec((T, LANES), lambda b, h: (blk0 + b, 0)), row128, row128,
                 pl.BlockSpec((T, LANES), lambda b, h: (blk0 + b, h)), row128]
    cb2 = conv_b.reshape(1, -1)
    args = [qkv, qkv, qkv, conv_w, conv_w, conv_w, cb2, cb2, cb2, misc, dt_row, alog_row, zg,
            gnorm.reshape(1, GDN_DV)]
    state_spec = pl.BlockSpec((None, 2, None, GDN_DK, GDN_DV), lambda b, h: (b, 0, h, 0, 0))
    if has_init:
        in_specs.append(state_spec)
        args.append(init)
    out_specs = [pl.BlockSpec((T, LANES), lambda b, h: (b, h))]
    out_shape = [jax.ShapeDtypeStruct((B * T, GDN_HEADS * GDN_DV), F32)]
    if prompt:
        out_specs.append(state_spec)
        out_shape.append(jax.ShapeDtypeStruct((B, 2, GDN_HEADS, GDN_DK, GDN_DV), F32))
    return pl.pallas_call(
        functools.partial(_gdn_kernel, T=T, has_init=has_init, emit_state=prompt),
        grid=(B, GDN_HEADS),
        in_specs=in_specs,
        out_specs=out_specs,
        out_shape=out_shape,
        scratch_shapes=[pltpu.VMEM((T + 2 * CONV_HALO, LANES), F32)]
        + [pltpu.VMEM((T, LANES), F32) for _ in range(4)]
        + [pltpu.VMEM((2, GDN_DK, GDN_DV), F32)],
        compiler_params=_cparams("arbitrary", "arbitrary"),
        name="gdn_prompt" if prompt else "gdn_sample",
    )(*args)


def _router_kernel(x_ref, g_ref, mod_ref, w_ref, b_ref, h_ref, comb_ref):
    mod = mod_ref[...]
    h = _rms(x_ref[...], g_ref[...]) * (1.0 + mod[:, D_MODEL:2 * D_MODEL]) + mod[:, :D_MODEL]
    h_ref[...] = h.astype(BF16)
    logit = jnp.dot(h, w_ref[...], preferred_element_type=F32, precision=lax.Precision.HIGHEST) + b_ref[...]
    lane = lax.broadcasted_iota(jnp.int32, logit.shape, 1)
    is_group = (lane >= MOE_EXPERTS) & (lane < MOE_EXPERTS + MOE_GROUPS)
    glog = jnp.where(is_group, logit, NEG_INF)
    gmax = jnp.max(glog, axis=1, keepdims=True)
    gsel = jnp.min(jnp.where(glog == gmax, lane, LANES), axis=1, keepdims=True) - MOE_EXPERTS
    gw = 1.0 / jnp.sum(jnp.exp(glog - gmax), axis=1, keepdims=True)
    lo = gsel * MOE_PER_GROUP
    in_group = (lane >= lo) & (lane < lo + MOE_PER_GROUP)
    elog = jnp.where(in_group, logit, NEG_INF)
    v1 = jnp.max(elog, axis=1, keepdims=True)
    i1 = jnp.min(jnp.where(elog == v1, lane, LANES), axis=1, keepdims=True)
    elog2 = jnp.where(lane == i1, NEG_INF, elog)
    v2 = jnp.max(elog2, axis=1, keepdims=True)
    i2 = jnp.min(jnp.where(elog2 == v2, lane, LANES), axis=1, keepdims=True)
    e2 = jnp.exp(v2 - v1)
    w1 = gw / (1.0 + e2)
    w2 = gw * e2 / (1.0 + e2)
    comb_ref[...] = jnp.where(lane == i1, w1, 0.0) + jnp.where(lane == i2, w2, 0.0)


def _router(x, gain, mod4, layer, w_route, b_route):
    tm = ROW_TILE
    return pl.pallas_call(
        _router_kernel,
        grid=(N_TOK // tm,),
        in_specs=[pl.BlockSpec((tm, D_MODEL), lambda i: (i, 0)),
                  pl.BlockSpec((1, D_MODEL), lambda i: (0, 0)),
                  pl.BlockSpec((None, None, 1, 3 * D_MODEL), lambda i: (layer, _mod_index(i, tm), 0, 1)),
                  pl.BlockSpec((D_MODEL, LANES), lambda i: (0, 0)),
                  pl.BlockSpec((1, LANES), lambda i: (0, 0))],
        out_specs=[pl.BlockSpec((tm, D_MODEL), lambda i: (i, 0)),
                   pl.BlockSpec((tm, LANES), lambda i: (i, 0))],
        out_shape=[jax.ShapeDtypeStruct((N_TOK, D_MODEL), BF16),
                   jax.ShapeDtypeStruct((N_TOK, LANES), F32)],
        compiler_params=_cparams("arbitrary"),
        name="router",
    )(x, gain.reshape(1, D_MODEL), mod4, w_route, b_route)


MOE_ROWS = 1024


def _moe_kernel(h_ref, comb_ref, wg_ref, wu_ref, wd_ref, x_ref, mod_ref, gf_ref, o_ref, acc_ref, *, final):
    e = pl.program_id(1)

    @pl.when(e == 0)
    def _():
        acc_ref[...] = jnp.zeros_like(acc_ref)

    h = h_ref[...]
    hg = jnp.dot(h, wg_ref[...].astype(BF16), preferred_element_type=F32)
    hu = jnp.dot(h, wu_ref[...].astype(BF16), preferred_element_type=F32)
    cw = _colsel(comb_ref[...], e)
    act = _silu(hg) * hu * cw
    acc_ref[...] += _bdot(act, wd_ref[...])

    @pl.when(e == MOE_EXPERTS - 1)
    def _():
        y = x_ref[...] + mod_ref[...] * acc_ref[...]
        if final:
            y = _rms(y, gf_ref[...])
        o_ref[...] = y


def _moe(h, comb, wg, wu, wd, x, mod4, layer, norm_final, final):
    tm = MOE_ROWS
    return pl.pallas_call(
        functools.partial(_moe_kernel, final=final),
        grid=(N_TOK // tm, MOE_EXPERTS),
        in_specs=[pl.BlockSpec((tm, D_MODEL), lambda i, e: (i, 0)),
                  pl.BlockSpec((tm, LANES), lambda i, e: (i, 0)),
                  pl.BlockSpec((None, None, D_MODEL, MOE_FF), lambda i, e: (layer, e, 0, 0)),
                  pl.BlockSpec((None, None, D_MODEL, MOE_FF), lambda i, e: (layer, e, 0, 0)),
                  pl.BlockSpec((None, None, MOE_FF, D_MODEL), lambda i, e: (layer, e, 0, 0)),
                  pl.BlockSpec((tm, D_MODEL), lambda i, e: (i, 0)),
                  pl.BlockSpec((None, None, 1, D_MODEL), lambda i, e: (layer, _mod_index(i, tm), 0, 5)),
                  pl.BlockSpec((1, D_MODEL), lambda i, e: (0, 0))],
        out_specs=pl.BlockSpec((tm, D_MODEL), lambda i, e: (i, 0)),
        out_shape=jax.ShapeDtypeStruct((N_TOK, D_MODEL), F32),
        scratch_shapes=[pltpu.VMEM((tm, D_MODEL), F32)],
        compiler_params=_cparams("arbitrary", "arbitrary"),
        name="moe",
    )(h, comb, wg, wu, wd, x, mod4, norm_final.reshape(1, D_MODEL))


def _rope_tables():
    t = jnp.arange(DEC_SEQ)
    pos = jnp.stack([(t // GRID_W).astype(F32), (t % GRID_W).astype(F32)], axis=1)
    nf = MLA_ROPE // 4
    inv = ROPE_BASE ** (-jnp.arange(nf, dtype=F32) / nf)
    j = jnp.arange(MLA_ROPE)
    ang = pos[:, j // (2 * nf)] * inv[j % nf][None, :]
    sign = jnp.where((j % (2 * nf)) < nf, -1.0, 1.0)
    cos = jnp.pad(jnp.cos(ang), ((0, 0), (MLA_NOPE, LANES - MLA_NOPE - MLA_ROPE)), constant_values=1.0)
    sin = jnp.pad(jnp.sin(ang) * sign, ((0, 0), (MLA_NOPE, LANES - MLA_NOPE - MLA_ROPE)))
    cos = jnp.concatenate([jnp.ones((ROW_TILE, LANES), F32), cos], axis=0)
    sin = jnp.concatenate([jnp.zeros((ROW_TILE, LANES), F32), sin], axis=0)
    return cos, sin


def _pad_heads(w, n_heads, width, lo, hi):
    k = w.shape[0]
    w = w.reshape(k, n_heads, width)[:, :, lo:hi]
    w = jnp.pad(w, ((0, 0), (0, 0), (0, LANES - (hi - lo))))
    return w.reshape(k, n_heads * LANES)


def kernel(x_prompt, x_sample, c, cache_mla_kv, cache_mla_krope, state_mlstm_C, state_mlstm_n, state_mlstm_m, state_ssd, state_gdn, c_ctx, ada_w, ada_b, norm_mix, norm_ffn, w_in_even, ml_i_bias, ml_f_bias, ml_norm, mla_q_norm, mla_w_uq, mla_kv_norm, mla_w_ukv, w_out_even, w_in_odd, ssd_conv_w, ssd_conv_b, ssd_dt_bias, ssd_A_log, ssd_D, ssd_norm, gdn_conv_w, gdn_conv_b, gdn_dt_bias, gdn_A_log, gdn_norm, w_out_odd, moe_w_group, moe_b_group, moe_w_expert, moe_b_expert, moe_w_gate, moe_w_up, moe_w_down, norm_final):
    x = jnp.concatenate([x_prompt.reshape(N_PROMPT, D_MODEL), x_sample.reshape(N_SAMPLE, D_MODEL)], axis=0)
    cond = jnp.concatenate([c_ctx[None, :], c, jnp.zeros((N_COND - 1 - DEC_BATCH, D_MODEL), F32)], axis=0)
    mod4 = _ada(cond, ada_w, ada_b).reshape(DEPTH, N_COND, 1, 6 * D_MODEL)
    cos_tab, sin_tab = _rope_tables()

    def moe_layer(x, layer, final):
        w_route = jnp.concatenate([moe_w_expert[layer], moe_w_group[layer]], axis=1)
        w_route = jnp.pad(w_route, ((0, 0), (0, LANES - MOE_EXPERTS - MOE_GROUPS)))
        b_route = _lane_row(jnp.concatenate([moe_b_expert[layer], moe_b_group[layer]]))
        h, comb = _router(x, norm_ffn[layer], mod4, layer, w_route, b_route)
        return _moe(h, comb, moe_w_gate, moe_w_up, moe_w_down, x, mod4, layer, norm_final, final)

    e = 0
    w = w_in_even[e]
    off = np.cumsum([0, 4 * ML_HEADS * ML_DK, 2 * ML_HEADS, 2 * ML_HEADS, MLA_Q_RANK, MLA_KV_RANK, MLA_ROPE])
    w_misc = jnp.concatenate([w[:, off[1]:off[3]],
                              jnp.zeros((D_MODEL, MLA_NOPE - 4 * ML_HEADS), F32),
                              w[:, off[5]:off[6]],
                              jnp.zeros((D_MODEL, LANES - MLA_NOPE - MLA_ROPE), F32)], axis=1)
    weights = [w[:, :off[1]].astype(BF16), w[:, off[3]:off[4]].astype(BF16),
               w[:, off[4]:off[5]].astype(BF16), w_misc.astype(BF16)]
    qkvo, cq, ckv, misc0 = _inproj(x, norm_mix[0], mod4, 0, weights)

    hm_p, st_c, st_n, st_m = _mlstm(qkvo, misc0, ml_i_bias[e], ml_f_bias[e], ml_norm[e], prompt=True)
    (hm_s,) = _mlstm(qkvo, misc0, ml_i_bias[e], ml_f_bias[e], ml_norm[e], prompt=False,
                     init=(state_mlstm_C[:, e], state_mlstm_n[:, e], state_mlstm_m[:, e]))
    hm = jnp.concatenate([hm_p, hm_s], axis=0)

    dq = MLA_NOPE + MLA_ROPE
    w_uq_pad = _pad_heads(mla_w_uq[e], MLA_HEADS, dq, 0, dq).astype(BF16)
    w_uk_pad = _pad_heads(mla_w_ukv[e], MLA_HEADS, MLA_NOPE + MLA_V, 0, MLA_NOPE).astype(BF16)
    w_uv = mla_w_ukv[e].reshape(MLA_KV_RANK, MLA_HEADS, MLA_NOPE + MLA_V)[:, :, MLA_NOPE:]
    w_uv = w_uv.reshape(MLA_KV_RANK, MLA_HEADS * MLA_V).astype(BF16)
    q_cat = _mla_q(cq, mla_q_norm[e], w_uq_pad, cos_tab, sin_tab)
    ckv_n, k_cat, v_all = _mla_kv(ckv, misc0, w_uk_pad, w_uv, mla_kv_norm[e], cos_tab, sin_tab)
    cache_kpe = jnp.pad(cache_mla_krope[:, e].reshape(DEC_BATCH * PAST_LEN, MLA_ROPE),
                        ((0, 0), (MLA_NOPE, LANES - MLA_NOPE - MLA_ROPE)))
    k_cache, v_cache = _mla_kv(cache_mla_kv[:, e].reshape(DEC_BATCH * PAST_LEN, MLA_KV_RANK), cache_kpe,
                               w_uk_pad, w_uv)
    att_p = _attention(q_cat, k_cat, v_all, B=BATCH, Tq=SEQ, Tk=SEQ, tq=SEQ, q_row0=0)

    def with_cache(cache, new):
        width = new.shape[1]
        both = jnp.concatenate([cache.reshape(DEC_BATCH, PAST_LEN, width),
                                new[N_PROMPT:].reshape(DEC_BATCH, DEC_SEQ, width)], axis=1)
        return both.reshape(DEC_BATCH * (PAST_LEN + DEC_SEQ), width)

    att_s = _attention(q_cat, with_cache(k_cache, k_cat), with_cache(v_cache, v_all), B=DEC_BATCH, Tq=DEC_SEQ,
                       Tk=PAST_LEN + DEC_SEQ, tq=256, q_row0=N_PROMPT)
    att = jnp.concatenate([att_p, att_s], axis=0)
    wo = w_out_even[e].astype(BF16)
    x = _outproj(hm, att, wo[:ML_HEADS * ML_DV], wo[ML_HEADS * ML_DV:], x, mod4, 0)
    x = moe_layer(x, 0, final=False)

    oi = 0
    w = w_in_odd[oi]
    ssd_w = SSD_HEADS * SSD_P
    ssd_cc = ssd_w + 2 * SSD_GROUPS * SSD_N
    gdn_w = GDN_HEADS * GDN_DK
    off = np.cumsum([0, ssd_w, ssd_cc, 2 * SSD_HEADS, 3 * gdn_w, gdn_w, 2 * GDN_HEADS, 2 * GDN_HEADS])
    w_misc = jnp.concatenate([w[:, off[2]:off[3]], w[:, off[5]:off[7]],
                              jnp.zeros((D_MODEL, LANES - 2 * SSD_HEADS - 4 * GDN_HEADS), F32)], axis=1)
    weights = [w[:, off[0]:off[1]].astype(BF16), w[:, off[1]:off[2]].astype(BF16),
               w[:, off[3]:off[4]].astype(BF16), w[:, off[4]:off[5]].astype(BF16), w_misc.astype(BF16)]
    z_s, xbc, qkv_g, z_g, misc = _inproj(x, norm_mix[1], mod4, 1, weights)

    ssd_args = (xbc, misc, ssd_conv_w[oi], ssd_conv_b[oi], ssd_dt_bias[oi], ssd_A_log[oi], ssd_D[oi])
    ys_p, st_ssd = _ssd(*ssd_args, prompt=True)
    (ys_s,) = _ssd(*ssd_args, prompt=False, init=state_ssd[:, oi])
    ys = jnp.concatenate([ys_p, ys_s], axis=0)

    lo = 2 * SSD_HEADS
    gdn_dt_row = jnp.pad(gdn_dt_bias[oi].reshape(-1), (lo, LANES - lo - 2 * GDN_HEADS)).reshape(1, LANES)
    gdn_alog_row = jnp.pad(gdn_A_log[oi].reshape(-1), (lo, LANES - lo - 2 * GDN_HEADS)).reshape(1, LANES)
    gdn_args = (qkv_g, misc, z_g, gdn_conv_w[oi], gdn_conv_b[oi], gdn_dt_row, gdn_alog_row, gdn_norm[oi])
    og_p, st_gdn = _gdn(*gdn_args, prompt=True)
    (og_s,) = _gdn(*gdn_args, prompt=False, init=state_gdn[:, oi])
    og = jnp.concatenate([og_p, og_s], axis=0)

    wo = w_out_odd[oi].astype(BF16)
    x = _outproj(ys, og, wo[:ssd_w], wo[ssd_w:], x, mod4, 1, z=z_s, gnorm=ssd_norm[oi])
    x = moe_layer(x, 1, final=True)

    y_prompt = x[:N_PROMPT].reshape(BATCH, SEQ, D_MODEL)
    y_sample = x[N_PROMPT:].reshape(DEC_BATCH, DEC_SEQ, D_MODEL)
    new_mla_kv = ckv_n[:N_PROMPT].reshape(BATCH, 1, SEQ, MLA_KV_RANK)
    new_mla_krope = misc0[:N_PROMPT, MLA_NOPE:MLA_NOPE + MLA_ROPE].reshape(BATCH, 1, SEQ, MLA_ROPE)
    return (y_prompt, y_sample, new_mla_kv, new_mla_krope, st_c[:, None], st_n.reshape(BATCH, 1, 2, ML_HEADS, ML_DK),
            st_m.reshape(BATCH, 1, 2, ML_HEADS), st_ssd[:, None], st_gdn[:, None])
```

```python
import functools

import numpy as np
import jax
import jax.numpy as jnp
from jax import lax
from jax.experimental import pallas as pl
from jax.experimental.pallas import tpu as pltpu

F32 = jnp.float32
BF16 = jnp.bfloat16

D_MODEL = 1024
BATCH = 32
SEQ = 256
DEPTH = 2
DEC_BATCH = 2
DEC_SEQ = 2048
PAST_LEN = 256
GRID_W = 64
EPS = 1e-6
ML_HEADS = 4
ML_DK = 128
ML_DV = 128
MLA_HEADS = 8
MLA_Q_RANK = 384
MLA_KV_RANK = 256
MLA_NOPE = 64
MLA_ROPE = 32
MLA_V = 64
ROPE_BASE = 10000.0
SSD_HEADS = 8
SSD_P = 64
SSD_GROUPS = 2
SSD_N = 128
GDN_HEADS = 4
GDN_DK = 128
GDN_DV = 128
CONV_W = 5
MOE_GROUPS = 4
MOE_PER_GROUP = 8
MOE_EXPERTS = 32
MOE_FF = 256

N_PROMPT = BATCH * SEQ
N_SAMPLE = DEC_BATCH * DEC_SEQ
N_TOK = N_PROMPT + N_SAMPLE
N_COND = 8

LANES = 128
CHUNK = 64
ROW_TILE = 512
CONV_HALO = 8
CONV_ROWS = 256
VMEM_LIMIT = 56 * 1024 * 1024
PRE_CHUNKS = 8
STEP_CHUNKS = 4

assert LANES == 2 * CHUNK and SSD_P == CHUNK

NEG_INF = float("-inf")


def _cparams(*sem):
    return pltpu.CompilerParams(dimension_semantics=sem, vmem_limit_bytes=VMEM_LIMIT)


def _dot(a, b):
    return jnp.dot(a, b, preferred_element_type=F32)


def _dot_nt(a, b):
    return lax.dot_general(a, b, (((1,), (1,)), ((), ())), preferred_element_type=F32)


def _dot_tn(a, b):
    return lax.dot_general(a, b, (((0,), (0,)), ((), ())), preferred_element_type=F32)


def _bdot(a, b):
    return _dot(a.astype(BF16), b.astype(BF16))


def _rms(x, g):
    return x * lax.rsqrt(jnp.mean(x * x, axis=-1, keepdims=True) + EPS) * g


def _softplus(x):
    return jnp.maximum(x, 0.0) + jnp.log1p(jnp.exp(-jnp.abs(x)))


def _silu(x):
    return x * jax.nn.sigmoid(x)


def _colsel(x, j):
    lane = lax.broadcasted_iota(jnp.int32, x.shape, 1)
    return jnp.sum(jnp.where(lane == j, x, 0.0), axis=1, keepdims=True)


def _lanes(x):
    return jnp.broadcast_to(x, (x.shape[0], LANES))


def _half_masks(rev_lo, rev_hi):
    t = lax.broadcasted_iota(jnp.int32, (CHUNK, LANES), 0)
    lane = lax.broadcasted_iota(jnp.int32, (CHUNK, LANES), 1)
    s = lane & (CHUNK - 1)
    hi = lane >= CHUNK

    def pick(fwd, bwd):
        if rev_lo == rev_hi:
            return bwd if rev_lo else fwd
        on_hi, on_lo = (bwd, fwd) if rev_hi else (fwd, bwd)
        return (hi & on_hi) | (jnp.logical_not(hi) & on_lo)

    return dict(hi=hi, t=t, s=s, eye=(s == t), incl=pick(s <= t, s >= t), incl_t=pick(t <= s, t >= s),
                strict=pick(s < t, s > t))


def _pack(hm, col_lo, col_hi):
    return jnp.where(hm["hi"], col_hi, col_lo)


def _rows(hm, cols):
    return jnp.sum(jnp.where(hm["eye"], cols, 0.0), axis=0, keepdims=True)


def _half_sums(hm, x):
    lo = jnp.sum(jnp.where(hm["hi"], 0.0, x), axis=1, keepdims=True)
    hi = jnp.sum(jnp.where(hm["hi"], x, 0.0), axis=1, keepdims=True)
    return lo, hi


def _half_cum(hm, col_lo, col_hi):
    cols = _pack(hm, col_lo, col_hi)
    cum_lo, cum_hi = _half_sums(hm, jnp.where(hm["incl"], _rows(hm, cols), 0.0))
    cum_rows = jnp.sum(jnp.where(hm["incl_t"], cols, 0.0), axis=0, keepdims=True)
    return cum_lo, cum_hi, cum_rows


def _blockdiag(hm, x):
    return jnp.concatenate([jnp.where(hm["hi"], 0.0, x).astype(BF16),
                            jnp.where(hm["hi"], x, 0.0).astype(BF16)], axis=0)


def _chunk_rows(c):
    if isinstance(c, int):
        return pl.ds(c * CHUNK, CHUNK)
    return pl.ds(pl.multiple_of(c * CHUNK, CHUNK), CHUNK)


def _block_loop(n, body, size):
    if n <= size:
        body(list(range(n)))
        return

    def block(blk, carry):
        body([blk * size + j for j in range(size)])
        return carry

    lax.fori_loop(0, n // size, block, 0)


def _mod_index(i, rows_per_tile):
    p_tiles = N_PROMPT // rows_per_tile
    s_tiles = DEC_SEQ // rows_per_tile
    return jnp.where(i < p_tiles, 0, 1 + (i - p_tiles) // s_tiles)


def _ada_kernel(c_ref, w_ref, b_ref, o_ref):
    c = c_ref[...]
    o_ref[...] = _bdot(_silu(c), w_ref[...]) + b_ref[...]


def _ada(cond, ada_w, ada_b):
    nb = 6
    return pl.pallas_call(
        _ada_kernel,
        grid=(DEPTH, nb),
        in_specs=[pl.BlockSpec((N_COND, D_MODEL), lambda l, j: (0, 0)),
                  pl.BlockSpec((None, D_MODEL, D_MODEL), lambda l, j: (l, 0, j)),
                  pl.BlockSpec((None, 1, D_MODEL), lambda l, j: (l, 0, j))],
        out_specs=pl.BlockSpec((None, N_COND, D_MODEL), lambda l, j: (l, 0, j)),
        out_shape=jax.ShapeDtypeStruct((DEPTH, N_COND, 6 * D_MODEL), F32),
        compiler_params=_cparams("arbitrary", "arbitrary"),
        name="ada",
    )(cond, ada_w, ada_b.reshape(DEPTH, 1, 6 * D_MODEL))


def _inproj_kernel(x_ref, g_ref, mod_ref, *rest, n_out):
    w_refs, o_refs = rest[:n_out], rest[n_out:]
    mod = mod_ref[...]
    h = _rms(x_ref[...], g_ref[...]) * (1.0 + mod[:, D_MODEL:2 * D_MODEL]) + mod[:, :D_MODEL]
    hb = h.astype(BF16)
    for w_ref, o_ref in zip(w_refs, o_refs):
        o_ref[...] = _dot(hb, w_ref[...])


def _inproj(x, gain, mod4, layer, weights):
    n_out = len(weights)
    tm = ROW_TILE
    in_specs = [pl.BlockSpec((tm, D_MODEL), lambda i: (i, 0)),
                pl.BlockSpec((1, D_MODEL), lambda i: (0, 0)),
                pl.BlockSpec((None, None, 1, 2 * D_MODEL), lambda i: (layer, _mod_index(i, tm), 0, 0))]
    in_specs += [pl.BlockSpec(w.shape, lambda i: (0, 0)) for w in weights]
    return pl.pallas_call(
        functools.partial(_inproj_kernel, n_out=n_out),
        grid=(N_TOK // tm,),
        in_specs=in_specs,
        out_specs=[pl.BlockSpec((tm, w.shape[1]), lambda i: (i, 0)) for w in weights],
        out_shape=[jax.ShapeDtypeStruct((N_TOK, w.shape[1]), F32) for w in weights],
        compiler_params=_cparams("arbitrary"),
        name="inproj",
    )(x, gain.reshape(1, D_MODEL), mod4, *weights)


def _mlstm_kernel(*refs, T, has_init, emit_state):
    it = iter(refs)
    ib_ref, fb_ref = next(it), next(it)
    m0_ref = next(it) if has_init else None
    q_ref, k_ref, v_ref, o_ref, misc_ref, gn_ref = (next(it) for _ in range(6))
    c0_ref, n0_ref = (next(it), next(it)) if has_init else (None, None)
    hm_ref = next(it)
    cout_ref, nout_ref, mout_ref = (next(it), next(it), next(it)) if emit_state else (None, None, None)
    (hb_scr, c_scr, n_scr, m_scr, num_scr, st_scr, kv_scr, nl_scr, bl_scr, gm_scr) = (next(it) for _ in range(10))

    b = pl.program_id(0)
    h = pl.program_id(1)
    nc = T // CHUNK
    scale = ML_DK ** -0.5

    for d in range(2):
        if has_init:
            c_scr[d] = c0_ref[d]
            n_scr[d] = n0_ref[d]
            m_scr[d] = jnp.full((1, LANES), m0_ref[b, d, h], F32)
        else:
            c_scr[d] = jnp.zeros((ML_DK, ML_DV), F32)
            n_scr[d] = jnp.zeros((1, ML_DK), F32)
            m_scr[d] = jnp.zeros((1, LANES), F32)

    hm = _half_masks(False, True)

    def pre_gates(c):
        rows = _chunk_rows(c)
        mi = misc_ref[rows, :]
        li = [_colsel(mi, d * ML_HEADS + h) + ib_ref[d, h] for d in range(2)]
        lf = [-_softplus(-(_colsel(mi, 2 * ML_HEADS + d * ML_HEADS + h) + fb_ref[d, h])) for d in range(2)]
        b_cols = [None, None]
        b_cols[0], b_cols[1], b_rows = _half_cum(hm, lf[0], lf[1])
        li_rows = _rows(hm, _pack(hm, li[0], li[1]))
        dm = jnp.where(hm["incl"], _pack(hm, b_cols[0], b_cols[1]) - b_rows + li_rows, NEG_INF)
        mloc = [jnp.max(jnp.where(hm["hi"], NEG_INF, dm), axis=1, keepdims=True),
                jnp.max(jnp.where(hm["hi"], dm, NEG_INF), axis=1, keepdims=True)]
        kc = k_ref[rows, :]
        kws = []
        for d in range(2):
            bl = jnp.sum(lf[d], axis=0, keepdims=True)
            g = bl - b_cols[d] + li[d]
            gmax = jnp.max(g, axis=0, keepdims=True)
            kw = jnp.exp(g - gmax) * kc
            kws.append(kw.astype(BF16))
            nl_scr[d, c] = jnp.sum(kw, axis=0, keepdims=True)
            bl_scr[d, c] = _lanes(bl)
            gm_scr[d, c] = _lanes(gmax)
            st_scr[d, 0, rows, :] = _lanes(mloc[d])
            st_scr[d, 1, rows, :] = _lanes(b_cols[d])
        return dict(c=c, rows=rows, kc=kc.astype(BF16), kw2=jnp.concatenate(kws, axis=1),
                    e2=jnp.exp(dm - _pack(hm, mloc[0], mloc[1])))

    def pre_block(chunks):
        chains = [pre_gates(c) for c in chunks]
        for ch in chains:
            qc = (q_ref[ch["rows"], :] * scale).astype(BF16)
            ch["vb"] = v_ref[ch["rows"], :].astype(BF16)
            ch["qk2"] = _dot_nt(qc, jnp.concatenate([ch["kc"], ch["kc"]], axis=0))
            kv2 = _dot_tn(ch["kw2"], ch["vb"])
            kv_scr[0, ch["c"]] = kv2[:ML_DK]
            kv_scr[1, ch["c"]] = kv2[ML_DK:]
        for ch in chains:
            s2 = ch["qk2"] * ch["e2"]
            ch["s2"] = s2.astype(BF16)
            dens = _half_sums(hm, s2)
            for d in range(2):
                st_scr[d, 2, ch["rows"], :] = _lanes(dens[d])
        for ch in chains:
            vb = ch["vb"]
            zero = jnp.zeros_like(vb)
            vbd = jnp.concatenate([jnp.concatenate([vb, zero], axis=1),
                                   jnp.concatenate([zero, vb], axis=1)], axis=0)
            num2 = _dot(ch["s2"], vbd)
            num_scr[0, ch["rows"], :] = num2[:, :ML_DV]
            num_scr[1, ch["rows"], :] = num2[:, ML_DV:]

    def step_block(chunks):
        states = [(m_scr[d], c_scr[d], n_scr[d]) for d in range(2)]
        pairs = []
        for c in chunks:
            pair = []
            for d in range(2):
                cc = c if d == 0 else nc - 1 - c
                rows = _chunk_rows(cc)
                m, c_st, n_st = states[d]
                mloc, b_col, den_loc = st_scr[d, 0, rows, :], st_scr[d, 1, rows, :], st_scr[d, 2, rows, :]
                bl, gmax = bl_scr[d, cc], gm_scr[d, cc]
                qc = q_ref[rows, :] * scale
                inter = b_col + m
                mq = jnp.maximum(inter, mloc)
                a = jnp.exp(inter - mq)
                f = jnp.exp(mloc - mq)
                den = f * den_loc + a * jnp.sum(qc * n_st, axis=1, keepdims=True)
                pair.append(dict(rows=rows, qc=qc.astype(BF16), c_st=c_st.astype(BF16), a=a, f=f,
                                 inv=1.0 / jnp.maximum(jnp.abs(den), jnp.exp(-mq))))
                m_new = jnp.maximum(bl + m, gmax)
                dec = jnp.exp(bl + m - m_new)
                fk = jnp.exp(gmax - m_new)
                states[d] = (m_new, dec * c_st + fk * kv_scr[d, cc], dec * n_st + fk * nl_scr[d, cc])
            pairs.append(pair)
        for d in range(2):
            m_scr[d], c_scr[d], n_scr[d] = states[d]
        for pair in pairs:
            lhs = jnp.concatenate([pair[0]["qc"], pair[1]["qc"]], axis=0)
            rhs = jnp.concatenate([pair[0]["c_st"], pair[1]["c_st"]], axis=1)
            pair.append(_dot(lhs, rhs))
        for pair in pairs:
            res = pair[2]
            for d, dst in enumerate((hm_ref, hb_scr)):
                it_ = pair[d]
                qc_c = res[d * CHUNK:(d + 1) * CHUNK, d * ML_DV:(d + 1) * ML_DV]
                dst[it_["rows"], :] = (it_["f"] * num_scr[d, it_["rows"], :] + it_["a"] * qc_c) * it_["inv"]

    _block_loop(nc, pre_block, PRE_CHUNKS)
    _block_loop(nc, step_block, STEP_CHUNKS)

    hs = hm_ref[...] + hb_scr[...]
    hm_ref[...] = _rms(hs, gn_ref[...]) * jax.nn.sigmoid(o_ref[...])
    if emit_state:
        for d in range(2):
            cout_ref[d] = c_scr[d]
            nout_ref[d] = n_scr[d]
            mout_ref[d] = m_scr[d][:, 0:1]


def _mlstm(qkvo, misc, i_bias, f_bias, gnorm, *, prompt, init=None):
    T, B = (SEQ, BATCH) if prompt else (DEC_SEQ, DEC_BATCH)
    blk0 = 0 if prompt else N_PROMPT // T
    nc = T // CHUNK
    has_init = init is not None
    smem = pl.BlockSpec(memory_space=pltpu.SMEM)

    def col(j):
        return pl.BlockSpec((T, LANES), lambda b, h: (blk0 + b, j * ML_HEADS + h))

    in_specs = [smem, smem]
    args = [i_bias, f_bias]
    if has_init:
        in_specs.append(smem)
        args.append(init[2])
    in_specs += [col(0), col(1), col(2), col(3),
                 pl.BlockSpec((T, LANES), lambda b, h: (blk0 + b, 0)),
                 pl.BlockSpec((None, 1, ML_DV), lambda b, h: (h, 0, 0))]
    args += [qkvo, qkvo, qkvo, qkvo, misc, gnorm.reshape(ML_HEADS, 1, ML_DV)]
    if has_init:
        in_specs += [pl.BlockSpec((None, 2, None, ML_DK, ML_DV), lambda b, h: (b, 0, h, 0, 0)),
                     pl.BlockSpec((None, 2, None, 1, ML_DK), lambda b, h: (b, 0, h, 0, 0))]
        args += [init[0], init[1].reshape(B, 2, ML_HEADS, 1, ML_DK)]
    out_specs = [pl.BlockSpec((T, LANES), lambda b, h: (b, h))]
    out_shape = [jax.ShapeDtypeStruct((B * T, ML_HEADS * ML_DV), F32)]
    if prompt:
        out_specs += [pl.BlockSpec((None, 2, None, ML_DK, ML_DV), lambda b, h: (b, 0, h, 0, 0)),
                      pl.BlockSpec((None, 2, None, 1, ML_DK), lambda b, h: (b, 0, h, 0, 0)),
                      pl.BlockSpec((None, 2, None, 1, 1), lambda b, h: (b, 0, h, 0, 0))]
        out_shape += [jax.ShapeDtypeStruct((B, 2, ML_HEADS, ML_DK, ML_DV), F32),
                      jax.ShapeDtypeStruct((B, 2, ML_HEADS, 1, ML_DK), F32),
                      jax.ShapeDtypeStruct((B, 2, ML_HEADS, 1, 1), F32)]
    return pl.pallas_call(
        functools.partial(_mlstm_kernel, T=T, has_init=has_init, emit_state=prompt),
        grid=(B, ML_HEADS),
        in_specs=in_specs,
        out_specs=out_specs,
        out_shape=out_shape,
        scratch_shapes=[pltpu.VMEM((T, ML_DV), F32),
                        pltpu.VMEM((2, ML_DK, ML_DV), F32),
                        pltpu.VMEM((2, 1, ML_DK), F32),
                        pltpu.VMEM((2, 1, LANES), F32),
                        pltpu.VMEM((2, T, ML_DV), F32),
                        pltpu.VMEM((2, 3, T, LANES), F32),
                        pltpu.VMEM((2, nc, ML_DK, ML_DV), F32),
                        pltpu.VMEM((2, nc, 1, ML_DK), F32),
                        pltpu.VMEM((2, nc, 1, LANES), F32),
                        pltpu.VMEM((2, nc, 1, LANES), F32)],
        compiler_params=_cparams("arbitrary", "arbitrary"),
        name="mlstm_prompt" if prompt else "mlstm_sample",
    )(*args)


def _rope(x, cos, sin_signed):
    lane = lax.broadcasted_iota(jnp.int32, x.shape, 1)
    first = (lane & 15) < 8
    partner = jnp.where(first, pltpu.roll(x, LANES - 8, axis=1), pltpu.roll(x, 8, axis=1))
    return x * cos + partner * sin_signed


def _q_kernel(cq_ref, g_ref, w_ref, cos_ref, sin_ref, q_ref):
    cq = _rms(cq_ref[...], g_ref[...])
    y = _bdot(cq, w_ref[...])
    cos, sin = cos_ref[...], sin_ref[...]
    for hd in range(MLA_HEADS):
        sl = slice(hd * LANES, (hd + 1) * LANES)
        q_ref[:, sl] = _rope(y[:, sl], cos, sin).astype(BF16)


def _rope_block_index(i, tm):
    p_tiles = N_PROMPT // tm
    s_tiles = DEC_SEQ // tm
    return jnp.where(i < p_tiles, 0, 1 + (i - p_tiles) % s_tiles)


def _mla_q(cq, gain, w_uq_pad, cos_tab, sin_tab):
    tm = ROW_TILE
    tab = pl.BlockSpec((tm, LANES), lambda i: (_rope_block_index(i, tm), 0))
    return pl.pallas_call(
        _q_kernel,
        grid=(N_TOK // tm,),
        in_specs=[pl.BlockSpec((tm, MLA_Q_RANK), lambda i: (i, 0)),
                  pl.BlockSpec((1, MLA_Q_RANK), lambda i: (0, 0)),
                  pl.BlockSpec(w_uq_pad.shape, lambda i: (0, 0)),
                  tab, tab],
        out_specs=pl.BlockSpec((tm, MLA_HEADS * LANES), lambda i: (i, 0)),
        out_shape=jax.ShapeDtypeStruct((N_TOK, MLA_HEADS * LANES), BF16),
        compiler_params=_cparams("arbitrary"),
        name="mla_q",
    )(cq, gain.reshape(1, MLA_Q_RANK), w_uq_pad, cos_tab, sin_tab)


def _kv_kernel(*refs, norm):
    if norm:
        ckv_ref, kpe_ref, g_ref, wk_ref, wv_ref, cos_ref, sin_ref, ckvn_ref, k_ref, v_ref = refs
        c = _rms(ckv_ref[...], g_ref[...])
        ckvn_ref[...] = c
    else:
        ckv_ref, kpe_ref, wk_ref, wv_ref, k_ref, v_ref = refs
        c = ckv_ref[...]
    kp = kpe_ref[...]
    lane = lax.broadcasted_iota(jnp.int32, kp.shape, 1)
    kp = jnp.where((lane >= MLA_NOPE) & (lane < MLA_NOPE + MLA_ROPE), kp, 0.0)
    if norm:
        kp = _rope(kp, cos_ref[...], sin_ref[...])
    kn = _bdot(c, wk_ref[...])
    for hd in range(MLA_HEADS):
        sl = slice(hd * LANES, (hd + 1) * LANES)
        k_ref[:, sl] = (kn[:, sl] + kp).astype(BF16)
    v_ref[...] = _bdot(c, wv_ref[...]).astype(BF16)


def _mla_kv(ckv, kpe128, w_uk_pad, w_uv, gain=None, cos_tab=None, sin_tab=None):
    norm = gain is not None
    n = ckv.shape[0]
    tm = ROW_TILE
    row = lambda w: pl.BlockSpec((tm, w), lambda i: (i, 0))
    full = lambda a: pl.BlockSpec(a.shape, lambda i: (0, 0))
    in_specs = [row(MLA_KV_RANK), row(LANES)]
    args = [ckv, kpe128]
    if norm:
        in_specs.append(pl.BlockSpec((1, MLA_KV_RANK), lambda i: (0, 0)))
        args.append(gain.reshape(1, MLA_KV_RANK))
    in_specs += [full(w_uk_pad), full(w_uv)]
    args += [w_uk_pad, w_uv]
    out_specs = [row(MLA_HEADS * LANES), row(MLA_HEADS * MLA_V)]
    out_shape = [jax.ShapeDtypeStruct((n, MLA_HEADS * LANES), BF16),
                 jax.ShapeDtypeStruct((n, MLA_HEADS * MLA_V), BF16)]
    if norm:
        tab = pl.BlockSpec((tm, LANES), lambda i: (_rope_block_index(i, tm), 0))
        in_specs += [tab, tab]
        args += [cos_tab, sin_tab]
        out_specs = [row(MLA_KV_RANK)] + out_specs
        out_shape = [jax.ShapeDtypeStruct((n, MLA_KV_RANK), F32)] + out_shape
    return pl.pallas_call(
        functools.partial(_kv_kernel, norm=norm),
        grid=(n // tm,),
        in_specs=in_specs,
        out_specs=out_specs,
        out_shape=out_shape,
        compiler_params=_cparams("arbitrary"),
        name="mla_kv" if norm else "mla_kv_cache",
    )(*args)


def _attn_kernel(q_ref, k_ref, v_ref, o_ref):
    scale = (MLA_NOPE + MLA_ROPE) ** -0.5
    outs = []
    for j in range(2):
        q = q_ref[:, j * LANES:(j + 1) * LANES]
        k = k_ref[:, j * LANES:(j + 1) * LANES]
        s = _dot_nt(q, k) * scale
        p = jnp.exp(s - jnp.max(s, axis=1, keepdims=True))
        l = jnp.sum(p, axis=1, keepdims=True)
        pv = _dot(p.astype(BF16), v_ref[:, j * MLA_V:(j + 1) * MLA_V])
        outs.append(pv / l)
    o_ref[...] = jnp.concatenate(outs, axis=1)


def _attention(q, k, v, *, B, Tq, Tk, tq, q_row0):
    nq = Tq // tq
    blk0 = q_row0 // tq
    return pl.pallas_call(
        _attn_kernel,
        grid=(B, MLA_HEADS // 2, nq),
        in_specs=[pl.BlockSpec((tq, 2 * LANES), lambda b, hp, i: (blk0 + b * nq + i, hp)),
                  pl.BlockSpec((Tk, 2 * LANES), lambda b, hp, i: (b, hp)),
                  pl.BlockSpec((Tk, 2 * MLA_V), lambda b, hp, i: (b, hp))],
        out_specs=pl.BlockSpec((tq, 2 * MLA_V), lambda b, hp, i: (b * nq + i, hp)),
        out_shape=jax.ShapeDtypeStruct((B * Tq, MLA_HEADS * MLA_V), F32),
        compiler_params=_cparams("arbitrary", "arbitrary", "arbitrary"),
        name="attention",
    )(q, k, v)


def _outproj_kernel(*refs, odd):
    if odd:
        a1_ref, z_ref, gn_ref, a2_ref, w1_ref, w2_ref, x_ref, g1_ref, o_ref = refs
        a1 = _rms(a1_ref[...] * _silu(z_ref[...]), gn_ref[...])
    else:
        a1_ref, a2_ref, w1_ref, w2_ref, x_ref, g1_ref, o_ref = refs
        a1 = a1_ref[...]
    out = _bdot(a1, w1_ref[...]) + _bdot(a2_ref[...], w2_ref[...])
    o_ref[...] = x_ref[...] + g1_ref[...] * out


def _outproj(a1, a2, w1, w2, x, mod4, layer, z=None, gnorm=None):
    odd = z is not None
    tm = ROW_TILE
    half = a1.shape[1]
    row = lambda w: pl.BlockSpec((tm, w), lambda i: (i, 0))
    full = lambda a: pl.BlockSpec(a.shape, lambda i: (0, 0))
    in_specs, args = [row(half)], [a1]
    if odd:
        in_specs += [row(half), pl.BlockSpec((1, half), lambda i: (0, 0))]
        args += [z, gnorm.reshape(1, half)]
    in_specs += [row(half), full(w1), full(w2), row(D_MODEL),
                 pl.BlockSpec((None, None, 1, D_MODEL), lambda i: (layer, _mod_index(i, tm), 0, 2))]
    args += [a2, w1, w2, x, mod4]
    return pl.pallas_call(
        functools.partial(_outproj_kernel, odd=odd),
        grid=(N_TOK // tm,),
        in_specs=in_specs,
        out_specs=row(D_MODEL),
        out_shape=jax.ShapeDtypeStruct((N_TOK, D_MODEL), F32),
        compiler_params=_cparams("arbitrary"),
        name="outproj",
    )(*args)


def _conv_silu(src_ref, dst_ref, pad_ref, w_ref, b_ref, T):
    width = src_ref.shape[1]
    zeros = jnp.zeros((CONV_HALO, width), F32)
    pad_ref[pl.ds(0, CONV_HALO), :] = zeros
    pad_ref[pl.ds(CONV_HALO + T, CONV_HALO), :] = zeros
    pad_ref[pl.ds(CONV_HALO, T), :] = src_ref[...]
    w = w_ref[...]
    bias = b_ref[...]
    for r0 in range(0, T, CONV_ROWS):
        acc = bias
        for j in range(CONV_W):
            start = r0 + CONV_HALO + j - CONV_W // 2
            acc = acc + w[j:j + 1, :] * pad_ref[pl.ds(start, CONV_ROWS), :]
        dst_ref[pl.ds(r0, CONV_ROWS), :] = _silu(acc)


def _ssd_kernel(*refs, T, has_init, emit_state):
    it = iter(refs)
    x_ref, b_ref, c_ref = next(it), next(it), next(it)
    wx_ref, wb_ref, wc_ref = next(it), next(it), next(it)
    bx_ref, bb_ref, bc_ref = next(it), next(it), next(it)
    misc_ref, dtb_ref, alog_ref, dskip_ref = next(it), next(it), next(it), next(it)
    h0_ref = next(it) if has_init else None
    y_ref = next(it)
    hout_ref = next(it) if emit_state else None
    (pad_scr, xc_scr, bc_scr, cc_scr, yb_scr, hs_scr, yi_scr, ea_scr, upd_scr, eal_scr) = (
        next(it) for _ in range(10))

    p = pl.program_id(1)
    nc = T // CHUNK

    _conv_silu(x_ref, xc_scr, pad_scr, wx_ref, bx_ref, T)
    _conv_silu(b_ref, bc_scr, pad_scr, wb_ref, bb_ref, T)
    _conv_silu(c_ref, cc_scr, pad_scr, wc_ref, bc_ref, T)

    for d in range(2):
        if has_init:
            hs_scr[d] = h0_ref[d].reshape(2 * SSD_P, SSD_N).T
        else:
            hs_scr[d] = jnp.zeros((SSD_N, 2 * SSD_P), F32)

    dtb = dtb_ref[...]
    neg_a = -jnp.exp(alog_ref[...])
    hms = [_half_masks(False, False), _half_masks(True, True)]
    hi = hms[0]["hi"]
    hi_row = lax.broadcasted_iota(jnp.int32, (1, LANES), 1) >= CHUNK

    def pre_block(chunks):
        chains = []
        for c in chunks:
            rows = _chunk_rows(c)
            bc = bc_scr[rows, :].astype(BF16)
            cb2 = _dot_nt(cc_scr[rows, :].astype(BF16), jnp.concatenate([bc, bc], axis=0))
            chains.append(dict(c=c, rows=rows, bc=bc, cb2=cb2))
        for ch in chains:
            rows = ch["rows"]
            xc = xc_scr[rows, :]
            dts = _softplus(misc_ref[rows, :] + dtb)
            adt = dts * neg_a
            ch["x2m"] = _blockdiag(hms[0], xc)
            ch["g2"], ch["xw"] = [], []
            for d in range(2):
                hm = hms[d]
                dt_h = [_colsel(dts, d * SSD_HEADS + 2 * p + hh) for hh in range(2)]
                a_h = [_colsel(adt, d * SSD_HEADS + 2 * p + hh) for hh in range(2)]
                cum0, cum1, cum_rows = _half_cum(hm, a_h[0], a_h[1])
                seg2 = jnp.exp(jnp.where(hm["incl"], _pack(hm, cum0, cum1) - cum_rows, NEG_INF))
                dt_rows = _rows(hm, _pack(hm, dt_h[0], dt_h[1]))
                ch["g2"].append((ch["cb2"] * seg2 * dt_rows).astype(BF16))
                al = [jnp.sum(a, axis=0, keepdims=True) for a in a_h]
                wgt = [jnp.exp(al[hh] - cum) * dt_h[hh] for hh, cum in enumerate((cum0, cum1))]
                ch["xw"].append((xc * _pack(hm, wgt[0], wgt[1])).astype(BF16))
                ea_scr[d, rows, :] = _pack(hm, jnp.exp(cum0), jnp.exp(cum1))
                eal_scr[d, ch["c"]] = jnp.where(hi_row, jnp.exp(al[1]), jnp.exp(al[0]))
        for ch in chains:
            for d in range(2):
                yi_scr[d, ch["rows"], :] = _dot(ch["g2"][d], ch["x2m"])
                upd_scr[d, ch["c"]] = _dot_tn(ch["bc"], ch["xw"][d])

    def step_block(chunks):
        states = [hs_scr[d] for d in range(2)]
        pairs = []
        for c in chunks:
            pair = []
            for d in range(2):
                cc = c if d == 0 else nc - 1 - c
                pair.append(dict(rows=_chunk_rows(cc), hs=states[d].astype(BF16)))
                states[d] = eal_scr[d, cc] * states[d] + upd_scr[d, cc]
            pairs.append(pair)
        for d in range(2):
            hs_scr[d] = states[d]
        for pair in pairs:
            lhs = jnp.concatenate([cc_scr[pair[d]["rows"], :].astype(BF16) for d in range(2)], axis=0)
            rhs = jnp.concatenate([pair[d]["hs"] for d in range(2)], axis=1)
            pair.append(_dot(lhs, rhs))
        for pair in pairs:
            for d, dst in enumerate((y_ref, yb_scr)):
                rows = pair[d]["rows"]
                ch = pair[2][d * CHUNK:(d + 1) * CHUNK, d * LANES:(d + 1) * LANES]
                dst[rows, :] = yi_scr[d, rows, :] + ea_scr[d, rows, :] * ch

    _block_loop(nc, pre_block, PRE_CHUNKS)
    _block_loop(nc, step_block, STEP_CHUNKS)

    y_ref[...] = y_ref[...] + yb_scr[...] + dskip_ref[...] * xc_scr[...]
    if emit_state:
        for d in range(2):
            hout_ref[d] = hs_scr[d].T.reshape(2, SSD_P, SSD_N)


def _lane_row(v):
    return jnp.pad(v.astype(F32), (0, LANES - v.shape[0])).reshape(1, LANES)


def _ssd(xbc, misc, conv_w, conv_b, dt_bias, a_log, d_skip, *, prompt, init=None):
    T, B = (SEQ, BATCH) if prompt else (DEC_SEQ, DEC_BATCH)
    blk0 = 0 if prompt else N_PROMPT // T
    nc = T // CHUNK
    has_init = init is not None
    n_pairs = SSD_HEADS // 2
    pairs_per_group = n_pairs // SSD_GROUPS
    xb = SSD_HEADS * SSD_P // LANES
    cb = xb + SSD_GROUPS * SSD_N // LANES
    colx = lambda b, p: (blk0 + b, p)
    colb = lambda b, p: (blk0 + b, xb + p // pairs_per_group)
    colc = lambda b, p: (blk0 + b, cb + p // pairs_per_group)
    wsel = lambda f: (lambda b, p: (0, f(b, p)[1]))
    row128 = pl.BlockSpec((1, LANES), lambda b, p: (0, 0))
    in_specs = [pl.BlockSpec((T, LANES), colx), pl.BlockSpec((T, LANES), colb), pl.BlockSpec((T, LANES), colc),
                pl.BlockSpec((CONV_W, LANES), wsel(colx)), pl.BlockSpec((CONV_W, LANES), wsel(colb)),
                pl.BlockSpec((CONV_W, LANES), wsel(colc)),
                pl.BlockSpec((1, LANES), wsel(colx)), pl.BlockSpec((1, LANES), wsel(colb)),
                pl.BlockSpec((1, LANES), wsel(colc)),
                pl.BlockSpec((T, LANES), lambda b, p: (blk0 + b, 0)), row128, row128,
                pl.BlockSpec((1, LANES), lambda b, p: (0, p))]
    cb2 = conv_b.reshape(1, -1)
    args = [xbc, xbc, xbc, conv_w, conv_w, conv_w, cb2, cb2, cb2, misc,
            _lane_row(dt_bias.reshape(-1)), _lane_row(a_log.reshape(-1)),
            jnp.repeat(d_skip, SSD_P).reshape(1, SSD_HEADS * SSD_P)]
    state_spec = pl.BlockSpec((None, 2, 2, SSD_P, SSD_N), lambda b, p: (b, 0, p, 0, 0))
    if has_init:
        in_specs.append(state_spec)
        args.append(init)
    out_specs = [pl.BlockSpec((T, LANES), lambda b, p: (b, p))]
    out_shape = [jax.ShapeDtypeStruct((B * T, SSD_HEADS * SSD_P), F32)]
    if prompt:
        out_specs.append(state_spec)
        out_shape.append(jax.ShapeDtypeStruct((B, 2, SSD_HEADS, SSD_P, SSD_N), F32))
    return pl.pallas_call(
        functools.partial(_ssd_kernel, T=T, has_init=has_init, emit_state=prompt),
        grid=(B, n_pairs),
        in_specs=in_specs,
        out_specs=out_specs,
        out_shape=out_shape,
        scratch_shapes=[pltpu.VMEM((T + 2 * CONV_HALO, LANES), F32)]
        + [pltpu.VMEM((T, LANES), F32) for _ in range(4)]
        + [pltpu.VMEM((2, SSD_N, 2 * SSD_P), F32),
           pltpu.VMEM((2, T, LANES), F32),
           pltpu.VMEM((2, T, LANES), F32),
           pltpu.VMEM((2, nc, SSD_N, 2 * SSD_P), F32),
           pltpu.VMEM((2, nc, 1, LANES), F32)],
        compiler_params=_cparams("arbitrary", "arbitrary"),
        name="ssd_prompt" if prompt else "ssd_sample",
    )(*args)


def _tri_inverse(hm, nmat, eye_f):
    levels = int(np.log2(CHUNK))

    def off(level):
        same_big = lax.shift_right_logical(hm["t"], level) == lax.shift_right_logical(hm["s"], level)
        same_small = lax.shift_right_logical(hm["t"], level - 1) == lax.shift_right_logical(hm["s"], level - 1)
        return jnp.where(same_big & jnp.logical_not(same_small), nmat, 0.0)

    state = dict(dinv=eye_f - off(1))

    def first(level):
        def run():
            state["t1"] = _dot(state["dinv"].astype(BF16), _blockdiag(hm, off(level)))
        return run

    def second():
        state["dinv"] = state["dinv"] - _dot(state["t1"].astype(BF16), _blockdiag(hm, state["dinv"]))

    stages = []
    for level in range(2, levels + 1):
        stages += [first(level), second]
    return state, stages


def _gdn_kernel(*refs, T, has_init, emit_state):
    it = iter(refs)
    q_ref, k_ref, v_ref = next(it), next(it), next(it)
    wq_ref, wk_ref, wv_ref = next(it), next(it), next(it)
    bq_ref, bk_ref, bv_ref = next(it), next(it), next(it)
    misc_ref, dtb_ref, alog_ref, z_ref, gn_ref = (next(it) for _ in range(5))
    s0_ref = next(it) if has_init else None
    o_ref = next(it)
    sout_ref = next(it) if emit_state else None
    (pad_scr, qc_scr, kc_scr, vc_scr, ob_scr, s_scr,
     u0_scr, wq_scr, kcf_scr, p2_scr, egl_scr) = (next(it) for _ in range(11))

    h = pl.program_id(1)
    nc = T // CHUNK

    _conv_silu(q_ref, qc_scr, pad_scr, wq_ref, bq_ref, T)
    _conv_silu(k_ref, kc_scr, pad_scr, wk_ref, bk_ref, T)
    _conv_silu(v_ref, vc_scr, pad_scr, wv_ref, bv_ref, T)
    q = qc_scr[...]
    qc_scr[...] = q * lax.rsqrt(jnp.sum(q * q, axis=-1, keepdims=True) + EPS) * (GDN_DK ** -0.5)
    k = kc_scr[...]
    kc_scr[...] = k * lax.rsqrt(jnp.sum(k * k, axis=-1, keepdims=True) + EPS)

    for d in range(2):
        if has_init:
            s_scr[d] = s0_ref[d]
        else:
            s_scr[d] = jnp.zeros((GDN_DK, GDN_DV), F32)

    dtb = dtb_ref[...]
    neg_a = -jnp.exp(alog_ref[...])
    hm = _half_masks(False, True)
    eye_f = hm["eye"].astype(F32)

    def pre_gates(c):
        rows = _chunk_rows(c)
        kc = kc_scr[rows, :]
        qc = qc_scr[rows, :]
        mi = misc_ref[rows, :]
        gdec = neg_a * _softplus(mi + dtb)
        g_src = [_colsel(gdec, 2 * SSD_HEADS + d * GDN_HEADS + h) for d in range(2)]
        beta = [jax.nn.sigmoid(_colsel(mi, 2 * SSD_HEADS + 2 * GDN_HEADS + d * GDN_HEADS + h)) for d in range(2)]
        gc = [None, None]
        gc[0], gc[1], gc_rows = _half_cum(hm, g_src[0], g_src[1])
        diff = _pack(hm, gc[0], gc[1]) - gc_rows
        beta_rows = _rows(hm, _pack(hm, beta[0], beta[1]))
        kb = kc.astype(BF16)
        ek = []
        for d in range(2):
            egc = jnp.exp(gc[d])
            gl = jnp.sum(g_src[d], axis=0, keepdims=True)
            ek.append((egc * kc).astype(BF16))
            wq_scr[d, c, CHUNK:, :] = (egc * qc).astype(BF16)
            kcf_scr[d, rows, :] = (jnp.exp(gl - gc[d]) * beta[d] * kc).astype(BF16)
            egl_scr[d, c] = _lanes(jnp.exp(gl))
        return dict(c=c, rows=rows, kb=kb, qb=qc.astype(BF16), ek=ek,
                    dec_s=jnp.exp(jnp.where(hm["strict"], diff, NEG_INF)) * beta_rows,
                    dec_i=jnp.exp(jnp.where(hm["incl"], diff, NEG_INF)) * beta_rows)

    def pre_block(chunks):
        chains = [pre_gates(c) for c in chunks]
        for ch in chains:
            kb = ch["kb"]
            ch["kq"] = _dot_nt(jnp.concatenate([kb, ch["qb"]], axis=0), jnp.concatenate([kb, kb], axis=0))
        for ch in chains:
            p2_scr[ch["rows"], :] = (ch["dec_i"] * ch["kq"][CHUNK:]).astype(BF16)
            ch["inv"], ch["stages"] = _tri_inverse(hm, ch["dec_s"] * ch["kq"][:CHUNK], eye_f)
        for i in range(len(chains[0]["stages"])):
            for ch in chains:
                ch["stages"][i]()
        for ch in chains:
            vb = vc_scr[ch["rows"], :].astype(BF16)
            zero = jnp.zeros_like(vb)
            rhs = jnp.concatenate([jnp.concatenate([vb, ch["ek"][0], zero, zero], axis=1),
                                   jnp.concatenate([zero, zero, vb, ch["ek"][1]], axis=1)], axis=0)
            uw = _dot(ch["inv"]["dinv"].astype(BF16), rhs)
            for d in range(2):
                u0_scr[d, ch["rows"], :] = uw[:, 2 * d * LANES:(2 * d + 1) * LANES]
                wq_scr[d, ch["c"], :CHUNK, :] = uw[:, (2 * d + 1) * LANES:(2 * d + 2) * LANES].astype(BF16)

    def step_block(chunks):
        states = [s_scr[d] for d in range(2)]
        for c in chunks:
            cs = [c, nc - 1 - c]
            rows = [_chunk_rows(cc) for cc in cs]
            lhs = jnp.concatenate([wq_scr[d, cs[d]] for d in range(2)], axis=0)
            rhs = jnp.concatenate([states[d].astype(BF16) for d in range(2)], axis=1)
            a = _dot(lhs, rhs)
            ub, qs = [], []
            for d in range(2):
                blk = a[2 * d * CHUNK:(2 * d + 2) * CHUNK, d * LANES:(d + 1) * LANES]
                ub.append((u0_scr[d, rows[d], :] - blk[:CHUNK]).astype(BF16))
                qs.append(blk[CHUNK:])
            zero = jnp.zeros_like(ub[0])
            pu = _dot(jnp.concatenate([p2_scr[rows[d], :] for d in range(2)], axis=0),
                      jnp.concatenate([jnp.concatenate([ub[0], zero], axis=1),
                                       jnp.concatenate([zero, ub[1]], axis=1)], axis=0))
            ktu = _dot_tn(jnp.concatenate([kcf_scr[d, rows[d], :] for d in range(2)], axis=1),
                          jnp.concatenate(ub, axis=1))
            for d, dst in enumerate((o_ref, ob_scr)):
                dst[rows[d], :] = qs[d] + pu[d * CHUNK:(d + 1) * CHUNK, d * LANES:(d + 1) * LANES]
                states[d] = (egl_scr[d, cs[d]] * states[d]
                             + ktu[d * GDN_DK:(d + 1) * GDN_DK, d * GDN_DV:(d + 1) * GDN_DV])
        for d in range(2):
            s_scr[d] = states[d]

    _block_loop(nc, pre_block, PRE_CHUNKS)
    _block_loop(nc, step_block, STEP_CHUNKS)

    og = o_ref[...] + ob_scr[...]
    o_ref[...] = _rms(og, gn_ref[...]) * _silu(z_ref[...])
    if emit_state:
        for d in range(2):
            sout_ref[d] = s_scr[d]


def _gdn(qkv, misc, zg, conv_w, conv_b, dt_row, alog_row, gnorm, *, prompt, init=None):
    T, B = (SEQ, BATCH) if prompt else (DEC_SEQ, DEC_BATCH)
    blk0 = 0 if prompt else N_PROMPT // T
    nc = T // CHUNK
    has_init = init is not None
    col = lambda j: (lambda b, h: (blk0 + b, j * GDN_HEADS + h))
    wsel = lambda j: (lambda b, h: (0, j * GDN_HEADS + h))
    row128 = pl.BlockSpec((1, LANES), lambda b, h: (0, 0))
    in_specs = [pl.BlockSpec((T, LANES), col(j)) for j in range(3)]
    in_specs += [pl.BlockSpec((CONV_W, LANES), wsel(j)) for j in range(3)]
    in_specs += [pl.BlockSpec((1, LANES), wsel(j)) for j in range(3)]
    in_specs += [pl.BlockSpec((T, LANES), lambda b, h: (blk0 + b, 0)), row128, row128,
                 pl.BlockSpec((T, LANES), lambda b, h: (blk0 + b, h)), row128]
    cb2 = conv_b.reshape(1, -1)
    args = [qkv, qkv, qkv, conv_w, conv_w, conv_w, cb2, cb2, cb2, misc, dt_row, alog_row, zg,
            gnorm.reshape(1, GDN_DV)]
    state_spec = pl.BlockSpec((None, 2, None, GDN_DK, GDN_DV), lambda b, h: (b, 0, h, 0, 0))
    if has_init:
        in_specs.append(state_spec)
        args.append(init)
    out_specs = [pl.BlockSpec((T, LANES), lambda b, h: (b, h))]
    out_shape = [jax.ShapeDtypeStruct((B * T, GDN_HEADS * GDN_DV), F32)]
    if prompt:
        out_specs.append(state_spec)
        out_shape.append(jax.ShapeDtypeStruct((B, 2, GDN_HEADS, GDN_DK, GDN_DV), F32))
    return pl.pallas_call(
        functools.partial(_gdn_kernel, T=T, has_init=has_init, emit_state=prompt),
        grid=(B, GDN_HEADS),
        in_specs=in_specs,
        out_specs=out_specs,
        out_shape=out_shape,
        scratch_shapes=[pltpu.VMEM((T + 2 * CONV_HALO, LANES), F32)]
        + [pltpu.VMEM((T, LANES), F32) for _ in range(4)]
        + [pltpu.VMEM((2, GDN_DK, GDN_DV), F32),
           pltpu.VMEM((2, T, GDN_DV), F32),
           pltpu.VMEM((2, nc, 2 * CHUNK, GDN_DK), BF16),
           pltpu.VMEM((2, T, GDN_DK), BF16),
           pltpu.VMEM((T, LANES), BF16),
           pltpu.VMEM((2, nc, 1, LANES), F32)],
        compiler_params=_cparams("arbitrary", "arbitrary"),
        name="gdn_prompt" if prompt else "gdn_sample",
    )(*args)


def _router_kernel(x_ref, g_ref, mod_ref, w_ref, b_ref, h_ref, comb_ref):
    mod = mod_ref[...]
    h = _rms(x_ref[...], g_ref[...]) * (1.0 + mod[:, D_MODEL:2 * D_MODEL]) + mod[:, :D_MODEL]
    h_ref[...] = h.astype(BF16)
    logit = jnp.dot(h, w_ref[...], preferred_element_type=F32, precision=lax.Precision.HIGHEST) + b_ref[...]
    lane = lax.broadcasted_iota(jnp.int32, logit.shape, 1)
    is_group = (lane >= MOE_EXPERTS) & (lane < MOE_EXPERTS + MOE_GROUPS)
    glog = jnp.where(is_group, logit, NEG_INF)
    gmax = jnp.max(glog, axis=1, keepdims=True)
    gsel = jnp.min(jnp.where(glog == gmax, lane, LANES), axis=1, keepdims=True) - MOE_EXPERTS
    gw = 1.0 / jnp.sum(jnp.exp(glog - gmax), axis=1, keepdims=True)
    lo = gsel * MOE_PER_GROUP
    in_group = (lane >= lo) & (lane < lo + MOE_PER_GROUP)
    elog = jnp.where(in_group, logit, NEG_INF)
    v1 = jnp.max(elog, axis=1, keepdims=True)
    i1 = jnp.min(jnp.where(elog == v1, lane, LANES), axis=1, keepdims=True)
    elog2 = jnp.where(lane == i1, NEG_INF, elog)
    v2 = jnp.max(elog2, axis=1, keepdims=True)
    i2 = jnp.min(jnp.where(elog2 == v2, lane, LANES), axis=1, keepdims=True)
    e2 = jnp.exp(v2 - v1)
    w1 = gw / (1.0 + e2)
    w2 = gw * e2 / (1.0 + e2)
    comb_ref[...] = jnp.where(lane == i1, w1, 0.0) + jnp.where(lane == i2, w2, 0.0)


def _router(x, gain, mod4, layer, w_route, b_route):
    tm = ROW_TILE
    return pl.pallas_call(
        _router_kernel,
        grid=(N_TOK // tm,),
        in_specs=[pl.BlockSpec((tm, D_MODEL), lambda i: (i, 0)),
                  pl.BlockSpec((1, D_MODEL), lambda i: (0, 0)),
                  pl.BlockSpec((None, None, 1, 3 * D_MODEL), lambda i: (layer, _mod_index(i, tm), 0, 1)),
                  pl.BlockSpec((D_MODEL, LANES), lambda i: (0, 0)),
                  pl.BlockSpec((1, LANES), lambda i: (0, 0))],
        out_specs=[pl.BlockSpec((tm, D_MODEL), lambda i: (i, 0)),
                   pl.BlockSpec((tm, LANES), lambda i: (i, 0))],
        out_shape=[jax.ShapeDtypeStruct((N_TOK, D_MODEL), BF16),
                   jax.ShapeDtypeStruct((N_TOK, LANES), F32)],
        compiler_params=_cparams("arbitrary"),
        name="router",
    )(x, gain.reshape(1, D_MODEL), mod4, w_route, b_route)


MOE_ROWS = 1024


def _moe_kernel(h_ref, comb_ref, wg_ref, wu_ref, wd_ref, x_ref, mod_ref, gf_ref, o_ref, acc_ref, *, final):
    e = pl.program_id(1)

    @pl.when(e == 0)
    def _():
        acc_ref[...] = jnp.zeros_like(acc_ref)

    h = h_ref[...]
    hg = _dot(h, wg_ref[...].astype(BF16))
    hu = _dot(h, wu_ref[...].astype(BF16))
    cw = _colsel(comb_ref[...], e)
    act = _silu(hg) * hu * cw
    acc_ref[...] += _bdot(act, wd_ref[...])

    @pl.when(e == MOE_EXPERTS - 1)
    def _():
        y = x_ref[...] + mod_ref[...] * acc_ref[...]
        if final:
            y = _rms(y, gf_ref[...])
        o_ref[...] = y


def _moe(h, comb, wg, wu, wd, x, mod4, layer, norm_final, final):
    tm = MOE_ROWS
    return pl.pallas_call(
        functools.partial(_moe_kernel, final=final),
        grid=(N_TOK // tm, MOE_EXPERTS),
        in_specs=[pl.BlockSpec((tm, D_MODEL), lambda i, e: (i, 0)),
                  pl.BlockSpec((tm, LANES), lambda i, e: (i, 0)),
                  pl.BlockSpec((None, None, D_MODEL, MOE_FF), lambda i, e: (layer, e, 0, 0)),
                  pl.BlockSpec((None, None, D_MODEL, MOE_FF), lambda i, e: (layer, e, 0, 0)),
                  pl.BlockSpec((None, None, MOE_FF, D_MODEL), lambda i, e: (layer, e, 0, 0)),
                  pl.BlockSpec((tm, D_MODEL), lambda i, e: (i, 0)),
                  pl.BlockSpec((None, None, 1, D_MODEL), lambda i, e: (layer, _mod_index(i, tm), 0, 5)),
                  pl.BlockSpec((1, D_MODEL), lambda i, e: (0, 0))],
        out_specs=pl.BlockSpec((tm, D_MODEL), lambda i, e: (i, 0)),
        out_shape=jax.ShapeDtypeStruct((N_TOK, D_MODEL), F32),
        scratch_shapes=[pltpu.VMEM((tm, D_MODEL), F32)],
        compiler_params=_cparams("arbitrary", "arbitrary"),
        name="moe",
    )(h, comb, wg, wu, wd, x, mod4, norm_final.reshape(1, D_MODEL))


def _rope_tables():
    t = jnp.arange(DEC_SEQ)
    pos = jnp.stack([(t // GRID_W).astype(F32), (t % GRID_W).astype(F32)], axis=1)
    nf = MLA_ROPE // 4
    inv = ROPE_BASE ** (-jnp.arange(nf, dtype=F32) / nf)
    j = jnp.arange(MLA_ROPE)
    ang = pos[:, j // (2 * nf)] * inv[j % nf][None, :]
    sign = jnp.where((j % (2 * nf)) < nf, -1.0, 1.0)
    cos = jnp.pad(jnp.cos(ang), ((0, 0), (MLA_NOPE, LANES - MLA_NOPE - MLA_ROPE)), constant_values=1.0)
    sin = jnp.pad(jnp.sin(ang) * sign, ((0, 0), (MLA_NOPE, LANES - MLA_NOPE - MLA_ROPE)))
    cos = jnp.concatenate([jnp.ones((ROW_TILE, LANES), F32), cos], axis=0)
    sin = jnp.concatenate([jnp.zeros((ROW_TILE, LANES), F32), sin], axis=0)
    return cos, sin


def _pad_heads(w, n_heads, width, lo, hi):
    k = w.shape[0]
    w = w.reshape(k, n_heads, width)[:, :, lo:hi]
    w = jnp.pad(w, ((0, 0), (0, 0), (0, LANES - (hi - lo))))
    return w.reshape(k, n_heads * LANES)


def kernel(x_prompt, x_sample, c, cache_mla_kv, cache_mla_krope, state_mlstm_C, state_mlstm_n, state_mlstm_m, state_ssd, state_gdn, c_ctx, ada_w, ada_b, norm_mix, norm_ffn, w_in_even, ml_i_bias, ml_f_bias, ml_norm, mla_q_norm, mla_w_uq, mla_kv_norm, mla_w_ukv, w_out_even, w_in_odd, ssd_conv_w, ssd_conv_b, ssd_dt_bias, ssd_A_log, ssd_D, ssd_norm, gdn_conv_w, gdn_conv_b, gdn_dt_bias, gdn_A_log, gdn_norm, w_out_odd, moe_w_group, moe_b_group, moe_w_expert, moe_b_expert, moe_w_gate, moe_w_up, moe_w_down, norm_final):
    x = jnp.concatenate([x_prompt.reshape(N_PROMPT, D_MODEL), x_sample.reshape(N_SAMPLE, D_MODEL)], axis=0)
    cond = jnp.concatenate([c_ctx[None, :], c, jnp.zeros((N_COND - 1 - DEC_BATCH, D_MODEL), F32)], axis=0)
    mod4 = _ada(cond, ada_w, ada_b).reshape(DEPTH, N_COND, 1, 6 * D_MODEL)
    cos_tab, sin_tab = _rope_tables()

    def moe_layer(x, layer, final):
        w_route = jnp.concatenate([moe_w_expert[layer], moe_w_group[layer]], axis=1)
        w_route = jnp.pad(w_route, ((0, 0), (0, LANES - MOE_EXPERTS - MOE_GROUPS)))
        b_route = _lane_row(jnp.concatenate([moe_b_expert[layer], moe_b_group[layer]]))
        h, comb = _router(x, norm_ffn[layer], mod4, layer, w_route, b_route)
        return _moe(h, comb, moe_w_gate, moe_w_up, moe_w_down, x, mod4, layer, norm_final, final)

    e = 0
    w = w_in_even[e]
    off = np.cumsum([0, 4 * ML_HEADS * ML_DK, 2 * ML_HEADS, 2 * ML_HEADS, MLA_Q_RANK, MLA_KV_RANK, MLA_ROPE])
    w_misc = jnp.concatenate([w[:, off[1]:off[3]],
                              jnp.zeros((D_MODEL, MLA_NOPE - 4 * ML_HEADS), F32),
                              w[:, off[5]:off[6]],
                              jnp.zeros((D_MODEL, LANES - MLA_NOPE - MLA_ROPE), F32)], axis=1)
    weights = [w[:, :off[1]].astype(BF16), w[:, off[3]:off[4]].astype(BF16),
               w[:, off[4]:off[5]].astype(BF16), w_misc.astype(BF16)]
    qkvo, cq, ckv, misc0 = _inproj(x, norm_mix[0], mod4, 0, weights)

    hm_p, st_c, st_n, st_m = _mlstm(qkvo, misc0, ml_i_bias[e], ml_f_bias[e], ml_norm[e], prompt=True)
    (hm_s,) = _mlstm(qkvo, misc0, ml_i_bias[e], ml_f_bias[e], ml_norm[e], prompt=False,
                     init=(state_mlstm_C[:, e], state_mlstm_n[:, e], state_mlstm_m[:, e]))
    hm = jnp.concatenate([hm_p, hm_s], axis=0)

    dq = MLA_NOPE + MLA_ROPE
    w_uq_pad = _pad_heads(mla_w_uq[e], MLA_HEADS, dq, 0, dq).astype(BF16)
    w_uk_pad = _pad_heads(mla_w_ukv[e], MLA_HEADS, MLA_NOPE + MLA_V, 0, MLA_NOPE).astype(BF16)
    w_uv = mla_w_ukv[e].reshape(MLA_KV_RANK, MLA_HEADS, MLA_NOPE + MLA_V)[:, :, MLA_NOPE:]
    w_uv = w_uv.reshape(MLA_KV_RANK, MLA_HEADS * MLA_V).astype(BF16)
    q_cat = _mla_q(cq, mla_q_norm[e], w_uq_pad, cos_tab, sin_tab)
    ckv_n, k_cat, v_all = _mla_kv(ckv, misc0, w_uk_pad, w_uv, mla_kv_norm[e], cos_tab, sin_tab)
    cache_kpe = jnp.pad(cache_mla_krope[:, e].reshape(DEC_BATCH * PAST_LEN, MLA_ROPE),
                        ((0, 0), (MLA_NOPE, LANES - MLA_NOPE - MLA_ROPE)))
    k_cache, v_cache = _mla_kv(cache_mla_kv[:, e].reshape(DEC_BATCH * PAST_LEN, MLA_KV_RANK), cache_kpe,
                               w_uk_pad, w_uv)
    att_p = _attention(q_cat, k_cat, v_all, B=BATCH, Tq=SEQ, Tk=SEQ, tq=SEQ, q_row0=0)

    def with_cache(cache, new):
        width = new.shape[1]
        both = jnp.concatenate([cache.reshape(DEC_BATCH, PAST_LEN, width),
                                new[N_PROMPT:].reshape(DEC_BATCH, DEC_SEQ, width)], axis=1)
        return both.reshape(DEC_BATCH * (PAST_LEN + DEC_SEQ), width)

    att_s = _attention(q_cat, with_cache(k_cache, k_cat), with_cache(v_cache, v_all), B=DEC_BATCH, Tq=DEC_SEQ,
                       Tk=PAST_LEN + DEC_SEQ, tq=256, q_row0=N_PROMPT)
    att = jnp.concatenate([att_p, att_s], axis=0)
    wo = w_out_even[e].astype(BF16)
    x = _outproj(hm, att, wo[:ML_HEADS * ML_DV], wo[ML_HEADS * ML_DV:], x, mod4, 0)
    x = moe_layer(x, 0, final=False)

    oi = 0
    w = w_in_odd[oi]
    ssd_w = SSD_HEADS * SSD_P
    ssd_cc = ssd_w + 2 * SSD_GROUPS * SSD_N
    gdn_w = GDN_HEADS * GDN_DK
    off = np.cumsum([0, ssd_w, ssd_cc, 2 * SSD_HEADS, 3 * gdn_w, gdn_w, 2 * GDN_HEADS, 2 * GDN_HEADS])
    w_misc = jnp.concatenate([w[:, off[2]:off[3]], w[:, off[5]:off[7]],
                              jnp.zeros((D_MODEL, LANES - 2 * SSD_HEADS - 4 * GDN_HEADS), F32)], axis=1)
    weights = [w[:, off[0]:off[1]].astype(BF16), w[:, off[1]:off[2]].astype(BF16),
               w[:, off[3]:off[4]].astype(BF16), w[:, off[4]:off[5]].astype(BF16), w_misc.astype(BF16)]
    z_s, xbc, qkv_g, z_g, misc = _inproj(x, norm_mix[1], mod4, 1, weights)

    ssd_args = (xbc, misc, ssd_conv_w[oi], ssd_conv_b[oi], ssd_dt_bias[oi], ssd_A_log[oi], ssd_D[oi])
    ys_p, st_ssd = _ssd(*ssd_args, prompt=True)
    (ys_s,) = _ssd(*ssd_args, prompt=False, init=state_ssd[:, oi])
    ys = jnp.concatenate([ys_p, ys_s], axis=0)

    lo = 2 * SSD_HEADS
    gdn_dt_row = jnp.pad(gdn_dt_bias[oi].reshape(-1), (lo, LANES - lo - 2 * GDN_HEADS)).reshape(1, LANES)
    gdn_alog_row = jnp.pad(gdn_A_log[oi].reshape(-1), (lo, LANES - lo - 2 * GDN_HEADS)).reshape(1, LANES)
    gdn_args = (qkv_g, misc, z_g, gdn_conv_w[oi], gdn_conv_b[oi], gdn_dt_row, gdn_alog_row, gdn_norm[oi])
    og_p, st_gdn = _gdn(*gdn_args, prompt=True)
    (og_s,) = _gdn(*gdn_args, prompt=False, init=state_gdn[:, oi])
    og = jnp.concatenate([og_p, og_s], axis=0)

    wo = w_out_odd[oi].astype(BF16)
    x = _outproj(ys, og, wo[:ssd_w], wo[ssd_w:], x, mod4, 1, z=z_s, gnorm=ssd_norm[oi])
    x = moe_layer(x, 1, final=True)

    y_prompt = x[:N_PROMPT].reshape(BATCH, SEQ, D_MODEL)
    y_sample = x[N_PROMPT:].reshape(DEC_BATCH, DEC_SEQ, D_MODEL)
    new_mla_kv = ckv_n[:N_PROMPT].reshape(BATCH, 1, SEQ, MLA_KV_RANK)
    new_mla_krope = misc0[:N_PROMPT, MLA_NOPE:MLA_NOPE + MLA_ROPE].reshape(BATCH, 1, SEQ, MLA_ROPE)
    return (y_prompt, y_sample, new_mla_kv, new_mla_krope, st_c[:, None], st_n.reshape(BATCH, 1, 2, ML_HEADS, ML_DK),
            st_m.reshape(BATCH, 1, 2, ML_HEADS), st_ssd[:, None], st_gdn[:, None])
```

```python
import functools

import numpy as np
import jax
import jax.numpy as jnp
from jax import lax
from jax.experimental import pallas as pl
from jax.experimental.pallas import tpu as pltpu

F32 = jnp.float32
BF16 = jnp.bfloat16

D_MODEL = 1024
BATCH = 32
SEQ = 256
DEPTH = 2
DEC_BATCH = 2
DEC_SEQ = 2048
PAST_LEN = 256
GRID_W = 64
EPS = 1e-6
ML_HEADS = 4
ML_DK = 128
ML_DV = 128
MLA_HEADS = 8
MLA_Q_RANK = 384
MLA_KV_RANK = 256
MLA_NOPE = 64
MLA_ROPE = 32
MLA_V = 64
ROPE_BASE = 10000.0
SSD_HEADS = 8
SSD_P = 64
SSD_GROUPS = 2
SSD_N = 128
GDN_HEADS = 4
GDN_DK = 128
GDN_DV = 128
CONV_W = 5
MOE_GROUPS = 4
MOE_PER_GROUP = 8
MOE_EXPERTS = 32
MOE_FF = 256

N_PROMPT = BATCH * SEQ
N_SAMPLE = DEC_BATCH * DEC_SEQ
N_TOK = N_PROMPT + N_SAMPLE
N_COND = 8

LANES = 128
CHUNK = 64
ROW_TILE = 512
CONV_HALO = 8
CONV_ROWS = 256
VMEM_LIMIT = 56 * 1024 * 1024
PRE_CHUNKS = 8
STEP_CHUNKS = 4
GDN_PROMPT_SEQS_PER_STEP = 4

assert LANES == 2 * CHUNK and SSD_P == CHUNK

NEG_INF = float("-inf")


def _cparams(*sem):
    return pltpu.CompilerParams(dimension_semantics=sem, vmem_limit_bytes=VMEM_LIMIT)


def _dot(a, b):
    return jnp.dot(a, b, preferred_element_type=F32)


def _dot_nt(a, b):
    return lax.dot_general(a, b, (((1,), (1,)), ((), ())), preferred_element_type=F32)


def _dot_tn(a, b):
    return lax.dot_general(a, b, (((0,), (0,)), ((), ())), preferred_element_type=F32)


def _bdot(a, b):
    return _dot(a.astype(BF16), b.astype(BF16))


def _rms(x, g):
    return x * lax.rsqrt(jnp.mean(x * x, axis=-1, keepdims=True) + EPS) * g


def _softplus(x):
    return jnp.maximum(x, 0.0) + jnp.log1p(jnp.exp(-jnp.abs(x)))


def _silu(x):
    return x * jax.nn.sigmoid(x)


def _colsel(x, j):
    lane = lax.broadcasted_iota(jnp.int32, x.shape, 1)
    return jnp.sum(jnp.where(lane == j, x, 0.0), axis=1, keepdims=True)


def _lanes(x):
    return jnp.broadcast_to(x, (x.shape[0], LANES))


def _half_masks(rev_lo, rev_hi):
    t = lax.broadcasted_iota(jnp.int32, (CHUNK, LANES), 0)
    lane = lax.broadcasted_iota(jnp.int32, (CHUNK, LANES), 1)
    s = lane & (CHUNK - 1)
    hi = lane >= CHUNK

    def pick(fwd, bwd):
        if rev_lo == rev_hi:
            return bwd if rev_lo else fwd
        on_hi, on_lo = (bwd, fwd) if rev_hi else (fwd, bwd)
        return (hi & on_hi) | (jnp.logical_not(hi) & on_lo)

    return dict(hi=hi, t=t, s=s, eye=(s == t), incl=pick(s <= t, s >= t), incl_t=pick(t <= s, t >= s),
                strict=pick(s < t, s > t))


def _pack(hm, col_lo, col_hi):
    return jnp.where(hm["hi"], col_hi, col_lo)


def _rows(hm, cols):
    return jnp.sum(jnp.where(hm["eye"], cols, 0.0), axis=0, keepdims=True)


def _half_sums(hm, x):
    lo = jnp.sum(jnp.where(hm["hi"], 0.0, x), axis=1, keepdims=True)
    hi = jnp.sum(jnp.where(hm["hi"], x, 0.0), axis=1, keepdims=True)
    return lo, hi


def _half_cum(hm, col_lo, col_hi):
    cols = _pack(hm, col_lo, col_hi)
    cum_lo, cum_hi = _half_sums(hm, jnp.where(hm["incl"], _rows(hm, cols), 0.0))
    cum_rows = jnp.sum(jnp.where(hm["incl_t"], cols, 0.0), axis=0, keepdims=True)
    return cum_lo, cum_hi, cum_rows


def _blockdiag(hm, x):
    return jnp.concatenate([jnp.where(hm["hi"], 0.0, x).astype(BF16),
                            jnp.where(hm["hi"], x, 0.0).astype(BF16)], axis=0)


def _chunk_rows(c):
    if isinstance(c, int):
        return pl.ds(c * CHUNK, CHUNK)
    return pl.ds(pl.multiple_of(c * CHUNK, CHUNK), CHUNK)


def _block_loop(n, body, size):
    if n <= size:
        body(list(range(n)))
        return

    def block(blk, carry):
        body([blk * size + j for j in range(size)])
        return carry

    lax.fori_loop(0, n // size, block, 0)


def _mod_index(i, rows_per_tile):
    p_tiles = N_PROMPT // rows_per_tile
    s_tiles = DEC_SEQ // rows_per_tile
    return jnp.where(i < p_tiles, 0, 1 + (i - p_tiles) // s_tiles)


def _ada_kernel(c_ref, w_ref, b_ref, o_ref):
    c = c_ref[...]
    o_ref[...] = _bdot(_silu(c), w_ref[...]) + b_ref[...]


def _ada(cond, ada_w, ada_b):
    nb = 6
    return pl.pallas_call(
        _ada_kernel,
        grid=(DEPTH, nb),
        in_specs=[pl.BlockSpec((N_COND, D_MODEL), lambda l, j: (0, 0)),
                  pl.BlockSpec((None, D_MODEL, D_MODEL), lambda l, j: (l, 0, j)),
                  pl.BlockSpec((None, 1, D_MODEL), lambda l, j: (l, 0, j))],
        out_specs=pl.BlockSpec((None, N_COND, D_MODEL), lambda l, j: (l, 0, j)),
        out_shape=jax.ShapeDtypeStruct((DEPTH, N_COND, 6 * D_MODEL), F32),
        compiler_params=_cparams("arbitrary", "arbitrary"),
        name="ada",
    )(cond, ada_w, ada_b.reshape(DEPTH, 1, 6 * D_MODEL))


def _token_operand(x, tm):
    if not isinstance(x, tuple):
        return [pl.BlockSpec((tm, x.shape[1]), lambda i: (i, 0))], [x]
    pt = N_PROMPT // tm
    width = x[0].shape[1]
    return ([pl.BlockSpec((tm, width), lambda i: (jnp.minimum(i, pt - 1), 0)),
             pl.BlockSpec((tm, width), lambda i: (jnp.maximum(i - pt, 0), 0))], list(x))


def _take_tile(refs, split, tm):
    if not split:
        return refs.pop(0)[...]
    p_ref, s_ref = refs.pop(0), refs.pop(0)
    return jnp.where(pl.program_id(0) < N_PROMPT // tm, p_ref[...], s_ref[...])


def _inproj_kernel(*refs, n_out, split, tm):
    refs = list(refs)
    x = _take_tile(refs, split, tm)
    g_ref, mod_ref = refs[:2]
    w_refs, o_refs = refs[2:2 + n_out], refs[2 + n_out:]
    mod = mod_ref[...]
    h = _rms(x, g_ref[...]) * (1.0 + mod[:, D_MODEL:2 * D_MODEL]) + mod[:, :D_MODEL]
    hb = h.astype(BF16)
    for w_ref, o_ref in zip(w_refs, o_refs):
        o_ref[...] = _dot(hb, w_ref[...])


def _inproj(x, gain, mod4, layer, weights):
    n_out = len(weights)
    tm = ROW_TILE
    in_specs, args = _token_operand(x, tm)
    in_specs += [pl.BlockSpec((1, D_MODEL), lambda i: (0, 0)),
                 pl.BlockSpec((None, None, 1, 2 * D_MODEL), lambda i: (layer, _mod_index(i, tm), 0, 0))]
    in_specs += [pl.BlockSpec(w.shape, lambda i: (0, 0)) for w in weights]
    return pl.pallas_call(
        functools.partial(_inproj_kernel, n_out=n_out, split=isinstance(x, tuple), tm=tm),
        grid=(N_TOK // tm,),
        in_specs=in_specs,
        out_specs=[pl.BlockSpec((tm, w.shape[1]), lambda i: (i, 0)) for w in weights],
        out_shape=[jax.ShapeDtypeStruct((N_TOK, w.shape[1]), F32) for w in weights],
        compiler_params=_cparams("arbitrary"),
        name="inproj",
    )(*args, gain.reshape(1, D_MODEL), mod4, *weights)


def _mlstm_kernel(*refs, T, NS, has_init, emit_state):
    it = iter(refs)
    ib_ref, fb_ref = next(it), next(it)
    m0_ref = next(it) if has_init else None
    q_ref, k_ref, v_ref, o_ref, misc_ref, gn_ref = (next(it) for _ in range(6))
    c0_ref, n0_ref = (next(it), next(it)) if has_init else (None, None)
    hm_ref = next(it)
    cout_ref, nout_ref, mout_ref = (next(it), next(it), next(it)) if emit_state else (None, None, None)
    (hb_scr, c_scr, n_scr, m_scr, num_scr, st_scr, kv_scr, nl_scr, bl_scr, gm_scr) = (next(it) for _ in range(10))

    b = pl.program_id(0)
    h = pl.program_id(1)
    nc = T // CHUNK
    scale = ML_DK ** -0.5

    for s in range(NS):
        for d in range(2):
            if has_init:
                c_scr[2 * s + d] = c0_ref[s, d]
                n_scr[2 * s + d] = n0_ref[s, d]
                m_scr[2 * s + d] = jnp.full((1, LANES), m0_ref[b * NS + s, d, h], F32)
            else:
                c_scr[2 * s + d] = jnp.zeros((ML_DK, ML_DV), F32)
                n_scr[2 * s + d] = jnp.zeros((1, ML_DK), F32)
                m_scr[2 * s + d] = jnp.zeros((1, LANES), F32)

    hm = _half_masks(False, True)

    def pre_gates(c):
        rows = _chunk_rows(c)
        mi = misc_ref[rows, :]
        li = [_colsel(mi, d * ML_HEADS + h) + ib_ref[d, h] for d in range(2)]
        lf = [-_softplus(-(_colsel(mi, 2 * ML_HEADS + d * ML_HEADS + h) + fb_ref[d, h])) for d in range(2)]
        b_cols = [None, None]
        b_cols[0], b_cols[1], b_rows = _half_cum(hm, lf[0], lf[1])
        li_rows = _rows(hm, _pack(hm, li[0], li[1]))
        dm = jnp.where(hm["incl"], _pack(hm, b_cols[0], b_cols[1]) - b_rows + li_rows, NEG_INF)
        mloc = [jnp.max(jnp.where(hm["hi"], NEG_INF, dm), axis=1, keepdims=True),
                jnp.max(jnp.where(hm["hi"], dm, NEG_INF), axis=1, keepdims=True)]
        kc = k_ref[rows, :]
        kws = []
        for d in range(2):
            bl = jnp.sum(lf[d], axis=0, keepdims=True)
            g = bl - b_cols[d] + li[d]
            gmax = jnp.max(g, axis=0, keepdims=True)
            kw = jnp.exp(g - gmax) * kc
            kws.append(kw.astype(BF16))
            nl_scr[d, c] = jnp.sum(kw, axis=0, keepdims=True)
            bl_scr[d, c] = _lanes(bl)
            gm_scr[d, c] = _lanes(gmax)
            st_scr[d, 0, rows, :] = _lanes(mloc[d])
            st_scr[d, 1, rows, :] = _lanes(b_cols[d])
        return dict(c=c, rows=rows, kc=kc.astype(BF16), kw2=jnp.concatenate(kws, axis=1),
                    e2=jnp.exp(dm - _pack(hm, mloc[0], mloc[1])))

    def pre_block(chunks):
        chains = [pre_gates(c) for c in chunks]
        for ch in chains:
            qc = (q_ref[ch["rows"], :] * scale).astype(BF16)
            ch["vb"] = v_ref[ch["rows"], :].astype(BF16)
            ch["qk2"] = _dot_nt(qc, jnp.concatenate([ch["kc"], ch["kc"]], axis=0))
            kv2 = _dot_tn(ch["kw2"], ch["vb"])
            kv_scr[0, ch["c"]] = kv2[:ML_DK]
            kv_scr[1, ch["c"]] = kv2[ML_DK:]
        for ch in chains:
            s2 = ch["qk2"] * ch["e2"]
            ch["s2"] = s2.astype(BF16)
            dens = _half_sums(hm, s2)
            for d in range(2):
                st_scr[d, 2, ch["rows"], :] = _lanes(dens[d])
        for ch in chains:
            vb = ch["vb"]
            zero = jnp.zeros_like(vb)
            vbd = jnp.concatenate([jnp.concatenate([vb, zero], axis=1),
                                   jnp.concatenate([zero, vb], axis=1)], axis=0)
            num2 = _dot(ch["s2"], vbd)
            num_scr[0, ch["rows"], :] = num2[:, :ML_DV]
            num_scr[1, ch["rows"], :] = num2[:, ML_DV:]

    def step_block(chunks):
        states = [(m_scr[j], c_scr[j], n_scr[j]) for j in range(2 * NS)]
        pairs = []
        for c in chunks:
            for s in range(NS):
                pair = []
                for d in range(2):
                    cc = s * nc + (c if d == 0 else nc - 1 - c)
                    rows = _chunk_rows(cc)
                    m, c_st, n_st = states[2 * s + d]
                    mloc, b_col, den_loc = st_scr[d, 0, rows, :], st_scr[d, 1, rows, :], st_scr[d, 2, rows, :]
                    bl, gmax = bl_scr[d, cc], gm_scr[d, cc]
                    qc = q_ref[rows, :] * scale
                    inter = b_col + m
                    mq = jnp.maximum(inter, mloc)
                    a = jnp.exp(inter - mq)
                    f = jnp.exp(mloc - mq)
                    den = f * den_loc + a * jnp.sum(qc * n_st, axis=1, keepdims=True)
                    pair.append(dict(rows=rows, qc=qc.astype(BF16), c_st=c_st.astype(BF16), a=a, f=f,
                                     inv=1.0 / jnp.maximum(jnp.abs(den), jnp.exp(-mq))))
                    m_new = jnp.maximum(bl + m, gmax)
                    dec = jnp.exp(bl + m - m_new)
                    fk = jnp.exp(gmax - m_new)
                    states[2 * s + d] = (m_new, dec * c_st + fk * kv_scr[d, cc], dec * n_st + fk * nl_scr[d, cc])
                pairs.append(pair)
        for j in range(2 * NS):
            m_scr[j], c_scr[j], n_scr[j] = states[j]
        for pair in pairs:
            lhs = jnp.concatenate([pair[0]["qc"], pair[1]["qc"]], axis=0)
            rhs = jnp.concatenate([pair[0]["c_st"], pair[1]["c_st"]], axis=1)
            pair.append(_dot(lhs, rhs))
        for pair in pairs:
            res = pair[2]
            for d, dst in enumerate((hm_ref, hb_scr)):
                it_ = pair[d]
                qc_c = res[d * CHUNK:(d + 1) * CHUNK, d * ML_DV:(d + 1) * ML_DV]
                dst[it_["rows"], :] = (it_["f"] * num_scr[d, it_["rows"], :] + it_["a"] * qc_c) * it_["inv"]

    _block_loop(NS * nc, pre_block, PRE_CHUNKS)
    _block_loop(nc, step_block, STEP_CHUNKS)

    hs = hm_ref[...] + hb_scr[...]
    hm_ref[...] = _rms(hs, gn_ref[...]) * jax.nn.sigmoid(o_ref[...])
    if emit_state:
        for s in range(NS):
            for d in range(2):
                cout_ref[s, d] = c_scr[2 * s + d]
                nout_ref[s, d] = n_scr[2 * s + d]
                mout_ref[s, d] = m_scr[2 * s + d][:, 0:1]


def _seq_geometry(prompt, prompt_seqs_per_step=1):
    if prompt:
        return SEQ, BATCH, prompt_seqs_per_step, 0
    return DEC_SEQ, DEC_BATCH, 1, N_PROMPT // DEC_SEQ


def _mlstm(qkvo, misc, i_bias, f_bias, gnorm, *, prompt, init=None):
    T, B, NS, blk0 = _seq_geometry(prompt)
    R = NS * T
    nc = R // CHUNK
    has_init = init is not None
    smem = pl.BlockSpec(memory_space=pltpu.SMEM)

    def col(j):
        return pl.BlockSpec((R, LANES), lambda b, h: (blk0 + b, j * ML_HEADS + h))

    in_specs = [smem, smem]
    args = [i_bias, f_bias]
    if has_init:
        in_specs.append(smem)
        args.append(init[2])
    in_specs += [col(0), col(1), col(2), col(3),
                 pl.BlockSpec((R, LANES), lambda b, h: (blk0 + b, 0)),
                 pl.BlockSpec((None, 1, ML_DV), lambda b, h: (h, 0, 0))]
    args += [qkvo, qkvo, qkvo, qkvo, misc, gnorm.reshape(ML_HEADS, 1, ML_DV)]
    c_spec = pl.BlockSpec((NS, 2, None, ML_DK, ML_DV), lambda b, h: (b, 0, h, 0, 0))
    n_spec = pl.BlockSpec((NS, 2, None, 1, ML_DK), lambda b, h: (b, 0, h, 0, 0))
    if has_init:
        in_specs += [c_spec, n_spec]
        args += [init[0], init[1].reshape(B, 2, ML_HEADS, 1, ML_DK)]
    out_specs = [pl.BlockSpec((R, LANES), lambda b, h: (b, h))]
    out_shape = [jax.ShapeDtypeStruct((B * T, ML_HEADS * ML_DV), F32)]
    if prompt:
        out_specs += [c_spec, n_spec, pl.BlockSpec((NS, 2, None, 1, 1), lambda b, h: (b, 0, h, 0, 0))]
        out_shape += [jax.ShapeDtypeStruct((B, 2, ML_HEADS, ML_DK, ML_DV), F32),
                      jax.ShapeDtypeStruct((B, 2, ML_HEADS, 1, ML_DK), F32),
                      jax.ShapeDtypeStruct((B, 2, ML_HEADS, 1, 1), F32)]
    return pl.pallas_call(
        functools.partial(_mlstm_kernel, T=T, NS=NS, has_init=has_init, emit_state=prompt),
        grid=(B // NS, ML_HEADS),
        in_specs=in_specs,
        out_specs=out_specs,
        out_shape=out_shape,
        scratch_shapes=[pltpu.VMEM((R, ML_DV), F32),
                        pltpu.VMEM((2 * NS, ML_DK, ML_DV), F32),
                        pltpu.VMEM((2 * NS, 1, ML_DK), F32),
                        pltpu.VMEM((2 * NS, 1, LANES), F32),
                        pltpu.VMEM((2, R, ML_DV), F32),
                        pltpu.VMEM((2, 3, R, LANES), F32),
                        pltpu.VMEM((2, nc, ML_DK, ML_DV), F32),
                        pltpu.VMEM((2, nc, 1, ML_DK), F32),
                        pltpu.VMEM((2, nc, 1, LANES), F32),
                        pltpu.VMEM((2, nc, 1, LANES), F32)],
        compiler_params=_cparams("arbitrary", "arbitrary"),
        name="mlstm_prompt" if prompt else "mlstm_sample",
    )(*args)


def _rope(x, cos, sin_signed):
    lane = lax.broadcasted_iota(jnp.int32, x.shape, 1)
    first = (lane & 15) < 8
    partner = jnp.where(first, pltpu.roll(x, LANES - 8, axis=1), pltpu.roll(x, 8, axis=1))
    return x * cos + partner * sin_signed


def _q_kernel(cq_ref, g_ref, w_ref, cos_ref, sin_ref, q_ref):
    cq = _rms(cq_ref[...], g_ref[...])
    y = _bdot(cq, w_ref[...])
    cos, sin = cos_ref[...], sin_ref[...]
    for hd in range(MLA_HEADS):
        sl = slice(hd * LANES, (hd + 1) * LANES)
        q_ref[:, sl] = _rope(y[:, sl], cos, sin).astype(BF16)


def _rope_block_index(i, tm):
    p_tiles = N_PROMPT // tm
    s_tiles = DEC_SEQ // tm
    return jnp.where(i < p_tiles, 0, 1 + (i - p_tiles) % s_tiles)


def _mla_q(cq, gain, w_uq_pad, cos_tab, sin_tab):
    tm = ROW_TILE
    tab = pl.BlockSpec((tm, LANES), lambda i: (_rope_block_index(i, tm), 0))
    return pl.pallas_call(
        _q_kernel,
        grid=(N_TOK // tm,),
        in_specs=[pl.BlockSpec((tm, MLA_Q_RANK), lambda i: (i, 0)),
                  pl.BlockSpec((1, MLA_Q_RANK), lambda i: (0, 0)),
                  pl.BlockSpec(w_uq_pad.shape, lambda i: (0, 0)),
                  tab, tab],
        out_specs=pl.BlockSpec((tm, MLA_HEADS * LANES), lambda i: (i, 0)),
        out_shape=jax.ShapeDtypeStruct((N_TOK, MLA_HEADS * LANES), BF16),
        compiler_params=_cparams("arbitrary"),
        name="mla_q",
    )(cq, gain.reshape(1, MLA_Q_RANK), w_uq_pad, cos_tab, sin_tab)


def _kv_kernel(*refs, norm):
    if norm:
        ckv_ref, kpe_ref, g_ref, wk_ref, wv_ref, cos_ref, sin_ref, ckvn_ref, k_ref, v_ref = refs
        c = _rms(ckv_ref[...], g_ref[...])
        ckvn_ref[...] = c
    else:
        ckv_ref, kpe_ref, wk_ref, wv_ref, k_ref, v_ref = refs
        c = ckv_ref[...]
    kp = kpe_ref[...]
    lane = lax.broadcasted_iota(jnp.int32, kp.shape, 1)
    kp = jnp.where((lane >= MLA_NOPE) & (lane < MLA_NOPE + MLA_ROPE), kp, 0.0)
    if norm:
        kp = _rope(kp, cos_ref[...], sin_ref[...])
    kn = _bdot(c, wk_ref[...])
    for hd in range(MLA_HEADS):
        sl = slice(hd * LANES, (hd + 1) * LANES)
        k_ref[:, sl] = (kn[:, sl] + kp).astype(BF16)
    v_ref[...] = _bdot(c, wv_ref[...]).astype(BF16)


def _mla_kv(ckv, kpe128, w_uk_pad, w_uv, gain=None, cos_tab=None, sin_tab=None):
    norm = gain is not None
    n = ckv.shape[0]
    tm = ROW_TILE
    row = lambda w: pl.BlockSpec((tm, w), lambda i: (i, 0))
    full = lambda a: pl.BlockSpec(a.shape, lambda i: (0, 0))
    in_specs = [row(MLA_KV_RANK), row(LANES)]
    args = [ckv, kpe128]
    if norm:
        in_specs.append(pl.BlockSpec((1, MLA_KV_RANK), lambda i: (0, 0)))
        args.append(gain.reshape(1, MLA_KV_RANK))
    in_specs += [full(w_uk_pad), full(w_uv)]
    args += [w_uk_pad, w_uv]
    out_specs = [row(MLA_HEADS * LANES), row(MLA_HEADS * MLA_V)]
    out_shape = [jax.ShapeDtypeStruct((n, MLA_HEADS * LANES), BF16),
                 jax.ShapeDtypeStruct((n, MLA_HEADS * MLA_V), BF16)]
    if norm:
        tab = pl.BlockSpec((tm, LANES), lambda i: (_rope_block_index(i, tm), 0))
        in_specs += [tab, tab]
        args += [cos_tab, sin_tab]
        out_specs = [row(MLA_KV_RANK)] + out_specs
        out_shape = [jax.ShapeDtypeStruct((n, MLA_KV_RANK), F32)] + out_shape
    return pl.pallas_call(
        functools.partial(_kv_kernel, norm=norm),
        grid=(n // tm,),
        in_specs=in_specs,
        out_specs=out_specs,
        out_shape=out_shape,
        compiler_params=_cparams("arbitrary"),
        name="mla_kv" if norm else "mla_kv_cache",
    )(*args)


def _attn_kernel(q_ref, k_ref, v_ref, o_ref, *, heads):
    scale = (MLA_NOPE + MLA_ROPE) ** -0.5
    scores = [_dot_nt(q_ref[:, j * LANES:(j + 1) * LANES], k_ref[:, j * LANES:(j + 1) * LANES]) * scale
              for j in range(heads)]
    probs, sums = [], []
    for s in scores:
        p = jnp.exp(s - jnp.max(s, axis=1, keepdims=True))
        sums.append(jnp.sum(p, axis=1, keepdims=True))
        probs.append(p.astype(BF16))
    pvs = [_dot(probs[j], v_ref[:, j * MLA_V:(j + 1) * MLA_V]) for j in range(heads)]
    o_ref[...] = jnp.concatenate([pvs[j] / sums[j] for j in range(heads)], axis=1)


def _attention(q, k, v, *, B, Tq, Tk, tq, q_row0, heads):
    nq = Tq // tq
    blk0 = q_row0 // tq
    return pl.pallas_call(
        functools.partial(_attn_kernel, heads=heads),
        grid=(B, MLA_HEADS // heads, nq),
        in_specs=[pl.BlockSpec((tq, heads * LANES), lambda b, hp, i: (blk0 + b * nq + i, hp)),
                  pl.BlockSpec((Tk, heads * LANES), lambda b, hp, i: (b, hp)),
                  pl.BlockSpec((Tk, heads * MLA_V), lambda b, hp, i: (b, hp))],
        out_specs=pl.BlockSpec((tq, heads * MLA_V), lambda b, hp, i: (b * nq + i, hp)),
        out_shape=jax.ShapeDtypeStruct((B * Tq, MLA_HEADS * MLA_V), F32),
        compiler_params=_cparams("arbitrary", "arbitrary", "arbitrary"),
        name="attention",
    )(q, k, v)


def _outproj_kernel(*refs, odd, split_x, tm):
    refs = list(refs)
    a1 = _take_tile(refs, True, tm)
    a2 = _take_tile(refs, True, tm)
    x = _take_tile(refs, split_x, tm)
    if odd:
        z_ref, gn_ref = refs.pop(0), refs.pop(0)
        a1 = _rms(a1 * _silu(z_ref[...]), gn_ref[...])
    w1_ref, w2_ref, g1_ref, o_ref = refs
    out = _bdot(a1, w1_ref[...]) + _bdot(a2, w2_ref[...])
    o_ref[...] = x + g1_ref[...] * out


def _outproj(a1, a2, w1, w2, x, mod4, layer, z=None, gnorm=None):
    odd = z is not None
    tm = ROW_TILE
    half = w1.shape[0]
    full = lambda a: pl.BlockSpec(a.shape, lambda i: (0, 0))
    in_specs, args = [], []
    for operand in (a1, a2, x):
        specs, arrays = _token_operand(operand, tm)
        in_specs += specs
        args += arrays
    if odd:
        in_specs += [pl.BlockSpec((tm, half), lambda i: (i, 0)), pl.BlockSpec((1, half), lambda i: (0, 0))]
        args += [z, gnorm.reshape(1, half)]
    in_specs += [full(w1), full(w2),
                 pl.BlockSpec((None, None, 1, D_MODEL), lambda i: (layer, _mod_index(i, tm), 0, 2))]
    args += [w1, w2, mod4]
    return pl.pallas_call(
        functools.partial(_outproj_kernel, odd=odd, split_x=isinstance(x, tuple), tm=tm),
        grid=(N_TOK // tm,),
        in_specs=in_specs,
        out_specs=pl.BlockSpec((tm, D_MODEL), lambda i: (i, 0)),
        out_shape=jax.ShapeDtypeStruct((N_TOK, D_MODEL), F32),
        compiler_params=_cparams("arbitrary"),
        name="outproj",
    )(*args)


def _conv_silu(src_ref, dst_ref, pad_ref, w_ref, b_ref, T, NS):
    width = src_ref.shape[1]
    zeros = jnp.zeros((CONV_HALO, width), F32)
    pad_ref[pl.ds(0, CONV_HALO), :] = zeros
    pad_ref[pl.ds(CONV_HALO + T, CONV_HALO), :] = zeros
    w = w_ref[...]
    bias = b_ref[...]
    for s in range(NS):
        pad_ref[pl.ds(CONV_HALO, T), :] = src_ref[pl.ds(s * T, T), :]
        for r0 in range(0, T, CONV_ROWS):
            acc = bias
            for j in range(CONV_W):
                start = r0 + CONV_HALO + j - CONV_W // 2
                acc = acc + w[j:j + 1, :] * pad_ref[pl.ds(start, CONV_ROWS), :]
            dst_ref[pl.ds(s * T + r0, CONV_ROWS), :] = _silu(acc)


def _ssd_kernel(*refs, T, NS, has_init, emit_state):
    it = iter(refs)
    x_ref, b_ref, c_ref = next(it), next(it), next(it)
    wx_ref, wb_ref, wc_ref = next(it), next(it), next(it)
    bx_ref, bb_ref, bc_ref = next(it), next(it), next(it)
    misc_ref, dtb_ref, alog_ref, dskip_ref = next(it), next(it), next(it), next(it)
    h0_ref = next(it) if has_init else None
    y_ref = next(it)
    hout_ref = next(it) if emit_state else None
    (pad_scr, xc_scr, bc_scr, cc_scr, yb_scr, hs_scr, yi_scr, ea_scr, upd_scr, eal_scr) = (
        next(it) for _ in range(10))

    p = pl.program_id(1)
    nc = T // CHUNK

    _conv_silu(x_ref, xc_scr, pad_scr, wx_ref, bx_ref, T, NS)
    _conv_silu(b_ref, bc_scr, pad_scr, wb_ref, bb_ref, T, NS)
    _conv_silu(c_ref, cc_scr, pad_scr, wc_ref, bc_ref, T, NS)

    for s in range(NS):
        for d in range(2):
            if has_init:
                hs_scr[2 * s + d] = h0_ref[s, d].reshape(2 * SSD_P, SSD_N).T
            else:
                hs_scr[2 * s + d] = jnp.zeros((SSD_N, 2 * SSD_P), F32)

    dtb = dtb_ref[...]
    neg_a = -jnp.exp(alog_ref[...])
    hms = [_half_masks(False, False), _half_masks(True, True)]
    hi_row = lax.broadcasted_iota(jnp.int32, (1, LANES), 1) >= CHUNK

    def pre_block(chunks):
        chains = []
        for c in chunks:
            rows = _chunk_rows(c)
            bc = bc_scr[rows, :].astype(BF16)
            cb2 = _dot_nt(cc_scr[rows, :].astype(BF16), jnp.concatenate([bc, bc], axis=0))
            chains.append(dict(c=c, rows=rows, bc=bc, cb2=cb2))
        for ch in chains:
            rows = ch["rows"]
            xc = xc_scr[rows, :]
            dts = _softplus(misc_ref[rows, :] + dtb)
            adt = dts * neg_a
            ch["x2m"] = _blockdiag(hms[0], xc)
            ch["g2"], ch["xw"] = [], []
            for d in range(2):
                hm = hms[d]
                dt_h = [_colsel(dts, d * SSD_HEADS + 2 * p + hh) for hh in range(2)]
                a_h = [_colsel(adt, d * SSD_HEADS + 2 * p + hh) for hh in range(2)]
                cum0, cum1, cum_rows = _half_cum(hm, a_h[0], a_h[1])
                seg2 = jnp.exp(jnp.where(hm["incl"], _pack(hm, cum0, cum1) - cum_rows, NEG_INF))
                dt_rows = _rows(hm, _pack(hm, dt_h[0], dt_h[1]))
                ch["g2"].append((ch["cb2"] * seg2 * dt_rows).astype(BF16))
                al = [jnp.sum(a, axis=0, keepdims=True) for a in a_h]
                wgt = [jnp.exp(al[hh] - cum) * dt_h[hh] for hh, cum in enumerate((cum0, cum1))]
                ch["xw"].append((xc * _pack(hm, wgt[0], wgt[1])).astype(BF16))
                ea_scr[d, rows, :] = _pack(hm, jnp.exp(cum0), jnp.exp(cum1))
                eal_scr[d, ch["c"]] = jnp.where(hi_row, jnp.exp(al[1]), jnp.exp(al[0]))
        for ch in chains:
            for d in range(2):
                yi_scr[d, ch["rows"], :] = _dot(ch["g2"][d], ch["x2m"])
                upd_scr[d, ch["c"]] = _dot_tn(ch["bc"], ch["xw"][d])

    def step_block(chunks):
        states = [hs_scr[j] for j in range(2 * NS)]
        pairs = []
        for c in chunks:
            for s in range(NS):
                pair = []
                for d in range(2):
                    cc = s * nc + (c if d == 0 else nc - 1 - c)
                    pair.append(dict(rows=_chunk_rows(cc), hs=states[2 * s + d].astype(BF16)))
                    states[2 * s + d] = eal_scr[d, cc] * states[2 * s + d] + upd_scr[d, cc]
                pairs.append(pair)
        for j in range(2 * NS):
            hs_scr[j] = states[j]
        for pair in pairs:
            lhs = jnp.concatenate([cc_scr[pair[d]["rows"], :].astype(BF16) for d in range(2)], axis=0)
            rhs = jnp.concatenate([pair[d]["hs"] for d in range(2)], axis=1)
            pair.append(_dot(lhs, rhs))
        for pair in pairs:
            for d, dst in enumerate((y_ref, yb_scr)):
                rows = pair[d]["rows"]
                ch = pair[2][d * CHUNK:(d + 1) * CHUNK, d * LANES:(d + 1) * LANES]
                dst[rows, :] = yi_scr[d, rows, :] + ea_scr[d, rows, :] * ch

    _block_loop(NS * nc, pre_block, PRE_CHUNKS)
    _block_loop(nc, step_block, STEP_CHUNKS)

    y_ref[...] = y_ref[...] + yb_scr[...] + dskip_ref[...] * xc_scr[...]
    if emit_state:
        for s in range(NS):
            for d in range(2):
                hout_ref[s, d] = hs_scr[2 * s + d].T.reshape(2, SSD_P, SSD_N)


def _lane_row(v):
    return jnp.pad(v.astype(F32), (0, LANES - v.shape[0])).reshape(1, LANES)


def _ssd(xbc, misc, conv_w, conv_b, dt_bias, a_log, d_skip, *, prompt, init=None):
    T, B, NS, blk0 = _seq_geometry(prompt)
    R = NS * T
    nc = R // CHUNK
    has_init = init is not None
    n_pairs = SSD_HEADS // 2
    pairs_per_group = n_pairs // SSD_GROUPS
    xb = SSD_HEADS * SSD_P // LANES
    cb = xb + SSD_GROUPS * SSD_N // LANES
    colx = lambda b, p: (blk0 + b, p)
    colb = lambda b, p: (blk0 + b, xb + p // pairs_per_group)
    colc = lambda b, p: (blk0 + b, cb + p // pairs_per_group)
    wsel = lambda f: (lambda b, p: (0, f(b, p)[1]))
    row128 = pl.BlockSpec((1, LANES), lambda b, p: (0, 0))
    in_specs = [pl.BlockSpec((R, LANES), colx), pl.BlockSpec((R, LANES), colb), pl.BlockSpec((R, LANES), colc),
                pl.BlockSpec((CONV_W, LANES), wsel(colx)), pl.BlockSpec((CONV_W, LANES), wsel(colb)),
                pl.BlockSpec((CONV_W, LANES), wsel(colc)),
                pl.BlockSpec((1, LANES), wsel(colx)), pl.BlockSpec((1, LANES), wsel(colb)),
                pl.BlockSpec((1, LANES), wsel(colc)),
                pl.BlockSpec((R, LANES), lambda b, p: (blk0 + b, 0)), row128, row128,
                pl.BlockSpec((1, LANES), lambda b, p: (0, p))]
    cb2 = conv_b.reshape(1, -1)
    args = [xbc, xbc, xbc, conv_w, conv_w, conv_w, cb2, cb2, cb2, misc,
            _lane_row(dt_bias.reshape(-1)), _lane_row(a_log.reshape(-1)),
            jnp.repeat(d_skip, SSD_P).reshape(1, SSD_HEADS * SSD_P)]
    state_spec = pl.BlockSpec((NS, 2, 2, SSD_P, SSD_N), lambda b, p: (b, 0, p, 0, 0))
    if has_init:
        in_specs.append(state_spec)
        args.append(init)
    out_specs = [pl.BlockSpec((R, LANES), lambda b, p: (b, p))]
    out_shape = [jax.ShapeDtypeStruct((B * T, SSD_HEADS * SSD_P), F32)]
    if prompt:
        out_specs.append(state_spec)
        out_shape.append(jax.ShapeDtypeStruct((B, 2, SSD_HEADS, SSD_P, SSD_N), F32))
    return pl.pallas_call(
        functools.partial(_ssd_kernel, T=T, NS=NS, has_init=has_init, emit_state=prompt),
        grid=(B // NS, n_pairs),
        in_specs=in_specs,
        out_specs=out_specs,
        out_shape=out_shape,
        scratch_shapes=[pltpu.VMEM((T + 2 * CONV_HALO, LANES), F32)]
        + [pltpu.VMEM((R, LANES), F32) for _ in range(4)]
        + [pltpu.VMEM((2 * NS, SSD_N, 2 * SSD_P), F32),
           pltpu.VMEM((2, R, LANES), F32),
           pltpu.VMEM((2, R, LANES), F32),
           pltpu.VMEM((2, nc, SSD_N, 2 * SSD_P), F32),
           pltpu.VMEM((2, nc, 1, LANES), F32)],
        compiler_params=_cparams("arbitrary", "arbitrary"),
        name="ssd_prompt" if prompt else "ssd_sample",
    )(*args)


def _tri_inverse(hm, nmat, eye_f):
    levels = int(np.log2(CHUNK))

    def off(level):
        same_big = lax.shift_right_logical(hm["t"], level) == lax.shift_right_logical(hm["s"], level)
        same_small = lax.shift_right_logical(hm["t"], level - 1) == lax.shift_right_logical(hm["s"], level - 1)
        return jnp.where(same_big & jnp.logical_not(same_small), nmat, 0.0)

    state = dict(dinv=eye_f - off(1))

    def first(level):
        def run():
            state["t1"] = _dot(state["dinv"].astype(BF16), _blockdiag(hm, off(level)))
        return run

    def second():
        state["dinv"] = state["dinv"] - _dot(state["t1"].astype(BF16), _blockdiag(hm, state["dinv"]))

    stages = []
    for level in range(2, levels + 1):
        stages += [first(level), second]
    return state, stages


def _gdn_kernel(*refs, T, NS, has_init, emit_state):
    it = iter(refs)
    q_ref, k_ref, v_ref = next(it), next(it), next(it)
    wq_ref, wk_ref, wv_ref = next(it), next(it), next(it)
    bq_ref, bk_ref, bv_ref = next(it), next(it), next(it)
    misc_ref, dtb_ref, alog_ref, z_ref, gn_ref = (next(it) for _ in range(5))
    s0_ref = next(it) if has_init else None
    o_ref = next(it)
    sout_ref = next(it) if emit_state else None
    (pad_scr, qc_scr, kc_scr, vc_scr, ob_scr, s_scr,
     u0_scr, wq_scr, kcf_scr, p2_scr, egl_scr) = (next(it) for _ in range(11))

    h = pl.program_id(1)
    nc = T // CHUNK

    _conv_silu(q_ref, qc_scr, pad_scr, wq_ref, bq_ref, T, NS)
    _conv_silu(k_ref, kc_scr, pad_scr, wk_ref, bk_ref, T, NS)
    _conv_silu(v_ref, vc_scr, pad_scr, wv_ref, bv_ref, T, NS)
    q = qc_scr[...]
    qc_scr[...] = q * lax.rsqrt(jnp.sum(q * q, axis=-1, keepdims=True) + EPS) * (GDN_DK ** -0.5)
    k = kc_scr[...]
    kc_scr[...] = k * lax.rsqrt(jnp.sum(k * k, axis=-1, keepdims=True) + EPS)

    for s in range(NS):
        for d in range(2):
            if has_init:
                s_scr[2 * s + d] = s0_ref[s, d]
            else:
                s_scr[2 * s + d] = jnp.zeros((GDN_DK, GDN_DV), F32)

    dtb = dtb_ref[...]
    neg_a = -jnp.exp(alog_ref[...])
    hm = _half_masks(False, True)
    eye_f = hm["eye"].astype(F32)

    def pre_gates(c):
        rows = _chunk_rows(c)
        kc = kc_scr[rows, :]
        qc = qc_scr[rows, :]
        mi = misc_ref[rows, :]
        gdec = neg_a * _softplus(mi + dtb)
        g_src = [_colsel(gdec, 2 * SSD_HEADS + d * GDN_HEADS + h) for d in range(2)]
        beta = [jax.nn.sigmoid(_colsel(mi, 2 * SSD_HEADS + 2 * GDN_HEADS + d * GDN_HEADS + h)) for d in range(2)]
        gc = [None, None]
        gc[0], gc[1], gc_rows = _half_cum(hm, g_src[0], g_src[1])
        diff = _pack(hm, gc[0], gc[1]) - gc_rows
        beta_rows = _rows(hm, _pack(hm, beta[0], beta[1]))
        kb = kc.astype(BF16)
        ek = []
        for d in range(2):
            egc = jnp.exp(gc[d])
            gl = jnp.sum(g_src[d], axis=0, keepdims=True)
            ek.append((egc * kc).astype(BF16))
            wq_scr[d, c, CHUNK:, :] = (egc * qc).astype(BF16)
            kcf_scr[d, rows, :] = (jnp.exp(gl - gc[d]) * beta[d] * kc).astype(BF16)
            egl_scr[d, c] = _lanes(jnp.exp(gl))
        return dict(c=c, rows=rows, kb=kb, qb=qc.astype(BF16), ek=ek,
                    dec_s=jnp.exp(jnp.where(hm["strict"], diff, NEG_INF)) * beta_rows,
                    dec_i=jnp.exp(jnp.where(hm["incl"], diff, NEG_INF)) * beta_rows)

    def pre_block(chunks):
        chains = [pre_gates(c) for c in chunks]
        for ch in chains:
            kb = ch["kb"]
            ch["kq"] = _dot_nt(jnp.concatenate([kb, ch["qb"]], axis=0), jnp.concatenate([kb, kb], axis=0))
        for ch in chains:
            p2_scr[ch["rows"], :] = (ch["dec_i"] * ch["kq"][CHUNK:]).astype(BF16)
            ch["inv"], ch["stages"] = _tri_inverse(hm, ch["dec_s"] * ch["kq"][:CHUNK], eye_f)
        for i in range(len(chains[0]["stages"])):
            for ch in chains:
                ch["stages"][i]()
        for ch in chains:
            vb = vc_scr[ch["rows"], :].astype(BF16)
            zero = jnp.zeros_like(vb)
            rhs = jnp.concatenate([jnp.concatenate([vb, ch["ek"][0], zero, zero], axis=1),
                                   jnp.concatenate([zero, zero, vb, ch["ek"][1]], axis=1)], axis=0)
            uw = _dot(ch["inv"]["dinv"].astype(BF16), rhs)
            for d in range(2):
                u0_scr[d, ch["rows"], :] = uw[:, 2 * d * LANES:(2 * d + 1) * LANES]
                wq_scr[d, ch["c"], :CHUNK, :] = uw[:, (2 * d + 1) * LANES:(2 * d + 2) * LANES].astype(BF16)

    def step_block(chunks):
        states = [s_scr[j] for j in range(2 * NS)]
        for c in chunks:
            work = []
            for s in range(NS):
                cs = [s * nc + c, s * nc + nc - 1 - c]
                lhs = jnp.concatenate([wq_scr[d, cs[d]] for d in range(2)], axis=0)
                rhs = jnp.concatenate([states[2 * s + d].astype(BF16) for d in range(2)], axis=1)
                work.append(dict(s=s, cs=cs, rows=[_chunk_rows(cc) for cc in cs], a=_dot(lhs, rhs)))
            for w in work:
                rows, ub, w["qs"] = w["rows"], [], []
                for d in range(2):
                    blk = w["a"][2 * d * CHUNK:(2 * d + 2) * CHUNK, d * LANES:(d + 1) * LANES]
                    ub.append((u0_scr[d, rows[d], :] - blk[:CHUNK]).astype(BF16))
                    w["qs"].append(blk[CHUNK:])
                zero = jnp.zeros_like(ub[0])
                w["pu"] = _dot(jnp.concatenate([p2_scr[rows[d], :] for d in range(2)], axis=0),
                               jnp.concatenate([jnp.concatenate([ub[0], zero], axis=1),
                                                jnp.concatenate([zero, ub[1]], axis=1)], axis=0))
                w["ktu"] = _dot_tn(jnp.concatenate([kcf_scr[d, rows[d], :] for d in range(2)], axis=1),
                                   jnp.concatenate(ub, axis=1))
            for w in work:
                for d, dst in enumerate((o_ref, ob_scr)):
                    j = 2 * w["s"] + d
                    dst[w["rows"][d], :] = w["qs"][d] + w["pu"][d * CHUNK:(d + 1) * CHUNK, d * LANES:(d + 1) * LANES]
                    states[j] = (egl_scr[d, w["cs"][d]] * states[j]
                                 + w["ktu"][d * GDN_DK:(d + 1) * GDN_DK, d * GDN_DV:(d + 1) * GDN_DV])
        for j in range(2 * NS):
            s_scr[j] = states[j]

    _block_loop(NS * nc, pre_block, PRE_CHUNKS)
    _block_loop(nc, step_block, STEP_CHUNKS)

    og = o_ref[...] + ob_scr[...]
    o_ref[...] = _rms(og, gn_ref[...]) * _silu(z_ref[...])
    if emit_state:
        for s in range(NS):
            for d in range(2):
                sout_ref[s, d] = s_scr[2 * s + d]


def _gdn(qkv, misc, zg, conv_w, conv_b, dt_row, alog_row, gnorm, *, prompt, init=None):
    T, B, NS, blk0 = _seq_geometry(prompt, GDN_PROMPT_SEQS_PER_STEP)
    R = NS * T
    nc = R // CHUNK
    has_init = init is not None
    col = lambda j: (lambda b, h: (blk0 + b, j * GDN_HEADS + h))
    wsel = lambda j: (lambda b, h: (0, j * GDN_HEADS + h))
    row128 = pl.BlockSpec((1, LANES), lambda b, h: (0, 0))
    in_specs = [pl.BlockSpec((R, LANES), col(j)) for j in range(3)]
    in_specs += [pl.BlockSpec((CONV_W, LANES), wsel(j)) for j in range(3)]
    in_specs += [pl.BlockSpec((1, LANES), wsel(j)) for j in range(3)]
    in_specs += [pl.BlockSpec((R, LANES), lambda b, h: (blk0 + b, 0)), row128, row128,
                 pl.BlockSpec((R, LANES), lambda b, h: (blk0 + b, h)), row128]
    cb2 = conv_b.reshape(1, -1)
    args = [qkv, qkv, qkv, conv_w, conv_w, conv_w, cb2, cb2, cb2, misc, dt_row, alog_row, zg,
            gnorm.reshape(1, GDN_DV)]
    state_spec = pl.BlockSpec((NS, 2, None, GDN_DK, GDN_DV), lambda b, h: (b, 0, h, 0, 0))
    if has_init:
        in_specs.append(state_spec)
        args.append(init)
    out_specs = [pl.BlockSpec((R, LANES), lambda b, h: (b, h))]
    out_shape = [jax.ShapeDtypeStruct((B * T, GDN_HEADS * GDN_DV), F32)]
    if prompt:
        out_specs.append(state_spec)
        out_shape.append(jax.ShapeDtypeStruct((B, 2, GDN_HEADS, GDN_DK, GDN_DV), F32))
    return pl.pallas_call(
        functools.partial(_gdn_kernel, T=T, NS=NS, has_init=has_init, emit_state=prompt),
        grid=(B // NS, GDN_HEADS),
        in_specs=in_specs,
        out_specs=out_specs,
        out_shape=out_shape,
        scratch_shapes=[pltpu.VMEM((T + 2 * CONV_HALO, LANES), F32)]
        + [pltpu.VMEM((R, LANES), F32) for _ in range(4)]
        + [pltpu.VMEM((2 * NS, GDN_DK, GDN_DV), F32),
           pltpu.VMEM((2, R, GDN_DV), F32),
           pltpu.VMEM((2, nc, 2 * CHUNK, GDN_DK), BF16),
           pltpu.VMEM((2, R, GDN_DK), BF16),
           pltpu.VMEM((R, LANES), BF16),
           pltpu.VMEM((2, nc, 1, LANES), F32)],
        compiler_params=_cparams("arbitrary", "arbitrary"),
        name="gdn_prompt" if prompt else "gdn_sample",
    )(*args)


def _router_kernel(x_ref, g_ref, mod_ref, whi_ref, wlo_ref, b_ref, h_ref, comb_ref):
    mod = mod_ref[...]
    h = _rms(x_ref[...], g_ref[...]) * (1.0 + mod[:, D_MODEL:2 * D_MODEL]) + mod[:, :D_MODEL]
    hb = h.astype(BF16)
    h_ref[...] = hb
    hl = (h - hb.astype(F32)).astype(BF16)
    whi = whi_ref[...]
    logit = _dot(hb, whi) + (_dot(hl, whi) + _dot(hb, wlo_ref[...])) + b_ref[...]
    lane = lax.broadcasted_iota(jnp.int32, logit.shape, 1)
    is_group = (lane >= MOE_EXPERTS) & (lane < MOE_EXPERTS + MOE_GROUPS)
    glog = jnp.where(is_group, logit, NEG_INF)
    gmax = jnp.max(glog, axis=1, keepdims=True)
    gsel = jnp.min(jnp.where(glog == gmax, lane, LANES), axis=1, keepdims=True) - MOE_EXPERTS
    gw = 1.0 / jnp.sum(jnp.exp(glog - gmax), axis=1, keepdims=True)
    lo = gsel * MOE_PER_GROUP
    in_group = (lane >= lo) & (lane < lo + MOE_PER_GROUP)
    elog = jnp.where(in_group, logit, NEG_INF)
    v1 = jnp.max(elog, axis=1, keepdims=True)
    i1 = jnp.min(jnp.where(elog == v1, lane, LANES), axis=1, keepdims=True)
    elog2 = jnp.where(lane == i1, NEG_INF, elog)
    v2 = jnp.max(elog2, axis=1, keepdims=True)
    i2 = jnp.min(jnp.where(elog2 == v2, lane, LANES), axis=1, keepdims=True)
    e2 = jnp.exp(v2 - v1)
    w1 = gw / (1.0 + e2)
    w2 = gw * e2 / (1.0 + e2)
    comb_ref[...] = jnp.where(lane == i1, w1, 0.0) + jnp.where(lane == i2, w2, 0.0)


def _router(x, gain, mod4, layer, w_route, b_route):
    tm = ROW_TILE
    w_hi = w_route.astype(BF16)
    w_lo = (w_route - w_hi.astype(F32)).astype(BF16)
    return pl.pallas_call(
        _router_kernel,
        grid=(N_TOK // tm,),
        in_specs=[pl.BlockSpec((tm, D_MODEL), lambda i: (i, 0)),
                  pl.BlockSpec((1, D_MODEL), lambda i: (0, 0)),
                  pl.BlockSpec((None, None, 1, 3 * D_MODEL), lambda i: (layer, _mod_index(i, tm), 0, 1)),
                  pl.BlockSpec((D_MODEL, LANES), lambda i: (0, 0)),
                  pl.BlockSpec((D_MODEL, LANES), lambda i: (0, 0)),
                  pl.BlockSpec((1, LANES), lambda i: (0, 0))],
        out_specs=[pl.BlockSpec((tm, D_MODEL), lambda i: (i, 0)),
                   pl.BlockSpec((tm, LANES), lambda i: (i, 0))],
        out_shape=[jax.ShapeDtypeStruct((N_TOK, D_MODEL), BF16),
                   jax.ShapeDtypeStruct((N_TOK, LANES), F32)],
        compiler_params=_cparams("arbitrary"),
        name="router",
    )(x, gain.reshape(1, D_MODEL), mod4, w_hi, w_lo, b_route)


MOE_ROWS = 1024


def _moe_kernel(h_ref, comb_ref, wg_ref, wu_ref, wd_ref, x_ref, mod_ref, gf_ref, o_ref, acc_ref, *, final):
    e = pl.program_id(1)

    @pl.when(e == 0)
    def _():
        acc_ref[...] = jnp.zeros_like(acc_ref)

    h = h_ref[...]
    hg = _dot(h, wg_ref[...].astype(BF16))
    hu = _dot(h, wu_ref[...].astype(BF16))
    cw = _colsel(comb_ref[...], e)
    act = _silu(hg) * hu * cw
    acc_ref[...] += _bdot(act, wd_ref[...])

    @pl.when(e == MOE_EXPERTS - 1)
    def _():
        y = x_ref[...] + mod_ref[...] * acc_ref[...]
        if final:
            y = _rms(y, gf_ref[...])
        o_ref[...] = y


def _moe(h, comb, wg, wu, wd, x, mod4, layer, norm_final, final):
    tm = MOE_ROWS
    return pl.pallas_call(
        functools.partial(_moe_kernel, final=final),
        grid=(N_TOK // tm, MOE_EXPERTS),
        in_specs=[pl.BlockSpec((tm, D_MODEL), lambda i, e: (i, 0)),
                  pl.BlockSpec((tm, LANES), lambda i, e: (i, 0)),
                  pl.BlockSpec((None, None, D_MODEL, MOE_FF), lambda i, e: (layer, e, 0, 0)),
                  pl.BlockSpec((None, None, D_MODEL, MOE_FF), lambda i, e: (layer, e, 0, 0)),
                  pl.BlockSpec((None, None, MOE_FF, D_MODEL), lambda i, e: (layer, e, 0, 0)),
                  pl.BlockSpec((tm, D_MODEL), lambda i, e: (i, 0)),
                  pl.BlockSpec((None, None, 1, D_MODEL), lambda i, e: (layer, _mod_index(i, tm), 0, 5)),
                  pl.BlockSpec((1, D_MODEL), lambda i, e: (0, 0))],
        out_specs=pl.BlockSpec((tm, D_MODEL), lambda i, e: (i, 0)),
        out_shape=jax.ShapeDtypeStruct((N_TOK, D_MODEL), F32),
        scratch_shapes=[pltpu.VMEM((tm, D_MODEL), F32)],
        compiler_params=_cparams("arbitrary", "arbitrary"),
        name="moe",
    )(h, comb, wg, wu, wd, x, mod4, norm_final.reshape(1, D_MODEL))


def _rope_tables():
    t = jnp.arange(DEC_SEQ)
    pos = jnp.stack([(t // GRID_W).astype(F32), (t % GRID_W).astype(F32)], axis=1)
    nf = MLA_ROPE // 4
    inv = ROPE_BASE ** (-jnp.arange(nf, dtype=F32) / nf)
    j = jnp.arange(MLA_ROPE)
    ang = pos[:, j // (2 * nf)] * inv[j % nf][None, :]
    sign = jnp.where((j % (2 * nf)) < nf, -1.0, 1.0)
    cos = jnp.pad(jnp.cos(ang), ((0, 0), (MLA_NOPE, LANES - MLA_NOPE - MLA_ROPE)), constant_values=1.0)
    sin = jnp.pad(jnp.sin(ang) * sign, ((0, 0), (MLA_NOPE, LANES - MLA_NOPE - MLA_ROPE)))
    cos = jnp.concatenate([jnp.ones((ROW_TILE, LANES), F32), cos], axis=0)
    sin = jnp.concatenate([jnp.zeros((ROW_TILE, LANES), F32), sin], axis=0)
    return cos, sin


def _pad_heads(w, n_heads, width, lo, hi):
    k = w.shape[0]
    w = w.reshape(k, n_heads, width)[:, :, lo:hi]
    w = jnp.pad(w, ((0, 0), (0, 0), (0, LANES - (hi - lo))))
    return w.reshape(k, n_heads * LANES)


def kernel(x_prompt, x_sample, c, cache_mla_kv, cache_mla_krope, state_mlstm_C, state_mlstm_n, state_mlstm_m, state_ssd, state_gdn, c_ctx, ada_w, ada_b, norm_mix, norm_ffn, w_in_even, ml_i_bias, ml_f_bias, ml_norm, mla_q_norm, mla_w_uq, mla_kv_norm, mla_w_ukv, w_out_even, w_in_odd, ssd_conv_w, ssd_conv_b, ssd_dt_bias, ssd_A_log, ssd_D, ssd_norm, gdn_conv_w, gdn_conv_b, gdn_dt_bias, gdn_A_log, gdn_norm, w_out_odd, moe_w_group, moe_b_group, moe_w_expert, moe_b_expert, moe_w_gate, moe_w_up, moe_w_down, norm_final):
    x = (x_prompt.reshape(N_PROMPT, D_MODEL), x_sample.reshape(N_SAMPLE, D_MODEL))
    cond = jnp.concatenate([c_ctx[None, :], c, jnp.zeros((N_COND - 1 - DEC_BATCH, D_MODEL), F32)], axis=0)
    mod4 = _ada(cond, ada_w, ada_b).reshape(DEPTH, N_COND, 1, 6 * D_MODEL)
    cos_tab, sin_tab = _rope_tables()

    def moe_layer(x, layer, final):
        w_route = jnp.concatenate([moe_w_expert[layer], moe_w_group[layer]], axis=1)
        w_route = jnp.pad(w_route, ((0, 0), (0, LANES - MOE_EXPERTS - MOE_GROUPS)))
        b_route = _lane_row(jnp.concatenate([moe_b_expert[layer], moe_b_group[layer]]))
        h, comb = _router(x, norm_ffn[layer], mod4, layer, w_route, b_route)
        return _moe(h, comb, moe_w_gate, moe_w_up, moe_w_down, x, mod4, layer, norm_final, final)

    e = 0
    w = w_in_even[e]
    off = np.cumsum([0, 4 * ML_HEADS * ML_DK, 2 * ML_HEADS, 2 * ML_HEADS, MLA_Q_RANK, MLA_KV_RANK, MLA_ROPE])
    w_misc = jnp.concatenate([w[:, off[1]:off[3]],
                              jnp.zeros((D_MODEL, MLA_NOPE - 4 * ML_HEADS), F32),
                              w[:, off[5]:off[6]],
                              jnp.zeros((D_MODEL, LANES - MLA_NOPE - MLA_ROPE), F32)], axis=1)
    weights = [w[:, :off[1]].astype(BF16), w[:, off[3]:off[4]].astype(BF16),
               w[:, off[4]:off[5]].astype(BF16), w_misc.astype(BF16)]
    qkvo, cq, ckv, misc0 = _inproj(x, norm_mix[0], mod4, 0, weights)

    hm_p, st_c, st_n, st_m = _mlstm(qkvo, misc0, ml_i_bias[e], ml_f_bias[e], ml_norm[e], prompt=True)
    (hm_s,) = _mlstm(qkvo, misc0, ml_i_bias[e], ml_f_bias[e], ml_norm[e], prompt=False,
                     init=(state_mlstm_C[:, e], state_mlstm_n[:, e], state_mlstm_m[:, e]))

    dq = MLA_NOPE + MLA_ROPE
    w_uq_pad = _pad_heads(mla_w_uq[e], MLA_HEADS, dq, 0, dq).astype(BF16)
    w_uk_pad = _pad_heads(mla_w_ukv[e], MLA_HEADS, MLA_NOPE + MLA_V, 0, MLA_NOPE).astype(BF16)
    w_uv = mla_w_ukv[e].reshape(MLA_KV_RANK, MLA_HEADS, MLA_NOPE + MLA_V)[:, :, MLA_NOPE:]
    w_uv = w_uv.reshape(MLA_KV_RANK, MLA_HEADS * MLA_V).astype(BF16)
    q_cat = _mla_q(cq, mla_q_norm[e], w_uq_pad, cos_tab, sin_tab)
    ckv_n, k_cat, v_all = _mla_kv(ckv, misc0, w_uk_pad, w_uv, mla_kv_norm[e], cos_tab, sin_tab)
    cache_kpe = jnp.pad(cache_mla_krope[:, e].reshape(DEC_BATCH * PAST_LEN, MLA_ROPE),
                        ((0, 0), (MLA_NOPE, LANES - MLA_NOPE - MLA_ROPE)))
    k_cache, v_cache = _mla_kv(cache_mla_kv[:, e].reshape(DEC_BATCH * PAST_LEN, MLA_KV_RANK), cache_kpe,
                               w_uk_pad, w_uv)
    att_p = _attention(q_cat, k_cat, v_all, B=BATCH, Tq=SEQ, Tk=SEQ, tq=SEQ, q_row0=0, heads=MLA_HEADS)

    def with_cache(cache, new):
        width = new.shape[1]
        both = jnp.concatenate([cache.reshape(DEC_BATCH, PAST_LEN, width),
                                new[N_PROMPT:].reshape(DEC_BATCH, DEC_SEQ, width)], axis=1)
        return both.reshape(DEC_BATCH * (PAST_LEN + DEC_SEQ), width)

    att_s = _attention(q_cat, with_cache(k_cache, k_cat), with_cache(v_cache, v_all), B=DEC_BATCH, Tq=DEC_SEQ,
                       Tk=PAST_LEN + DEC_SEQ, tq=256, q_row0=N_PROMPT, heads=2)
    wo = w_out_even[e].astype(BF16)
    x = _outproj((hm_p, hm_s), (att_p, att_s), wo[:ML_HEADS * ML_DV], wo[ML_HEADS * ML_DV:], x, mod4, 0)
    x = moe_layer(x, 0, final=False)

    oi = 0
    w = w_in_odd[oi]
    ssd_w = SSD_HEADS * SSD_P
    ssd_cc = ssd_w + 2 * SSD_GROUPS * SSD_N
    gdn_w = GDN_HEADS * GDN_DK
    off = np.cumsum([0, ssd_w, ssd_cc, 2 * SSD_HEADS, 3 * gdn_w, gdn_w, 2 * GDN_HEADS, 2 * GDN_HEADS])
    w_misc = jnp.concatenate([w[:, off[2]:off[3]], w[:, off[5]:off[7]],
                              jnp.zeros((D_MODEL, LANES - 2 * SSD_HEADS - 4 * GDN_HEADS), F32)], axis=1)
    weights = [w[:, off[0]:off[1]].astype(BF16), w[:, off[1]:off[2]].astype(BF16),
               w[:, off[3]:off[4]].astype(BF16), w[:, off[4]:off[5]].astype(BF16), w_misc.astype(BF16)]
    z_s, xbc, qkv_g, z_g, misc = _inproj(x, norm_mix[1], mod4, 1, weights)

    ssd_args = (xbc, misc, ssd_conv_w[oi], ssd_conv_b[oi], ssd_dt_bias[oi], ssd_A_log[oi], ssd_D[oi])
    ys_p, st_ssd = _ssd(*ssd_args, prompt=True)
    (ys_s,) = _ssd(*ssd_args, prompt=False, init=state_ssd[:, oi])

    lo = 2 * SSD_HEADS
    gdn_dt_row = jnp.pad(gdn_dt_bias[oi].reshape(-1), (lo, LANES - lo - 2 * GDN_HEADS)).reshape(1, LANES)
    gdn_alog_row = jnp.pad(gdn_A_log[oi].reshape(-1), (lo, LANES - lo - 2 * GDN_HEADS)).reshape(1, LANES)
    gdn_args = (qkv_g, misc, z_g, gdn_conv_w[oi], gdn_conv_b[oi], gdn_dt_row, gdn_alog_row, gdn_norm[oi])
    og_p, st_gdn = _gdn(*gdn_args, prompt=True)
    (og_s,) = _gdn(*gdn_args, prompt=False, init=state_gdn[:, oi])

    wo = w_out_odd[oi].astype(BF16)
    x = _outproj((ys_p, ys_s), (og_p, og_s), wo[:ssd_w], wo[ssd_w:], x, mod4, 1, z=z_s, gnorm=ssd_norm[oi])
    x = moe_layer(x, 1, final=True)

    y_prompt = x[:N_PROMPT].reshape(BATCH, SEQ, D_MODEL)
    y_sample = x[N_PROMPT:].reshape(DEC_BATCH, DEC_SEQ, D_MODEL)
    new_mla_kv = ckv_n[:N_PROMPT].reshape(BATCH, 1, SEQ, MLA_KV_RANK)
    new_mla_krope = misc0[:N_PROMPT, MLA_NOPE:MLA_NOPE + MLA_ROPE].reshape(BATCH, 1, SEQ, MLA_ROPE)
    return (y_prompt, y_sample, new_mla_kv, new_mla_krope, st_c[:, None], st_n.reshape(BATCH, 1, 2, ML_HEADS, ML_DK),
            st_m.reshape(BATCH, 1, 2, ML_HEADS), st_ssd[:, None], st_gdn[:, None])
```

```python
import functools

import numpy as np
import jax
import jax.numpy as jnp
from jax import lax
from jax.experimental import pallas as pl
from jax.experimental.pallas import tpu as pltpu

F32 = jnp.float32
BF16 = jnp.bfloat16

D_MODEL = 1024
BATCH = 32
SEQ = 256
DEPTH = 2
DEC_BATCH = 2
DEC_SEQ = 2048
PAST_LEN = 256
GRID_W = 64
EPS = 1e-6
ML_HEADS = 4
ML_DK = 128
ML_DV = 128
MLA_HEADS = 8
MLA_Q_RANK = 384
MLA_KV_RANK = 256
MLA_NOPE = 64
MLA_ROPE = 32
MLA_V = 64
ROPE_BASE = 10000.0
SSD_HEADS = 8
SSD_P = 64
SSD_GROUPS = 2
SSD_N = 128
GDN_HEADS = 4
GDN_DK = 128
GDN_DV = 128
CONV_W = 5
MOE_GROUPS = 4
MOE_PER_GROUP = 8
MOE_EXPERTS = 32
MOE_FF = 256

N_PROMPT = BATCH * SEQ
N_SAMPLE = DEC_BATCH * DEC_SEQ
N_TOK = N_PROMPT + N_SAMPLE
N_COND = 8

LANES = 128
CHUNK = 64
ROW_TILE = 512
CONV_HALO = 8
CONV_ROWS = 256
VMEM_LIMIT = 56 * 1024 * 1024
PRE_CHUNKS = 8
STEP_CHUNKS = 4
GDN_PROMPT_SEQS_PER_STEP = 4

assert LANES == 2 * CHUNK and SSD_P == CHUNK

NEG_INF = float("-inf")


def _cparams(*sem):
    return pltpu.CompilerParams(dimension_semantics=sem, vmem_limit_bytes=VMEM_LIMIT)


def _dot(a, b):
    return jnp.dot(a, b, preferred_element_type=F32)


def _dot_nt(a, b):
    return lax.dot_general(a, b, (((1,), (1,)), ((), ())), preferred_element_type=F32)


def _dot_tn(a, b):
    return lax.dot_general(a, b, (((0,), (0,)), ((), ())), preferred_element_type=F32)


def _bdot(a, b):
    return _dot(a.astype(BF16), b.astype(BF16))


def _rms(x, g):
    return x * lax.rsqrt(jnp.mean(x * x, axis=-1, keepdims=True) + EPS) * g


def _softplus(x):
    return jnp.maximum(x, 0.0) + jnp.log1p(jnp.exp(-jnp.abs(x)))


def _silu(x):
    return x * jax.nn.sigmoid(x)


def _colsel(x, j):
    lane = lax.broadcasted_iota(jnp.int32, x.shape, 1)
    return jnp.sum(jnp.where(lane == j, x, 0.0), axis=1, keepdims=True)


def _lanes(x):
    return jnp.broadcast_to(x, (x.shape[0], LANES))


def _half_masks(rev_lo, rev_hi):
    t = lax.broadcasted_iota(jnp.int32, (CHUNK, LANES), 0)
    lane = lax.broadcasted_iota(jnp.int32, (CHUNK, LANES), 1)
    s = lane & (CHUNK - 1)
    hi = lane >= CHUNK

    def pick(fwd, bwd):
        if rev_lo == rev_hi:
            return bwd if rev_lo else fwd
        on_hi, on_lo = (bwd, fwd) if rev_hi else (fwd, bwd)
        return (hi & on_hi) | (jnp.logical_not(hi) & on_lo)

    return dict(hi=hi, t=t, s=s, eye=(s == t), incl=pick(s <= t, s >= t), incl_t=pick(t <= s, t >= s),
                strict=pick(s < t, s > t))


def _pack(hm, col_lo, col_hi):
    return jnp.where(hm["hi"], col_hi, col_lo)


def _rows(hm, cols):
    return jnp.sum(jnp.where(hm["eye"], cols, 0.0), axis=0, keepdims=True)


def _half_sums(hm, x):
    lo = jnp.sum(jnp.where(hm["hi"], 0.0, x), axis=1, keepdims=True)
    hi = jnp.sum(jnp.where(hm["hi"], x, 0.0), axis=1, keepdims=True)
    return lo, hi


def _half_cum(hm, col_lo, col_hi):
    cols = _pack(hm, col_lo, col_hi)
    cum_lo, cum_hi = _half_sums(hm, jnp.where(hm["incl"], _rows(hm, cols), 0.0))
    cum_rows = jnp.sum(jnp.where(hm["incl_t"], cols, 0.0), axis=0, keepdims=True)
    return cum_lo, cum_hi, cum_rows


def _blockdiag(hm, x):
    return jnp.concatenate([jnp.where(hm["hi"], 0.0, x).astype(BF16),
                            jnp.where(hm["hi"], x, 0.0).astype(BF16)], axis=0)


def _chunk_rows(c):
    if isinstance(c, int):
        return pl.ds(c * CHUNK, CHUNK)
    return pl.ds(pl.multiple_of(c * CHUNK, CHUNK), CHUNK)


def _block_loop(n, body, size):
    if n <= size:
        body(list(range(n)))
        return

    def block(blk, carry):
        body([blk * size + j for j in range(size)])
        return carry

    lax.fori_loop(0, n // size, block, 0)


def _mod_index(i, rows_per_tile):
    p_tiles = N_PROMPT // rows_per_tile
    s_tiles = DEC_SEQ // rows_per_tile
    return jnp.where(i < p_tiles, 0, 1 + (i - p_tiles) // s_tiles)


def _ada_kernel(c_ref, w_ref, b_ref, o_ref):
    c = c_ref[...]
    o_ref[...] = _bdot(_silu(c), w_ref[...]) + b_ref[...]


def _ada(cond, ada_w, ada_b):
    nb = 6
    return pl.pallas_call(
        _ada_kernel,
        grid=(DEPTH, nb),
        in_specs=[pl.BlockSpec((N_COND, D_MODEL), lambda l, j: (0, 0)),
                  pl.BlockSpec((None, D_MODEL, D_MODEL), lambda l, j: (l, 0, j)),
                  pl.BlockSpec((None, 1, D_MODEL), lambda l, j: (l, 0, j))],
        out_specs=pl.BlockSpec((None, N_COND, D_MODEL), lambda l, j: (l, 0, j)),
        out_shape=jax.ShapeDtypeStruct((DEPTH, N_COND, 6 * D_MODEL), F32),
        compiler_params=_cparams("arbitrary", "arbitrary"),
        name="ada",
    )(cond, ada_w, ada_b.reshape(DEPTH, 1, 6 * D_MODEL))


def _token_operand(x, tm):
    if not isinstance(x, tuple):
        return [pl.BlockSpec((tm, x.shape[1]), lambda i: (i, 0))], [x]
    pt = N_PROMPT // tm
    width = x[0].shape[1]
    return ([pl.BlockSpec((tm, width), lambda i: (jnp.minimum(i, pt - 1), 0)),
             pl.BlockSpec((tm, width), lambda i: (jnp.maximum(i - pt, 0), 0))], list(x))


def _take_tile(refs, split, tm):
    if not split:
        return refs.pop(0)[...]
    p_ref, s_ref = refs.pop(0), refs.pop(0)
    return jnp.where(pl.program_id(0) < N_PROMPT // tm, p_ref[...], s_ref[...])


def _inproj_kernel(*refs, n_out, split, tm):
    refs = list(refs)
    x = _take_tile(refs, split, tm)
    g_ref, mod_ref = refs[:2]
    w_refs, o_refs = refs[2:2 + n_out], refs[2 + n_out:]
    mod = mod_ref[...]
    h = _rms(x, g_ref[...]) * (1.0 + mod[:, D_MODEL:2 * D_MODEL]) + mod[:, :D_MODEL]
    hb = h.astype(BF16)
    for w_ref, o_ref in zip(w_refs, o_refs):
        o_ref[...] = _dot(hb, w_ref[...])


def _inproj(x, gain, mod4, layer, weights):
    n_out = len(weights)
    tm = ROW_TILE
    in_specs, args = _token_operand(x, tm)
    in_specs += [pl.BlockSpec((1, D_MODEL), lambda i: (0, 0)),
                 pl.BlockSpec((None, None, 1, 2 * D_MODEL), lambda i: (layer, _mod_index(i, tm), 0, 0))]
    in_specs += [pl.BlockSpec(w.shape, lambda i: (0, 0)) for w in weights]
    return pl.pallas_call(
        functools.partial(_inproj_kernel, n_out=n_out, split=isinstance(x, tuple), tm=tm),
        grid=(N_TOK // tm,),
        in_specs=in_specs,
        out_specs=[pl.BlockSpec((tm, w.shape[1]), lambda i: (i, 0)) for w in weights],
        out_shape=[jax.ShapeDtypeStruct((N_TOK, w.shape[1]), F32) for w in weights],
        compiler_params=_cparams("arbitrary"),
        name="inproj",
    )(*args, gain.reshape(1, D_MODEL), mod4, *weights)


def _mlstm_kernel(*refs, T, NS, has_init, emit_state):
    it = iter(refs)
    ib_ref, fb_ref = next(it), next(it)
    m0_ref = next(it) if has_init else None
    q_ref, k_ref, v_ref, o_ref, misc_ref, gn_ref = (next(it) for _ in range(6))
    c0_ref, n0_ref = (next(it), next(it)) if has_init else (None, None)
    hm_ref = next(it)
    cout_ref, nout_ref, mout_ref = (next(it), next(it), next(it)) if emit_state else (None, None, None)
    (hb_scr, c_scr, n_scr, m_scr, num_scr, st_scr, kv_scr, nl_scr, bl_scr, gm_scr) = (next(it) for _ in range(10))

    b = pl.program_id(0)
    h = pl.program_id(1)
    nc = T // CHUNK
    scale = ML_DK ** -0.5

    for s in range(NS):
        for d in range(2):
            if has_init:
                c_scr[2 * s + d] = c0_ref[s, d]
                n_scr[2 * s + d] = n0_ref[s, d]
                m_scr[2 * s + d] = jnp.full((1, LANES), m0_ref[b * NS + s, d, h], F32)
            else:
                c_scr[2 * s + d] = jnp.zeros((ML_DK, ML_DV), F32)
                n_scr[2 * s + d] = jnp.zeros((1, ML_DK), F32)
                m_scr[2 * s + d] = jnp.zeros((1, LANES), F32)

    hm = _half_masks(False, True)

    def pre_gates(c):
        rows = _chunk_rows(c)
        mi = misc_ref[rows, :]
        li = [_colsel(mi, d * ML_HEADS + h) + ib_ref[d, h] for d in range(2)]
        lf = [-_softplus(-(_colsel(mi, 2 * ML_HEADS + d * ML_HEADS + h) + fb_ref[d, h])) for d in range(2)]
        b_cols = [None, None]
        b_cols[0], b_cols[1], b_rows = _half_cum(hm, lf[0], lf[1])
        li_rows = _rows(hm, _pack(hm, li[0], li[1]))
        dm = jnp.where(hm["incl"], _pack(hm, b_cols[0], b_cols[1]) - b_rows + li_rows, NEG_INF)
        mloc = [jnp.max(jnp.where(hm["hi"], NEG_INF, dm), axis=1, keepdims=True),
                jnp.max(jnp.where(hm["hi"], dm, NEG_INF), axis=1, keepdims=True)]
        kc = k_ref[rows, :]
        kws = []
        for d in range(2):
            bl = jnp.sum(lf[d], axis=0, keepdims=True)
            g = bl - b_cols[d] + li[d]
            gmax = jnp.max(g, axis=0, keepdims=True)
            kw = jnp.exp(g - gmax) * kc
            kws.append(kw.astype(BF16))
            nl_scr[d, c] = jnp.sum(kw, axis=0, keepdims=True)
            bl_scr[d, c] = _lanes(bl)
            gm_scr[d, c] = _lanes(gmax)
            st_scr[d, 0, rows, :] = _lanes(mloc[d])
            st_scr[d, 1, rows, :] = _lanes(b_cols[d])
        return dict(c=c, rows=rows, kc=kc.astype(BF16), kw2=jnp.concatenate(kws, axis=1),
                    e2=jnp.exp(dm - _pack(hm, mloc[0], mloc[1])))

    def pre_block(chunks):
        chains = [pre_gates(c) for c in chunks]
        for ch in chains:
            qc = (q_ref[ch["rows"], :] * scale).astype(BF16)
            ch["vb"] = v_ref[ch["rows"], :].astype(BF16)
            ch["qk2"] = _dot_nt(qc, jnp.concatenate([ch["kc"], ch["kc"]], axis=0))
            kv2 = _dot_tn(ch["kw2"], ch["vb"])
            kv_scr[0, ch["c"]] = kv2[:ML_DK]
            kv_scr[1, ch["c"]] = kv2[ML_DK:]
        for ch in chains:
            s2 = ch["qk2"] * ch["e2"]
            ch["s2"] = s2.astype(BF16)
            dens = _half_sums(hm, s2)
            for d in range(2):
                st_scr[d, 2, ch["rows"], :] = _lanes(dens[d])
        for ch in chains:
            vb = ch["vb"]
            zero = jnp.zeros_like(vb)
            vbd = jnp.concatenate([jnp.concatenate([vb, zero], axis=1),
                                   jnp.concatenate([zero, vb], axis=1)], axis=0)
            num2 = _dot(ch["s2"], vbd)
            num_scr[0, ch["rows"], :] = num2[:, :ML_DV]
            num_scr[1, ch["rows"], :] = num2[:, ML_DV:]

    def step_block(chunks):
        states = [(m_scr[j], c_scr[j], n_scr[j]) for j in range(2 * NS)]
        pairs = []
        for c in chunks:
            for s in range(NS):
                pair = []
                for d in range(2):
                    cc = s * nc + (c if d == 0 else nc - 1 - c)
                    rows = _chunk_rows(cc)
                    m, c_st, n_st = states[2 * s + d]
                    mloc, b_col, den_loc = st_scr[d, 0, rows, :], st_scr[d, 1, rows, :], st_scr[d, 2, rows, :]
                    bl, gmax = bl_scr[d, cc], gm_scr[d, cc]
                    qc = q_ref[rows, :] * scale
                    inter = b_col + m
                    mq = jnp.maximum(inter, mloc)
                    a = jnp.exp(inter - mq)
                    f = jnp.exp(mloc - mq)
                    den = f * den_loc + a * jnp.sum(qc * n_st, axis=1, keepdims=True)
                    pair.append(dict(rows=rows, qc=qc.astype(BF16), c_st=c_st.astype(BF16), a=a, f=f,
                                     inv=1.0 / jnp.maximum(jnp.abs(den), jnp.exp(-mq))))
                    m_new = jnp.maximum(bl + m, gmax)
                    dec = jnp.exp(bl + m - m_new)
                    fk = jnp.exp(gmax - m_new)
                    states[2 * s + d] = (m_new, dec * c_st + fk * kv_scr[d, cc], dec * n_st + fk * nl_scr[d, cc])
                pairs.append(pair)
        for j in range(2 * NS):
            m_scr[j], c_scr[j], n_scr[j] = states[j]
        for pair in pairs:
            lhs = jnp.concatenate([pair[0]["qc"], pair[1]["qc"]], axis=0)
            rhs = jnp.concatenate([pair[0]["c_st"], pair[1]["c_st"]], axis=1)
            pair.append(_dot(lhs, rhs))
        for pair in pairs:
            res = pair[2]
            for d, dst in enumerate((hm_ref, hb_scr)):
                it_ = pair[d]
                qc_c = res[d * CHUNK:(d + 1) * CHUNK, d * ML_DV:(d + 1) * ML_DV]
                dst[it_["rows"], :] = (it_["f"] * num_scr[d, it_["rows"], :] + it_["a"] * qc_c) * it_["inv"]

    _block_loop(NS * nc, pre_block, PRE_CHUNKS)
    _block_loop(nc, step_block, STEP_CHUNKS)

    hs = hm_ref[...] + hb_scr[...]
    hm_ref[...] = _rms(hs, gn_ref[...]) * jax.nn.sigmoid(o_ref[...])
    if emit_state:
        for s in range(NS):
            for d in range(2):
                cout_ref[s, d] = c_scr[2 * s + d]
                nout_ref[s, d] = n_scr[2 * s + d]
                mout_ref[s, d] = m_scr[2 * s + d][:, 0:1]


def _seq_geometry(prompt, prompt_seqs_per_step=1):
    if prompt:
        return SEQ, BATCH, prompt_seqs_per_step, 0
    return DEC_SEQ, DEC_BATCH, 1, N_PROMPT // DEC_SEQ


def _mlstm(qkvo, misc, i_bias, f_bias, gnorm, *, prompt, init=None):
    T, B, NS, blk0 = _seq_geometry(prompt)
    R = NS * T
    nc = R // CHUNK
    has_init = init is not None
    smem = pl.BlockSpec(memory_space=pltpu.SMEM)

    def col(j):
        return pl.BlockSpec((R, LANES), lambda b, h: (blk0 + b, j * ML_HEADS + h))

    in_specs = [smem, smem]
    args = [i_bias, f_bias]
    if has_init:
        in_specs.append(smem)
        args.append(init[2])
    in_specs += [col(0), col(1), col(2), col(3),
                 pl.BlockSpec((R, LANES), lambda b, h: (blk0 + b, 0)),
                 pl.BlockSpec((None, 1, ML_DV), lambda b, h: (h, 0, 0))]
    args += [qkvo, qkvo, qkvo, qkvo, misc, gnorm.reshape(ML_HEADS, 1, ML_DV)]
    c_spec = pl.BlockSpec((NS, 2, None, ML_DK, ML_DV), lambda b, h: (b, 0, h, 0, 0))
    n_spec = pl.BlockSpec((NS, 2, None, 1, ML_DK), lambda b, h: (b, 0, h, 0, 0))
    if has_init:
        in_specs += [c_spec, n_spec]
        args += [init[0], init[1].reshape(B, 2, ML_HEADS, 1, ML_DK)]
    out_specs = [pl.BlockSpec((R, LANES), lambda b, h: (b, h))]
    out_shape = [jax.ShapeDtypeStruct((B * T, ML_HEADS * ML_DV), F32)]
    if prompt:
        out_specs += [c_spec, n_spec, pl.BlockSpec((NS, 2, None, 1, 1), lambda b, h: (b, 0, h, 0, 0))]
        out_shape += [jax.ShapeDtypeStruct((B, 2, ML_HEADS, ML_DK, ML_DV), F32),
                      jax.ShapeDtypeStruct((B, 2, ML_HEADS, 1, ML_DK), F32),
                      jax.ShapeDtypeStruct((B, 2, ML_HEADS, 1, 1), F32)]
    return pl.pallas_call(
        functools.partial(_mlstm_kernel, T=T, NS=NS, has_init=has_init, emit_state=prompt),
        grid=(B // NS, ML_HEADS),
        in_specs=in_specs,
        out_specs=out_specs,
        out_shape=out_shape,
        scratch_shapes=[pltpu.VMEM((R, ML_DV), F32),
                        pltpu.VMEM((2 * NS, ML_DK, ML_DV), F32),
                        pltpu.VMEM((2 * NS, 1, ML_DK), F32),
                        pltpu.VMEM((2 * NS, 1, LANES), F32),
                        pltpu.VMEM((2, R, ML_DV), F32),
                        pltpu.VMEM((2, 3, R, LANES), F32),
                        pltpu.VMEM((2, nc, ML_DK, ML_DV), F32),
                        pltpu.VMEM((2, nc, 1, ML_DK), F32),
                        pltpu.VMEM((2, nc, 1, LANES), F32),
                        pltpu.VMEM((2, nc, 1, LANES), F32)],
        compiler_params=_cparams("arbitrary", "arbitrary"),
        name="mlstm_prompt" if prompt else "mlstm_sample",
    )(*args)


def _rope(x, cos, sin_signed):
    lane = lax.broadcasted_iota(jnp.int32, x.shape, 1)
    first = (lane & 15) < 8
    partner = jnp.where(first, pltpu.roll(x, LANES - 8, axis=1), pltpu.roll(x, 8, axis=1))
    return x * cos + partner * sin_signed


def _q_kernel(cq_ref, g_ref, w_ref, cos_ref, sin_ref, q_ref):
    cq = _rms(cq_ref[...], g_ref[...])
    y = _bdot(cq, w_ref[...])
    cos, sin = cos_ref[...], sin_ref[...]
    for hd in range(MLA_HEADS):
        sl = slice(hd * LANES, (hd + 1) * LANES)
        q_ref[:, sl] = _rope(y[:, sl], cos, sin).astype(BF16)


def _rope_block_index(i, tm):
    p_tiles = N_PROMPT // tm
    s_tiles = DEC_SEQ // tm
    return jnp.where(i < p_tiles, 0, 1 + (i - p_tiles) % s_tiles)


def _mla_q(cq, gain, w_uq_pad, cos_tab, sin_tab):
    tm = ROW_TILE
    tab = pl.BlockSpec((tm, LANES), lambda i: (_rope_block_index(i, tm), 0))
    return pl.pallas_call(
        _q_kernel,
        grid=(N_TOK // tm,),
        in_specs=[pl.BlockSpec((tm, MLA_Q_RANK), lambda i: (i, 0)),
                  pl.BlockSpec((1, MLA_Q_RANK), lambda i: (0, 0)),
                  pl.BlockSpec(w_uq_pad.shape, lambda i: (0, 0)),
                  tab, tab],
        out_specs=pl.BlockSpec((tm, MLA_HEADS * LANES), lambda i: (i, 0)),
        out_shape=jax.ShapeDtypeStruct((N_TOK, MLA_HEADS * LANES), BF16),
        compiler_params=_cparams("arbitrary"),
        name="mla_q",
    )(cq, gain.reshape(1, MLA_Q_RANK), w_uq_pad, cos_tab, sin_tab)


def _kv_kernel(*refs, norm):
    if norm:
        ckv_ref, kpe_ref, g_ref, wk_ref, wv_ref, cos_ref, sin_ref, ckvn_ref, k_ref, v_ref = refs
        c = _rms(ckv_ref[...], g_ref[...])
        ckvn_ref[...] = c
    else:
        ckv_ref, kpe_ref, wk_ref, wv_ref, k_ref, v_ref = refs
        c = ckv_ref[...]
    kp = kpe_ref[...]
    lane = lax.broadcasted_iota(jnp.int32, kp.shape, 1)
    kp = jnp.where((lane >= MLA_NOPE) & (lane < MLA_NOPE + MLA_ROPE), kp, 0.0)
    if norm:
        kp = _rope(kp, cos_ref[...], sin_ref[...])
    kn = _bdot(c, wk_ref[...])
    for hd in range(MLA_HEADS):
        sl = slice(hd * LANES, (hd + 1) * LANES)
        k_ref[:, sl] = (kn[:, sl] + kp).astype(BF16)
    v_ref[...] = _bdot(c, wv_ref[...]).astype(BF16)


def _mla_kv(ckv, kpe128, w_uk_pad, w_uv, gain=None, cos_tab=None, sin_tab=None):
    norm = gain is not None
    n = ckv.shape[0]
    tm = ROW_TILE
    row = lambda w: pl.BlockSpec((tm, w), lambda i: (i, 0))
    full = lambda a: pl.BlockSpec(a.shape, lambda i: (0, 0))
    in_specs = [row(MLA_KV_RANK), row(LANES)]
    args = [ckv, kpe128]
    if norm:
        in_specs.append(pl.BlockSpec((1, MLA_KV_RANK), lambda i: (0, 0)))
        args.append(gain.reshape(1, MLA_KV_RANK))
    in_specs += [full(w_uk_pad), full(w_uv)]
    args += [w_uk_pad, w_uv]
    out_specs = [row(MLA_HEADS * LANES), row(MLA_HEADS * MLA_V)]
    out_shape = [jax.ShapeDtypeStruct((n, MLA_HEADS * LANES), BF16),
                 jax.ShapeDtypeStruct((n, MLA_HEADS * MLA_V), BF16)]
    if norm:
        tab = pl.BlockSpec((tm, LANES), lambda i: (_rope_block_index(i, tm), 0))
        in_specs += [tab, tab]
        args += [cos_tab, sin_tab]
        out_specs = [row(MLA_KV_RANK)] + out_specs
        out_shape = [jax.ShapeDtypeStruct((n, MLA_KV_RANK), F32)] + out_shape
    return pl.pallas_call(
        functools.partial(_kv_kernel, norm=norm),
        grid=(n // tm,),
        in_specs=in_specs,
        out_specs=out_specs,
        out_shape=out_shape,
        compiler_params=_cparams("arbitrary"),
        name="mla_kv" if norm else "mla_kv_cache",
    )(*args)


def _attn_kernel(q_ref, k_ref, v_ref, o_ref, *, heads):
    scale = (MLA_NOPE + MLA_ROPE) ** -0.5
    scores = [_dot_nt(q_ref[:, j * LANES:(j + 1) * LANES], k_ref[:, j * LANES:(j + 1) * LANES]) * scale
              for j in range(heads)]
    probs, sums = [], []
    for s in scores:
        p = jnp.exp(s - jnp.max(s, axis=1, keepdims=True))
        sums.append(jnp.sum(p, axis=1, keepdims=True))
        probs.append(p.astype(BF16))
    pvs = [_dot(probs[j], v_ref[:, j * MLA_V:(j + 1) * MLA_V]) for j in range(heads)]
    o_ref[...] = jnp.concatenate([pvs[j] / sums[j] for j in range(heads)], axis=1)


def _attention(q, k, v, *, B, Tq, Tk, tq, q_row0, heads):
    nq = Tq // tq
    blk0 = q_row0 // tq
    return pl.pallas_call(
        functools.partial(_attn_kernel, heads=heads),
        grid=(B, MLA_HEADS // heads, nq),
        in_specs=[pl.BlockSpec((tq, heads * LANES), lambda b, hp, i: (blk0 + b * nq + i, hp)),
                  pl.BlockSpec((Tk, heads * LANES), lambda b, hp, i: (b, hp)),
                  pl.BlockSpec((Tk, heads * MLA_V), lambda b, hp, i: (b, hp))],
        out_specs=pl.BlockSpec((tq, heads * MLA_V), lambda b, hp, i: (b * nq + i, hp)),
        out_shape=jax.ShapeDtypeStruct((B * Tq, MLA_HEADS * MLA_V), F32),
        compiler_params=_cparams("arbitrary", "arbitrary", "arbitrary"),
        name="attention",
    )(q, k, v)


def _outproj_kernel(*refs, odd, split_x, tm):
    refs = list(refs)
    a1 = _take_tile(refs, True, tm)
    a2 = _take_tile(refs, True, tm)
    x = _take_tile(refs, split_x, tm)
    if odd:
        z_ref, gn_ref = refs.pop(0), refs.pop(0)
        a1 = _rms(a1 * _silu(z_ref[...]), gn_ref[...])
    w1_ref, w2_ref, g1_ref, o_ref = refs
    out = _bdot(a1, w1_ref[...]) + _bdot(a2, w2_ref[...])
    o_ref[...] = x + g1_ref[...] * out


def _outproj(a1, a2, w1, w2, x, mod4, layer, z=None, gnorm=None):
    odd = z is not None
    tm = ROW_TILE
    half = w1.shape[0]
    full = lambda a: pl.BlockSpec(a.shape, lambda i: (0, 0))
    in_specs, args = [], []
    for operand in (a1, a2, x):
        specs, arrays = _token_operand(operand, tm)
        in_specs += specs
        args += arrays
    if odd:
        in_specs += [pl.BlockSpec((tm, half), lambda i: (i, 0)), pl.BlockSpec((1, half), lambda i: (0, 0))]
        args += [z, gnorm.reshape(1, half)]
    in_specs += [full(w1), full(w2),
                 pl.BlockSpec((None, None, 1, D_MODEL), lambda i: (layer, _mod_index(i, tm), 0, 2))]
    args += [w1, w2, mod4]
    return pl.pallas_call(
        functools.partial(_outproj_kernel, odd=odd, split_x=isinstance(x, tuple), tm=tm),
        grid=(N_TOK // tm,),
        in_specs=in_specs,
        out_specs=pl.BlockSpec((tm, D_MODEL), lambda i: (i, 0)),
        out_shape=jax.ShapeDtypeStruct((N_TOK, D_MODEL), F32),
        compiler_params=_cparams("arbitrary"),
        name="outproj",
    )(*args)


def _conv_silu(src_ref, dst_ref, pad_ref, w_ref, b_ref, T, NS):
    width = src_ref.shape[1]
    zeros = jnp.zeros((CONV_HALO, width), F32)
    pad_ref[pl.ds(0, CONV_HALO), :] = zeros
    pad_ref[pl.ds(CONV_HALO + T, CONV_HALO), :] = zeros
    w = w_ref[...]
    bias = b_ref[...]
    for s in range(NS):
        pad_ref[pl.ds(CONV_HALO, T), :] = src_ref[pl.ds(s * T, T), :]
        for r0 in range(0, T, CONV_ROWS):
            acc = bias
            for j in range(CONV_W):
                start = r0 + CONV_HALO + j - CONV_W // 2
                acc = acc + w[j:j + 1, :] * pad_ref[pl.ds(start, CONV_ROWS), :]
            dst_ref[pl.ds(s * T + r0, CONV_ROWS), :] = _silu(acc)


def _ssd_kernel(*refs, T, NS, has_init, emit_state):
    it = iter(refs)
    x_ref, b_ref, c_ref = next(it), next(it), next(it)
    wx_ref, wb_ref, wc_ref = next(it), next(it), next(it)
    bx_ref, bb_ref, bc_ref = next(it), next(it), next(it)
    misc_ref, dtb_ref, alog_ref, dskip_ref = next(it), next(it), next(it), next(it)
    h0_ref = next(it) if has_init else None
    y_ref = next(it)
    hout_ref = next(it) if emit_state else None
    (pad_scr, xc_scr, bc_scr, cc_scr, yb_scr, hs_scr, yi_scr, ea_scr, upd_scr, eal_scr) = (
        next(it) for _ in range(10))

    p = pl.program_id(1)
    nc = T // CHUNK

    _conv_silu(x_ref, xc_scr, pad_scr, wx_ref, bx_ref, T, NS)
    _conv_silu(b_ref, bc_scr, pad_scr, wb_ref, bb_ref, T, NS)
    _conv_silu(c_ref, cc_scr, pad_scr, wc_ref, bc_ref, T, NS)

    for s in range(NS):
        for d in range(2):
            if has_init:
                hs_scr[2 * s + d] = h0_ref[s, d].reshape(2 * SSD_P, SSD_N).T
            else:
                hs_scr[2 * s + d] = jnp.zeros((SSD_N, 2 * SSD_P), F32)

    dtb = dtb_ref[...]
    neg_a = -jnp.exp(alog_ref[...])
    hms = [_half_masks(False, False), _half_masks(True, True)]
    hi_row = lax.broadcasted_iota(jnp.int32, (1, LANES), 1) >= CHUNK

    def pre_block(chunks):
        chains = []
        for c in chunks:
            rows = _chunk_rows(c)
            bc = bc_scr[rows, :].astype(BF16)
            cb2 = _dot_nt(cc_scr[rows, :].astype(BF16), jnp.concatenate([bc, bc], axis=0))
            chains.append(dict(c=c, rows=rows, bc=bc, cb2=cb2))
        for ch in chains:
            rows = ch["rows"]
            xc = xc_scr[rows, :]
            dts = _softplus(misc_ref[rows, :] + dtb)
            adt = dts * neg_a
            ch["x2m"] = _blockdiag(hms[0], xc)
            ch["g2"], ch["xw"] = [], []
            for d in range(2):
                hm = hms[d]
                dt_h = [_colsel(dts, d * SSD_HEADS + 2 * p + hh) for hh in range(2)]
                a_h = [_colsel(adt, d * SSD_HEADS + 2 * p + hh) for hh in range(2)]
                cum0, cum1, cum_rows = _half_cum(hm, a_h[0], a_h[1])
                seg2 = jnp.exp(jnp.where(hm["incl"], _pack(hm, cum0, cum1) - cum_rows, NEG_INF))
                dt_rows = _rows(hm, _pack(hm, dt_h[0], dt_h[1]))
                ch["g2"].append((ch["cb2"] * seg2 * dt_rows).astype(BF16))
                al = [jnp.sum(a, axis=0, keepdims=True) for a in a_h]
                wgt = [jnp.exp(al[hh] - cum) * dt_h[hh] for hh, cum in enumerate((cum0, cum1))]
                ch["xw"].append((xc * _pack(hm, wgt[0], wgt[1])).astype(BF16))
                ea_scr[d, rows, :] = _pack(hm, jnp.exp(cum0), jnp.exp(cum1))
                eal_scr[d, ch["c"]] = jnp.where(hi_row, jnp.exp(al[1]), jnp.exp(al[0]))
        for ch in chains:
            for d in range(2):
                yi_scr[d, ch["rows"], :] = _dot(ch["g2"][d], ch["x2m"])
                upd_scr[d, ch["c"]] = _dot_tn(ch["bc"], ch["xw"][d])

    def step_block(chunks):
        states = [hs_scr[j] for j in range(2 * NS)]
        pairs = []
        for c in chunks:
            for s in range(NS):
                pair = []
                for d in range(2):
                    cc = s * nc + (c if d == 0 else nc - 1 - c)
                    pair.append(dict(rows=_chunk_rows(cc), hs=states[2 * s + d].astype(BF16)))
                    states[2 * s + d] = eal_scr[d, cc] * states[2 * s + d] + upd_scr[d, cc]
                pairs.append(pair)
        for j in range(2 * NS):
            hs_scr[j] = states[j]
        for pair in pairs:
            lhs = jnp.concatenate([cc_scr[pair[d]["rows"], :].astype(BF16) for d in range(2)], axis=0)
            rhs = jnp.concatenate([pair[d]["hs"] for d in range(2)], axis=1)
            pair.append(_dot(lhs, rhs))
        for pair in pairs:
            for d, dst in enumerate((y_ref, yb_scr)):
                rows = pair[d]["rows"]
                ch = pair[2][d * CHUNK:(d + 1) * CHUNK, d * LANES:(d + 1) * LANES]
                dst[rows, :] = yi_scr[d, rows, :] + ea_scr[d, rows, :] * ch

    _block_loop(NS * nc, pre_block, PRE_CHUNKS)
    _block_loop(nc, step_block, STEP_CHUNKS)

    y_ref[...] = y_ref[...] + yb_scr[...] + dskip_ref[...] * xc_scr[...]
    if emit_state:
        for s in range(NS):
            for d in range(2):
                hout_ref[s, d] = hs_scr[2 * s + d].T.reshape(2, SSD_P, SSD_N)


def _lane_row(v):
    return jnp.pad(v.astype(F32), (0, LANES - v.shape[0])).reshape(1, LANES)


def _ssd(xbc, misc, conv_w, conv_b, dt_bias, a_log, d_skip, *, prompt, init=None):
    T, B, NS, blk0 = _seq_geometry(prompt)
    R = NS * T
    nc = R // CHUNK
    has_init = init is not None
    n_pairs = SSD_HEADS // 2
    pairs_per_group = n_pairs // SSD_GROUPS
    xb = SSD_HEADS * SSD_P // LANES
    cb = xb + SSD_GROUPS * SSD_N // LANES
    colx = lambda b, p: (blk0 + b, p)
    colb = lambda b, p: (blk0 + b, xb + p // pairs_per_group)
    colc = lambda b, p: (blk0 + b, cb + p // pairs_per_group)
    wsel = lambda f: (lambda b, p: (0, f(b, p)[1]))
    row128 = pl.BlockSpec((1, LANES), lambda b, p: (0, 0))
    in_specs = [pl.BlockSpec((R, LANES), colx), pl.BlockSpec((R, LANES), colb), pl.BlockSpec((R, LANES), colc),
                pl.BlockSpec((CONV_W, LANES), wsel(colx)), pl.BlockSpec((CONV_W, LANES), wsel(colb)),
                pl.BlockSpec((CONV_W, LANES), wsel(colc)),
                pl.BlockSpec((1, LANES), wsel(colx)), pl.BlockSpec((1, LANES), wsel(colb)),
                pl.BlockSpec((1, LANES), wsel(colc)),
                pl.BlockSpec((R, LANES), lambda b, p: (blk0 + b, 0)), row128, row128,
                pl.BlockSpec((1, LANES), lambda b, p: (0, p))]
    cb2 = conv_b.reshape(1, -1)
    args = [xbc, xbc, xbc, conv_w, conv_w, conv_w, cb2, cb2, cb2, misc,
            _lane_row(dt_bias.reshape(-1)), _lane_row(a_log.reshape(-1)),
            jnp.repeat(d_skip, SSD_P).reshape(1, SSD_HEADS * SSD_P)]
    state_spec = pl.BlockSpec((NS, 2, 2, SSD_P, SSD_N), lambda b, p: (b, 0, p, 0, 0))
    if has_init:
        in_specs.append(state_spec)
        args.append(init)
    out_specs = [pl.BlockSpec((R, LANES), lambda b, p: (b, p))]
    out_shape = [jax.ShapeDtypeStruct((B * T, SSD_HEADS * SSD_P), F32)]
    if prompt:
        out_specs.append(state_spec)
        out_shape.append(jax.ShapeDtypeStruct((B, 2, SSD_HEADS, SSD_P, SSD_N), F32))
    return pl.pallas_call(
        functools.partial(_ssd_kernel, T=T, NS=NS, has_init=has_init, emit_state=prompt),
        grid=(B // NS, n_pairs),
        in_specs=in_specs,
        out_specs=out_specs,
        out_shape=out_shape,
        scratch_shapes=[pltpu.VMEM((T + 2 * CONV_HALO, LANES), F32)]
        + [pltpu.VMEM((R, LANES), F32) for _ in range(4)]
        + [pltpu.VMEM((2 * NS, SSD_N, 2 * SSD_P), F32),
           pltpu.VMEM((2, R, LANES), F32),
           pltpu.VMEM((2, R, LANES), F32),
           pltpu.VMEM((2, nc, SSD_N, 2 * SSD_P), F32),
           pltpu.VMEM((2, nc, 1, LANES), F32)],
        compiler_params=_cparams("arbitrary", "arbitrary"),
        name="ssd_prompt" if prompt else "ssd_sample",
    )(*args)


def _tri_inverse(hm, nmat, eye_f):
    levels = int(np.log2(CHUNK))

    def off(level):
        same_big = lax.shift_right_logical(hm["t"], level) == lax.shift_right_logical(hm["s"], level)
        same_small = lax.shift_right_logical(hm["t"], level - 1) == lax.shift_right_logical(hm["s"], level - 1)
        return jnp.where(same_big & jnp.logical_not(same_small), nmat, 0.0)

    state = dict(dinv=eye_f - off(1))

    def first(level):
        def run():
            state["t1"] = _dot(state["dinv"].astype(BF16), _blockdiag(hm, off(level)))
        return run

    def second():
        state["dinv"] = state["dinv"] - _dot(state["t1"].astype(BF16), _blockdiag(hm, state["dinv"]))

    stages = []
    for level in range(2, levels + 1):
        stages += [first(level), second]
    return state, stages


def _gdn_kernel(*refs, T, NS, has_init, emit_state):
    it = iter(refs)
    q_ref, k_ref, v_ref = next(it), next(it), next(it)
    wq_ref, wk_ref, wv_ref = next(it), next(it), next(it)
    bq_ref, bk_ref, bv_ref = next(it), next(it), next(it)
    misc_ref, dtb_ref, alog_ref, z_ref, gn_ref = (next(it) for _ in range(5))
    s0_ref = next(it) if has_init else None
    o_ref = next(it)
    sout_ref = next(it) if emit_state else None
    (pad_scr, qc_scr, kc_scr, vc_scr, ob_scr, s_scr,
     u0_scr, wq_scr, kcf_scr, p2_scr, egl_scr) = (next(it) for _ in range(11))

    h = pl.program_id(1)
    nc = T // CHUNK

    _conv_silu(q_ref, qc_scr, pad_scr, wq_ref, bq_ref, T, NS)
    _conv_silu(k_ref, kc_scr, pad_scr, wk_ref, bk_ref, T, NS)
    _conv_silu(v_ref, vc_scr, pad_scr, wv_ref, bv_ref, T, NS)
    q = qc_scr[...]
    qc_scr[...] = q * lax.rsqrt(jnp.sum(q * q, axis=-1, keepdims=True) + EPS) * (GDN_DK ** -0.5)
    k = kc_scr[...]
    kc_scr[...] = k * lax.rsqrt(jnp.sum(k * k, axis=-1, keepdims=True) + EPS)

    for s in range(NS):
        for d in range(2):
            if has_init:
                s_scr[2 * s + d] = s0_ref[s, d]
            else:
                s_scr[2 * s + d] = jnp.zeros((GDN_DK, GDN_DV), F32)

    dtb = dtb_ref[...]
    neg_a = -jnp.exp(alog_ref[...])
    hm = _half_masks(False, True)
    eye_f = hm["eye"].astype(F32)

    def pre_gates(c):
        rows = _chunk_rows(c)
        kc = kc_scr[rows, :]
        qc = qc_scr[rows, :]
        mi = misc_ref[rows, :]
        gdec = neg_a * _softplus(mi + dtb)
        g_src = [_colsel(gdec, 2 * SSD_HEADS + d * GDN_HEADS + h) for d in range(2)]
        beta = [jax.nn.sigmoid(_colsel(mi, 2 * SSD_HEADS + 2 * GDN_HEADS + d * GDN_HEADS + h)) for d in range(2)]
        gc = [None, None]
        gc[0], gc[1], gc_rows = _half_cum(hm, g_src[0], g_src[1])
        diff = _pack(hm, gc[0], gc[1]) - gc_rows
        beta_rows = _rows(hm, _pack(hm, beta[0], beta[1]))
        kb = kc.astype(BF16)
        ek = []
        for d in range(2):
            egc = jnp.exp(gc[d])
            gl = jnp.sum(g_src[d], axis=0, keepdims=True)
            ek.append((egc * kc).astype(BF16))
            wq_scr[d, c, CHUNK:, :] = (egc * qc).astype(BF16)
            kcf_scr[d, rows, :] = (jnp.exp(gl - gc[d]) * beta[d] * kc).astype(BF16)
            egl_scr[d, c] = _lanes(jnp.exp(gl))
        return dict(c=c, rows=rows, kb=kb, qb=qc.astype(BF16), ek=ek,
                    dec_s=jnp.exp(jnp.where(hm["strict"], diff, NEG_INF)) * beta_rows,
                    dec_i=jnp.exp(jnp.where(hm["incl"], diff, NEG_INF)) * beta_rows)

    def pre_block(chunks):
        chains = [pre_gates(c) for c in chunks]
        for ch in chains:
            kb = ch["kb"]
            ch["kq"] = _dot_nt(jnp.concatenate([kb, ch["qb"]], axis=0), jnp.concatenate([kb, kb], axis=0))
        for ch in chains:
            p2_scr[ch["rows"], :] = (ch["dec_i"] * ch["kq"][CHUNK:]).astype(BF16)
            ch["inv"], ch["stages"] = _tri_inverse(hm, ch["dec_s"] * ch["kq"][:CHUNK], eye_f)
        for i in range(len(chains[0]["stages"])):
            for ch in chains:
                ch["stages"][i]()
        for ch in chains:
            vb = vc_scr[ch["rows"], :].astype(BF16)
            zero = jnp.zeros_like(vb)
            rhs = jnp.concatenate([jnp.concatenate([vb, ch["ek"][0], zero, zero], axis=1),
                                   jnp.concatenate([zero, zero, vb, ch["ek"][1]], axis=1)], axis=0)
            uw = _dot(ch["inv"]["dinv"].astype(BF16), rhs)
            for d in range(2):
                u0_scr[d, ch["rows"], :] = uw[:, 2 * d * LANES:(2 * d + 1) * LANES]
                wq_scr[d, ch["c"], :CHUNK, :] = uw[:, (2 * d + 1) * LANES:(2 * d + 2) * LANES].astype(BF16)

    def step_block(chunks):
        states = [s_scr[j] for j in range(2 * NS)]
        for c in chunks:
            work = []
            for s in range(NS):
                cs = [s * nc + c, s * nc + nc - 1 - c]
                lhs = jnp.concatenate([wq_scr[d, cs[d]] for d in range(2)], axis=0)
                rhs = jnp.concatenate([states[2 * s + d].astype(BF16) for d in range(2)], axis=1)
                work.append(dict(s=s, cs=cs, rows=[_chunk_rows(cc) for cc in cs], a=_dot(lhs, rhs)))
            for w in work:
                rows, ub, w["qs"] = w["rows"], [], []
                for d in range(2):
                    blk = w["a"][2 * d * CHUNK:(2 * d + 2) * CHUNK, d * LANES:(d + 1) * LANES]
                    ub.append((u0_scr[d, rows[d], :] - blk[:CHUNK]).astype(BF16))
                    w["qs"].append(blk[CHUNK:])
                zero = jnp.zeros_like(ub[0])
                w["pu"] = _dot(jnp.concatenate([p2_scr[rows[d], :] for d in range(2)], axis=0),
                               jnp.concatenate([jnp.concatenate([ub[0], zero], axis=1),
                                                jnp.concatenate([zero, ub[1]], axis=1)], axis=0))
                w["ktu"] = _dot_tn(jnp.concatenate([kcf_scr[d, rows[d], :] for d in range(2)], axis=1),
                                   jnp.concatenate(ub, axis=1))
            for w in work:
                for d, dst in enumerate((o_ref, ob_scr)):
                    j = 2 * w["s"] + d
                    dst[w["rows"][d], :] = w["qs"][d] + w["pu"][d * CHUNK:(d + 1) * CHUNK, d * LANES:(d + 1) * LANES]
                    states[j] = (egl_scr[d, w["cs"][d]] * states[j]
                                 + w["ktu"][d * GDN_DK:(d + 1) * GDN_DK, d * GDN_DV:(d + 1) * GDN_DV])
        for j in range(2 * NS):
            s_scr[j] = states[j]

    _block_loop(NS * nc, pre_block, PRE_CHUNKS)
    _block_loop(nc, step_block, STEP_CHUNKS)

    og = o_ref[...] + ob_scr[...]
    o_ref[...] = _rms(og, gn_ref[...]) * _silu(z_ref[...])
    if emit_state:
        for s in range(NS):
            for d in range(2):
                sout_ref[s, d] = s_scr[2 * s + d]


def _gdn(qkv, misc, zg, conv_w, conv_b, dt_row, alog_row, gnorm, *, prompt, init=None):
    T, B, NS, blk0 = _seq_geometry(prompt, GDN_PROMPT_SEQS_PER_STEP)
    R = NS * T
    nc = R // CHUNK
    has_init = init is not None
    col = lambda j: (lambda b, h: (blk0 + b, j * GDN_HEADS + h))
    wsel = lambda j: (lambda b, h: (0, j * GDN_HEADS + h))
    row128 = pl.BlockSpec((1, LANES), lambda b, h: (0, 0))
    in_specs = [pl.BlockSpec((R, LANES), col(j)) for j in range(3)]
    in_specs += [pl.BlockSpec((CONV_W, LANES), wsel(j)) for j in range(3)]
    in_specs += [pl.BlockSpec((1, LANES), wsel(j)) for j in range(3)]
    in_specs += [pl.BlockSpec((R, LANES), lambda b, h: (blk0 + b, 0)), row128, row128,
                 pl.BlockSpec((R, LANES), lambda b, h: (blk0 + b, h)), row128]
    cb2 = conv_b.reshape(1, -1)
    args = [qkv, qkv, qkv, conv_w, conv_w, conv_w, cb2, cb2, cb2, misc, dt_row, alog_row, zg,
            gnorm.reshape(1, GDN_DV)]
    state_spec = pl.BlockSpec((NS, 2, None, GDN_DK, GDN_DV), lambda b, h: (b, 0, h, 0, 0))
    if has_init:
        in_specs.append(state_spec)
        args.append(init)
    out_specs = [pl.BlockSpec((R, LANES), lambda b, h: (b, h))]
    out_shape = [jax.ShapeDtypeStruct((B * T, GDN_HEADS * GDN_DV), F32)]
    if prompt:
        out_specs.append(state_spec)
        out_shape.append(jax.ShapeDtypeStruct((B, 2, GDN_HEADS, GDN_DK, GDN_DV), F32))
    return pl.pallas_call(
        functools.partial(_gdn_kernel, T=T, NS=NS, has_init=has_init, emit_state=prompt),
        grid=(B // NS, GDN_HEADS),
        in_specs=in_specs,
        out_specs=out_specs,
        out_shape=out_shape,
        scratch_shapes=[pltpu.VMEM((T + 2 * CONV_HALO, LANES), F32)]
        + [pltpu.VMEM((R, LANES), F32) for _ in range(4)]
        + [pltpu.VMEM((2 * NS, GDN_DK, GDN_DV), F32),
           pltpu.VMEM((2, R, GDN_DV), F32),
           pltpu.VMEM((2, nc, 2 * CHUNK, GDN_DK), BF16),
           pltpu.VMEM((2, R, GDN_DK), BF16),
           pltpu.VMEM((R, LANES), BF16),
           pltpu.VMEM((2, nc, 1, LANES), F32)],
        compiler_params=_cparams("arbitrary", "arbitrary"),
        name="gdn_prompt" if prompt else "gdn_sample",
    )(*args)


def _router_kernel(x_ref, g_ref, mod_ref, whi_ref, wlo_ref, b_ref, h_ref, ri_ref, w_ref, cnt_ref, count_scr):
    @pl.when(pl.program_id(0) == 0)
    def _():
        count_scr[...] = jnp.zeros_like(count_scr)

    mod = mod_ref[...]
    h = _rms(x_ref[...], g_ref[...]) * (1.0 + mod[:, D_MODEL:2 * D_MODEL]) + mod[:, :D_MODEL]
    hb = h.astype(BF16)
    h_ref[...] = h
    hl = (h - hb.astype(F32)).astype(BF16)
    whi = whi_ref[...]
    logit = _dot(hb, whi) + (_dot(hl, whi) + _dot(hb, wlo_ref[...])) + b_ref[...]
    lane = lax.broadcasted_iota(jnp.int32, logit.shape, 1)
    is_group = (lane >= MOE_EXPERTS) & (lane < MOE_EXPERTS + MOE_GROUPS)
    glog = jnp.where(is_group, logit, NEG_INF)
    gmax = jnp.max(glog, axis=1, keepdims=True)
    gsel = jnp.min(jnp.where(glog == gmax, lane, LANES), axis=1, keepdims=True) - MOE_EXPERTS
    gw = 1.0 / jnp.sum(jnp.exp(glog - gmax), axis=1, keepdims=True)
    lo = gsel * MOE_PER_GROUP
    in_group = (lane >= lo) & (lane < lo + MOE_PER_GROUP)
    elog = jnp.where(in_group, logit, NEG_INF)
    v1 = jnp.max(elog, axis=1, keepdims=True)
    i1 = jnp.min(jnp.where(elog == v1, lane, LANES), axis=1, keepdims=True)
    elog2 = jnp.where(lane == i1, NEG_INF, elog)
    v2 = jnp.max(elog2, axis=1, keepdims=True)
    i2 = jnp.min(jnp.where(elog2 == v2, lane, LANES), axis=1, keepdims=True)
    e2 = jnp.exp(v2 - v1)
    w1 = gw / (1.0 + e2)
    w2 = gw * e2 / (1.0 + e2)
    w_ref[...] = jnp.where(lane == 0, w1, jnp.where(lane == 1, w2, 0.0))
    tm = logit.shape[0]
    onehot = jnp.where((lane == i1) | (lane == i2), 1.0, 0.0)
    t_i = lax.broadcasted_iota(jnp.int32, (tm, tm), 0)
    s_i = lax.broadcasted_iota(jnp.int32, (tm, tm), 1)
    earlier = jnp.where(s_i < t_i, 1.0, 0.0).astype(BF16)
    before = _dot(earlier, onehot.astype(BF16)) + count_scr[...]
    r1 = jnp.sum(jnp.where(lane == i1, before, 0.0), axis=1, keepdims=True).astype(jnp.int32)
    r2 = jnp.sum(jnp.where(lane == i2, before, 0.0), axis=1, keepdims=True).astype(jnp.int32)
    ri_ref[...] = jnp.where(lane == 0, i1, jnp.where(lane == 1, i2, jnp.where(lane == 2, r1, jnp.where(lane == 3, r2, 0))))
    count_scr[...] += jnp.sum(onehot, axis=0, keepdims=True)
    cnt_ref[...] = count_scr[...]


def _router(x, gain, mod4, layer, w_route, b_route):
    tm = ROW_TILE
    w_hi = w_route.astype(BF16)
    w_lo = (w_route - w_hi.astype(F32)).astype(BF16)
    return pl.pallas_call(
        _router_kernel,
        grid=(N_TOK // tm,),
        in_specs=[pl.BlockSpec((tm, D_MODEL), lambda i: (i, 0)),
                  pl.BlockSpec((1, D_MODEL), lambda i: (0, 0)),
                  pl.BlockSpec((None, None, 1, 3 * D_MODEL), lambda i: (layer, _mod_index(i, tm), 0, 1)),
                  pl.BlockSpec((D_MODEL, LANES), lambda i: (0, 0)),
                  pl.BlockSpec((D_MODEL, LANES), lambda i: (0, 0)),
                  pl.BlockSpec((1, LANES), lambda i: (0, 0))],
        out_specs=[pl.BlockSpec((tm, D_MODEL), lambda i: (i, 0)),
                   pl.BlockSpec((tm, LANES), lambda i: (i, 0)),
                   pl.BlockSpec((tm, LANES), lambda i: (i, 0)),
                   pl.BlockSpec((1, LANES), lambda i: (0, 0))],
        out_shape=[jax.ShapeDtypeStruct((N_TOK, D_MODEL), F32),
                   jax.ShapeDtypeStruct((N_TOK, LANES), jnp.int32),
                   jax.ShapeDtypeStruct((N_TOK, LANES), F32),
                   jax.ShapeDtypeStruct((1, LANES), F32)],
        scratch_shapes=[pltpu.VMEM((1, LANES), F32)],
        compiler_params=_cparams("arbitrary"),
        name="router",
    )(x, gain.reshape(1, D_MODEL), mod4, w_hi, w_lo, b_route)


N_ASSIGN = 2 * N_TOK
MOE_TILE = 256
MOE_TILES = N_ASSIGN // MOE_TILE
MOE_SEGMENTS = MOE_TILES + MOE_EXPERTS - 1
DISPATCH_ROWS = 2048
COMBINE_ROWS = 256


def _routing_tables(ri, cnt):
    counts = cnt[0, :MOE_EXPERTS].astype(jnp.int32)
    cum_end = jnp.cumsum(counts)
    cum_start = cum_end - counts
    pos = (cum_start[ri[:, 0:2]] + ri[:, 2:4]).reshape(N_ASSIGN)
    tile_start = jnp.arange(MOE_TILES, dtype=jnp.int32) * MOE_TILE
    expert_start = cum_start[1:]
    tile_rank = jnp.arange(MOE_TILES, dtype=jnp.int32) + jnp.sum(
        (expert_start[None, :] < tile_start[:, None]).astype(jnp.int32), axis=1)
    expert_rank = jnp.arange(MOE_EXPERTS - 1, dtype=jnp.int32) + jnp.minimum(expert_start // MOE_TILE + 1, MOE_TILES)
    values = jnp.concatenate([tile_start, expert_start])
    ranks = jnp.concatenate([tile_rank, expert_rank])
    slot = jnp.arange(MOE_SEGMENTS, dtype=jnp.int32)
    cuts = jnp.sum(jnp.where(ranks[None, :] == slot[:, None], values[None, :], 0), axis=1)
    seg_lo = cuts
    seg_hi = jnp.concatenate([cuts[1:], jnp.full((1,), N_ASSIGN, jnp.int32)])
    seg_tile = jnp.minimum(seg_lo // MOE_TILE, MOE_TILES - 1)
    seg_expert = jnp.minimum(jnp.sum((cum_end[None, :] <= seg_lo[:, None]).astype(jnp.int32), axis=1),
                             MOE_EXPERTS - 1)
    seg_first = jnp.concatenate([jnp.ones((1,), jnp.int32), (seg_tile[1:] != seg_tile[:-1]).astype(jnp.int32)])
    return pos, (seg_tile, seg_expert, seg_lo, seg_hi, seg_first)


def _row_copy(src_ref, src_row, dst_ref, dst_row, sem):
    return pltpu.make_async_copy(src_ref.at[pl.ds(src_row, 1), :], dst_ref.at[pl.ds(dst_row, 1), :], sem)


def _dispatch_kernel(pos_ref, h_ref, xs_ref, sem):
    base = pl.program_id(0) * DISPATCH_ROWS

    def start(j, carry):
        a = base + j
        _row_copy(h_ref, a // 2, xs_ref, pos_ref[a], sem).start()
        return carry

    def wait(j, carry):
        _row_copy(h_ref, 0, xs_ref, 0, sem).wait()
        return carry

    lax.fori_loop(0, DISPATCH_ROWS, start, 0, unroll=8)
    lax.fori_loop(0, DISPATCH_ROWS, wait, 0, unroll=8)


def _dispatch(pos, h):
    return pl.pallas_call(
        _dispatch_kernel,
        grid_spec=pltpu.PrefetchScalarGridSpec(
            num_scalar_prefetch=1,
            grid=(N_ASSIGN // DISPATCH_ROWS,),
            in_specs=[pl.BlockSpec(memory_space=pl.ANY)],
            out_specs=pl.BlockSpec(memory_space=pl.ANY),
            scratch_shapes=[pltpu.SemaphoreType.DMA(())]),
        out_shape=jax.ShapeDtypeStruct((N_ASSIGN, D_MODEL), F32),
        compiler_params=_cparams("arbitrary"),
        name="moe_dispatch",
    )(pos, h)


def _experts_kernel(tile_ref, expert_ref, lo_ref, hi_ref, first_ref, xs_ref, wg_ref, wu_ref, wd_ref, o_ref):
    p = pl.program_id(0)
    lo, hi = lo_ref[p], hi_ref[p]

    @pl.when(first_ref[p] == 1)
    def _():
        o_ref[...] = jnp.zeros_like(o_ref)

    @pl.when(hi > lo)
    def _():
        x = xs_ref[...].astype(BF16)
        hg = _dot(x, wg_ref[...].astype(BF16))
        hu = _dot(x, wu_ref[...].astype(BF16))
        row = tile_ref[p] * MOE_TILE + lax.broadcasted_iota(jnp.int32, (MOE_TILE, 1), 0)
        act = jnp.where((row >= lo) & (row < hi), _silu(hg) * hu, 0.0)
        o_ref[...] += _bdot(act, wd_ref[...])


def _experts(tables, xs, wg, wu, wd, layer):
    weight = lambda shape: pl.BlockSpec((None, None) + shape, lambda p, tile, expert, lo, hi, first: (layer, expert[p], 0, 0))
    rows = pl.BlockSpec((MOE_TILE, D_MODEL), lambda p, tile, expert, lo, hi, first: (tile[p], 0))
    return pl.pallas_call(
        _experts_kernel,
        grid_spec=pltpu.PrefetchScalarGridSpec(
            num_scalar_prefetch=5,
            grid=(MOE_SEGMENTS,),
            in_specs=[rows, weight((D_MODEL, MOE_FF)), weight((D_MODEL, MOE_FF)), weight((MOE_FF, D_MODEL))],
            out_specs=rows),
        out_shape=jax.ShapeDtypeStruct((N_ASSIGN, D_MODEL), F32),
        compiler_params=_cparams("arbitrary"),
        name="moe_experts",
    )(*tables, xs, wg, wu, wd)


def _combine_kernel(pos_ref, ys_ref, w_ref, x_ref, mod_ref, gf_ref, o_ref, buf, sem, *, final):
    base = pl.program_id(0) * COMBINE_ROWS

    def start(t, carry):
        for k in range(2):
            _row_copy(ys_ref, pos_ref[2 * (base + t) + k], buf.at[k], t, sem).start()
        return carry

    def wait(t, carry):
        for k in range(2):
            _row_copy(ys_ref, 0, buf.at[k], 0, sem).wait()
        return carry

    lax.fori_loop(0, COMBINE_ROWS, start, 0, unroll=4)
    lax.fori_loop(0, COMBINE_ROWS, wait, 0, unroll=4)
    w = w_ref[...]
    y = x_ref[...] + mod_ref[...] * (w[:, 0:1] * buf[0] + w[:, 1:2] * buf[1])
    if final:
        y = _rms(y, gf_ref[...])
    o_ref[...] = y


def _combine(pos, ys, wts, x, mod4, layer, norm_final, final):
    tm = COMBINE_ROWS
    return pl.pallas_call(
        functools.partial(_combine_kernel, final=final),
        grid_spec=pltpu.PrefetchScalarGridSpec(
            num_scalar_prefetch=1,
            grid=(N_TOK // tm,),
            in_specs=[pl.BlockSpec(memory_space=pl.ANY),
                      pl.BlockSpec((tm, LANES), lambda i, pos: (i, 0)),
                      pl.BlockSpec((tm, D_MODEL), lambda i, pos: (i, 0)),
                      pl.BlockSpec((None, None, 1, D_MODEL), lambda i, pos: (layer, _mod_index(i, tm), 0, 5)),
                      pl.BlockSpec((1, D_MODEL), lambda i, pos: (0, 0))],
            out_specs=pl.BlockSpec((tm, D_MODEL), lambda i, pos: (i, 0)),
            scratch_shapes=[pltpu.VMEM((2, tm, D_MODEL), F32), pltpu.SemaphoreType.DMA(())]),
        out_shape=jax.ShapeDtypeStruct((N_TOK, D_MODEL), F32),
        compiler_params=_cparams("arbitrary"),
        name="moe_combine",
    )(pos, ys, wts, x, mod4, norm_final.reshape(1, D_MODEL))


def _rope_tables():
    t = jnp.arange(DEC_SEQ)
    pos = jnp.stack([(t // GRID_W).astype(F32), (t % GRID_W).astype(F32)], axis=1)
    nf = MLA_ROPE // 4
    inv = ROPE_BASE ** (-jnp.arange(nf, dtype=F32) / nf)
    j = jnp.arange(MLA_ROPE)
    ang = pos[:, j // (2 * nf)] * inv[j % nf][None, :]
    sign = jnp.where((j % (2 * nf)) < nf, -1.0, 1.0)
    cos = jnp.pad(jnp.cos(ang), ((0, 0), (MLA_NOPE, LANES - MLA_NOPE - MLA_ROPE)), constant_values=1.0)
    sin = jnp.pad(jnp.sin(ang) * sign, ((0, 0), (MLA_NOPE, LANES - MLA_NOPE - MLA_ROPE)))
    cos = jnp.concatenate([jnp.ones((ROW_TILE, LANES), F32), cos], axis=0)
    sin = jnp.concatenate([jnp.zeros((ROW_TILE, LANES), F32), sin], axis=0)
    return cos, sin


def _pad_heads(w, n_heads, width, lo, hi):
    k = w.shape[0]
    w = w.reshape(k, n_heads, width)[:, :, lo:hi]
    w = jnp.pad(w, ((0, 0), (0, 0), (0, LANES - (hi - lo))))
    return w.reshape(k, n_heads * LANES)


def kernel(x_prompt, x_sample, c, cache_mla_kv, cache_mla_krope, state_mlstm_C, state_mlstm_n, state_mlstm_m, state_ssd, state_gdn, c_ctx, ada_w, ada_b, norm_mix, norm_ffn, w_in_even, ml_i_bias, ml_f_bias, ml_norm, mla_q_norm, mla_w_uq, mla_kv_norm, mla_w_ukv, w_out_even, w_in_odd, ssd_conv_w, ssd_conv_b, ssd_dt_bias, ssd_A_log, ssd_D, ssd_norm, gdn_conv_w, gdn_conv_b, gdn_dt_bias, gdn_A_log, gdn_norm, w_out_odd, moe_w_group, moe_b_group, moe_w_expert, moe_b_expert, moe_w_gate, moe_w_up, moe_w_down, norm_final):
    x = (x_prompt.reshape(N_PROMPT, D_MODEL), x_sample.reshape(N_SAMPLE, D_MODEL))
    cond = jnp.concatenate([c_ctx[None, :], c, jnp.zeros((N_COND - 1 - DEC_BATCH, D_MODEL), F32)], axis=0)
    mod4 = _ada(cond, ada_w, ada_b).reshape(DEPTH, N_COND, 1, 6 * D_MODEL)
    cos_tab, sin_tab = _rope_tables()

    def moe_layer(x, layer, final):
        w_route = jnp.concatenate([moe_w_expert[layer], moe_w_group[layer]], axis=1)
        w_route = jnp.pad(w_route, ((0, 0), (0, LANES - MOE_EXPERTS - MOE_GROUPS)))
        b_route = _lane_row(jnp.concatenate([moe_b_expert[layer], moe_b_group[layer]]))
        h, ri, wts, cnt = _router(x, norm_ffn[layer], mod4, layer, w_route, b_route)
        pos, tables = _routing_tables(ri, cnt)
        ys = _experts(tables, _dispatch(pos, h), moe_w_gate, moe_w_up, moe_w_down, layer)
        return _combine(pos, ys, wts, x, mod4, layer, norm_final, final)

    e = 0
    w = w_in_even[e]
    off = np.cumsum([0, 4 * ML_HEADS * ML_DK, 2 * ML_HEADS, 2 * ML_HEADS, MLA_Q_RANK, MLA_KV_RANK, MLA_ROPE])
    w_misc = jnp.concatenate([w[:, off[1]:off[3]],
                              jnp.zeros((D_MODEL, MLA_NOPE - 4 * ML_HEADS), F32),
                              w[:, off[5]:off[6]],
                              jnp.zeros((D_MODEL, LANES - MLA_NOPE - MLA_ROPE), F32)], axis=1)
    weights = [w[:, :off[1]].astype(BF16), w[:, off[3]:off[4]].astype(BF16),
               w[:, off[4]:off[5]].astype(BF16), w_misc.astype(BF16)]
    qkvo, cq, ckv, misc0 = _inproj(x, norm_mix[0], mod4, 0, weights)

    hm_p, st_c, st_n, st_m = _mlstm(qkvo, misc0, ml_i_bias[e], ml_f_bias[e], ml_norm[e], prompt=True)
    (hm_s,) = _mlstm(qkvo, misc0, ml_i_bias[e], ml_f_bias[e], ml_norm[e], prompt=False,
                     init=(state_mlstm_C[:, e], state_mlstm_n[:, e], state_mlstm_m[:, e]))

    dq = MLA_NOPE + MLA_ROPE
    w_uq_pad = _pad_heads(mla_w_uq[e], MLA_HEADS, dq, 0, dq).astype(BF16)
    w_uk_pad = _pad_heads(mla_w_ukv[e], MLA_HEADS, MLA_NOPE + MLA_V, 0, MLA_NOPE).astype(BF16)
    w_uv = mla_w_ukv[e].reshape(MLA_KV_RANK, MLA_HEADS, MLA_NOPE + MLA_V)[:, :, MLA_NOPE:]
    w_uv = w_uv.reshape(MLA_KV_RANK, MLA_HEADS * MLA_V).astype(BF16)
    q_cat = _mla_q(cq, mla_q_norm[e], w_uq_pad, cos_tab, sin_tab)
    ckv_n, k_cat, v_all = _mla_kv(ckv, misc0, w_uk_pad, w_uv, mla_kv_norm[e], cos_tab, sin_tab)
    cache_kpe = jnp.pad(cache_mla_krope[:, e].reshape(DEC_BATCH * PAST_LEN, MLA_ROPE),
                        ((0, 0), (MLA_NOPE, LANES - MLA_NOPE - MLA_ROPE)))
    k_cache, v_cache = _mla_kv(cache_mla_kv[:, e].reshape(DEC_BATCH * PAST_LEN, MLA_KV_RANK), cache_kpe,
                               w_uk_pad, w_uv)
    att_p = _attention(q_cat, k_cat, v_all, B=BATCH, Tq=SEQ, Tk=SEQ, tq=SEQ, q_row0=0, heads=MLA_HEADS)

    def with_cache(cache, new):
        width = new.shape[1]
        both = jnp.concatenate([cache.reshape(DEC_BATCH, PAST_LEN, width),
                                new[N_PROMPT:].reshape(DEC_BATCH, DEC_SEQ, width)], axis=1)
        return both.reshape(DEC_BATCH * (PAST_LEN + DEC_SEQ), width)

    att_s = _attention(q_cat, with_cache(k_cache, k_cat), with_cache(v_cache, v_all), B=DEC_BATCH, Tq=DEC_SEQ,
                       Tk=PAST_LEN + DEC_SEQ, tq=256, q_row0=N_PROMPT, heads=2)
    wo = w_out_even[e].astype(BF16)
    x = _outproj((hm_p, hm_s), (att_p, att_s), wo[:ML_HEADS * ML_DV], wo[ML_HEADS * ML_DV:], x, mod4, 0)
    x = moe_layer(x, 0, final=False)

    oi = 0
    w = w_in_odd[oi]
    ssd_w = SSD_HEADS * SSD_P
    ssd_cc = ssd_w + 2 * SSD_GROUPS * SSD_N
    gdn_w = GDN_HEADS * GDN_DK
    off = np.cumsum([0, ssd_w, ssd_cc, 2 * SSD_HEADS, 3 * gdn_w, gdn_w, 2 * GDN_HEADS, 2 * GDN_HEADS])
    w_misc = jnp.concatenate([w[:, off[2]:off[3]], w[:, off[5]:off[7]],
                              jnp.zeros((D_MODEL, LANES - 2 * SSD_HEADS - 4 * GDN_HEADS), F32)], axis=1)
    weights = [w[:, off[0]:off[1]].astype(BF16), w[:, off[1]:off[2]].astype(BF16),
               w[:, off[3]:off[4]].astype(BF16), w[:, off[4]:off[5]].astype(BF16), w_misc.astype(BF16)]
    z_s, xbc, qkv_g, z_g, misc = _inproj(x, norm_mix[1], mod4, 1, weights)

    ssd_args = (xbc, misc, ssd_conv_w[oi], ssd_conv_b[oi], ssd_dt_bias[oi], ssd_A_log[oi], ssd_D[oi])
    ys_p, st_ssd = _ssd(*ssd_args, prompt=True)
    (ys_s,) = _ssd(*ssd_args, prompt=False, init=state_ssd[:, oi])

    lo = 2 * SSD_HEADS
    gdn_dt_row = jnp.pad(gdn_dt_bias[oi].reshape(-1), (lo, LANES - lo - 2 * GDN_HEADS)).reshape(1, LANES)
    gdn_alog_row = jnp.pad(gdn_A_log[oi].reshape(-1), (lo, LANES - lo - 2 * GDN_HEADS)).reshape(1, LANES)
    gdn_args = (qkv_g, misc, z_g, gdn_conv_w[oi], gdn_conv_b[oi], gdn_dt_row, gdn_alog_row, gdn_norm[oi])
    og_p, st_gdn = _gdn(*gdn_args, prompt=True)
    (og_s,) = _gdn(*gdn_args, prompt=False, init=state_gdn[:, oi])

    wo = w_out_odd[oi].astype(BF16)
    x = _outproj((ys_p, ys_s), (og_p, og_s), wo[:ssd_w], wo[ssd_w:], x, mod4, 1, z=z_s, gnorm=ssd_norm[oi])
    x = moe_layer(x, 1, final=True)

    y_prompt = x[:N_PROMPT].reshape(BATCH, SEQ, D_MODEL)
    y_sample = x[N_PROMPT:].reshape(DEC_BATCH, DEC_SEQ, D_MODEL)
    new_mla_kv = ckv_n[:N_PROMPT].reshape(BATCH, 1, SEQ, MLA_KV_RANK)
    new_mla_krope = misc0[:N_PROMPT, MLA_NOPE:MLA_NOPE + MLA_ROPE].reshape(BATCH, 1, SEQ, MLA_ROPE)
    return (y_prompt, y_sample, new_mla_kv, new_mla_krope, st_c[:, None], st_n.reshape(BATCH, 1, 2, ML_HEADS, ML_DK),
            st_m.reshape(BATCH, 1, 2, ML_HEADS), st_ssd[:, None], st_gdn[:, None])
```

```python
import functools

import numpy as np
import jax
import jax.numpy as jnp
from jax import lax
from jax.experimental import pallas as pl
from jax.experimental.pallas import tpu as pltpu

F32 = jnp.float32
BF16 = jnp.bfloat16

D_MODEL = 1024
BATCH = 32
SEQ = 256
DEPTH = 2
DEC_BATCH = 2
DEC_SEQ = 2048
PAST_LEN = 256
GRID_W = 64
EPS = 1e-6
ML_HEADS = 4
ML_DK = 128
ML_DV = 128
MLA_HEADS = 8
MLA_Q_RANK = 384
MLA_KV_RANK = 256
MLA_NOPE = 64
MLA_ROPE = 32
MLA_V = 64
ROPE_BASE = 10000.0
SSD_HEADS = 8
SSD_P = 64
SSD_GROUPS = 2
SSD_N = 128
GDN_HEADS = 4
GDN_DK = 128
GDN_DV = 128
CONV_W = 5
MOE_GROUPS = 4
MOE_PER_GROUP = 8
MOE_EXPERTS = 32
MOE_FF = 256

N_PROMPT = BATCH * SEQ
N_SAMPLE = DEC_BATCH * DEC_SEQ
N_TOK = N_PROMPT + N_SAMPLE
N_COND = 8

LANES = 128
CHUNK = 64
ROW_TILE = 512
CONV_HALO = 8
CONV_ROWS = 256
VMEM_LIMIT = 56 * 1024 * 1024
PRE_CHUNKS = 8
STEP_CHUNKS = 4
GDN_PROMPT_SEQS_PER_STEP = 4

assert LANES == 2 * CHUNK and SSD_P == CHUNK

NEG_INF = float("-inf")


def _cparams(*sem):
    return pltpu.CompilerParams(dimension_semantics=sem, vmem_limit_bytes=VMEM_LIMIT)


def _dot(a, b):
    return jnp.dot(a, b, preferred_element_type=F32)


def _dot_nt(a, b):
    return lax.dot_general(a, b, (((1,), (1,)), ((), ())), preferred_element_type=F32)


def _dot_tn(a, b):
    return lax.dot_general(a, b, (((0,), (0,)), ((), ())), preferred_element_type=F32)


def _bdot(a, b):
    return _dot(a.astype(BF16), b.astype(BF16))


def _rms(x, g):
    return x * lax.rsqrt(jnp.mean(x * x, axis=-1, keepdims=True) + EPS) * g


def _softplus(x):
    return jnp.maximum(x, 0.0) + jnp.log1p(jnp.exp(-jnp.abs(x)))


def _silu(x):
    return x * jax.nn.sigmoid(x)


def _colsel(x, j):
    lane = lax.broadcasted_iota(jnp.int32, x.shape, 1)
    return jnp.sum(jnp.where(lane == j, x, 0.0), axis=1, keepdims=True)


def _lanes(x):
    return jnp.broadcast_to(x, (x.shape[0], LANES))


def _half_masks(rev_lo, rev_hi):
    t = lax.broadcasted_iota(jnp.int32, (CHUNK, LANES), 0)
    lane = lax.broadcasted_iota(jnp.int32, (CHUNK, LANES), 1)
    s = lane & (CHUNK - 1)
    hi = lane >= CHUNK

    def pick(fwd, bwd):
        if rev_lo == rev_hi:
            return bwd if rev_lo else fwd
        on_hi, on_lo = (bwd, fwd) if rev_hi else (fwd, bwd)
        return (hi & on_hi) | (jnp.logical_not(hi) & on_lo)

    return dict(hi=hi, t=t, s=s, eye=(s == t), incl=pick(s <= t, s >= t), incl_t=pick(t <= s, t >= s),
                strict=pick(s < t, s > t))


def _pack(hm, col_lo, col_hi):
    return jnp.where(hm["hi"], col_hi, col_lo)


def _rows(hm, cols):
    return jnp.sum(jnp.where(hm["eye"], cols, 0.0), axis=0, keepdims=True)


def _half_sums(hm, x):
    lo = jnp.sum(jnp.where(hm["hi"], 0.0, x), axis=1, keepdims=True)
    hi = jnp.sum(jnp.where(hm["hi"], x, 0.0), axis=1, keepdims=True)
    return lo, hi


def _half_cum(hm, col_lo, col_hi):
    cols = _pack(hm, col_lo, col_hi)
    cum_lo, cum_hi = _half_sums(hm, jnp.where(hm["incl"], _rows(hm, cols), 0.0))
    cum_rows = jnp.sum(jnp.where(hm["incl_t"], cols, 0.0), axis=0, keepdims=True)
    return cum_lo, cum_hi, cum_rows


def _blockdiag(hm, x):
    return jnp.concatenate([jnp.where(hm["hi"], 0.0, x).astype(BF16),
                            jnp.where(hm["hi"], x, 0.0).astype(BF16)], axis=0)


def _chunk_rows(c):
    if isinstance(c, int):
        return pl.ds(c * CHUNK, CHUNK)
    return pl.ds(pl.multiple_of(c * CHUNK, CHUNK), CHUNK)


def _block_loop(n, body, size):
    if n <= size:
        body(list(range(n)))
        return

    def block(blk, carry):
        body([blk * size + j for j in range(size)])
        return carry

    lax.fori_loop(0, n // size, block, 0)


def _mod_index(i, rows_per_tile):
    p_tiles = N_PROMPT // rows_per_tile
    s_tiles = DEC_SEQ // rows_per_tile
    return jnp.where(i < p_tiles, 0, 1 + (i - p_tiles) // s_tiles)


def _ada_kernel(c_ref, w_ref, b_ref, o_ref):
    c = c_ref[...]
    o_ref[...] = _bdot(_silu(c), w_ref[...]) + b_ref[...]


def _ada(cond, ada_w, ada_b):
    nb = 6
    return pl.pallas_call(
        _ada_kernel,
        grid=(DEPTH, nb),
        in_specs=[pl.BlockSpec((N_COND, D_MODEL), lambda l, j: (0, 0)),
                  pl.BlockSpec((None, D_MODEL, D_MODEL), lambda l, j: (l, 0, j)),
                  pl.BlockSpec((None, 1, D_MODEL), lambda l, j: (l, 0, j))],
        out_specs=pl.BlockSpec((None, N_COND, D_MODEL), lambda l, j: (l, 0, j)),
        out_shape=jax.ShapeDtypeStruct((DEPTH, N_COND, 6 * D_MODEL), F32),
        compiler_params=_cparams("arbitrary", "arbitrary"),
        name="ada",
    )(cond, ada_w, ada_b.reshape(DEPTH, 1, 6 * D_MODEL))


def _token_operand(x, tm):
    if not isinstance(x, tuple):
        return [pl.BlockSpec((tm, x.shape[1]), lambda i: (i, 0))], [x]
    pt = N_PROMPT // tm
    width = x[0].shape[1]
    return ([pl.BlockSpec((tm, width), lambda i: (jnp.minimum(i, pt - 1), 0)),
             pl.BlockSpec((tm, width), lambda i: (jnp.maximum(i - pt, 0), 0))], list(x))


def _take_tile(refs, split, tm):
    if not split:
        return refs.pop(0)[...]
    p_ref, s_ref = refs.pop(0), refs.pop(0)
    return jnp.where(pl.program_id(0) < N_PROMPT // tm, p_ref[...], s_ref[...])


def _inproj_kernel(*refs, n_out, split, tm):
    refs = list(refs)
    x = _take_tile(refs, split, tm)
    g_ref, mod_ref = refs[:2]
    w_refs, o_refs = refs[2:2 + n_out], refs[2 + n_out:]
    mod = mod_ref[...]
    h = _rms(x, g_ref[...]) * (1.0 + mod[:, D_MODEL:2 * D_MODEL]) + mod[:, :D_MODEL]
    hb = h.astype(BF16)
    for w_ref, o_ref in zip(w_refs, o_refs):
        o_ref[...] = _dot(hb, w_ref[...])


def _inproj(x, gain, mod4, layer, weights):
    n_out = len(weights)
    tm = ROW_TILE
    in_specs, args = _token_operand(x, tm)
    in_specs += [pl.BlockSpec((1, D_MODEL), lambda i: (0, 0)),
                 pl.BlockSpec((None, None, 1, 2 * D_MODEL), lambda i: (layer, _mod_index(i, tm), 0, 0))]
    in_specs += [pl.BlockSpec(w.shape, lambda i: (0, 0)) for w in weights]
    return pl.pallas_call(
        functools.partial(_inproj_kernel, n_out=n_out, split=isinstance(x, tuple), tm=tm),
        grid=(N_TOK // tm,),
        in_specs=in_specs,
        out_specs=[pl.BlockSpec((tm, w.shape[1]), lambda i: (i, 0)) for w in weights],
        out_shape=[jax.ShapeDtypeStruct((N_TOK, w.shape[1]), F32) for w in weights],
        compiler_params=_cparams("arbitrary"),
        name="inproj",
    )(*args, gain.reshape(1, D_MODEL), mod4, *weights)


def _mlstm_kernel(*refs, T, NS, has_init, emit_state):
    it = iter(refs)
    ib_ref, fb_ref = next(it), next(it)
    m0_ref = next(it) if has_init else None
    q_ref, k_ref, v_ref, o_ref, misc_ref, gn_ref = (next(it) for _ in range(6))
    c0_ref, n0_ref = (next(it), next(it)) if has_init else (None, None)
    hm_ref = next(it)
    cout_ref, nout_ref, mout_ref = (next(it), next(it), next(it)) if emit_state else (None, None, None)
    (hb_scr, c_scr, n_scr, m_scr, num_scr, st_scr, kv_scr, nl_scr, bl_scr, gm_scr) = (next(it) for _ in range(10))

    b = pl.program_id(0)
    h = pl.program_id(1)
    nc = T // CHUNK
    scale = ML_DK ** -0.5

    for s in range(NS):
        for d in range(2):
            if has_init:
                c_scr[2 * s + d] = c0_ref[s, d]
                n_scr[2 * s + d] = n0_ref[s, d]
                m_scr[2 * s + d] = jnp.full((1, LANES), m0_ref[b * NS + s, d, h], F32)
            else:
                c_scr[2 * s + d] = jnp.zeros((ML_DK, ML_DV), F32)
                n_scr[2 * s + d] = jnp.zeros((1, ML_DK), F32)
                m_scr[2 * s + d] = jnp.zeros((1, LANES), F32)

    hm = _half_masks(False, True)

    def pre_gates(c):
        rows = _chunk_rows(c)
        mi = misc_ref[rows, :]
        li = [_colsel(mi, d * ML_HEADS + h) + ib_ref[d, h] for d in range(2)]
        lf = [-_softplus(-(_colsel(mi, 2 * ML_HEADS + d * ML_HEADS + h) + fb_ref[d, h])) for d in range(2)]
        b_cols = [None, None]
        b_cols[0], b_cols[1], b_rows = _half_cum(hm, lf[0], lf[1])
        li_rows = _rows(hm, _pack(hm, li[0], li[1]))
        dm = jnp.where(hm["incl"], _pack(hm, b_cols[0], b_cols[1]) - b_rows + li_rows, NEG_INF)
        mloc = [jnp.max(jnp.where(hm["hi"], NEG_INF, dm), axis=1, keepdims=True),
                jnp.max(jnp.where(hm["hi"], dm, NEG_INF), axis=1, keepdims=True)]
        kc = k_ref[rows, :]
        kws = []
        for d in range(2):
            bl = jnp.sum(lf[d], axis=0, keepdims=True)
            g = bl - b_cols[d] + li[d]
            gmax = jnp.max(g, axis=0, keepdims=True)
            kw = jnp.exp(g - gmax) * kc
            kws.append(kw.astype(BF16))
            nl_scr[d, c] = jnp.sum(kw, axis=0, keepdims=True)
            bl_scr[d, c] = _lanes(bl)
            gm_scr[d, c] = _lanes(gmax)
            st_scr[d, 0, rows, :] = _lanes(mloc[d])
            st_scr[d, 1, rows, :] = _lanes(b_cols[d])
        return dict(c=c, rows=rows, kc=kc.astype(BF16), kw2=jnp.concatenate(kws, axis=1),
                    e2=jnp.exp(dm - _pack(hm, mloc[0], mloc[1])))

    def pre_block(chunks):
        chains = [pre_gates(c) for c in chunks]
        for ch in chains:
            qc = (q_ref[ch["rows"], :] * scale).astype(BF16)
            ch["vb"] = v_ref[ch["rows"], :].astype(BF16)
            ch["qk2"] = _dot_nt(qc, jnp.concatenate([ch["kc"], ch["kc"]], axis=0))
            kv2 = _dot_tn(ch["kw2"], ch["vb"])
            kv_scr[0, ch["c"]] = kv2[:ML_DK]
            kv_scr[1, ch["c"]] = kv2[ML_DK:]
        for ch in chains:
            s2 = ch["qk2"] * ch["e2"]
            ch["s2"] = s2.astype(BF16)
            dens = _half_sums(hm, s2)
            for d in range(2):
                st_scr[d, 2, ch["rows"], :] = _lanes(dens[d])
        for ch in chains:
            vb = ch["vb"]
            zero = jnp.zeros_like(vb)
            vbd = jnp.concatenate([jnp.concatenate([vb, zero], axis=1),
                                   jnp.concatenate([zero, vb], axis=1)], axis=0)
            num2 = _dot(ch["s2"], vbd)
            num_scr[0, ch["rows"], :] = num2[:, :ML_DV]
            num_scr[1, ch["rows"], :] = num2[:, ML_DV:]

    def step_block(chunks):
        states = [(m_scr[j], c_scr[j], n_scr[j]) for j in range(2 * NS)]
        pairs = []
        for c in chunks:
            for s in range(NS):
                pair = []
                for d in range(2):
                    cc = s * nc + (c if d == 0 else nc - 1 - c)
                    rows = _chunk_rows(cc)
                    m, c_st, n_st = states[2 * s + d]
                    mloc, b_col, den_loc = st_scr[d, 0, rows, :], st_scr[d, 1, rows, :], st_scr[d, 2, rows, :]
                    bl, gmax = bl_scr[d, cc], gm_scr[d, cc]
                    qc = q_ref[rows, :] * scale
                    inter = b_col + m
                    mq = jnp.maximum(inter, mloc)
                    a = jnp.exp(inter - mq)
                    f = jnp.exp(mloc - mq)
                    den = f * den_loc + a * jnp.sum(qc * n_st, axis=1, keepdims=True)
                    pair.append(dict(rows=rows, qc=qc.astype(BF16), c_st=c_st.astype(BF16), a=a, f=f,
                                     inv=1.0 / jnp.maximum(jnp.abs(den), jnp.exp(-mq))))
                    m_new = jnp.maximum(bl + m, gmax)
                    dec = jnp.exp(bl + m - m_new)
                    fk = jnp.exp(gmax - m_new)
                    states[2 * s + d] = (m_new, dec * c_st + fk * kv_scr[d, cc], dec * n_st + fk * nl_scr[d, cc])
                pairs.append(pair)
        for j in range(2 * NS):
            m_scr[j], c_scr[j], n_scr[j] = states[j]
        for pair in pairs:
            lhs = jnp.concatenate([pair[0]["qc"], pair[1]["qc"]], axis=0)
            rhs = jnp.concatenate([pair[0]["c_st"], pair[1]["c_st"]], axis=1)
            pair.append(_dot(lhs, rhs))
        for pair in pairs:
            res = pair[2]
            for d, dst in enumerate((hm_ref, hb_scr)):
                it_ = pair[d]
                qc_c = res[d * CHUNK:(d + 1) * CHUNK, d * ML_DV:(d + 1) * ML_DV]
                dst[it_["rows"], :] = (it_["f"] * num_scr[d, it_["rows"], :] + it_["a"] * qc_c) * it_["inv"]

    _block_loop(NS * nc, pre_block, PRE_CHUNKS)
    _block_loop(nc, step_block, STEP_CHUNKS)

    hs = hm_ref[...] + hb_scr[...]
    hm_ref[...] = _rms(hs, gn_ref[...]) * jax.nn.sigmoid(o_ref[...])
    if emit_state:
        for s in range(NS):
            for d in range(2):
                cout_ref[s, d] = c_scr[2 * s + d]
                nout_ref[s, d] = n_scr[2 * s + d]
                mout_ref[s, d] = m_scr[2 * s + d][:, 0:1]


def _seq_geometry(prompt, prompt_seqs_per_step=1):
    if prompt:
        return SEQ, BATCH, prompt_seqs_per_step, 0
    return DEC_SEQ, DEC_BATCH, 1, N_PROMPT // DEC_SEQ


def _mlstm(qkvo, misc, i_bias, f_bias, gnorm, *, prompt, init=None):
    T, B, NS, blk0 = _seq_geometry(prompt)
    R = NS * T
    nc = R // CHUNK
    has_init = init is not None
    smem = pl.BlockSpec(memory_space=pltpu.SMEM)

    def col(j):
        return pl.BlockSpec((R, LANES), lambda b, h: (blk0 + b, j * ML_HEADS + h))

    in_specs = [smem, smem]
    args = [i_bias, f_bias]
    if has_init:
        in_specs.append(smem)
        args.append(init[2])
    in_specs += [col(0), col(1), col(2), col(3),
                 pl.BlockSpec((R, LANES), lambda b, h: (blk0 + b, 0)),
                 pl.BlockSpec((None, 1, ML_DV), lambda b, h: (h, 0, 0))]
    args += [qkvo, qkvo, qkvo, qkvo, misc, gnorm.reshape(ML_HEADS, 1, ML_DV)]
    c_spec = pl.BlockSpec((NS, 2, None, ML_DK, ML_DV), lambda b, h: (b, 0, h, 0, 0))
    n_spec = pl.BlockSpec((NS, 2, None, 1, ML_DK), lambda b, h: (b, 0, h, 0, 0))
    if has_init:
        in_specs += [c_spec, n_spec]
        args += [init[0], init[1].reshape(B, 2, ML_HEADS, 1, ML_DK)]
    out_specs = [pl.BlockSpec((R, LANES), lambda b, h: (b, h))]
    out_shape = [jax.ShapeDtypeStruct((B * T, ML_HEADS * ML_DV), F32)]
    if prompt:
        out_specs += [c_spec, n_spec, pl.BlockSpec((NS, 2, None, 1, 1), lambda b, h: (b, 0, h, 0, 0))]
        out_shape += [jax.ShapeDtypeStruct((B, 2, ML_HEADS, ML_DK, ML_DV), F32),
                      jax.ShapeDtypeStruct((B, 2, ML_HEADS, 1, ML_DK), F32),
                      jax.ShapeDtypeStruct((B, 2, ML_HEADS, 1, 1), F32)]
    return pl.pallas_call(
        functools.partial(_mlstm_kernel, T=T, NS=NS, has_init=has_init, emit_state=prompt),
        grid=(B // NS, ML_HEADS),
        in_specs=in_specs,
        out_specs=out_specs,
        out_shape=out_shape,
        scratch_shapes=[pltpu.VMEM((R, ML_DV), F32),
                        pltpu.VMEM((2 * NS, ML_DK, ML_DV), F32),
                        pltpu.VMEM((2 * NS, 1, ML_DK), F32),
                        pltpu.VMEM((2 * NS, 1, LANES), F32),
                        pltpu.VMEM((2, R, ML_DV), F32),
                        pltpu.VMEM((2, 3, R, LANES), F32),
                        pltpu.VMEM((2, nc, ML_DK, ML_DV), F32),
                        pltpu.VMEM((2, nc, 1, ML_DK), F32),
                        pltpu.VMEM((2, nc, 1, LANES), F32),
                        pltpu.VMEM((2, nc, 1, LANES), F32)],
        compiler_params=_cparams("arbitrary", "arbitrary"),
        name="mlstm_prompt" if prompt else "mlstm_sample",
    )(*args)


def _rope(x, cos, sin_signed):
    lane = lax.broadcasted_iota(jnp.int32, x.shape, 1)
    first = (lane & 15) < 8
    partner = jnp.where(first, pltpu.roll(x, LANES - 8, axis=1), pltpu.roll(x, 8, axis=1))
    return x * cos + partner * sin_signed


def _q_kernel(cq_ref, g_ref, w_ref, cos_ref, sin_ref, q_ref):
    cq = _rms(cq_ref[...], g_ref[...])
    y = _bdot(cq, w_ref[...])
    cos, sin = cos_ref[...], sin_ref[...]
    for hd in range(MLA_HEADS):
        sl = slice(hd * LANES, (hd + 1) * LANES)
        q_ref[:, sl] = _rope(y[:, sl], cos, sin).astype(BF16)


def _rope_block_index(i, tm):
    p_tiles = N_PROMPT // tm
    s_tiles = DEC_SEQ // tm
    return jnp.where(i < p_tiles, 0, 1 + (i - p_tiles) % s_tiles)


def _mla_q(cq, gain, w_uq_pad, cos_tab, sin_tab):
    tm = ROW_TILE
    tab = pl.BlockSpec((tm, LANES), lambda i: (_rope_block_index(i, tm), 0))
    return pl.pallas_call(
        _q_kernel,
        grid=(N_TOK // tm,),
        in_specs=[pl.BlockSpec((tm, MLA_Q_RANK), lambda i: (i, 0)),
                  pl.BlockSpec((1, MLA_Q_RANK), lambda i: (0, 0)),
                  pl.BlockSpec(w_uq_pad.shape, lambda i: (0, 0)),
                  tab, tab],
        out_specs=pl.BlockSpec((tm, MLA_HEADS * LANES), lambda i: (i, 0)),
        out_shape=jax.ShapeDtypeStruct((N_TOK, MLA_HEADS * LANES), BF16),
        compiler_params=_cparams("arbitrary"),
        name="mla_q",
    )(cq, gain.reshape(1, MLA_Q_RANK), w_uq_pad, cos_tab, sin_tab)


def _kv_kernel(*refs, norm):
    if norm:
        ckv_ref, kpe_ref, g_ref, wk_ref, wv_ref, cos_ref, sin_ref, ckvn_ref, k_ref, v_ref = refs
        c = _rms(ckv_ref[...], g_ref[...])
        ckvn_ref[...] = c
    else:
        ckv_ref, kpe_ref, wk_ref, wv_ref, k_ref, v_ref = refs
        c = ckv_ref[...]
    kp = kpe_ref[...]
    lane = lax.broadcasted_iota(jnp.int32, kp.shape, 1)
    kp = jnp.where((lane >= MLA_NOPE) & (lane < MLA_NOPE + MLA_ROPE), kp, 0.0)
    if norm:
        kp = _rope(kp, cos_ref[...], sin_ref[...])
    kn = _bdot(c, wk_ref[...])
    for hd in range(MLA_HEADS):
        sl = slice(hd * LANES, (hd + 1) * LANES)
        k_ref[:, sl] = (kn[:, sl] + kp).astype(BF16)
    v_ref[...] = _bdot(c, wv_ref[...]).astype(BF16)


def _mla_kv(ckv, kpe128, w_uk_pad, w_uv, gain=None, cos_tab=None, sin_tab=None):
    norm = gain is not None
    n = ckv.shape[0]
    tm = ROW_TILE
    row = lambda w: pl.BlockSpec((tm, w), lambda i: (i, 0))
    full = lambda a: pl.BlockSpec(a.shape, lambda i: (0, 0))
    in_specs = [row(MLA_KV_RANK), row(LANES)]
    args = [ckv, kpe128]
    if norm:
        in_specs.append(pl.BlockSpec((1, MLA_KV_RANK), lambda i: (0, 0)))
        args.append(gain.reshape(1, MLA_KV_RANK))
    in_specs += [full(w_uk_pad), full(w_uv)]
    args += [w_uk_pad, w_uv]
    out_specs = [row(MLA_HEADS * LANES), row(MLA_HEADS * MLA_V)]
    out_shape = [jax.ShapeDtypeStruct((n, MLA_HEADS * LANES), BF16),
                 jax.ShapeDtypeStruct((n, MLA_HEADS * MLA_V), BF16)]
    if norm:
        tab = pl.BlockSpec((tm, LANES), lambda i: (_rope_block_index(i, tm), 0))
        in_specs += [tab, tab]
        args += [cos_tab, sin_tab]
        out_specs = [row(MLA_KV_RANK)] + out_specs
        out_shape = [jax.ShapeDtypeStruct((n, MLA_KV_RANK), F32)] + out_shape
    return pl.pallas_call(
        functools.partial(_kv_kernel, norm=norm),
        grid=(n // tm,),
        in_specs=in_specs,
        out_specs=out_specs,
        out_shape=out_shape,
        compiler_params=_cparams("arbitrary"),
        name="mla_kv" if norm else "mla_kv_cache",
    )(*args)


def _attn_kernel(q_ref, k_ref, v_ref, o_ref, *, heads):
    scale = (MLA_NOPE + MLA_ROPE) ** -0.5
    scores = [_dot_nt(q_ref[:, j * LANES:(j + 1) * LANES], k_ref[:, j * LANES:(j + 1) * LANES]) * scale
              for j in range(heads)]
    probs, sums = [], []
    for s in scores:
        p = jnp.exp(s - jnp.max(s, axis=1, keepdims=True))
        sums.append(jnp.sum(p, axis=1, keepdims=True))
        probs.append(p.astype(BF16))
    pvs = [_dot(probs[j], v_ref[:, j * MLA_V:(j + 1) * MLA_V]) for j in range(heads)]
    o_ref[...] = jnp.concatenate([pvs[j] / sums[j] for j in range(heads)], axis=1)


def _attention(q, k, v, *, B, Tq, Tk, tq, q_row0, heads):
    nq = Tq // tq
    blk0 = q_row0 // tq
    return pl.pallas_call(
        functools.partial(_attn_kernel, heads=heads),
        grid=(B, MLA_HEADS // heads, nq),
        in_specs=[pl.BlockSpec((tq, heads * LANES), lambda b, hp, i: (blk0 + b * nq + i, hp)),
                  pl.BlockSpec((Tk, heads * LANES), lambda b, hp, i: (b, hp)),
                  pl.BlockSpec((Tk, heads * MLA_V), lambda b, hp, i: (b, hp))],
        out_specs=pl.BlockSpec((tq, heads * MLA_V), lambda b, hp, i: (b * nq + i, hp)),
        out_shape=jax.ShapeDtypeStruct((B * Tq, MLA_HEADS * MLA_V), F32),
        compiler_params=_cparams("arbitrary", "arbitrary", "arbitrary"),
        name="attention",
    )(q, k, v)


def _outproj_kernel(*refs, odd, split_x, tm):
    refs = list(refs)
    a1 = _take_tile(refs, True, tm)
    a2 = _take_tile(refs, True, tm)
    x = _take_tile(refs, split_x, tm)
    if odd:
        z_ref, gn_ref = refs.pop(0), refs.pop(0)
        a1 = _rms(a1 * _silu(z_ref[...]), gn_ref[...])
    w1_ref, w2_ref, g1_ref, o_ref = refs
    out = _bdot(a1, w1_ref[...]) + _bdot(a2, w2_ref[...])
    o_ref[...] = x + g1_ref[...] * out


def _outproj(a1, a2, w1, w2, x, mod4, layer, z=None, gnorm=None):
    odd = z is not None
    tm = ROW_TILE
    half = w1.shape[0]
    full = lambda a: pl.BlockSpec(a.shape, lambda i: (0, 0))
    in_specs, args = [], []
    for operand in (a1, a2, x):
        specs, arrays = _token_operand(operand, tm)
        in_specs += specs
        args += arrays
    if odd:
        in_specs += [pl.BlockSpec((tm, half), lambda i: (i, 0)), pl.BlockSpec((1, half), lambda i: (0, 0))]
        args += [z, gnorm.reshape(1, half)]
    in_specs += [full(w1), full(w2),
                 pl.BlockSpec((None, None, 1, D_MODEL), lambda i: (layer, _mod_index(i, tm), 0, 2))]
    args += [w1, w2, mod4]
    return pl.pallas_call(
        functools.partial(_outproj_kernel, odd=odd, split_x=isinstance(x, tuple), tm=tm),
        grid=(N_TOK // tm,),
        in_specs=in_specs,
        out_specs=pl.BlockSpec((tm, D_MODEL), lambda i: (i, 0)),
        out_shape=jax.ShapeDtypeStruct((N_TOK, D_MODEL), F32),
        compiler_params=_cparams("arbitrary"),
        name="outproj",
    )(*args)


def _conv_silu(src_ref, dst_ref, pad_ref, w_ref, b_ref, T, NS):
    width = src_ref.shape[1]
    zeros = jnp.zeros((CONV_HALO, width), F32)
    pad_ref[pl.ds(0, CONV_HALO), :] = zeros
    pad_ref[pl.ds(CONV_HALO + T, CONV_HALO), :] = zeros
    w = w_ref[...]
    bias = b_ref[...]
    for s in range(NS):
        pad_ref[pl.ds(CONV_HALO, T), :] = src_ref[pl.ds(s * T, T), :]
        for r0 in range(0, T, CONV_ROWS):
            acc = bias
            for j in range(CONV_W):
                start = r0 + CONV_HALO + j - CONV_W // 2
                acc = acc + w[j:j + 1, :] * pad_ref[pl.ds(start, CONV_ROWS), :]
            dst_ref[pl.ds(s * T + r0, CONV_ROWS), :] = _silu(acc)


def _ssd_kernel(*refs, T, NS, has_init, emit_state):
    it = iter(refs)
    x_ref, b_ref, c_ref = next(it), next(it), next(it)
    wx_ref, wb_ref, wc_ref = next(it), next(it), next(it)
    bx_ref, bb_ref, bc_ref = next(it), next(it), next(it)
    misc_ref, dtb_ref, alog_ref, dskip_ref = next(it), next(it), next(it), next(it)
    h0_ref = next(it) if has_init else None
    y_ref = next(it)
    hout_ref = next(it) if emit_state else None
    (pad_scr, xc_scr, bc_scr, cc_scr, yb_scr, hs_scr, yi_scr, ea_scr, upd_scr, eal_scr) = (
        next(it) for _ in range(10))

    p = pl.program_id(1)
    nc = T // CHUNK

    _conv_silu(x_ref, xc_scr, pad_scr, wx_ref, bx_ref, T, NS)
    _conv_silu(b_ref, bc_scr, pad_scr, wb_ref, bb_ref, T, NS)
    _conv_silu(c_ref, cc_scr, pad_scr, wc_ref, bc_ref, T, NS)

    for s in range(NS):
        for d in range(2):
            if has_init:
                hs_scr[2 * s + d] = h0_ref[s, d].reshape(2 * SSD_P, SSD_N).T
            else:
                hs_scr[2 * s + d] = jnp.zeros((SSD_N, 2 * SSD_P), F32)

    dtb = dtb_ref[...]
    neg_a = -jnp.exp(alog_ref[...])
    hms = [_half_masks(False, False), _half_masks(True, True)]
    hi_row = lax.broadcasted_iota(jnp.int32, (1, LANES), 1) >= CHUNK

    def pre_block(chunks):
        chains = []
        for c in chunks:
            rows = _chunk_rows(c)
            bc = bc_scr[rows, :].astype(BF16)
            cb2 = _dot_nt(cc_scr[rows, :].astype(BF16), jnp.concatenate([bc, bc], axis=0))
            chains.append(dict(c=c, rows=rows, bc=bc, cb2=cb2))
        for ch in chains:
            rows = ch["rows"]
            xc = xc_scr[rows, :]
            dts = _softplus(misc_ref[rows, :] + dtb)
            adt = dts * neg_a
            ch["x2m"] = _blockdiag(hms[0], xc)
            ch["g2"], ch["xw"] = [], []
            for d in range(2):
                hm = hms[d]
                dt_h = [_colsel(dts, d * SSD_HEADS + 2 * p + hh) for hh in range(2)]
                a_h = [_colsel(adt, d * SSD_HEADS + 2 * p + hh) for hh in range(2)]
                cum0, cum1, cum_rows = _half_cum(hm, a_h[0], a_h[1])
                seg2 = jnp.exp(jnp.where(hm["incl"], _pack(hm, cum0, cum1) - cum_rows, NEG_INF))
                dt_rows = _rows(hm, _pack(hm, dt_h[0], dt_h[1]))
                ch["g2"].append((ch["cb2"] * seg2 * dt_rows).astype(BF16))
                al = [jnp.sum(a, axis=0, keepdims=True) for a in a_h]
                wgt = [jnp.exp(al[hh] - cum) * dt_h[hh] for hh, cum in enumerate((cum0, cum1))]
                ch["xw"].append((xc * _pack(hm, wgt[0], wgt[1])).astype(BF16))
                ea_scr[d, rows, :] = _pack(hm, jnp.exp(cum0), jnp.exp(cum1))
                eal_scr[d, ch["c"]] = jnp.where(hi_row, jnp.exp(al[1]), jnp.exp(al[0]))
        for ch in chains:
            for d in range(2):
                yi_scr[d, ch["rows"], :] = _dot(ch["g2"][d], ch["x2m"])
                upd_scr[d, ch["c"]] = _dot_tn(ch["bc"], ch["xw"][d])

    def step_block(chunks):
        states = [hs_scr[j] for j in range(2 * NS)]
        pairs = []
        for c in chunks:
            for s in range(NS):
                pair = []
                for d in range(2):
                    cc = s * nc + (c if d == 0 else nc - 1 - c)
                    pair.append(dict(rows=_chunk_rows(cc), hs=states[2 * s + d].astype(BF16)))
                    states[2 * s + d] = eal_scr[d, cc] * states[2 * s + d] + upd_scr[d, cc]
                pairs.append(pair)
        for j in range(2 * NS):
            hs_scr[j] = states[j]
        for pair in pairs:
            lhs = jnp.concatenate([cc_scr[pair[d]["rows"], :].astype(BF16) for d in range(2)], axis=0)
            rhs = jnp.concatenate([pair[d]["hs"] for d in range(2)], axis=1)
            pair.append(_dot(lhs, rhs))
        for pair in pairs:
            for d, dst in enumerate((y_ref, yb_scr)):
                rows = pair[d]["rows"]
                ch = pair[2][d * CHUNK:(d + 1) * CHUNK, d * LANES:(d + 1) * LANES]
                dst[rows, :] = yi_scr[d, rows, :] + ea_scr[d, rows, :] * ch

    _block_loop(NS * nc, pre_block, PRE_CHUNKS)
    _block_loop(nc, step_block, STEP_CHUNKS)

    y_ref[...] = y_ref[...] + yb_scr[...] + dskip_ref[...] * xc_scr[...]
    if emit_state:
        for s in range(NS):
            for d in range(2):
                hout_ref[s, d] = hs_scr[2 * s + d].T.reshape(2, SSD_P, SSD_N)


def _lane_row(v):
    return jnp.pad(v.astype(F32), (0, LANES - v.shape[0])).reshape(1, LANES)


def _ssd(xbc, misc, conv_w, conv_b, dt_bias, a_log, d_skip, *, prompt, init=None):
    T, B, NS, blk0 = _seq_geometry(prompt)
    R = NS * T
    nc = R // CHUNK
    has_init = init is not None
    n_pairs = SSD_HEADS // 2
    pairs_per_group = n_pairs // SSD_GROUPS
    xb = SSD_HEADS * SSD_P // LANES
    cb = xb + SSD_GROUPS * SSD_N // LANES
    colx = lambda b, p: (blk0 + b, p)
    colb = lambda b, p: (blk0 + b, xb + p // pairs_per_group)
    colc = lambda b, p: (blk0 + b, cb + p // pairs_per_group)
    wsel = lambda f: (lambda b, p: (0, f(b, p)[1]))
    row128 = pl.BlockSpec((1, LANES), lambda b, p: (0, 0))
    in_specs = [pl.BlockSpec((R, LANES), colx), pl.BlockSpec((R, LANES), colb), pl.BlockSpec((R, LANES), colc),
                pl.BlockSpec((CONV_W, LANES), wsel(colx)), pl.BlockSpec((CONV_W, LANES), wsel(colb)),
                pl.BlockSpec((CONV_W, LANES), wsel(colc)),
                pl.BlockSpec((1, LANES), wsel(colx)), pl.BlockSpec((1, LANES), wsel(colb)),
                pl.BlockSpec((1, LANES), wsel(colc)),
                pl.BlockSpec((R, LANES), lambda b, p: (blk0 + b, 0)), row128, row128,
                pl.BlockSpec((1, LANES), lambda b, p: (0, p))]
    cb2 = conv_b.reshape(1, -1)
    args = [xbc, xbc, xbc, conv_w, conv_w, conv_w, cb2, cb2, cb2, misc,
            _lane_row(dt_bias.reshape(-1)), _lane_row(a_log.reshape(-1)),
            jnp.repeat(d_skip, SSD_P).reshape(1, SSD_HEADS * SSD_P)]
    state_spec = pl.BlockSpec((NS, 2, 2, SSD_P, SSD_N), lambda b, p: (b, 0, p, 0, 0))
    if has_init:
        in_specs.append(state_spec)
        args.append(init)
    out_specs = [pl.BlockSpec((R, LANES), lambda b, p: (b, p))]
    out_shape = [jax.ShapeDtypeStruct((B * T, SSD_HEADS * SSD_P), F32)]
    if prompt:
        out_specs.append(state_spec)
        out_shape.append(jax.ShapeDtypeStruct((B, 2, SSD_HEADS, SSD_P, SSD_N), F32))
    return pl.pallas_call(
        functools.partial(_ssd_kernel, T=T, NS=NS, has_init=has_init, emit_state=prompt),
        grid=(B // NS, n_pairs),
        in_specs=in_specs,
        out_specs=out_specs,
        out_shape=out_shape,
        scratch_shapes=[pltpu.VMEM((T + 2 * CONV_HALO, LANES), F32)]
        + [pltpu.VMEM((R, LANES), F32) for _ in range(4)]
        + [pltpu.VMEM((2 * NS, SSD_N, 2 * SSD_P), F32),
           pltpu.VMEM((2, R, LANES), F32),
           pltpu.VMEM((2, R, LANES), F32),
           pltpu.VMEM((2, nc, SSD_N, 2 * SSD_P), F32),
           pltpu.VMEM((2, nc, 1, LANES), F32)],
        compiler_params=_cparams("arbitrary", "arbitrary"),
        name="ssd_prompt" if prompt else "ssd_sample",
    )(*args)


def _tri_inverse(hm, nmat, eye_f):
    levels = int(np.log2(CHUNK))

    def off(level):
        same_big = lax.shift_right_logical(hm["t"], level) == lax.shift_right_logical(hm["s"], level)
        same_small = lax.shift_right_logical(hm["t"], level - 1) == lax.shift_right_logical(hm["s"], level - 1)
        return jnp.where(same_big & jnp.logical_not(same_small), nmat, 0.0)

    state = dict(dinv=eye_f - off(1))

    def first(level):
        def run():
            state["t1"] = _dot(state["dinv"].astype(BF16), _blockdiag(hm, off(level)))
        return run

    def second():
        state["dinv"] = state["dinv"] - _dot(state["t1"].astype(BF16), _blockdiag(hm, state["dinv"]))

    stages = []
    for level in range(2, levels + 1):
        stages += [first(level), second]
    return state, stages


def _gdn_kernel(*refs, T, NS, has_init, emit_state):
    it = iter(refs)
    q_ref, k_ref, v_ref = next(it), next(it), next(it)
    wq_ref, wk_ref, wv_ref = next(it), next(it), next(it)
    bq_ref, bk_ref, bv_ref = next(it), next(it), next(it)
    misc_ref, dtb_ref, alog_ref, z_ref, gn_ref = (next(it) for _ in range(5))
    s0_ref = next(it) if has_init else None
    o_ref = next(it)
    sout_ref = next(it) if emit_state else None
    (pad_scr, qc_scr, kc_scr, vc_scr, ob_scr, s_scr,
     u0_scr, wq_scr, kcf_scr, p2_scr, egl_scr) = (next(it) for _ in range(11))

    h = pl.program_id(1)
    nc = T // CHUNK

    _conv_silu(q_ref, qc_scr, pad_scr, wq_ref, bq_ref, T, NS)
    _conv_silu(k_ref, kc_scr, pad_scr, wk_ref, bk_ref, T, NS)
    _conv_silu(v_ref, vc_scr, pad_scr, wv_ref, bv_ref, T, NS)
    q = qc_scr[...]
    qc_scr[...] = q * lax.rsqrt(jnp.sum(q * q, axis=-1, keepdims=True) + EPS) * (GDN_DK ** -0.5)
    k = kc_scr[...]
    kc_scr[...] = k * lax.rsqrt(jnp.sum(k * k, axis=-1, keepdims=True) + EPS)

    for s in range(NS):
        for d in range(2):
            if has_init:
                s_scr[2 * s + d] = s0_ref[s, d]
            else:
                s_scr[2 * s + d] = jnp.zeros((GDN_DK, GDN_DV), F32)

    dtb = dtb_ref[...]
    neg_a = -jnp.exp(alog_ref[...])
    hm = _half_masks(False, True)
    eye_f = hm["eye"].astype(F32)

    def pre_gates(c):
        rows = _chunk_rows(c)
        kc = kc_scr[rows, :]
        qc = qc_scr[rows, :]
        mi = misc_ref[rows, :]
        gdec = neg_a * _softplus(mi + dtb)
        g_src = [_colsel(gdec, 2 * SSD_HEADS + d * GDN_HEADS + h) for d in range(2)]
        beta = [jax.nn.sigmoid(_colsel(mi, 2 * SSD_HEADS + 2 * GDN_HEADS + d * GDN_HEADS + h)) for d in range(2)]
        gc = [None, None]
        gc[0], gc[1], gc_rows = _half_cum(hm, g_src[0], g_src[1])
        diff = _pack(hm, gc[0], gc[1]) - gc_rows
        beta_rows = _rows(hm, _pack(hm, beta[0], beta[1]))
        kb = kc.astype(BF16)
        ek = []
        for d in range(2):
            egc = jnp.exp(gc[d])
            gl = jnp.sum(g_src[d], axis=0, keepdims=True)
            ek.append((egc * kc).astype(BF16))
            wq_scr[d, c, CHUNK:, :] = (egc * qc).astype(BF16)
            kcf_scr[d, rows, :] = (jnp.exp(gl - gc[d]) * beta[d] * kc).astype(BF16)
            egl_scr[d, c] = _lanes(jnp.exp(gl))
        return dict(c=c, rows=rows, kb=kb, qb=qc.astype(BF16), ek=ek,
                    dec_s=jnp.exp(jnp.where(hm["strict"], diff, NEG_INF)) * beta_rows,
                    dec_i=jnp.exp(jnp.where(hm["incl"], diff, NEG_INF)) * beta_rows)

    def pre_block(chunks):
        chains = [pre_gates(c) for c in chunks]
        for ch in chains:
            kb = ch["kb"]
            ch["kq"] = _dot_nt(jnp.concatenate([kb, ch["qb"]], axis=0), jnp.concatenate([kb, kb], axis=0))
        for ch in chains:
            p2_scr[ch["rows"], :] = (ch["dec_i"] * ch["kq"][CHUNK:]).astype(BF16)
            ch["inv"], ch["stages"] = _tri_inverse(hm, ch["dec_s"] * ch["kq"][:CHUNK], eye_f)
        for i in range(len(chains[0]["stages"])):
            for ch in chains:
                ch["stages"][i]()
        for ch in chains:
            vb = vc_scr[ch["rows"], :].astype(BF16)
            zero = jnp.zeros_like(vb)
            rhs = jnp.concatenate([jnp.concatenate([vb, ch["ek"][0], zero, zero], axis=1),
                                   jnp.concatenate([zero, zero, vb, ch["ek"][1]], axis=1)], axis=0)
            uw = _dot(ch["inv"]["dinv"].astype(BF16), rhs)
            for d in range(2):
                u0_scr[d, ch["rows"], :] = uw[:, 2 * d * LANES:(2 * d + 1) * LANES]
                wq_scr[d, ch["c"], :CHUNK, :] = uw[:, (2 * d + 1) * LANES:(2 * d + 2) * LANES].astype(BF16)

    def step_block(chunks):
        states = [s_scr[j] for j in range(2 * NS)]
        for c in chunks:
            work = []
            for s in range(NS):
                cs = [s * nc + c, s * nc + nc - 1 - c]
                lhs = jnp.concatenate([wq_scr[d, cs[d]] for d in range(2)], axis=0)
                rhs = jnp.concatenate([states[2 * s + d].astype(BF16) for d in range(2)], axis=1)
                work.append(dict(s=s, cs=cs, rows=[_chunk_rows(cc) for cc in cs], a=_dot(lhs, rhs)))
            for w in work:
                rows, ub, w["qs"] = w["rows"], [], []
                for d in range(2):
                    blk = w["a"][2 * d * CHUNK:(2 * d + 2) * CHUNK, d * LANES:(d + 1) * LANES]
                    ub.append((u0_scr[d, rows[d], :] - blk[:CHUNK]).astype(BF16))
                    w["qs"].append(blk[CHUNK:])
                zero = jnp.zeros_like(ub[0])
                w["pu"] = _dot(jnp.concatenate([p2_scr[rows[d], :] for d in range(2)], axis=0),
                               jnp.concatenate([jnp.concatenate([ub[0], zero], axis=1),
                                                jnp.concatenate([zero, ub[1]], axis=1)], axis=0))
                w["ktu"] = _dot_tn(jnp.concatenate([kcf_scr[d, rows[d], :] for d in range(2)], axis=1),
                                   jnp.concatenate(ub, axis=1))
            for w in work:
                for d, dst in enumerate((o_ref, ob_scr)):
                    j = 2 * w["s"] + d
                    dst[w["rows"][d], :] = w["qs"][d] + w["pu"][d * CHUNK:(d + 1) * CHUNK, d * LANES:(d + 1) * LANES]
                    states[j] = (egl_scr[d, w["cs"][d]] * states[j]
                                 + w["ktu"][d * GDN_DK:(d + 1) * GDN_DK, d * GDN_DV:(d + 1) * GDN_DV])
        for j in range(2 * NS):
            s_scr[j] = states[j]

    _block_loop(NS * nc, pre_block, PRE_CHUNKS)
    _block_loop(nc, step_block, STEP_CHUNKS)

    og = o_ref[...] + ob_scr[...]
    o_ref[...] = _rms(og, gn_ref[...]) * _silu(z_ref[...])
    if emit_state:
        for s in range(NS):
            for d in range(2):
                sout_ref[s, d] = s_scr[2 * s + d]


def _gdn(qkv, misc, zg, conv_w, conv_b, dt_row, alog_row, gnorm, *, prompt, init=None):
    T, B, NS, blk0 = _seq_geometry(prompt, GDN_PROMPT_SEQS_PER_STEP)
    R = NS * T
    nc = R // CHUNK
    has_init = init is not None
    col = lambda j: (lambda b, h: (blk0 + b, j * GDN_HEADS + h))
    wsel = lambda j: (lambda b, h: (0, j * GDN_HEADS + h))
    row128 = pl.BlockSpec((1, LANES), lambda b, h: (0, 0))
    in_specs = [pl.BlockSpec((R, LANES), col(j)) for j in range(3)]
    in_specs += [pl.BlockSpec((CONV_W, LANES), wsel(j)) for j in range(3)]
    in_specs += [pl.BlockSpec((1, LANES), wsel(j)) for j in range(3)]
    in_specs += [pl.BlockSpec((R, LANES), lambda b, h: (blk0 + b, 0)), row128, row128,
                 pl.BlockSpec((R, LANES), lambda b, h: (blk0 + b, h)), row128]
    cb2 = conv_b.reshape(1, -1)
    args = [qkv, qkv, qkv, conv_w, conv_w, conv_w, cb2, cb2, cb2, misc, dt_row, alog_row, zg,
            gnorm.reshape(1, GDN_DV)]
    state_spec = pl.BlockSpec((NS, 2, None, GDN_DK, GDN_DV), lambda b, h: (b, 0, h, 0, 0))
    if has_init:
        in_specs.append(state_spec)
        args.append(init)
    out_specs = [pl.BlockSpec((R, LANES), lambda b, h: (b, h))]
    out_shape = [jax.ShapeDtypeStruct((B * T, GDN_HEADS * GDN_DV), F32)]
    if prompt:
        out_specs.append(state_spec)
        out_shape.append(jax.ShapeDtypeStruct((B, 2, GDN_HEADS, GDN_DK, GDN_DV), F32))
    return pl.pallas_call(
        functools.partial(_gdn_kernel, T=T, NS=NS, has_init=has_init, emit_state=prompt),
        grid=(B // NS, GDN_HEADS),
        in_specs=in_specs,
        out_specs=out_specs,
        out_shape=out_shape,
        scratch_shapes=[pltpu.VMEM((T + 2 * CONV_HALO, LANES), F32)]
        + [pltpu.VMEM((R, LANES), F32) for _ in range(4)]
        + [pltpu.VMEM((2 * NS, GDN_DK, GDN_DV), F32),
           pltpu.VMEM((2, R, GDN_DV), F32),
           pltpu.VMEM((2, nc, 2 * CHUNK, GDN_DK), BF16),
           pltpu.VMEM((2, R, GDN_DK), BF16),
           pltpu.VMEM((R, LANES), BF16),
           pltpu.VMEM((2, nc, 1, LANES), F32)],
        compiler_params=_cparams("arbitrary", "arbitrary"),
        name="gdn_prompt" if prompt else "gdn_sample",
    )(*args)


def _router_kernel(x_ref, g_ref, mod_ref, whi_ref, wlo_ref, b_ref, h_ref, ri_ref, w_ref, cnt_ref, count_scr):
    @pl.when(pl.program_id(0) == 0)
    def _():
        count_scr[...] = jnp.zeros_like(count_scr)

    mod = mod_ref[...]
    h = _rms(x_ref[...], g_ref[...]) * (1.0 + mod[:, D_MODEL:2 * D_MODEL]) + mod[:, :D_MODEL]
    hb = h.astype(BF16)
    h_ref[...] = h
    hl = (h - hb.astype(F32)).astype(BF16)
    whi = whi_ref[...]
    logit = _dot(hb, whi) + (_dot(hl, whi) + _dot(hb, wlo_ref[...])) + b_ref[...]
    lane = lax.broadcasted_iota(jnp.int32, logit.shape, 1)
    is_group = (lane >= MOE_EXPERTS) & (lane < MOE_EXPERTS + MOE_GROUPS)
    glog = jnp.where(is_group, logit, NEG_INF)
    gmax = jnp.max(glog, axis=1, keepdims=True)
    gsel = jnp.min(jnp.where(glog == gmax, lane, LANES), axis=1, keepdims=True) - MOE_EXPERTS
    gw = 1.0 / jnp.sum(jnp.exp(glog - gmax), axis=1, keepdims=True)
    lo = gsel * MOE_PER_GROUP
    in_group = (lane >= lo) & (lane < lo + MOE_PER_GROUP)
    elog = jnp.where(in_group, logit, NEG_INF)
    v1 = jnp.max(elog, axis=1, keepdims=True)
    i1 = jnp.min(jnp.where(elog == v1, lane, LANES), axis=1, keepdims=True)
    elog2 = jnp.where(lane == i1, NEG_INF, elog)
    v2 = jnp.max(elog2, axis=1, keepdims=True)
    i2 = jnp.min(jnp.where(elog2 == v2, lane, LANES), axis=1, keepdims=True)
    e2 = jnp.exp(v2 - v1)
    w1 = gw / (1.0 + e2)
    w2 = gw * e2 / (1.0 + e2)
    w_ref[...] = jnp.where(lane == 0, w1, jnp.where(lane == 1, w2, 0.0))
    tm = logit.shape[0]
    onehot = jnp.where((lane == i1) | (lane == i2), 1.0, 0.0)
    t_i = lax.broadcasted_iota(jnp.int32, (tm, tm), 0)
    s_i = lax.broadcasted_iota(jnp.int32, (tm, tm), 1)
    earlier = jnp.where(s_i < t_i, 1.0, 0.0).astype(BF16)
    before = _dot(earlier, onehot.astype(BF16)) + count_scr[...]
    r1 = jnp.sum(jnp.where(lane == i1, before, 0.0), axis=1, keepdims=True).astype(jnp.int32)
    r2 = jnp.sum(jnp.where(lane == i2, before, 0.0), axis=1, keepdims=True).astype(jnp.int32)
    ri_ref[...] = jnp.where(lane == 0, i1, jnp.where(lane == 1, i2, jnp.where(lane == 2, r1, jnp.where(lane == 3, r2, 0))))
    count_scr[...] += jnp.sum(onehot, axis=0, keepdims=True)
    cnt_ref[...] = count_scr[...]


def _router(x, gain, mod4, layer, w_route, b_route):
    tm = ROW_TILE
    w_hi = w_route.astype(BF16)
    w_lo = (w_route - w_hi.astype(F32)).astype(BF16)
    return pl.pallas_call(
        _router_kernel,
        grid=(N_TOK // tm,),
        in_specs=[pl.BlockSpec((tm, D_MODEL), lambda i: (i, 0)),
                  pl.BlockSpec((1, D_MODEL), lambda i: (0, 0)),
                  pl.BlockSpec((None, None, 1, 3 * D_MODEL), lambda i: (layer, _mod_index(i, tm), 0, 1)),
                  pl.BlockSpec((D_MODEL, LANES), lambda i: (0, 0)),
                  pl.BlockSpec((D_MODEL, LANES), lambda i: (0, 0)),
                  pl.BlockSpec((1, LANES), lambda i: (0, 0))],
        out_specs=[pl.BlockSpec((tm, D_MODEL), lambda i: (i, 0)),
                   pl.BlockSpec((tm, LANES), lambda i: (i, 0)),
                   pl.BlockSpec((tm, LANES), lambda i: (i, 0)),
                   pl.BlockSpec((1, LANES), lambda i: (0, 0))],
        out_shape=[jax.ShapeDtypeStruct((N_TOK, D_MODEL), F32),
                   jax.ShapeDtypeStruct((N_TOK, LANES), jnp.int32),
                   jax.ShapeDtypeStruct((N_TOK, LANES), F32),
                   jax.ShapeDtypeStruct((1, LANES), F32)],
        scratch_shapes=[pltpu.VMEM((1, LANES), F32)],
        compiler_params=_cparams("arbitrary"),
        name="router",
    )(x, gain.reshape(1, D_MODEL), mod4, w_hi, w_lo, b_route)


N_ASSIGN = 2 * N_TOK
MOE_TILE = 256
MOE_TILES = N_ASSIGN // MOE_TILE
MOE_SEGMENTS = MOE_TILES + MOE_EXPERTS - 1
DISPATCH_ROWS = 512
COMBINE_ROWS = 256


def _routing_tables(ri, cnt):
    counts = cnt[0, :MOE_EXPERTS].astype(jnp.int32)
    cum_end = jnp.cumsum(counts)
    cum_start = cum_end - counts
    where = (ri[:, 0:2].reshape(N_ASSIGN), ri[:, 2:4].reshape(N_ASSIGN), cum_start)
    tile_start = jnp.arange(MOE_TILES, dtype=jnp.int32) * MOE_TILE
    expert_start = cum_start[1:]
    tile_rank = jnp.arange(MOE_TILES, dtype=jnp.int32) + jnp.sum(
        (expert_start[None, :] < tile_start[:, None]).astype(jnp.int32), axis=1)
    expert_rank = jnp.arange(MOE_EXPERTS - 1, dtype=jnp.int32) + jnp.minimum(expert_start // MOE_TILE + 1, MOE_TILES)
    values = jnp.concatenate([tile_start, expert_start])
    ranks = jnp.concatenate([tile_rank, expert_rank])
    slot = jnp.arange(MOE_SEGMENTS, dtype=jnp.int32)
    cuts = jnp.sum(jnp.where(ranks[None, :] == slot[:, None], values[None, :], 0), axis=1)
    seg_lo = cuts
    seg_hi = jnp.concatenate([cuts[1:], jnp.full((1,), N_ASSIGN, jnp.int32)])
    seg_tile = jnp.minimum(seg_lo // MOE_TILE, MOE_TILES - 1)
    seg_expert = jnp.minimum(jnp.sum((cum_end[None, :] <= seg_lo[:, None]).astype(jnp.int32), axis=1),
                             MOE_EXPERTS - 1)
    seg_first = jnp.concatenate([jnp.ones((1,), jnp.int32), (seg_tile[1:] != seg_tile[:-1]).astype(jnp.int32)])
    return where, (seg_tile, seg_expert, seg_lo, seg_hi, seg_first)


def _row_copy(src_ref, src_row, dst_ref, dst_row, sem):
    return pltpu.make_async_copy(src_ref.at[pl.ds(src_row, 1), :], dst_ref.at[pl.ds(dst_row, 1), :], sem)


def _dispatch_kernel(expert_ref, rank_ref, start_ref, h_ref, xs_ref, sem):
    base = pl.program_id(0) * DISPATCH_ROWS

    def start(t, carry):
        for k in range(2):
            a = 2 * (base + t) + k
            _row_copy(h_ref, t, xs_ref, start_ref[expert_ref[a]] + rank_ref[a], sem).start()
        return carry

    def wait(t, carry):
        for k in range(2):
            _row_copy(h_ref, 0, xs_ref, 0, sem).wait()
        return carry

    lax.fori_loop(0, DISPATCH_ROWS, start, 0, unroll=4)
    lax.fori_loop(0, DISPATCH_ROWS, wait, 0, unroll=4)


def _dispatch(where, h):
    tm = DISPATCH_ROWS
    return pl.pallas_call(
        _dispatch_kernel,
        grid_spec=pltpu.PrefetchScalarGridSpec(
            num_scalar_prefetch=3,
            grid=(N_TOK // tm,),
            in_specs=[pl.BlockSpec((tm, D_MODEL), lambda i, e, r, s: (i, 0))],
            out_specs=pl.BlockSpec(memory_space=pl.ANY),
            scratch_shapes=[pltpu.SemaphoreType.DMA(())]),
        out_shape=jax.ShapeDtypeStruct((N_ASSIGN, D_MODEL), F32),
        compiler_params=_cparams("arbitrary"),
        name="moe_dispatch",
    )(*where, h)


def _experts_kernel(tile_ref, expert_ref, lo_ref, hi_ref, first_ref, xs_ref, wg_ref, wu_ref, wd_ref, o_ref):
    p = pl.program_id(0)
    lo, hi = lo_ref[p], hi_ref[p]

    @pl.when(first_ref[p] == 1)
    def _():
        o_ref[...] = jnp.zeros_like(o_ref)

    @pl.when(hi > lo)
    def _():
        x = xs_ref[...].astype(BF16)
        hg = _dot(x, wg_ref[...].astype(BF16))
        hu = _dot(x, wu_ref[...].astype(BF16))
        row = tile_ref[p] * MOE_TILE + lax.broadcasted_iota(jnp.int32, (MOE_TILE, 1), 0)
        act = jnp.where((row >= lo) & (row < hi), _silu(hg) * hu, 0.0)
        o_ref[...] += _bdot(act, wd_ref[...])


def _experts(tables, xs, wg, wu, wd, layer):
    weight = lambda shape: pl.BlockSpec((None, None) + shape, lambda p, tile, expert, lo, hi, first: (layer, expert[p], 0, 0))
    rows = pl.BlockSpec((MOE_TILE, D_MODEL), lambda p, tile, expert, lo, hi, first: (tile[p], 0))
    return pl.pallas_call(
        _experts_kernel,
        grid_spec=pltpu.PrefetchScalarGridSpec(
            num_scalar_prefetch=5,
            grid=(MOE_SEGMENTS,),
            in_specs=[rows, weight((D_MODEL, MOE_FF)), weight((D_MODEL, MOE_FF)), weight((MOE_FF, D_MODEL))],
            out_specs=rows),
        out_shape=jax.ShapeDtypeStruct((N_ASSIGN, D_MODEL), F32),
        compiler_params=_cparams("arbitrary"),
        name="moe_experts",
    )(*tables, xs, wg, wu, wd)


def _combine_kernel(expert_ref, rank_ref, start_ref, ys_ref, w_ref, x_ref, mod_ref, gf_ref, o_ref, buf, sem, *, final):
    base = pl.program_id(0) * COMBINE_ROWS

    def start(t, carry):
        for k in range(2):
            a = 2 * (base + t) + k
            _row_copy(ys_ref, start_ref[expert_ref[a]] + rank_ref[a], buf.at[k], t, sem).start()
        return carry

    def wait(t, carry):
        for k in range(2):
            _row_copy(ys_ref, 0, buf.at[k], 0, sem).wait()
        return carry

    lax.fori_loop(0, COMBINE_ROWS, start, 0, unroll=4)
    lax.fori_loop(0, COMBINE_ROWS, wait, 0, unroll=4)
    w = w_ref[...]
    y = x_ref[...] + mod_ref[...] * (w[:, 0:1] * buf[0] + w[:, 1:2] * buf[1])
    if final:
        y = _rms(y, gf_ref[...])
    o_ref[...] = y


def _combine(where, ys, wts, x, mod4, layer, norm_final, final):
    tm = COMBINE_ROWS
    return pl.pallas_call(
        functools.partial(_combine_kernel, final=final),
        grid_spec=pltpu.PrefetchScalarGridSpec(
            num_scalar_prefetch=3,
            grid=(N_TOK // tm,),
            in_specs=[pl.BlockSpec(memory_space=pl.ANY),
                      pl.BlockSpec((tm, LANES), lambda i, e, r, s: (i, 0)),
                      pl.BlockSpec((tm, D_MODEL), lambda i, e, r, s: (i, 0)),
                      pl.BlockSpec((None, None, 1, D_MODEL), lambda i, e, r, s: (layer, _mod_index(i, tm), 0, 5)),
                      pl.BlockSpec((1, D_MODEL), lambda i, e, r, s: (0, 0))],
            out_specs=pl.BlockSpec((tm, D_MODEL), lambda i, e, r, s: (i, 0)),
            scratch_shapes=[pltpu.VMEM((2, tm, D_MODEL), F32), pltpu.SemaphoreType.DMA(())]),
        out_shape=jax.ShapeDtypeStruct((N_TOK, D_MODEL), F32),
        compiler_params=_cparams("arbitrary"),
        name="moe_combine",
    )(*where, ys, wts, x, mod4, norm_final.reshape(1, D_MODEL))


def _rope_tables():
    t = jnp.arange(DEC_SEQ)
    pos = jnp.stack([(t // GRID_W).astype(F32), (t % GRID_W).astype(F32)], axis=1)
    nf = MLA_ROPE // 4
    inv = ROPE_BASE ** (-jnp.arange(nf, dtype=F32) / nf)
    j = jnp.arange(MLA_ROPE)
    ang = pos[:, j // (2 * nf)] * inv[j % nf][None, :]
    sign = jnp.where((j % (2 * nf)) < nf, -1.0, 1.0)
    cos = jnp.pad(jnp.cos(ang), ((0, 0), (MLA_NOPE, LANES - MLA_NOPE - MLA_ROPE)), constant_values=1.0)
    sin = jnp.pad(jnp.sin(ang) * sign, ((0, 0), (MLA_NOPE, LANES - MLA_NOPE - MLA_ROPE)))
    cos = jnp.concatenate([jnp.ones((ROW_TILE, LANES), F32), cos], axis=0)
    sin = jnp.concatenate([jnp.zeros((ROW_TILE, LANES), F32), sin], axis=0)
    return cos, sin


def _pad_heads(w, n_heads, width, lo, hi):
    k = w.shape[0]
    w = w.reshape(k, n_heads, width)[:, :, lo:hi]
    w = jnp.pad(w, ((0, 0), (0, 0), (0, LANES - (hi - lo))))
    return w.reshape(k, n_heads * LANES)


def kernel(x_prompt, x_sample, c, cache_mla_kv, cache_mla_krope, state_mlstm_C, state_mlstm_n, state_mlstm_m, state_ssd, state_gdn, c_ctx, ada_w, ada_b, norm_mix, norm_ffn, w_in_even, ml_i_bias, ml_f_bias, ml_norm, mla_q_norm, mla_w_uq, mla_kv_norm, mla_w_ukv, w_out_even, w_in_odd, ssd_conv_w, ssd_conv_b, ssd_dt_bias, ssd_A_log, ssd_D, ssd_norm, gdn_conv_w, gdn_conv_b, gdn_dt_bias, gdn_A_log, gdn_norm, w_out_odd, moe_w_group, moe_b_group, moe_w_expert, moe_b_expert, moe_w_gate, moe_w_up, moe_w_down, norm_final):
    x = (x_prompt.reshape(N_PROMPT, D_MODEL), x_sample.reshape(N_SAMPLE, D_MODEL))
    cond = jnp.concatenate([c_ctx[None, :], c, jnp.zeros((N_COND - 1 - DEC_BATCH, D_MODEL), F32)], axis=0)
    mod4 = _ada(cond, ada_w, ada_b).reshape(DEPTH, N_COND, 1, 6 * D_MODEL)
    cos_tab, sin_tab = _rope_tables()

    def moe_layer(x, layer, final):
        w_route = jnp.concatenate([moe_w_expert[layer], moe_w_group[layer]], axis=1)
        w_route = jnp.pad(w_route, ((0, 0), (0, LANES - MOE_EXPERTS - MOE_GROUPS)))
        b_route = _lane_row(jnp.concatenate([moe_b_expert[layer], moe_b_group[layer]]))
        h, ri, wts, cnt = _router(x, norm_ffn[layer], mod4, layer, w_route, b_route)
        where, tables = _routing_tables(ri, cnt)
        ys = _experts(tables, _dispatch(where, h), moe_w_gate, moe_w_up, moe_w_down, layer)
        return _combine(where, ys, wts, x, mod4, layer, norm_final, final)

    e = 0
    w = w_in_even[e]
    off = np.cumsum([0, 4 * ML_HEADS * ML_DK, 2 * ML_HEADS, 2 * ML_HEADS, MLA_Q_RANK, MLA_KV_RANK, MLA_ROPE])
    w_misc = jnp.concatenate([w[:, off[1]:off[3]],
                              jnp.zeros((D_MODEL, MLA_NOPE - 4 * ML_HEADS), F32),
                              w[:, off[5]:off[6]],
                              jnp.zeros((D_MODEL, LANES - MLA_NOPE - MLA_ROPE), F32)], axis=1)
    weights = [w[:, :off[1]].astype(BF16), w[:, off[3]:off[4]].astype(BF16),
               w[:, off[4]:off[5]].astype(BF16), w_misc.astype(BF16)]
    qkvo, cq, ckv, misc0 = _inproj(x, norm_mix[0], mod4, 0, weights)

    hm_p, st_c, st_n, st_m = _mlstm(qkvo, misc0, ml_i_bias[e], ml_f_bias[e], ml_norm[e], prompt=True)
    (hm_s,) = _mlstm(qkvo, misc0, ml_i_bias[e], ml_f_bias[e], ml_norm[e], prompt=False,
                     init=(state_mlstm_C[:, e], state_mlstm_n[:, e], state_mlstm_m[:, e]))

    dq = MLA_NOPE + MLA_ROPE
    w_uq_pad = _pad_heads(mla_w_uq[e], MLA_HEADS, dq, 0, dq).astype(BF16)
    w_uk_pad = _pad_heads(mla_w_ukv[e], MLA_HEADS, MLA_NOPE + MLA_V, 0, MLA_NOPE).astype(BF16)
    w_uv = mla_w_ukv[e].reshape(MLA_KV_RANK, MLA_HEADS, MLA_NOPE + MLA_V)[:, :, MLA_NOPE:]
    w_uv = w_uv.reshape(MLA_KV_RANK, MLA_HEADS * MLA_V).astype(BF16)
    q_cat = _mla_q(cq, mla_q_norm[e], w_uq_pad, cos_tab, sin_tab)
    ckv_n, k_cat, v_all = _mla_kv(ckv, misc0, w_uk_pad, w_uv, mla_kv_norm[e], cos_tab, sin_tab)
    cache_kpe = jnp.pad(cache_mla_krope[:, e].reshape(DEC_BATCH * PAST_LEN, MLA_ROPE),
                        ((0, 0), (MLA_NOPE, LANES - MLA_NOPE - MLA_ROPE)))
    k_cache, v_cache = _mla_kv(cache_mla_kv[:, e].reshape(DEC_BATCH * PAST_LEN, MLA_KV_RANK), cache_kpe,
                               w_uk_pad, w_uv)
    att_p = _attention(q_cat, k_cat, v_all, B=BATCH, Tq=SEQ, Tk=SEQ, tq=SEQ, q_row0=0, heads=MLA_HEADS)

    def with_cache(cache, new):
        width = new.shape[1]
        both = jnp.concatenate([cache.reshape(DEC_BATCH, PAST_LEN, width),
                                new[N_PROMPT:].reshape(DEC_BATCH, DEC_SEQ, width)], axis=1)
        return both.reshape(DEC_BATCH * (PAST_LEN + DEC_SEQ), width)

    att_s = _attention(q_cat, with_cache(k_cache, k_cat), with_cache(v_cache, v_all), B=DEC_BATCH, Tq=DEC_SEQ,
                       Tk=PAST_LEN + DEC_SEQ, tq=256, q_row0=N_PROMPT, heads=2)
    wo = w_out_even[e].astype(BF16)
    x = _outproj((hm_p, hm_s), (att_p, att_s), wo[:ML_HEADS * ML_DV], wo[ML_HEADS * ML_DV:], x, mod4, 0)
    x = moe_layer(x, 0, final=False)

    oi = 0
    w = w_in_odd[oi]
    ssd_w = SSD_HEADS * SSD_P
    ssd_cc = ssd_w + 2 * SSD_GROUPS * SSD_N
    gdn_w = GDN_HEADS * GDN_DK
    off = np.cumsum([0, ssd_w, ssd_cc, 2 * SSD_HEADS, 3 * gdn_w, gdn_w, 2 * GDN_HEADS, 2 * GDN_HEADS])
    w_misc = jnp.concatenate([w[:, off[2]:off[3]], w[:, off[5]:off[7]],
                              jnp.zeros((D_MODEL, LANES - 2 * SSD_HEADS - 4 * GDN_HEADS), F32)], axis=1)
    weights = [w[:, off[0]:off[1]].astype(BF16), w[:, off[1]:off[2]].astype(BF16),
               w[:, off[3]:off[4]].astype(BF16), w[:, off[4]:off[5]].astype(BF16), w_misc.astype(BF16)]
    z_s, xbc, qkv_g, z_g, misc = _inproj(x, norm_mix[1], mod4, 1, weights)

    ssd_args = (xbc, misc, ssd_conv_w[oi], ssd_conv_b[oi], ssd_dt_bias[oi], ssd_A_log[oi], ssd_D[oi])
    ys_p, st_ssd = _ssd(*ssd_args, prompt=True)
    (ys_s,) = _ssd(*ssd_args, prompt=False, init=state_ssd[:, oi])

    lo = 2 * SSD_HEADS
    gdn_dt_row = jnp.pad(gdn_dt_bias[oi].reshape(-1), (lo, LANES - lo - 2 * GDN_HEADS)).reshape(1, LANES)
    gdn_alog_row = jnp.pad(gdn_A_log[oi].reshape(-1), (lo, LANES - lo - 2 * GDN_HEADS)).reshape(1, LANES)
    gdn_args = (qkv_g, misc, z_g, gdn_conv_w[oi], gdn_conv_b[oi], gdn_dt_row, gdn_alog_row, gdn_norm[oi])
    og_p, st_gdn = _gdn(*gdn_args, prompt=True)
    (og_s,) = _gdn(*gdn_args, prompt=False, init=state_gdn[:, oi])

    wo = w_out_odd[oi].astype(BF16)
    x = _outproj((ys_p, ys_s), (og_p, og_s), wo[:ssd_w], wo[ssd_w:], x, mod4, 1, z=z_s, gnorm=ssd_norm[oi])
    x = moe_layer(x, 1, final=True)

    y_prompt = x[:N_PROMPT].reshape(BATCH, SEQ, D_MODEL)
    y_sample = x[N_PROMPT:].reshape(DEC_BATCH, DEC_SEQ, D_MODEL)
    new_mla_kv = ckv_n[:N_PROMPT].reshape(BATCH, 1, SEQ, MLA_KV_RANK)
    new_mla_krope = misc0[:N_PROMPT, MLA_NOPE:MLA_NOPE + MLA_ROPE].reshape(BATCH, 1, SEQ, MLA_ROPE)
    return (y_prompt, y_sample, new_mla_kv, new_mla_krope, st_c[:, None], st_n.reshape(BATCH, 1, 2, ML_HEADS, ML_DK),
            st_m.reshape(BATCH, 1, 2, ML_HEADS), st_ssd[:, None], st_gdn[:, None])
```

```python
import functools

import numpy as np
import jax
import jax.numpy as jnp
from jax import lax
from jax.experimental import pallas as pl
from jax.experimental.pallas import tpu as pltpu

F32 = jnp.float32
BF16 = jnp.bfloat16

D_MODEL = 1024
BATCH = 32
SEQ = 256
DEPTH = 2
DEC_BATCH = 2
DEC_SEQ = 2048
PAST_LEN = 256
GRID_W = 64
EPS = 1e-6
ML_HEADS = 4
ML_DK = 128
ML_DV = 128
MLA_HEADS = 8
MLA_Q_RANK = 384
MLA_KV_RANK = 256
MLA_NOPE = 64
MLA_ROPE = 32
MLA_V = 64
ROPE_BASE = 10000.0
SSD_HEADS = 8
SSD_P = 64
SSD_GROUPS = 2
SSD_N = 128
GDN_HEADS = 4
GDN_DK = 128
GDN_DV = 128
CONV_W = 5
MOE_GROUPS = 4
MOE_PER_GROUP = 8
MOE_EXPERTS = 32
MOE_FF = 256

N_PROMPT = BATCH * SEQ
N_SAMPLE = DEC_BATCH * DEC_SEQ
N_TOK = N_PROMPT + N_SAMPLE
N_COND = 8

LANES = 128
CHUNK = 64
ROW_TILE = 512
CONV_HALO = 8
CONV_ROWS = 256
VMEM_LIMIT = 56 * 1024 * 1024
PRE_CHUNKS = 8
STEP_CHUNKS = 4
GDN_PROMPT_SEQS_PER_STEP = 4

assert LANES == 2 * CHUNK and SSD_P == CHUNK

NEG_INF = float("-inf")


def _cparams(*sem):
    return pltpu.CompilerParams(dimension_semantics=sem, vmem_limit_bytes=VMEM_LIMIT)


def _dot(a, b):
    return jnp.dot(a, b, preferred_element_type=F32)


def _dot_nt(a, b):
    return lax.dot_general(a, b, (((1,), (1,)), ((), ())), preferred_element_type=F32)


def _dot_tn(a, b):
    return lax.dot_general(a, b, (((0,), (0,)), ((), ())), preferred_element_type=F32)


def _bdot(a, b):
    return _dot(a.astype(BF16), b.astype(BF16))


def _rms(x, g):
    return x * lax.rsqrt(jnp.mean(x * x, axis=-1, keepdims=True) + EPS) * g


def _softplus(x):
    return jnp.maximum(x, 0.0) + jnp.log1p(jnp.exp(-jnp.abs(x)))


def _silu(x):
    return x * jax.nn.sigmoid(x)


def _colsel(x, j):
    lane = lax.broadcasted_iota(jnp.int32, x.shape, 1)
    return jnp.sum(jnp.where(lane == j, x, 0.0), axis=1, keepdims=True)


def _lanes(x):
    return jnp.broadcast_to(x, (x.shape[0], LANES))


def _half_masks(rev_lo, rev_hi):
    t = lax.broadcasted_iota(jnp.int32, (CHUNK, LANES), 0)
    lane = lax.broadcasted_iota(jnp.int32, (CHUNK, LANES), 1)
    s = lane & (CHUNK - 1)
    hi = lane >= CHUNK

    def pick(fwd, bwd):
        if rev_lo == rev_hi:
            return bwd if rev_lo else fwd
        on_hi, on_lo = (bwd, fwd) if rev_hi else (fwd, bwd)
        return (hi & on_hi) | (jnp.logical_not(hi) & on_lo)

    return dict(hi=hi, t=t, s=s, eye=(s == t), incl=pick(s <= t, s >= t), incl_t=pick(t <= s, t >= s),
                strict=pick(s < t, s > t))


def _pack(hm, col_lo, col_hi):
    return jnp.where(hm["hi"], col_hi, col_lo)


def _rows(hm, cols):
    return jnp.sum(jnp.where(hm["eye"], cols, 0.0), axis=0, keepdims=True)


def _half_sums(hm, x):
    lo = jnp.sum(jnp.where(hm["hi"], 0.0, x), axis=1, keepdims=True)
    hi = jnp.sum(jnp.where(hm["hi"], x, 0.0), axis=1, keepdims=True)
    return lo, hi


def _half_cum(hm, col_lo, col_hi):
    cols = _pack(hm, col_lo, col_hi)
    cum_lo, cum_hi = _half_sums(hm, jnp.where(hm["incl"], _rows(hm, cols), 0.0))
    cum_rows = jnp.sum(jnp.where(hm["incl_t"], cols, 0.0), axis=0, keepdims=True)
    return cum_lo, cum_hi, cum_rows


def _blockdiag(hm, x):
    return jnp.concatenate([jnp.where(hm["hi"], 0.0, x).astype(BF16),
                            jnp.where(hm["hi"], x, 0.0).astype(BF16)], axis=0)


def _chunk_rows(c):
    if isinstance(c, int):
        return pl.ds(c * CHUNK, CHUNK)
    return pl.ds(pl.multiple_of(c * CHUNK, CHUNK), CHUNK)


def _block_loop(n, body, size):
    if n <= size:
        body(list(range(n)))
        return

    def block(blk, carry):
        body([blk * size + j for j in range(size)])
        return carry

    lax.fori_loop(0, n // size, block, 0)


def _mod_index(i, rows_per_tile):
    p_tiles = N_PROMPT // rows_per_tile
    s_tiles = DEC_SEQ // rows_per_tile
    return jnp.where(i < p_tiles, 0, 1 + (i - p_tiles) // s_tiles)


def _ada_kernel(c_ref, w_ref, b_ref, o_ref):
    c = c_ref[...]
    o_ref[...] = _bdot(_silu(c), w_ref[...]) + b_ref[...]


def _ada(cond, ada_w, ada_b):
    nb = 6
    return pl.pallas_call(
        _ada_kernel,
        grid=(DEPTH, nb),
        in_specs=[pl.BlockSpec((N_COND, D_MODEL), lambda l, j: (0, 0)),
                  pl.BlockSpec((None, D_MODEL, D_MODEL), lambda l, j: (l, 0, j)),
                  pl.BlockSpec((None, 1, D_MODEL), lambda l, j: (l, 0, j))],
        out_specs=pl.BlockSpec((None, N_COND, D_MODEL), lambda l, j: (l, 0, j)),
        out_shape=jax.ShapeDtypeStruct((DEPTH, N_COND, 6 * D_MODEL), F32),
        compiler_params=_cparams("arbitrary", "arbitrary"),
        name="ada",
    )(cond, ada_w, ada_b.reshape(DEPTH, 1, 6 * D_MODEL))


def _token_operand(x, tm):
    if not isinstance(x, tuple):
        return [pl.BlockSpec((tm, x.shape[1]), lambda i: (i, 0))], [x]
    pt = N_PROMPT // tm
    width = x[0].shape[1]
    return ([pl.BlockSpec((tm, width), lambda i: (jnp.minimum(i, pt - 1), 0)),
             pl.BlockSpec((tm, width), lambda i: (jnp.maximum(i - pt, 0), 0))], list(x))


def _take_tile(refs, split, tm):
    if not split:
        return refs.pop(0)[...]
    p_ref, s_ref = refs.pop(0), refs.pop(0)
    return jnp.where(pl.program_id(0) < N_PROMPT // tm, p_ref[...], s_ref[...])


def _inproj_kernel(*refs, n_out, split, tm):
    refs = list(refs)
    x = _take_tile(refs, split, tm)
    g_ref, mod_ref = refs[:2]
    w_refs, o_refs = refs[2:2 + n_out], refs[2 + n_out:]
    mod = mod_ref[...]
    h = _rms(x, g_ref[...]) * (1.0 + mod[:, D_MODEL:2 * D_MODEL]) + mod[:, :D_MODEL]
    hb = h.astype(BF16)
    for w_ref, o_ref in zip(w_refs, o_refs):
        o_ref[...] = _dot(hb, w_ref[...])


def _inproj(x, gain, mod4, layer, weights):
    n_out = len(weights)
    tm = ROW_TILE
    in_specs, args = _token_operand(x, tm)
    in_specs += [pl.BlockSpec((1, D_MODEL), lambda i: (0, 0)),
                 pl.BlockSpec((None, None, 1, 2 * D_MODEL), lambda i: (layer, _mod_index(i, tm), 0, 0))]
    in_specs += [pl.BlockSpec(w.shape, lambda i: (0, 0)) for w in weights]
    return pl.pallas_call(
        functools.partial(_inproj_kernel, n_out=n_out, split=isinstance(x, tuple), tm=tm),
        grid=(N_TOK // tm,),
        in_specs=in_specs,
        out_specs=[pl.BlockSpec((tm, w.shape[1]), lambda i: (i, 0)) for w in weights],
        out_shape=[jax.ShapeDtypeStruct((N_TOK, w.shape[1]), F32) for w in weights],
        compiler_params=_cparams("arbitrary"),
        name="inproj",
    )(*args, gain.reshape(1, D_MODEL), mod4, *weights)


def _mlstm_kernel(*refs, T, NS, has_init, emit_state):
    it = iter(refs)
    ib_ref, fb_ref = next(it), next(it)
    m0_ref = next(it) if has_init else None
    q_ref, k_ref, v_ref, o_ref, misc_ref, gn_ref = (next(it) for _ in range(6))
    c0_ref, n0_ref = (next(it), next(it)) if has_init else (None, None)
    hm_ref = next(it)
    cout_ref, nout_ref, mout_ref = (next(it), next(it), next(it)) if emit_state else (None, None, None)
    (hb_scr, c_scr, n_scr, m_scr, num_scr, st_scr, kv_scr, nl_scr, bl_scr, gm_scr) = (next(it) for _ in range(10))

    b = pl.program_id(0)
    h = pl.program_id(1)
    nc = T // CHUNK
    scale = ML_DK ** -0.5

    for s in range(NS):
        for d in range(2):
            if has_init:
                c_scr[2 * s + d] = c0_ref[s, d]
                n_scr[2 * s + d] = n0_ref[s, d]
                m_scr[2 * s + d] = jnp.full((1, LANES), m0_ref[b * NS + s, d, h], F32)
            else:
                c_scr[2 * s + d] = jnp.zeros((ML_DK, ML_DV), F32)
                n_scr[2 * s + d] = jnp.zeros((1, ML_DK), F32)
                m_scr[2 * s + d] = jnp.zeros((1, LANES), F32)

    hm = _half_masks(False, True)

    def pre_gates(c):
        rows = _chunk_rows(c)
        mi = misc_ref[rows, :]
        li = [_colsel(mi, d * ML_HEADS + h) + ib_ref[d, h] for d in range(2)]
        lf = [-_softplus(-(_colsel(mi, 2 * ML_HEADS + d * ML_HEADS + h) + fb_ref[d, h])) for d in range(2)]
        b_cols = [None, None]
        b_cols[0], b_cols[1], b_rows = _half_cum(hm, lf[0], lf[1])
        li_rows = _rows(hm, _pack(hm, li[0], li[1]))
        dm = jnp.where(hm["incl"], _pack(hm, b_cols[0], b_cols[1]) - b_rows + li_rows, NEG_INF)
        mloc = [jnp.max(jnp.where(hm["hi"], NEG_INF, dm), axis=1, keepdims=True),
                jnp.max(jnp.where(hm["hi"], dm, NEG_INF), axis=1, keepdims=True)]
        kc = k_ref[rows, :]
        kws = []
        for d in range(2):
            bl = jnp.sum(lf[d], axis=0, keepdims=True)
            g = bl - b_cols[d] + li[d]
            gmax = jnp.max(g, axis=0, keepdims=True)
            kw = jnp.exp(g - gmax) * kc
            kws.append(kw.astype(BF16))
            nl_scr[d, c] = jnp.sum(kw, axis=0, keepdims=True)
            bl_scr[d, c] = _lanes(bl)
            gm_scr[d, c] = _lanes(gmax)
            st_scr[d, 0, rows, :] = _lanes(mloc[d])
            st_scr[d, 1, rows, :] = _lanes(b_cols[d])
        return dict(c=c, rows=rows, kc=kc.astype(BF16), kw2=jnp.concatenate(kws, axis=1),
                    e2=jnp.exp(dm - _pack(hm, mloc[0], mloc[1])))

    def pre_block(chunks):
        chains = [pre_gates(c) for c in chunks]
        for ch in chains:
            qc = (q_ref[ch["rows"], :] * scale).astype(BF16)
            ch["vb"] = v_ref[ch["rows"], :].astype(BF16)
            ch["qk2"] = _dot_nt(qc, jnp.concatenate([ch["kc"], ch["kc"]], axis=0))
            kv2 = _dot_tn(ch["kw2"], ch["vb"])
            kv_scr[0, ch["c"]] = kv2[:ML_DK]
            kv_scr[1, ch["c"]] = kv2[ML_DK:]
        for ch in chains:
            s2 = ch["qk2"] * ch["e2"]
            ch["s2"] = s2.astype(BF16)
            dens = _half_sums(hm, s2)
            for d in range(2):
                st_scr[d, 2, ch["rows"], :] = _lanes(dens[d])
        for ch in chains:
            vb = ch["vb"]
            zero = jnp.zeros_like(vb)
            vbd = jnp.concatenate([jnp.concatenate([vb, zero], axis=1),
                                   jnp.concatenate([zero, vb], axis=1)], axis=0)
            num2 = _dot(ch["s2"], vbd)
            num_scr[0, ch["rows"], :] = num2[:, :ML_DV]
            num_scr[1, ch["rows"], :] = num2[:, ML_DV:]

    def step_block(chunks):
        states = [(m_scr[j], c_scr[j], n_scr[j]) for j in range(2 * NS)]
        pairs = []
        for c in chunks:
            for s in range(NS):
                pair = []
                for d in range(2):
                    cc = s * nc + (c if d == 0 else nc - 1 - c)
                    rows = _chunk_rows(cc)
                    m, c_st, n_st = states[2 * s + d]
                    mloc, b_col, den_loc = st_scr[d, 0, rows, :], st_scr[d, 1, rows, :], st_scr[d, 2, rows, :]
                    bl, gmax = bl_scr[d, cc], gm_scr[d, cc]
                    qc = q_ref[rows, :] * scale
                    inter = b_col + m
                    mq = jnp.maximum(inter, mloc)
                    a = jnp.exp(inter - mq)
                    f = jnp.exp(mloc - mq)
                    den = f * den_loc + a * jnp.sum(qc * n_st, axis=1, keepdims=True)
                    pair.append(dict(rows=rows, qc=qc.astype(BF16), c_st=c_st.astype(BF16), a=a, f=f,
                                     inv=1.0 / jnp.maximum(jnp.abs(den), jnp.exp(-mq))))
                    m_new = jnp.maximum(bl + m, gmax)
                    dec = jnp.exp(bl + m - m_new)
                    fk = jnp.exp(gmax - m_new)
                    states[2 * s + d] = (m_new, dec * c_st + fk * kv_scr[d, cc], dec * n_st + fk * nl_scr[d, cc])
                pairs.append(pair)
        for j in range(2 * NS):
            m_scr[j], c_scr[j], n_scr[j] = states[j]
        for pair in pairs:
            lhs = jnp.concatenate([pair[0]["qc"], pair[1]["qc"]], axis=0)
            rhs = jnp.concatenate([pair[0]["c_st"], pair[1]["c_st"]], axis=1)
            pair.append(_dot(lhs, rhs))
        for pair in pairs:
            res = pair[2]
            for d, dst in enumerate((hm_ref, hb_scr)):
                it_ = pair[d]
                qc_c = res[d * CHUNK:(d + 1) * CHUNK, d * ML_DV:(d + 1) * ML_DV]
                dst[it_["rows"], :] = (it_["f"] * num_scr[d, it_["rows"], :] + it_["a"] * qc_c) * it_["inv"]

    _block_loop(NS * nc, pre_block, PRE_CHUNKS)
    _block_loop(nc, step_block, STEP_CHUNKS)

    hs = hm_ref[...] + hb_scr[...]
    hm_ref[...] = _rms(hs, gn_ref[...]) * jax.nn.sigmoid(o_ref[...])
    if emit_state:
        for s in range(NS):
            for d in range(2):
                cout_ref[s, d] = c_scr[2 * s + d]
                nout_ref[s, d] = n_scr[2 * s + d]
                mout_ref[s, d] = m_scr[2 * s + d][:, 0:1]


def _seq_geometry(prompt, prompt_seqs_per_step=1):
    if prompt:
        return SEQ, BATCH, prompt_seqs_per_step, 0
    return DEC_SEQ, DEC_BATCH, 1, N_PROMPT // DEC_SEQ


def _mlstm(qkvo, misc, i_bias, f_bias, gnorm, *, prompt, init=None):
    T, B, NS, blk0 = _seq_geometry(prompt)
    R = NS * T
    nc = R // CHUNK
    has_init = init is not None
    smem = pl.BlockSpec(memory_space=pltpu.SMEM)

    def col(j):
        return pl.BlockSpec((R, LANES), lambda b, h: (blk0 + b, j * ML_HEADS + h))

    in_specs = [smem, smem]
    args = [i_bias, f_bias]
    if has_init:
        in_specs.append(smem)
        args.append(init[2])
    in_specs += [col(0), col(1), col(2), col(3),
                 pl.BlockSpec((R, LANES), lambda b, h: (blk0 + b, 0)),
                 pl.BlockSpec((None, 1, ML_DV), lambda b, h: (h, 0, 0))]
    args += [qkvo, qkvo, qkvo, qkvo, misc, gnorm.reshape(ML_HEADS, 1, ML_DV)]
    c_spec = pl.BlockSpec((NS, 2, None, ML_DK, ML_DV), lambda b, h: (b, 0, h, 0, 0))
    n_spec = pl.BlockSpec((NS, 2, None, 1, ML_DK), lambda b, h: (b, 0, h, 0, 0))
    if has_init:
        in_specs += [c_spec, n_spec]
        args += [init[0], init[1].reshape(B, 2, ML_HEADS, 1, ML_DK)]
    out_specs = [pl.BlockSpec((R, LANES), lambda b, h: (b, h))]
    out_shape = [jax.ShapeDtypeStruct((B * T, ML_HEADS * ML_DV), F32)]
    if prompt:
        out_specs += [c_spec, n_spec, pl.BlockSpec((NS, 2, None, 1, 1), lambda b, h: (b, 0, h, 0, 0))]
        out_shape += [jax.ShapeDtypeStruct((B, 2, ML_HEADS, ML_DK, ML_DV), F32),
                      jax.ShapeDtypeStruct((B, 2, ML_HEADS, 1, ML_DK), F32),
                      jax.ShapeDtypeStruct((B, 2, ML_HEADS, 1, 1), F32)]
    return pl.pallas_call(
        functools.partial(_mlstm_kernel, T=T, NS=NS, has_init=has_init, emit_state=prompt),
        grid=(B // NS, ML_HEADS),
        in_specs=in_specs,
        out_specs=out_specs,
        out_shape=out_shape,
        scratch_shapes=[pltpu.VMEM((R, ML_DV), F32),
                        pltpu.VMEM((2 * NS, ML_DK, ML_DV), F32),
                        pltpu.VMEM((2 * NS, 1, ML_DK), F32),
                        pltpu.VMEM((2 * NS, 1, LANES), F32),
                        pltpu.VMEM((2, R, ML_DV), F32),
                        pltpu.VMEM((2, 3, R, LANES), F32),
                        pltpu.VMEM((2, nc, ML_DK, ML_DV), F32),
                        pltpu.VMEM((2, nc, 1, ML_DK), F32),
                        pltpu.VMEM((2, nc, 1, LANES), F32),
                        pltpu.VMEM((2, nc, 1, LANES), F32)],
        compiler_params=_cparams("arbitrary", "arbitrary"),
        name="mlstm_prompt" if prompt else "mlstm_sample",
    )(*args)


def _rope(x, cos, sin_signed):
    lane = lax.broadcasted_iota(jnp.int32, x.shape, 1)
    first = (lane & 15) < 8
    partner = jnp.where(first, pltpu.roll(x, LANES - 8, axis=1), pltpu.roll(x, 8, axis=1))
    return x * cos + partner * sin_signed


def _q_kernel(cq_ref, g_ref, w_ref, cos_ref, sin_ref, q_ref):
    cq = _rms(cq_ref[...], g_ref[...])
    y = _bdot(cq, w_ref[...])
    cos, sin = cos_ref[...], sin_ref[...]
    for hd in range(MLA_HEADS):
        sl = slice(hd * LANES, (hd + 1) * LANES)
        q_ref[:, sl] = _rope(y[:, sl], cos, sin).astype(BF16)


def _rope_block_index(i, tm):
    p_tiles = N_PROMPT // tm
    s_tiles = DEC_SEQ // tm
    return jnp.where(i < p_tiles, 0, 1 + (i - p_tiles) % s_tiles)


def _mla_q(cq, gain, w_uq_pad, cos_tab, sin_tab):
    tm = ROW_TILE
    tab = pl.BlockSpec((tm, LANES), lambda i: (_rope_block_index(i, tm), 0))
    return pl.pallas_call(
        _q_kernel,
        grid=(N_TOK // tm,),
        in_specs=[pl.BlockSpec((tm, MLA_Q_RANK), lambda i: (i, 0)),
                  pl.BlockSpec((1, MLA_Q_RANK), lambda i: (0, 0)),
                  pl.BlockSpec(w_uq_pad.shape, lambda i: (0, 0)),
                  tab, tab],
        out_specs=pl.BlockSpec((tm, MLA_HEADS * LANES), lambda i: (i, 0)),
        out_shape=jax.ShapeDtypeStruct((N_TOK, MLA_HEADS * LANES), BF16),
        compiler_params=_cparams("arbitrary"),
        name="mla_q",
    )(cq, gain.reshape(1, MLA_Q_RANK), w_uq_pad, cos_tab, sin_tab)


def _kv_kernel(*refs, norm):
    if norm:
        ckv_ref, kpe_ref, g_ref, wk_ref, wv_ref, cos_ref, sin_ref, ckvn_ref, k_ref, v_ref = refs
        c = _rms(ckv_ref[...], g_ref[...])
        ckvn_ref[...] = c
    else:
        ckv_ref, kpe_ref, wk_ref, wv_ref, k_ref, v_ref = refs
        c = ckv_ref[...]
    kp = kpe_ref[...]
    lane = lax.broadcasted_iota(jnp.int32, kp.shape, 1)
    kp = jnp.where((lane >= MLA_NOPE) & (lane < MLA_NOPE + MLA_ROPE), kp, 0.0)
    if norm:
        kp = _rope(kp, cos_ref[...], sin_ref[...])
    kn = _bdot(c, wk_ref[...])
    for hd in range(MLA_HEADS):
        sl = slice(hd * LANES, (hd + 1) * LANES)
        k_ref[:, sl] = (kn[:, sl] + kp).astype(BF16)
    v_ref[...] = _bdot(c, wv_ref[...]).astype(BF16)


def _mla_kv(ckv, kpe128, w_uk_pad, w_uv, gain=None, cos_tab=None, sin_tab=None):
    norm = gain is not None
    n = ckv.shape[0]
    tm = ROW_TILE
    row = lambda w: pl.BlockSpec((tm, w), lambda i: (i, 0))
    full = lambda a: pl.BlockSpec(a.shape, lambda i: (0, 0))
    in_specs = [row(MLA_KV_RANK), row(LANES)]
    args = [ckv, kpe128]
    if norm:
        in_specs.append(pl.BlockSpec((1, MLA_KV_RANK), lambda i: (0, 0)))
        args.append(gain.reshape(1, MLA_KV_RANK))
    in_specs += [full(w_uk_pad), full(w_uv)]
    args += [w_uk_pad, w_uv]
    out_specs = [row(MLA_HEADS * LANES), row(MLA_HEADS * MLA_V)]
    out_shape = [jax.ShapeDtypeStruct((n, MLA_HEADS * LANES), BF16),
                 jax.ShapeDtypeStruct((n, MLA_HEADS * MLA_V), BF16)]
    if norm:
        tab = pl.BlockSpec((tm, LANES), lambda i: (_rope_block_index(i, tm), 0))
        in_specs += [tab, tab]
        args += [cos_tab, sin_tab]
        out_specs = [row(MLA_KV_RANK)] + out_specs
        out_shape = [jax.ShapeDtypeStruct((n, MLA_KV_RANK), F32)] + out_shape
    return pl.pallas_call(
        functools.partial(_kv_kernel, norm=norm),
        grid=(n // tm,),
        in_specs=in_specs,
        out_specs=out_specs,
        out_shape=out_shape,
        compiler_params=_cparams("arbitrary"),
        name="mla_kv" if norm else "mla_kv_cache",
    )(*args)


def _attn_kernel(q_ref, k_ref, v_ref, o_ref, *, heads):
    scale = (MLA_NOPE + MLA_ROPE) ** -0.5
    scores = [_dot_nt(q_ref[:, j * LANES:(j + 1) * LANES], k_ref[:, j * LANES:(j + 1) * LANES]) * scale
              for j in range(heads)]
    probs, sums = [], []
    for s in scores:
        p = jnp.exp(s - jnp.max(s, axis=1, keepdims=True))
        sums.append(jnp.sum(p, axis=1, keepdims=True))
        probs.append(p.astype(BF16))
    pvs = [_dot(probs[j], v_ref[:, j * MLA_V:(j + 1) * MLA_V]) for j in range(heads)]
    o_ref[...] = jnp.concatenate([pvs[j] / sums[j] for j in range(heads)], axis=1)


def _attention(q, k, v, *, B, Tq, Tk, tq, q_row0, heads):
    nq = Tq // tq
    blk0 = q_row0 // tq
    return pl.pallas_call(
        functools.partial(_attn_kernel, heads=heads),
        grid=(B, MLA_HEADS // heads, nq),
        in_specs=[pl.BlockSpec((tq, heads * LANES), lambda b, hp, i: (blk0 + b * nq + i, hp)),
                  pl.BlockSpec((Tk, heads * LANES), lambda b, hp, i: (b, hp)),
                  pl.BlockSpec((Tk, heads * MLA_V), lambda b, hp, i: (b, hp))],
        out_specs=pl.BlockSpec((tq, heads * MLA_V), lambda b, hp, i: (b * nq + i, hp)),
        out_shape=jax.ShapeDtypeStruct((B * Tq, MLA_HEADS * MLA_V), F32),
        compiler_params=_cparams("arbitrary", "arbitrary", "arbitrary"),
        name="attention",
    )(q, k, v)


def _outproj_kernel(*refs, odd, split_x, tm):
    refs = list(refs)
    a1 = _take_tile(refs, True, tm)
    a2 = _take_tile(refs, True, tm)
    x = _take_tile(refs, split_x, tm)
    if odd:
        z_ref, gn_ref = refs.pop(0), refs.pop(0)
        a1 = _rms(a1 * _silu(z_ref[...]), gn_ref[...])
    w1_ref, w2_ref, g1_ref, o_ref = refs
    out = _bdot(a1, w1_ref[...]) + _bdot(a2, w2_ref[...])
    o_ref[...] = x + g1_ref[...] * out


def _outproj(a1, a2, w1, w2, x, mod4, layer, z=None, gnorm=None):
    odd = z is not None
    tm = ROW_TILE
    half = w1.shape[0]
    full = lambda a: pl.BlockSpec(a.shape, lambda i: (0, 0))
    in_specs, args = [], []
    for operand in (a1, a2, x):
        specs, arrays = _token_operand(operand, tm)
        in_specs += specs
        args += arrays
    if odd:
        in_specs += [pl.BlockSpec((tm, half), lambda i: (i, 0)), pl.BlockSpec((1, half), lambda i: (0, 0))]
        args += [z, gnorm.reshape(1, half)]
    in_specs += [full(w1), full(w2),
                 pl.BlockSpec((None, None, 1, D_MODEL), lambda i: (layer, _mod_index(i, tm), 0, 2))]
    args += [w1, w2, mod4]
    return pl.pallas_call(
        functools.partial(_outproj_kernel, odd=odd, split_x=isinstance(x, tuple), tm=tm),
        grid=(N_TOK // tm,),
        in_specs=in_specs,
        out_specs=pl.BlockSpec((tm, D_MODEL), lambda i: (i, 0)),
        out_shape=jax.ShapeDtypeStruct((N_TOK, D_MODEL), F32),
        compiler_params=_cparams("arbitrary"),
        name="outproj",
    )(*args)


def _conv_silu(src_ref, dst_ref, pad_ref, w_ref, b_ref, T, NS):
    width = src_ref.shape[1]
    zeros = jnp.zeros((CONV_HALO, width), F32)
    pad_ref[pl.ds(0, CONV_HALO), :] = zeros
    pad_ref[pl.ds(CONV_HALO + T, CONV_HALO), :] = zeros
    w = w_ref[...]
    bias = b_ref[...]
    for s in range(NS):
        pad_ref[pl.ds(CONV_HALO, T), :] = src_ref[pl.ds(s * T, T), :]
        for r0 in range(0, T, CONV_ROWS):
            acc = bias
            for j in range(CONV_W):
                start = r0 + CONV_HALO + j - CONV_W // 2
                acc = acc + w[j:j + 1, :] * pad_ref[pl.ds(start, CONV_ROWS), :]
            dst_ref[pl.ds(s * T + r0, CONV_ROWS), :] = _silu(acc)


def _ssd_kernel(*refs, T, NS, has_init, emit_state):
    it = iter(refs)
    x_ref, b_ref, c_ref = next(it), next(it), next(it)
    wx_ref, wb_ref, wc_ref = next(it), next(it), next(it)
    bx_ref, bb_ref, bc_ref = next(it), next(it), next(it)
    misc_ref, dtb_ref, alog_ref, dskip_ref = next(it), next(it), next(it), next(it)
    h0_ref = next(it) if has_init else None
    y_ref = next(it)
    hout_ref = next(it) if emit_state else None
    (pad_scr, xc_scr, bc_scr, cc_scr, yb_scr, hs_scr, yi_scr, ea_scr, upd_scr, eal_scr) = (
        next(it) for _ in range(10))

    p = pl.program_id(1)
    nc = T // CHUNK

    _conv_silu(x_ref, xc_scr, pad_scr, wx_ref, bx_ref, T, NS)
    _conv_silu(b_ref, bc_scr, pad_scr, wb_ref, bb_ref, T, NS)
    _conv_silu(c_ref, cc_scr, pad_scr, wc_ref, bc_ref, T, NS)

    for s in range(NS):
        for d in range(2):
            if has_init:
                hs_scr[2 * s + d] = h0_ref[s, d].reshape(2 * SSD_P, SSD_N).T
            else:
                hs_scr[2 * s + d] = jnp.zeros((SSD_N, 2 * SSD_P), F32)

    dtb = dtb_ref[...]
    neg_a = -jnp.exp(alog_ref[...])
    hms = [_half_masks(False, False), _half_masks(True, True)]
    hi_row = lax.broadcasted_iota(jnp.int32, (1, LANES), 1) >= CHUNK

    def pre_block(chunks):
        chains = []
        for c in chunks:
            rows = _chunk_rows(c)
            bc = bc_scr[rows, :].astype(BF16)
            cb2 = _dot_nt(cc_scr[rows, :].astype(BF16), jnp.concatenate([bc, bc], axis=0))
            chains.append(dict(c=c, rows=rows, bc=bc, cb2=cb2))
        for ch in chains:
            rows = ch["rows"]
            xc = xc_scr[rows, :]
            dts = _softplus(misc_ref[rows, :] + dtb)
            adt = dts * neg_a
            ch["x2m"] = _blockdiag(hms[0], xc)
            ch["g2"], ch["xw"] = [], []
            for d in range(2):
                hm = hms[d]
                dt_h = [_colsel(dts, d * SSD_HEADS + 2 * p + hh) for hh in range(2)]
                a_h = [_colsel(adt, d * SSD_HEADS + 2 * p + hh) for hh in range(2)]
                cum0, cum1, cum_rows = _half_cum(hm, a_h[0], a_h[1])
                seg2 = jnp.exp(jnp.where(hm["incl"], _pack(hm, cum0, cum1) - cum_rows, NEG_INF))
                dt_rows = _rows(hm, _pack(hm, dt_h[0], dt_h[1]))
                ch["g2"].append((ch["cb2"] * seg2 * dt_rows).astype(BF16))
                al = [jnp.sum(a, axis=0, keepdims=True) for a in a_h]
                wgt = [jnp.exp(al[hh] - cum) * dt_h[hh] for hh, cum in enumerate((cum0, cum1))]
                ch["xw"].append((xc * _pack(hm, wgt[0], wgt[1])).astype(BF16))
                ea_scr[d, rows, :] = _pack(hm, jnp.exp(cum0), jnp.exp(cum1))
                eal_scr[d, ch["c"]] = jnp.where(hi_row, jnp.exp(al[1]), jnp.exp(al[0]))
        for ch in chains:
            for d in range(2):
                yi_scr[d, ch["rows"], :] = _dot(ch["g2"][d], ch["x2m"])
                upd_scr[d, ch["c"]] = _dot_tn(ch["bc"], ch["xw"][d])

    def step_block(chunks):
        states = [hs_scr[j] for j in range(2 * NS)]
        pairs = []
        for c in chunks:
            for s in range(NS):
                pair = []
                for d in range(2):
                    cc = s * nc + (c if d == 0 else nc - 1 - c)
                    pair.append(dict(rows=_chunk_rows(cc), hs=states[2 * s + d].astype(BF16)))
                    states[2 * s + d] = eal_scr[d, cc] * states[2 * s + d] + upd_scr[d, cc]
                pairs.append(pair)
        for j in range(2 * NS):
            hs_scr[j] = states[j]
        for pair in pairs:
            lhs = jnp.concatenate([cc_scr[pair[d]["rows"], :].astype(BF16) for d in range(2)], axis=0)
            rhs = jnp.concatenate([pair[d]["hs"] for d in range(2)], axis=1)
            pair.append(_dot(lhs, rhs))
        for pair in pairs:
            for d, dst in enumerate((y_ref, yb_scr)):
                rows = pair[d]["rows"]
                ch = pair[2][d * CHUNK:(d + 1) * CHUNK, d * LANES:(d + 1) * LANES]
                dst[rows, :] = yi_scr[d, rows, :] + ea_scr[d, rows, :] * ch

    _block_loop(NS * nc, pre_block, PRE_CHUNKS)
    _block_loop(nc, step_block, STEP_CHUNKS)

    y_ref[...] = y_ref[...] + yb_scr[...] + dskip_ref[...] * xc_scr[...]
    if emit_state:
        for s in range(NS):
            for d in range(2):
                hout_ref[s, d] = hs_scr[2 * s + d].T.reshape(2, SSD_P, SSD_N)


def _lane_row(v):
    return jnp.pad(v.astype(F32), (0, LANES - v.shape[0])).reshape(1, LANES)


def _ssd(xbc, misc, conv_w, conv_b, dt_bias, a_log, d_skip, *, prompt, init=None):
    T, B, NS, blk0 = _seq_geometry(prompt)
    R = NS * T
    nc = R // CHUNK
    has_init = init is not None
    n_pairs = SSD_HEADS // 2
    pairs_per_group = n_pairs // SSD_GROUPS
    xb = SSD_HEADS * SSD_P // LANES
    cb = xb + SSD_GROUPS * SSD_N // LANES
    colx = lambda b, p: (blk0 + b, p)
    colb = lambda b, p: (blk0 + b, xb + p // pairs_per_group)
    colc = lambda b, p: (blk0 + b, cb + p // pairs_per_group)
    wsel = lambda f: (lambda b, p: (0, f(b, p)[1]))
    row128 = pl.BlockSpec((1, LANES), lambda b, p: (0, 0))
    in_specs = [pl.BlockSpec((R, LANES), colx), pl.BlockSpec((R, LANES), colb), pl.BlockSpec((R, LANES), colc),
                pl.BlockSpec((CONV_W, LANES), wsel(colx)), pl.BlockSpec((CONV_W, LANES), wsel(colb)),
                pl.BlockSpec((CONV_W, LANES), wsel(colc)),
                pl.BlockSpec((1, LANES), wsel(colx)), pl.BlockSpec((1, LANES), wsel(colb)),
                pl.BlockSpec((1, LANES), wsel(colc)),
                pl.BlockSpec((R, LANES), lambda b, p: (blk0 + b, 0)), row128, row128,
                pl.BlockSpec((1, LANES), lambda b, p: (0, p))]
    cb2 = conv_b.reshape(1, -1)
    args = [xbc, xbc, xbc, conv_w, conv_w, conv_w, cb2, cb2, cb2, misc,
            _lane_row(dt_bias.reshape(-1)), _lane_row(a_log.reshape(-1)),
            jnp.repeat(d_skip, SSD_P).reshape(1, SSD_HEADS * SSD_P)]
    state_spec = pl.BlockSpec((NS, 2, 2, SSD_P, SSD_N), lambda b, p: (b, 0, p, 0, 0))
    if has_init:
        in_specs.append(state_spec)
        args.append(init)
    out_specs = [pl.BlockSpec((R, LANES), lambda b, p: (b, p))]
    out_shape = [jax.ShapeDtypeStruct((B * T, SSD_HEADS * SSD_P), F32)]
    if prompt:
        out_specs.append(state_spec)
        out_shape.append(jax.ShapeDtypeStruct((B, 2, SSD_HEADS, SSD_P, SSD_N), F32))
    return pl.pallas_call(
        functools.partial(_ssd_kernel, T=T, NS=NS, has_init=has_init, emit_state=prompt),
        grid=(B // NS, n_pairs),
        in_specs=in_specs,
        out_specs=out_specs,
        out_shape=out_shape,
        scratch_shapes=[pltpu.VMEM((T + 2 * CONV_HALO, LANES), F32)]
        + [pltpu.VMEM((R, LANES), F32) for _ in range(4)]
        + [pltpu.VMEM((2 * NS, SSD_N, 2 * SSD_P), F32),
           pltpu.VMEM((2, R, LANES), F32),
           pltpu.VMEM((2, R, LANES), F32),
           pltpu.VMEM((2, nc, SSD_N, 2 * SSD_P), F32),
           pltpu.VMEM((2, nc, 1, LANES), F32)],
        compiler_params=_cparams("arbitrary", "arbitrary"),
        name="ssd_prompt" if prompt else "ssd_sample",
    )(*args)


def _tri_inverse(hm, nmat, eye_f):
    levels = int(np.log2(CHUNK))

    def off(level):
        same_big = lax.shift_right_logical(hm["t"], level) == lax.shift_right_logical(hm["s"], level)
        same_small = lax.shift_right_logical(hm["t"], level - 1) == lax.shift_right_logical(hm["s"], level - 1)
        return jnp.where(same_big & jnp.logical_not(same_small), nmat, 0.0)

    state = dict(dinv=eye_f - off(1))

    def first(level):
        def run():
            state["t1"] = _dot(state["dinv"].astype(BF16), _blockdiag(hm, off(level)))
        return run

    def second():
        state["dinv"] = state["dinv"] - _dot(state["t1"].astype(BF16), _blockdiag(hm, state["dinv"]))

    stages = []
    for level in range(2, levels + 1):
        stages += [first(level), second]
    return state, stages


def _gdn_kernel(*refs, T, NS, has_init, emit_state):
    it = iter(refs)
    q_ref, k_ref, v_ref = next(it), next(it), next(it)
    wq_ref, wk_ref, wv_ref = next(it), next(it), next(it)
    bq_ref, bk_ref, bv_ref = next(it), next(it), next(it)
    misc_ref, dtb_ref, alog_ref, z_ref, gn_ref = (next(it) for _ in range(5))
    s0_ref = next(it) if has_init else None
    o_ref = next(it)
    sout_ref = next(it) if emit_state else None
    (pad_scr, qc_scr, kc_scr, vc_scr, ob_scr, s_scr,
     u0_scr, wq_scr, kcf_scr, p2_scr, egl_scr) = (next(it) for _ in range(11))

    h = pl.program_id(1)
    nc = T // CHUNK

    _conv_silu(q_ref, qc_scr, pad_scr, wq_ref, bq_ref, T, NS)
    _conv_silu(k_ref, kc_scr, pad_scr, wk_ref, bk_ref, T, NS)
    _conv_silu(v_ref, vc_scr, pad_scr, wv_ref, bv_ref, T, NS)
    q = qc_scr[...]
    qc_scr[...] = q * lax.rsqrt(jnp.sum(q * q, axis=-1, keepdims=True) + EPS) * (GDN_DK ** -0.5)
    k = kc_scr[...]
    kc_scr[...] = k * lax.rsqrt(jnp.sum(k * k, axis=-1, keepdims=True) + EPS)

    for s in range(NS):
        for d in range(2):
            if has_init:
                s_scr[2 * s + d] = s0_ref[s, d]
            else:
                s_scr[2 * s + d] = jnp.zeros((GDN_DK, GDN_DV), F32)

    dtb = dtb_ref[...]
    neg_a = -jnp.exp(alog_ref[...])
    hm = _half_masks(False, True)
    eye_f = hm["eye"].astype(F32)

    def pre_gates(c):
        rows = _chunk_rows(c)
        kc = kc_scr[rows, :]
        qc = qc_scr[rows, :]
        mi = misc_ref[rows, :]
        gdec = neg_a * _softplus(mi + dtb)
        g_src = [_colsel(gdec, 2 * SSD_HEADS + d * GDN_HEADS + h) for d in range(2)]
        beta = [jax.nn.sigmoid(_colsel(mi, 2 * SSD_HEADS + 2 * GDN_HEADS + d * GDN_HEADS + h)) for d in range(2)]
        gc = [None, None]
        gc[0], gc[1], gc_rows = _half_cum(hm, g_src[0], g_src[1])
        diff = _pack(hm, gc[0], gc[1]) - gc_rows
        beta_rows = _rows(hm, _pack(hm, beta[0], beta[1]))
        kb = kc.astype(BF16)
        ek = []
        for d in range(2):
            egc = jnp.exp(gc[d])
            gl = jnp.sum(g_src[d], axis=0, keepdims=True)
            ek.append((egc * kc).astype(BF16))
            wq_scr[d, c, CHUNK:, :] = (egc * qc).astype(BF16)
            kcf_scr[d, rows, :] = (jnp.exp(gl - gc[d]) * beta[d] * kc).astype(BF16)
            egl_scr[d, c] = _lanes(jnp.exp(gl))
        return dict(c=c, rows=rows, kb=kb, qb=qc.astype(BF16), ek=ek,
                    dec_s=jnp.exp(jnp.where(hm["strict"], diff, NEG_INF)) * beta_rows,
                    dec_i=jnp.exp(jnp.where(hm["incl"], diff, NEG_INF)) * beta_rows)

    def pre_block(chunks):
        chains = [pre_gates(c) for c in chunks]
        for ch in chains:
            kb = ch["kb"]
            ch["kq"] = _dot_nt(jnp.concatenate([kb, ch["qb"]], axis=0), jnp.concatenate([kb, kb], axis=0))
        for ch in chains:
            p2_scr[ch["rows"], :] = (ch["dec_i"] * ch["kq"][CHUNK:]).astype(BF16)
            ch["inv"], ch["stages"] = _tri_inverse(hm, ch["dec_s"] * ch["kq"][:CHUNK], eye_f)
        for i in range(len(chains[0]["stages"])):
            for ch in chains:
                ch["stages"][i]()
        for ch in chains:
            vb = vc_scr[ch["rows"], :].astype(BF16)
            zero = jnp.zeros_like(vb)
            rhs = jnp.concatenate([jnp.concatenate([vb, ch["ek"][0], zero, zero], axis=1),
                                   jnp.concatenate([zero, zero, vb, ch["ek"][1]], axis=1)], axis=0)
            uw = _dot(ch["inv"]["dinv"].astype(BF16), rhs)
            for d in range(2):
                u0_scr[d, ch["rows"], :] = uw[:, 2 * d * LANES:(2 * d + 1) * LANES]
                wq_scr[d, ch["c"], :CHUNK, :] = uw[:, (2 * d + 1) * LANES:(2 * d + 2) * LANES].astype(BF16)

    def step_block(chunks):
        states = [s_scr[j] for j in range(2 * NS)]
        for c in chunks:
            work = []
            for s in range(NS):
                cs = [s * nc + c, s * nc + nc - 1 - c]
                lhs = jnp.concatenate([wq_scr[d, cs[d]] for d in range(2)], axis=0)
                rhs = jnp.concatenate([states[2 * s + d].astype(BF16) for d in range(2)], axis=1)
                work.append(dict(s=s, cs=cs, rows=[_chunk_rows(cc) for cc in cs], a=_dot(lhs, rhs)))
            for w in work:
                rows, ub, w["qs"] = w["rows"], [], []
                for d in range(2):
                    blk = w["a"][2 * d * CHUNK:(2 * d + 2) * CHUNK, d * LANES:(d + 1) * LANES]
                    ub.append((u0_scr[d, rows[d], :] - blk[:CHUNK]).astype(BF16))
                    w["qs"].append(blk[CHUNK:])
                zero = jnp.zeros_like(ub[0])
                w["pu"] = _dot(jnp.concatenate([p2_scr[rows[d], :] for d in range(2)], axis=0),
                               jnp.concatenate([jnp.concatenate([ub[0], zero], axis=1),
                                                jnp.concatenate([zero, ub[1]], axis=1)], axis=0))
                w["ktu"] = _dot_tn(jnp.concatenate([kcf_scr[d, rows[d], :] for d in range(2)], axis=1),
                                   jnp.concatenate(ub, axis=1))
            for w in work:
                for d, dst in enumerate((o_ref, ob_scr)):
                    j = 2 * w["s"] + d
                    dst[w["rows"][d], :] = w["qs"][d] + w["pu"][d * CHUNK:(d + 1) * CHUNK, d * LANES:(d + 1) * LANES]
                    states[j] = (egl_scr[d, w["cs"][d]] * states[j]
                                 + w["ktu"][d * GDN_DK:(d + 1) * GDN_DK, d * GDN_DV:(d + 1) * GDN_DV])
        for j in range(2 * NS):
            s_scr[j] = states[j]

    _block_loop(NS * nc, pre_block, PRE_CHUNKS)
    _block_loop(nc, step_block, STEP_CHUNKS)

    og = o_ref[...] + ob_scr[...]
    o_ref[...] = _rms(og, gn_ref[...]) * _silu(z_ref[...])
    if emit_state:
        for s in range(NS):
            for d in range(2):
                sout_ref[s, d] = s_scr[2 * s + d]


def _gdn(qkv, misc, zg, conv_w, conv_b, dt_row, alog_row, gnorm, *, prompt, init=None):
    T, B, NS, blk0 = _seq_geometry(prompt, GDN_PROMPT_SEQS_PER_STEP)
    R = NS * T
    nc = R // CHUNK
    has_init = init is not None
    col = lambda j: (lambda b, h: (blk0 + b, j * GDN_HEADS + h))
    wsel = lambda j: (lambda b, h: (0, j * GDN_HEADS + h))
    row128 = pl.BlockSpec((1, LANES), lambda b, h: (0, 0))
    in_specs = [pl.BlockSpec((R, LANES), col(j)) for j in range(3)]
    in_specs += [pl.BlockSpec((CONV_W, LANES), wsel(j)) for j in range(3)]
    in_specs += [pl.BlockSpec((1, LANES), wsel(j)) for j in range(3)]
    in_specs += [pl.BlockSpec((R, LANES), lambda b, h: (blk0 + b, 0)), row128, row128,
                 pl.BlockSpec((R, LANES), lambda b, h: (blk0 + b, h)), row128]
    cb2 = conv_b.reshape(1, -1)
    args = [qkv, qkv, qkv, conv_w, conv_w, conv_w, cb2, cb2, cb2, misc, dt_row, alog_row, zg,
            gnorm.reshape(1, GDN_DV)]
    state_spec = pl.BlockSpec((NS, 2, None, GDN_DK, GDN_DV), lambda b, h: (b, 0, h, 0, 0))
    if has_init:
        in_specs.append(state_spec)
        args.append(init)
    out_specs = [pl.BlockSpec((R, LANES), lambda b, h: (b, h))]
    out_shape = [jax.ShapeDtypeStruct((B * T, GDN_HEADS * GDN_DV), F32)]
    if prompt:
        out_specs.append(state_spec)
        out_shape.append(jax.ShapeDtypeStruct((B, 2, GDN_HEADS, GDN_DK, GDN_DV), F32))
    return pl.pallas_call(
        functools.partial(_gdn_kernel, T=T, NS=NS, has_init=has_init, emit_state=prompt),
        grid=(B // NS, GDN_HEADS),
        in_specs=in_specs,
        out_specs=out_specs,
        out_shape=out_shape,
        scratch_shapes=[pltpu.VMEM((T + 2 * CONV_HALO, LANES), F32)]
        + [pltpu.VMEM((R, LANES), F32) for _ in range(4)]
        + [pltpu.VMEM((2 * NS, GDN_DK, GDN_DV), F32),
           pltpu.VMEM((2, R, GDN_DV), F32),
           pltpu.VMEM((2, nc, 2 * CHUNK, GDN_DK), BF16),
           pltpu.VMEM((2, R, GDN_DK), BF16),
           pltpu.VMEM((R, LANES), BF16),
           pltpu.VMEM((2, nc, 1, LANES), F32)],
        compiler_params=_cparams("arbitrary", "arbitrary"),
        name="gdn_prompt" if prompt else "gdn_sample",
    )(*args)


def _router_kernel(x_ref, g_ref, mod_ref, whi_ref, wlo_ref, b_ref, h_ref, ri_ref, w_ref, cnt_ref, count_scr):
    @pl.when(pl.program_id(0) == 0)
    def _():
        count_scr[...] = jnp.zeros_like(count_scr)

    mod = mod_ref[...]
    h = _rms(x_ref[...], g_ref[...]) * (1.0 + mod[:, D_MODEL:2 * D_MODEL]) + mod[:, :D_MODEL]
    hb = h.astype(BF16)
    h_ref[...] = h
    hl = (h - hb.astype(F32)).astype(BF16)
    whi = whi_ref[...]
    logit = _dot(hb, whi) + (_dot(hl, whi) + _dot(hb, wlo_ref[...])) + b_ref[...]
    lane = lax.broadcasted_iota(jnp.int32, logit.shape, 1)
    is_group = (lane >= MOE_EXPERTS) & (lane < MOE_EXPERTS + MOE_GROUPS)
    glog = jnp.where(is_group, logit, NEG_INF)
    gmax = jnp.max(glog, axis=1, keepdims=True)
    gsel = jnp.min(jnp.where(glog == gmax, lane, LANES), axis=1, keepdims=True) - MOE_EXPERTS
    gw = 1.0 / jnp.sum(jnp.exp(glog - gmax), axis=1, keepdims=True)
    lo = gsel * MOE_PER_GROUP
    in_group = (lane >= lo) & (lane < lo + MOE_PER_GROUP)
    elog = jnp.where(in_group, logit, NEG_INF)
    v1 = jnp.max(elog, axis=1, keepdims=True)
    i1 = jnp.min(jnp.where(elog == v1, lane, LANES), axis=1, keepdims=True)
    elog2 = jnp.where(lane == i1, NEG_INF, elog)
    v2 = jnp.max(elog2, axis=1, keepdims=True)
    i2 = jnp.min(jnp.where(elog2 == v2, lane, LANES), axis=1, keepdims=True)
    e2 = jnp.exp(v2 - v1)
    w1 = gw / (1.0 + e2)
    w2 = gw * e2 / (1.0 + e2)
    w_ref[...] = jnp.where(lane == 0, w1, jnp.where(lane == 1, w2, 0.0))
    tm = logit.shape[0]
    onehot = jnp.where((lane == i1) | (lane == i2), 1.0, 0.0)
    t_i = lax.broadcasted_iota(jnp.int32, (tm, tm), 0)
    s_i = lax.broadcasted_iota(jnp.int32, (tm, tm), 1)
    earlier = jnp.where(s_i < t_i, 1.0, 0.0).astype(BF16)
    before = _dot(earlier, onehot.astype(BF16)) + count_scr[...]
    r1 = jnp.sum(jnp.where(lane == i1, before, 0.0), axis=1, keepdims=True).astype(jnp.int32)
    r2 = jnp.sum(jnp.where(lane == i2, before, 0.0), axis=1, keepdims=True).astype(jnp.int32)
    ri_ref[...] = jnp.where(lane == 0, i1, jnp.where(lane == 1, i2, jnp.where(lane == 2, r1, jnp.where(lane == 3, r2, 0))))
    count_scr[...] += jnp.sum(onehot, axis=0, keepdims=True)
    cnt_ref[...] = count_scr[...]


def _router(x, gain, mod4, layer, w_route, b_route):
    tm = ROW_TILE
    w_hi = w_route.astype(BF16)
    w_lo = (w_route - w_hi.astype(F32)).astype(BF16)
    return pl.pallas_call(
        _router_kernel,
        grid=(N_TOK // tm,),
        in_specs=[pl.BlockSpec((tm, D_MODEL), lambda i: (i, 0)),
                  pl.BlockSpec((1, D_MODEL), lambda i: (0, 0)),
                  pl.BlockSpec((None, None, 1, 3 * D_MODEL), lambda i: (layer, _mod_index(i, tm), 0, 1)),
                  pl.BlockSpec((D_MODEL, LANES), lambda i: (0, 0)),
                  pl.BlockSpec((D_MODEL, LANES), lambda i: (0, 0)),
                  pl.BlockSpec((1, LANES), lambda i: (0, 0))],
        out_specs=[pl.BlockSpec((tm, D_MODEL), lambda i: (i, 0)),
                   pl.BlockSpec((tm, LANES), lambda i: (i, 0)),
                   pl.BlockSpec((tm, LANES), lambda i: (i, 0)),
                   pl.BlockSpec((1, LANES), lambda i: (0, 0))],
        out_shape=[jax.ShapeDtypeStruct((N_TOK, D_MODEL), F32),
                   jax.ShapeDtypeStruct((N_TOK, LANES), jnp.int32),
                   jax.ShapeDtypeStruct((N_TOK, LANES), F32),
                   jax.ShapeDtypeStruct((1, LANES), F32)],
        scratch_shapes=[pltpu.VMEM((1, LANES), F32)],
        compiler_params=_cparams("arbitrary"),
        name="router",
    )(x, gain.reshape(1, D_MODEL), mod4, w_hi, w_lo, b_route)


N_ASSIGN = 2 * N_TOK
MOE_TILE = 256
MOE_TILES = N_ASSIGN // MOE_TILE
MOE_SEGMENTS = MOE_TILES + MOE_EXPERTS - 1
DISPATCH_ROWS = 1024
COMBINE_ROWS = 256


def _routing_tables(ri, cnt):
    counts = cnt[0, :MOE_EXPERTS].astype(jnp.int32)
    cum_end = jnp.cumsum(counts)
    cum_start = cum_end - counts
    where = (ri[:, 0:2].reshape(N_ASSIGN), ri[:, 2:4].reshape(N_ASSIGN), cum_start)
    tile_start = jnp.arange(MOE_TILES, dtype=jnp.int32) * MOE_TILE
    expert_start = cum_start[1:]
    tile_rank = jnp.arange(MOE_TILES, dtype=jnp.int32) + jnp.sum(
        (expert_start[None, :] < tile_start[:, None]).astype(jnp.int32), axis=1)
    expert_rank = jnp.arange(MOE_EXPERTS - 1, dtype=jnp.int32) + jnp.minimum(expert_start // MOE_TILE + 1, MOE_TILES)
    values = jnp.concatenate([tile_start, expert_start])
    ranks = jnp.concatenate([tile_rank, expert_rank])
    slot = jnp.arange(MOE_SEGMENTS, dtype=jnp.int32)
    cuts = jnp.sum(jnp.where(ranks[None, :] == slot[:, None], values[None, :], 0), axis=1)
    seg_lo = cuts
    seg_hi = jnp.concatenate([cuts[1:], jnp.full((1,), N_ASSIGN, jnp.int32)])
    seg_tile = jnp.minimum(seg_lo // MOE_TILE, MOE_TILES - 1)
    seg_expert = jnp.minimum(jnp.sum((cum_end[None, :] <= seg_lo[:, None]).astype(jnp.int32), axis=1),
                             MOE_EXPERTS - 1)
    seg_first = jnp.concatenate([jnp.ones((1,), jnp.int32), (seg_tile[1:] != seg_tile[:-1]).astype(jnp.int32)])
    return where, (seg_tile, seg_expert, seg_lo, seg_hi, seg_first)


def _row_copy(src_ref, src_row, dst_ref, dst_row, sem):
    return pltpu.make_async_copy(src_ref.at[pl.ds(src_row, 1), :], dst_ref.at[pl.ds(dst_row, 1), :], sem)


def _dispatch_kernel(expert_ref, rank_ref, start_ref, h_ref, xs_ref, sem):
    base = pl.program_id(0) * DISPATCH_ROWS

    def start(t, carry):
        for k in range(2):
            a = 2 * (base + t) + k
            _row_copy(h_ref, t, xs_ref, start_ref[expert_ref[a]] + rank_ref[a], sem).start()
        return carry

    def wait(t, carry):
        for k in range(2):
            _row_copy(h_ref, 0, xs_ref, 0, sem).wait()
        return carry

    lax.fori_loop(0, DISPATCH_ROWS, start, 0, unroll=4)
    lax.fori_loop(0, DISPATCH_ROWS, wait, 0, unroll=4)


def _dispatch(where, h):
    tm = DISPATCH_ROWS
    return pl.pallas_call(
        _dispatch_kernel,
        grid_spec=pltpu.PrefetchScalarGridSpec(
            num_scalar_prefetch=3,
            grid=(N_TOK // tm,),
            in_specs=[pl.BlockSpec((tm, D_MODEL), lambda i, e, r, s: (i, 0))],
            out_specs=pl.BlockSpec(memory_space=pl.ANY),
            scratch_shapes=[pltpu.SemaphoreType.DMA(())]),
        out_shape=jax.ShapeDtypeStruct((N_ASSIGN, D_MODEL), F32),
        compiler_params=_cparams("arbitrary"),
        name="moe_dispatch",
    )(*where, h)


def _experts_kernel(tile_ref, expert_ref, lo_ref, hi_ref, first_ref, xs_ref, wg_ref, wu_ref, wd_ref, o_ref):
    p = pl.program_id(0)
    lo, hi = lo_ref[p], hi_ref[p]

    @pl.when(first_ref[p] == 1)
    def _():
        o_ref[...] = jnp.zeros_like(o_ref)

    @pl.when(hi > lo)
    def _():
        x = xs_ref[...].astype(BF16)
        hg = _dot(x, wg_ref[...].astype(BF16))
        hu = _dot(x, wu_ref[...].astype(BF16))
        row = tile_ref[p] * MOE_TILE + lax.broadcasted_iota(jnp.int32, (MOE_TILE, 1), 0)
        act = jnp.where((row >= lo) & (row < hi), _silu(hg) * hu, 0.0)
        o_ref[...] += _bdot(act, wd_ref[...])


def _experts(tables, xs, wg, wu, wd, layer):
    weight = lambda shape: pl.BlockSpec((None, None) + shape, lambda p, tile, expert, lo, hi, first: (layer, expert[p], 0, 0))
    rows = pl.BlockSpec((MOE_TILE, D_MODEL), lambda p, tile, expert, lo, hi, first: (tile[p], 0))
    return pl.pallas_call(
        _experts_kernel,
        grid_spec=pltpu.PrefetchScalarGridSpec(
            num_scalar_prefetch=5,
            grid=(MOE_SEGMENTS,),
            in_specs=[rows, weight((D_MODEL, MOE_FF)), weight((D_MODEL, MOE_FF)), weight((MOE_FF, D_MODEL))],
            out_specs=rows),
        out_shape=jax.ShapeDtypeStruct((N_ASSIGN, D_MODEL), F32),
        compiler_params=_cparams("arbitrary"),
        name="moe_experts",
    )(*tables, xs, wg, wu, wd)


def _combine_kernel(expert_ref, rank_ref, start_ref, ys_ref, w_ref, x_ref, mod_ref, gf_ref, o_ref, buf, sem, *,
                    final, tile0, n_tiles):
    i = pl.program_id(0)
    slot = i % 2

    def gather(tile, into):
        base = (tile0 + tile) * COMBINE_ROWS

        def start(t, carry):
            for k in range(2):
                a = 2 * (base + t) + k
                _row_copy(ys_ref, start_ref[expert_ref[a]] + rank_ref[a], buf.at[into, k], t, sem.at[into]).start()
            return carry

        lax.fori_loop(0, COMBINE_ROWS, start, 0, unroll=4)

    @pl.when(i == 0)
    def _():
        gather(0, 0)

    @pl.when(i + 1 < n_tiles)
    def _():
        gather(i + 1, 1 - slot)

    def wait(t, carry):
        for k in range(2):
            _row_copy(ys_ref, 0, buf.at[slot, k], 0, sem.at[slot]).wait()
        return carry

    lax.fori_loop(0, COMBINE_ROWS, wait, 0, unroll=4)
    w = w_ref[...]
    y = x_ref[...] + mod_ref[...] * (w[:, 0:1] * buf[slot, 0] + w[:, 1:2] * buf[slot, 1])
    if final:
        y = _rms(y, gf_ref[...])
    o_ref[...] = y


def _combine(where, ys, wts, x, mod4, layer, norm_final, final, tile0, n_tiles):
    tm = COMBINE_ROWS
    return pl.pallas_call(
        functools.partial(_combine_kernel, final=final, tile0=tile0, n_tiles=n_tiles),
        grid_spec=pltpu.PrefetchScalarGridSpec(
            num_scalar_prefetch=3,
            grid=(n_tiles,),
            in_specs=[pl.BlockSpec(memory_space=pl.ANY),
                      pl.BlockSpec((tm, LANES), lambda i, e, r, s: (tile0 + i, 0)),
                      pl.BlockSpec((tm, D_MODEL), lambda i, e, r, s: (tile0 + i, 0)),
                      pl.BlockSpec((None, None, 1, D_MODEL),
                                   lambda i, e, r, s: (layer, _mod_index(tile0 + i, tm), 0, 5)),
                      pl.BlockSpec((1, D_MODEL), lambda i, e, r, s: (0, 0))],
            out_specs=pl.BlockSpec((tm, D_MODEL), lambda i, e, r, s: (i, 0)),
            scratch_shapes=[pltpu.VMEM((2, 2, tm, D_MODEL), F32), pltpu.SemaphoreType.DMA((2,))]),
        out_shape=jax.ShapeDtypeStruct((n_tiles * tm, D_MODEL), F32),
        compiler_params=_cparams("arbitrary"),
        name="moe_combine",
    )(*where, ys, wts, x, mod4, norm_final.reshape(1, D_MODEL))


def _rope_tables():
    t = jnp.arange(DEC_SEQ)
    pos = jnp.stack([(t // GRID_W).astype(F32), (t % GRID_W).astype(F32)], axis=1)
    nf = MLA_ROPE // 4
    inv = ROPE_BASE ** (-jnp.arange(nf, dtype=F32) / nf)
    j = jnp.arange(MLA_ROPE)
    ang = pos[:, j // (2 * nf)] * inv[j % nf][None, :]
    sign = jnp.where((j % (2 * nf)) < nf, -1.0, 1.0)
    cos = jnp.pad(jnp.cos(ang), ((0, 0), (MLA_NOPE, LANES - MLA_NOPE - MLA_ROPE)), constant_values=1.0)
    sin = jnp.pad(jnp.sin(ang) * sign, ((0, 0), (MLA_NOPE, LANES - MLA_NOPE - MLA_ROPE)))
    cos = jnp.concatenate([jnp.ones((ROW_TILE, LANES), F32), cos], axis=0)
    sin = jnp.concatenate([jnp.zeros((ROW_TILE, LANES), F32), sin], axis=0)
    return cos, sin


def _pad_heads(w, n_heads, width, lo, hi):
    k = w.shape[0]
    w = w.reshape(k, n_heads, width)[:, :, lo:hi]
    w = jnp.pad(w, ((0, 0), (0, 0), (0, LANES - (hi - lo))))
    return w.reshape(k, n_heads * LANES)


def kernel(x_prompt, x_sample, c, cache_mla_kv, cache_mla_krope, state_mlstm_C, state_mlstm_n, state_mlstm_m, state_ssd, state_gdn, c_ctx, ada_w, ada_b, norm_mix, norm_ffn, w_in_even, ml_i_bias, ml_f_bias, ml_norm, mla_q_norm, mla_w_uq, mla_kv_norm, mla_w_ukv, w_out_even, w_in_odd, ssd_conv_w, ssd_conv_b, ssd_dt_bias, ssd_A_log, ssd_D, ssd_norm, gdn_conv_w, gdn_conv_b, gdn_dt_bias, gdn_A_log, gdn_norm, w_out_odd, moe_w_group, moe_b_group, moe_w_expert, moe_b_expert, moe_w_gate, moe_w_up, moe_w_down, norm_final):
    x = (x_prompt.reshape(N_PROMPT, D_MODEL), x_sample.reshape(N_SAMPLE, D_MODEL))
    cond = jnp.concatenate([c_ctx[None, :], c, jnp.zeros((N_COND - 1 - DEC_BATCH, D_MODEL), F32)], axis=0)
    mod4 = _ada(cond, ada_w, ada_b).reshape(DEPTH, N_COND, 1, 6 * D_MODEL)
    cos_tab, sin_tab = _rope_tables()

    def moe_layer(x, layer, final):
        w_route = jnp.concatenate([moe_w_expert[layer], moe_w_group[layer]], axis=1)
        w_route = jnp.pad(w_route, ((0, 0), (0, LANES - MOE_EXPERTS - MOE_GROUPS)))
        b_route = _lane_row(jnp.concatenate([moe_b_expert[layer], moe_b_group[layer]]))
        h, ri, wts, cnt = _router(x, norm_ffn[layer], mod4, layer, w_route, b_route)
        where, tables = _routing_tables(ri, cnt)
        ys = _experts(tables, _dispatch(where, h), moe_w_gate, moe_w_up, moe_w_down, layer)
        combine = functools.partial(_combine, where, ys, wts, x, mod4, layer, norm_final, final)
        if not final:
            return combine(0, N_TOK // COMBINE_ROWS)
        p_tiles = N_PROMPT // COMBINE_ROWS
        return combine(0, p_tiles), combine(p_tiles, N_SAMPLE // COMBINE_ROWS)

    e = 0
    w = w_in_even[e]
    off = np.cumsum([0, 4 * ML_HEADS * ML_DK, 2 * ML_HEADS, 2 * ML_HEADS, MLA_Q_RANK, MLA_KV_RANK, MLA_ROPE])
    w_misc = jnp.concatenate([w[:, off[1]:off[3]],
                              jnp.zeros((D_MODEL, MLA_NOPE - 4 * ML_HEADS), F32),
                              w[:, off[5]:off[6]],
                              jnp.zeros((D_MODEL, LANES - MLA_NOPE - MLA_ROPE), F32)], axis=1)
    weights = [w[:, :off[1]].astype(BF16), w[:, off[3]:off[4]].astype(BF16),
               w[:, off[4]:off[5]].astype(BF16), w_misc.astype(BF16)]
    qkvo, cq, ckv, misc0 = _inproj(x, norm_mix[0], mod4, 0, weights)

    hm_p, st_c, st_n, st_m = _mlstm(qkvo, misc0, ml_i_bias[e], ml_f_bias[e], ml_norm[e], prompt=True)
    (hm_s,) = _mlstm(qkvo, misc0, ml_i_bias[e], ml_f_bias[e], ml_norm[e], prompt=False,
                     init=(state_mlstm_C[:, e], state_mlstm_n[:, e], state_mlstm_m[:, e]))

    dq = MLA_NOPE + MLA_ROPE
    w_uq_pad = _pad_heads(mla_w_uq[e], MLA_HEADS, dq, 0, dq).astype(BF16)
    w_uk_pad = _pad_heads(mla_w_ukv[e], MLA_HEADS, MLA_NOPE + MLA_V, 0, MLA_NOPE).astype(BF16)
    w_uv = mla_w_ukv[e].reshape(MLA_KV_RANK, MLA_HEADS, MLA_NOPE + MLA_V)[:, :, MLA_NOPE:]
    w_uv = w_uv.reshape(MLA_KV_RANK, MLA_HEADS * MLA_V).astype(BF16)
    q_cat = _mla_q(cq, mla_q_norm[e], w_uq_pad, cos_tab, sin_tab)
    ckv_n, k_cat, v_all = _mla_kv(ckv, misc0, w_uk_pad, w_uv, mla_kv_norm[e], cos_tab, sin_tab)
    cache_kpe = jnp.pad(cache_mla_krope[:, e].reshape(DEC_BATCH * PAST_LEN, MLA_ROPE),
                        ((0, 0), (MLA_NOPE, LANES - MLA_NOPE - MLA_ROPE)))
    k_cache, v_cache = _mla_kv(cache_mla_kv[:, e].reshape(DEC_BATCH * PAST_LEN, MLA_KV_RANK), cache_kpe,
                               w_uk_pad, w_uv)
    att_p = _attention(q_cat, k_cat, v_all, B=BATCH, Tq=SEQ, Tk=SEQ, tq=SEQ, q_row0=0, heads=MLA_HEADS)

    def with_cache(cache, new):
        width = new.shape[1]
        both = jnp.concatenate([cache.reshape(DEC_BATCH, PAST_LEN, width),
                                new[N_PROMPT:].reshape(DEC_BATCH, DEC_SEQ, width)], axis=1)
        return both.reshape(DEC_BATCH * (PAST_LEN + DEC_SEQ), width)

    att_s = _attention(q_cat, with_cache(k_cache, k_cat), with_cache(v_cache, v_all), B=DEC_BATCH, Tq=DEC_SEQ,
                       Tk=PAST_LEN + DEC_SEQ, tq=256, q_row0=N_PROMPT, heads=2)
    wo = w_out_even[e].astype(BF16)
    x = _outproj((hm_p, hm_s), (att_p, att_s), wo[:ML_HEADS * ML_DV], wo[ML_HEADS * ML_DV:], x, mod4, 0)
    x = moe_layer(x, 0, final=False)

    oi = 0
    w = w_in_odd[oi]
    ssd_w = SSD_HEADS * SSD_P
    ssd_cc = ssd_w + 2 * SSD_GROUPS * SSD_N
    gdn_w = GDN_HEADS * GDN_DK
    off = np.cumsum([0, ssd_w, ssd_cc, 2 * SSD_HEADS, 3 * gdn_w, gdn_w, 2 * GDN_HEADS, 2 * GDN_HEADS])
    w_misc = jnp.concatenate([w[:, off[2]:off[3]], w[:, off[5]:off[7]],
                              jnp.zeros((D_MODEL, LANES - 2 * SSD_HEADS - 4 * GDN_HEADS), F32)], axis=1)
    weights = [w[:, off[0]:off[1]].astype(BF16), w[:, off[1]:off[2]].astype(BF16),
               w[:, off[3]:off[4]].astype(BF16), w[:, off[4]:off[5]].astype(BF16), w_misc.astype(BF16)]
    z_s, xbc, qkv_g, z_g, misc = _inproj(x, norm_mix[1], mod4, 1, weights)

    ssd_args = (xbc, misc, ssd_conv_w[oi], ssd_conv_b[oi], ssd_dt_bias[oi], ssd_A_log[oi], ssd_D[oi])
    ys_p, st_ssd = _ssd(*ssd_args, prompt=True)
    (ys_s,) = _ssd(*ssd_args, prompt=False, init=state_ssd[:, oi])

    lo = 2 * SSD_HEADS
    gdn_dt_row = jnp.pad(gdn_dt_bias[oi].reshape(-1), (lo, LANES - lo - 2 * GDN_HEADS)).reshape(1, LANES)
    gdn_alog_row = jnp.pad(gdn_A_log[oi].reshape(-1), (lo, LANES - lo - 2 * GDN_HEADS)).reshape(1, LANES)
    gdn_args = (qkv_g, misc, z_g, gdn_conv_w[oi], gdn_conv_b[oi], gdn_dt_row, gdn_alog_row, gdn_norm[oi])
    og_p, st_gdn = _gdn(*gdn_args, prompt=True)
    (og_s,) = _gdn(*gdn_args, prompt=False, init=state_gdn[:, oi])

    wo = w_out_odd[oi].astype(BF16)
    x = _outproj((ys_p, ys_s), (og_p, og_s), wo[:ssd_w], wo[ssd_w:], x, mod4, 1, z=z_s, gnorm=ssd_norm[oi])
    y_p, y_s = moe_layer(x, 1, final=True)

    y_prompt = y_p.reshape(BATCH, SEQ, D_MODEL)
    y_sample = y_s.reshape(DEC_BATCH, DEC_SEQ, D_MODEL)
    new_mla_kv = ckv_n[:N_PROMPT].reshape(BATCH, 1, SEQ, MLA_KV_RANK)
    new_mla_krope = misc0[:N_PROMPT, MLA_NOPE:MLA_NOPE + MLA_ROPE].reshape(BATCH, 1, SEQ, MLA_ROPE)
    return (y_prompt, y_sample, new_mla_kv, new_mla_krope, st_c[:, None], st_n.reshape(BATCH, 1, 2, ML_HEADS, ML_DK),
            st_m.reshape(BATCH, 1, 2, ML_HEADS), st_ssd[:, None], st_gdn[:, None])
```

```python
import functools

import numpy as np
import jax
import jax.numpy as jnp
from jax import lax
from jax.experimental import pallas as pl
from jax.experimental.pallas import tpu as pltpu

F32 = jnp.float32
BF16 = jnp.bfloat16

D_MODEL = 1024
BATCH = 32
SEQ = 256
DEPTH = 2
DEC_BATCH = 2
DEC_SEQ = 2048
PAST_LEN = 256
GRID_W = 64
EPS = 1e-6
ML_HEADS = 4
ML_DK = 128
ML_DV = 128
MLA_HEADS = 8
MLA_Q_RANK = 384
MLA_KV_RANK = 256
MLA_NOPE = 64
MLA_ROPE = 32
MLA_V = 64
ROPE_BASE = 10000.0
SSD_HEADS = 8
SSD_P = 64
SSD_GROUPS = 2
SSD_N = 128
GDN_HEADS = 4
GDN_DK = 128
GDN_DV = 128
CONV_W = 5
MOE_GROUPS = 4
MOE_PER_GROUP = 8
MOE_EXPERTS = 32
MOE_FF = 256

N_PROMPT = BATCH * SEQ
N_SAMPLE = DEC_BATCH * DEC_SEQ
N_TOK = N_PROMPT + N_SAMPLE
N_COND = 8

LANES = 128
CHUNK = 64
ROW_TILE = 512
CONV_HALO = 8
CONV_ROWS = 256
VMEM_LIMIT = 56 * 1024 * 1024
PRE_CHUNKS = 8
STEP_CHUNKS = 4
GDN_PROMPT_SEQS_PER_STEP = 8
GDN_SAMPLE_SEQS_PER_STEP = 1

assert LANES == 2 * CHUNK and SSD_P == CHUNK

NEG_INF = float("-inf")


def _cparams(*sem):
    return pltpu.CompilerParams(dimension_semantics=sem, vmem_limit_bytes=VMEM_LIMIT)


def _dot(a, b):
    return jnp.dot(a, b, preferred_element_type=F32)


def _dot_nt(a, b):
    return lax.dot_general(a, b, (((1,), (1,)), ((), ())), preferred_element_type=F32)


def _dot_tn(a, b):
    return lax.dot_general(a, b, (((0,), (0,)), ((), ())), preferred_element_type=F32)


def _bdot(a, b):
    return _dot(a.astype(BF16), b.astype(BF16))


def _rms(x, g):
    return x * lax.rsqrt(jnp.mean(x * x, axis=-1, keepdims=True) + EPS) * g


def _softplus(x):
    return jnp.maximum(x, 0.0) + jnp.log1p(jnp.exp(-jnp.abs(x)))


def _silu(x):
    return x * jax.nn.sigmoid(x)


def _colsel(x, j):
    lane = lax.broadcasted_iota(jnp.int32, x.shape, 1)
    return jnp.sum(jnp.where(lane == j, x, 0.0), axis=1, keepdims=True)


def _lanes(x):
    return jnp.broadcast_to(x, (x.shape[0], LANES))


def _half_masks(rev_lo, rev_hi):
    t = lax.broadcasted_iota(jnp.int32, (CHUNK, LANES), 0)
    lane = lax.broadcasted_iota(jnp.int32, (CHUNK, LANES), 1)
    s = lane & (CHUNK - 1)
    hi = lane >= CHUNK

    def pick(fwd, bwd):
        if rev_lo == rev_hi:
            return bwd if rev_lo else fwd
        on_hi, on_lo = (bwd, fwd) if rev_hi else (fwd, bwd)
        return (hi & on_hi) | (jnp.logical_not(hi) & on_lo)

    return dict(hi=hi, t=t, s=s, eye=(s == t), incl=pick(s <= t, s >= t), incl_t=pick(t <= s, t >= s),
                strict=pick(s < t, s > t))


def _pack(hm, col_lo, col_hi):
    return jnp.where(hm["hi"], col_hi, col_lo)


def _rows(hm, cols):
    return jnp.sum(jnp.where(hm["eye"], cols, 0.0), axis=0, keepdims=True)


def _half_sums(hm, x):
    lo = jnp.sum(jnp.where(hm["hi"], 0.0, x), axis=1, keepdims=True)
    hi = jnp.sum(jnp.where(hm["hi"], x, 0.0), axis=1, keepdims=True)
    return lo, hi


def _half_cum(hm, col_lo, col_hi):
    cols = _pack(hm, col_lo, col_hi)
    cum_lo, cum_hi = _half_sums(hm, jnp.where(hm["incl"], _rows(hm, cols), 0.0))
    cum_rows = jnp.sum(jnp.where(hm["incl_t"], cols, 0.0), axis=0, keepdims=True)
    return cum_lo, cum_hi, cum_rows


def _blockdiag(hm, x):
    return jnp.concatenate([jnp.where(hm["hi"], 0.0, x).astype(BF16),
                            jnp.where(hm["hi"], x, 0.0).astype(BF16)], axis=0)


def _chunk_rows(c):
    if isinstance(c, int):
        return pl.ds(c * CHUNK, CHUNK)
    return pl.ds(pl.multiple_of(c * CHUNK, CHUNK), CHUNK)


def _block_loop(n, body, size):
    if n <= size:
        body(list(range(n)))
        return

    def block(blk, carry):
        body([blk * size + j for j in range(size)])
        return carry

    lax.fori_loop(0, n // size, block, 0)


def _mod_index(i, rows_per_tile):
    p_tiles = N_PROMPT // rows_per_tile
    s_tiles = DEC_SEQ // rows_per_tile
    return jnp.where(i < p_tiles, 0, 1 + (i - p_tiles) // s_tiles)


def _ada_kernel(c_ref, w_ref, b_ref, o_ref):
    c = c_ref[...]
    o_ref[...] = _bdot(_silu(c), w_ref[...]) + b_ref[...]


def _ada(cond, ada_w, ada_b):
    nb = 6
    return pl.pallas_call(
        _ada_kernel,
        grid=(DEPTH, nb),
        in_specs=[pl.BlockSpec((N_COND, D_MODEL), lambda l, j: (0, 0)),
                  pl.BlockSpec((None, D_MODEL, D_MODEL), lambda l, j: (l, 0, j)),
                  pl.BlockSpec((None, 1, D_MODEL), lambda l, j: (l, 0, j))],
        out_specs=pl.BlockSpec((None, N_COND, D_MODEL), lambda l, j: (l, 0, j)),
        out_shape=jax.ShapeDtypeStruct((DEPTH, N_COND, 6 * D_MODEL), F32),
        compiler_params=_cparams("arbitrary", "arbitrary"),
        name="ada",
    )(cond, ada_w, ada_b.reshape(DEPTH, 1, 6 * D_MODEL))


def _token_operand(x, tm):
    if not isinstance(x, tuple):
        return [pl.BlockSpec((tm, x.shape[1]), lambda i: (i, 0))], [x]
    pt = N_PROMPT // tm
    width = x[0].shape[1]
    return ([pl.BlockSpec((tm, width), lambda i: (jnp.minimum(i, pt - 1), 0)),
             pl.BlockSpec((tm, width), lambda i: (jnp.maximum(i - pt, 0), 0))], list(x))


def _take_tile(refs, split, tm):
    if not split:
        return refs.pop(0)[...]
    p_ref, s_ref = refs.pop(0), refs.pop(0)
    return jnp.where(pl.program_id(0) < N_PROMPT // tm, p_ref[...], s_ref[...])


def _inproj_kernel(*refs, n_out, split, tm):
    refs = list(refs)
    x = _take_tile(refs, split, tm)
    g_ref, mod_ref = refs[:2]
    w_refs, o_refs = refs[2:2 + n_out], refs[2 + n_out:]
    mod = mod_ref[...]
    h = _rms(x, g_ref[...]) * (1.0 + mod[:, D_MODEL:2 * D_MODEL]) + mod[:, :D_MODEL]
    hb = h.astype(BF16)
    for w_ref, o_ref in zip(w_refs, o_refs):
        o_ref[...] = _dot(hb, w_ref[...])


def _inproj(x, gain, mod4, layer, weights):
    n_out = len(weights)
    tm = ROW_TILE
    in_specs, args = _token_operand(x, tm)
    in_specs += [pl.BlockSpec((1, D_MODEL), lambda i: (0, 0)),
                 pl.BlockSpec((None, None, 1, 2 * D_MODEL), lambda i: (layer, _mod_index(i, tm), 0, 0))]
    in_specs += [pl.BlockSpec(w.shape, lambda i: (0, 0)) for w in weights]
    return pl.pallas_call(
        functools.partial(_inproj_kernel, n_out=n_out, split=isinstance(x, tuple), tm=tm),
        grid=(N_TOK // tm,),
        in_specs=in_specs,
        out_specs=[pl.BlockSpec((tm, w.shape[1]), lambda i: (i, 0)) for w in weights],
        out_shape=[jax.ShapeDtypeStruct((N_TOK, w.shape[1]), F32) for w in weights],
        compiler_params=_cparams("arbitrary"),
        name="inproj",
    )(*args, gain.reshape(1, D_MODEL), mod4, *weights)


def _mlstm_kernel(*refs, T, NS, has_init, emit_state):
    it = iter(refs)
    ib_ref, fb_ref = next(it), next(it)
    m0_ref = next(it) if has_init else None
    q_ref, k_ref, v_ref, o_ref, misc_ref, gn_ref = (next(it) for _ in range(6))
    c0_ref, n0_ref = (next(it), next(it)) if has_init else (None, None)
    hm_ref = next(it)
    cout_ref, nout_ref, mout_ref = (next(it), next(it), next(it)) if emit_state else (None, None, None)
    (hb_scr, c_scr, n_scr, m_scr, num_scr, st_scr, kv_scr, nl_scr, bl_scr, gm_scr) = (next(it) for _ in range(10))

    b = pl.program_id(0)
    h = pl.program_id(1)
    nc = T // CHUNK
    scale = ML_DK ** -0.5

    for s in range(NS):
        for d in range(2):
            if has_init:
                c_scr[2 * s + d] = c0_ref[s, d]
                n_scr[2 * s + d] = n0_ref[s, d]
                m_scr[2 * s + d] = jnp.full((1, LANES), m0_ref[b * NS + s, d, h], F32)
            else:
                c_scr[2 * s + d] = jnp.zeros((ML_DK, ML_DV), F32)
                n_scr[2 * s + d] = jnp.zeros((1, ML_DK), F32)
                m_scr[2 * s + d] = jnp.zeros((1, LANES), F32)

    hm = _half_masks(False, True)

    def pre_gates(c):
        rows = _chunk_rows(c)
        mi = misc_ref[rows, :]
        li = [_colsel(mi, d * ML_HEADS + h) + ib_ref[d, h] for d in range(2)]
        lf = [-_softplus(-(_colsel(mi, 2 * ML_HEADS + d * ML_HEADS + h) + fb_ref[d, h])) for d in range(2)]
        b_cols = [None, None]
        b_cols[0], b_cols[1], b_rows = _half_cum(hm, lf[0], lf[1])
        li_rows = _rows(hm, _pack(hm, li[0], li[1]))
        dm = jnp.where(hm["incl"], _pack(hm, b_cols[0], b_cols[1]) - b_rows + li_rows, NEG_INF)
        mloc = [jnp.max(jnp.where(hm["hi"], NEG_INF, dm), axis=1, keepdims=True),
                jnp.max(jnp.where(hm["hi"], dm, NEG_INF), axis=1, keepdims=True)]
        kc = k_ref[rows, :]
        kws = []
        for d in range(2):
            bl = jnp.sum(lf[d], axis=0, keepdims=True)
            g = bl - b_cols[d] + li[d]
            gmax = jnp.max(g, axis=0, keepdims=True)
            kw = jnp.exp(g - gmax) * kc
            kws.append(kw.astype(BF16))
            nl_scr[d, c] = jnp.sum(kw, axis=0, keepdims=True)
            bl_scr[d, c] = _lanes(bl)
            gm_scr[d, c] = _lanes(gmax)
            st_scr[d, 0, rows, :] = _lanes(mloc[d])
            st_scr[d, 1, rows, :] = _lanes(b_cols[d])
        return dict(c=c, rows=rows, kc=kc.astype(BF16), kw2=jnp.concatenate(kws, axis=1),
                    e2=jnp.exp(dm - _pack(hm, mloc[0], mloc[1])))

    def pre_block(chunks):
        chains = [pre_gates(c) for c in chunks]
        for ch in chains:
            qc = (q_ref[ch["rows"], :] * scale).astype(BF16)
            ch["vb"] = v_ref[ch["rows"], :].astype(BF16)
            ch["qk2"] = _dot_nt(qc, jnp.concatenate([ch["kc"], ch["kc"]], axis=0))
            kv2 = _dot_tn(ch["kw2"], ch["vb"])
            kv_scr[0, ch["c"]] = kv2[:ML_DK]
            kv_scr[1, ch["c"]] = kv2[ML_DK:]
        for ch in chains:
            s2 = ch["qk2"] * ch["e2"]
            ch["s2"] = s2.astype(BF16)
            dens = _half_sums(hm, s2)
            for d in range(2):
                st_scr[d, 2, ch["rows"], :] = _lanes(dens[d])
        for ch in chains:
            vb = ch["vb"]
            zero = jnp.zeros_like(vb)
            vbd = jnp.concatenate([jnp.concatenate([vb, zero], axis=1),
                                   jnp.concatenate([zero, vb], axis=1)], axis=0)
            num2 = _dot(ch["s2"], vbd)
            num_scr[0, ch["rows"], :] = num2[:, :ML_DV]
            num_scr[1, ch["rows"], :] = num2[:, ML_DV:]

    def step_block(chunks):
        states = [(m_scr[j], c_scr[j], n_scr[j]) for j in range(2 * NS)]
        pairs = []
        for c in chunks:
            for s in range(NS):
                pair = []
                for d in range(2):
                    cc = s * nc + (c if d == 0 else nc - 1 - c)
                    rows = _chunk_rows(cc)
                    m, c_st, n_st = states[2 * s + d]
                    mloc, b_col, den_loc = st_scr[d, 0, rows, :], st_scr[d, 1, rows, :], st_scr[d, 2, rows, :]
                    bl, gmax = bl_scr[d, cc], gm_scr[d, cc]
                    qc = q_ref[rows, :] * scale
                    inter = b_col + m
                    mq = jnp.maximum(inter, mloc)
                    a = jnp.exp(inter - mq)
                    f = jnp.exp(mloc - mq)
                    den = f * den_loc + a * jnp.sum(qc * n_st, axis=1, keepdims=True)
                    pair.append(dict(rows=rows, qc=qc.astype(BF16), c_st=c_st.astype(BF16), a=a, f=f,
                                     inv=1.0 / jnp.maximum(jnp.abs(den), jnp.exp(-mq))))
                    m_new = jnp.maximum(bl + m, gmax)
                    dec = jnp.exp(bl + m - m_new)
                    fk = jnp.exp(gmax - m_new)
                    states[2 * s + d] = (m_new, dec * c_st + fk * kv_scr[d, cc], dec * n_st + fk * nl_scr[d, cc])
                pairs.append(pair)
        for j in range(2 * NS):
            m_scr[j], c_scr[j], n_scr[j] = states[j]
        for pair in pairs:
            lhs = jnp.concatenate([pair[0]["qc"], pair[1]["qc"]], axis=0)
            rhs = jnp.concatenate([pair[0]["c_st"], pair[1]["c_st"]], axis=1)
            pair.append(_dot(lhs, rhs))
        for pair in pairs:
            res = pair[2]
            for d, dst in enumerate((hm_ref, hb_scr)):
                it_ = pair[d]
                qc_c = res[d * CHUNK:(d + 1) * CHUNK, d * ML_DV:(d + 1) * ML_DV]
                dst[it_["rows"], :] = (it_["f"] * num_scr[d, it_["rows"], :] + it_["a"] * qc_c) * it_["inv"]

    _block_loop(NS * nc, pre_block, PRE_CHUNKS)
    _block_loop(nc, step_block, STEP_CHUNKS)

    hs = hm_ref[...] + hb_scr[...]
    hm_ref[...] = _rms(hs, gn_ref[...]) * jax.nn.sigmoid(o_ref[...])
    if emit_state:
        for s in range(NS):
            for d in range(2):
                cout_ref[s, d] = c_scr[2 * s + d]
                nout_ref[s, d] = n_scr[2 * s + d]
                mout_ref[s, d] = m_scr[2 * s + d][:, 0:1]


def _seq_geometry(prompt, prompt_seqs_per_step=1, sample_seqs_per_step=1):
    if prompt:
        return SEQ, BATCH, prompt_seqs_per_step, 0
    return DEC_SEQ, DEC_BATCH, sample_seqs_per_step, N_PROMPT // (sample_seqs_per_step * DEC_SEQ)


def _mlstm(qkvo, misc, i_bias, f_bias, gnorm, *, prompt, init=None):
    T, B, NS, blk0 = _seq_geometry(prompt)
    R = NS * T
    nc = R // CHUNK
    has_init = init is not None
    smem = pl.BlockSpec(memory_space=pltpu.SMEM)

    def col(j):
        return pl.BlockSpec((R, LANES), lambda b, h: (blk0 + b, j * ML_HEADS + h))

    in_specs = [smem, smem]
    args = [i_bias, f_bias]
    if has_init:
        in_specs.append(smem)
        args.append(init[2])
    in_specs += [col(0), col(1), col(2), col(3),
                 pl.BlockSpec((R, LANES), lambda b, h: (blk0 + b, 0)),
                 pl.BlockSpec((None, 1, ML_DV), lambda b, h: (h, 0, 0))]
    args += [qkvo, qkvo, qkvo, qkvo, misc, gnorm.reshape(ML_HEADS, 1, ML_DV)]
    c_spec = pl.BlockSpec((NS, 2, None, ML_DK, ML_DV), lambda b, h: (b, 0, h, 0, 0))
    n_spec = pl.BlockSpec((NS, 2, None, 1, ML_DK), lambda b, h: (b, 0, h, 0, 0))
    if has_init:
        in_specs += [c_spec, n_spec]
        args += [init[0], init[1].reshape(B, 2, ML_HEADS, 1, ML_DK)]
    out_specs = [pl.BlockSpec((R, LANES), lambda b, h: (b, h))]
    out_shape = [jax.ShapeDtypeStruct((B * T, ML_HEADS * ML_DV), F32)]
    if prompt:
        out_specs += [c_spec, n_spec, pl.BlockSpec((NS, 2, None, 1, 1), lambda b, h: (b, 0, h, 0, 0))]
        out_shape += [jax.ShapeDtypeStruct((B, 2, ML_HEADS, ML_DK, ML_DV), F32),
                      jax.ShapeDtypeStruct((B, 2, ML_HEADS, 1, ML_DK), F32),
                      jax.ShapeDtypeStruct((B, 2, ML_HEADS, 1, 1), F32)]
    return pl.pallas_call(
        functools.partial(_mlstm_kernel, T=T, NS=NS, has_init=has_init, emit_state=prompt),
        grid=(B // NS, ML_HEADS),
        in_specs=in_specs,
        out_specs=out_specs,
        out_shape=out_shape,
        scratch_shapes=[pltpu.VMEM((R, ML_DV), F32),
                        pltpu.VMEM((2 * NS, ML_DK, ML_DV), F32),
                        pltpu.VMEM((2 * NS, 1, ML_DK), F32),
                        pltpu.VMEM((2 * NS, 1, LANES), F32),
                        pltpu.VMEM((2, R, ML_DV), F32),
                        pltpu.VMEM((2, 3, R, LANES), F32),
                        pltpu.VMEM((2, nc, ML_DK, ML_DV), F32),
                        pltpu.VMEM((2, nc, 1, ML_DK), F32),
                        pltpu.VMEM((2, nc, 1, LANES), F32),
                        pltpu.VMEM((2, nc, 1, LANES), F32)],
        compiler_params=_cparams("arbitrary", "arbitrary"),
        name="mlstm_prompt" if prompt else "mlstm_sample",
    )(*args)


def _rope(x, cos, sin_signed):
    lane = lax.broadcasted_iota(jnp.int32, x.shape, 1)
    first = (lane & 15) < 8
    partner = jnp.where(first, pltpu.roll(x, LANES - 8, axis=1), pltpu.roll(x, 8, axis=1))
    return x * cos + partner * sin_signed


def _q_kernel(cq_ref, g_ref, w_ref, cos_ref, sin_ref, q_ref):
    cq = _rms(cq_ref[...], g_ref[...])
    y = _bdot(cq, w_ref[...])
    cos, sin = cos_ref[...], sin_ref[...]
    for hd in range(MLA_HEADS):
        sl = slice(hd * LANES, (hd + 1) * LANES)
        q_ref[:, sl] = _rope(y[:, sl], cos, sin).astype(BF16)


def _rope_block_index(i, tm):
    p_tiles = N_PROMPT // tm
    s_tiles = DEC_SEQ // tm
    return jnp.where(i < p_tiles, 0, 1 + (i - p_tiles) % s_tiles)


def _mla_q(cq, gain, w_uq_pad, cos_tab, sin_tab):
    tm = ROW_TILE
    tab = pl.BlockSpec((tm, LANES), lambda i: (_rope_block_index(i, tm), 0))
    return pl.pallas_call(
        _q_kernel,
        grid=(N_TOK // tm,),
        in_specs=[pl.BlockSpec((tm, MLA_Q_RANK), lambda i: (i, 0)),
                  pl.BlockSpec((1, MLA_Q_RANK), lambda i: (0, 0)),
                  pl.BlockSpec(w_uq_pad.shape, lambda i: (0, 0)),
                  tab, tab],
        out_specs=pl.BlockSpec((tm, MLA_HEADS * LANES), lambda i: (i, 0)),
        out_shape=jax.ShapeDtypeStruct((N_TOK, MLA_HEADS * LANES), BF16),
        compiler_params=_cparams("arbitrary"),
        name="mla_q",
    )(cq, gain.reshape(1, MLA_Q_RANK), w_uq_pad, cos_tab, sin_tab)


def _kv_kernel(*refs, norm):
    if norm:
        ckv_ref, kpe_ref, g_ref, wk_ref, wv_ref, cos_ref, sin_ref, ckvn_ref, k_ref, v_ref = refs
        c = _rms(ckv_ref[...], g_ref[...])
        ckvn_ref[...] = c
    else:
        ckv_ref, kpe_ref, wk_ref, wv_ref, k_ref, v_ref = refs
        c = ckv_ref[...]
    kp = kpe_ref[...]
    lane = lax.broadcasted_iota(jnp.int32, kp.shape, 1)
    kp = jnp.where((lane >= MLA_NOPE) & (lane < MLA_NOPE + MLA_ROPE), kp, 0.0)
    if norm:
        kp = _rope(kp, cos_ref[...], sin_ref[...])
    kn = _bdot(c, wk_ref[...])
    for hd in range(MLA_HEADS):
        sl = slice(hd * LANES, (hd + 1) * LANES)
        k_ref[:, sl] = (kn[:, sl] + kp).astype(BF16)
    v_ref[...] = _bdot(c, wv_ref[...]).astype(BF16)


def _mla_kv(ckv, kpe128, w_uk_pad, w_uv, gain=None, cos_tab=None, sin_tab=None):
    norm = gain is not None
    n = ckv.shape[0]
    tm = ROW_TILE
    row = lambda w: pl.BlockSpec((tm, w), lambda i: (i, 0))
    full = lambda a: pl.BlockSpec(a.shape, lambda i: (0, 0))
    in_specs = [row(MLA_KV_RANK), row(LANES)]
    args = [ckv, kpe128]
    if norm:
        in_specs.append(pl.BlockSpec((1, MLA_KV_RANK), lambda i: (0, 0)))
        args.append(gain.reshape(1, MLA_KV_RANK))
    in_specs += [full(w_uk_pad), full(w_uv)]
    args += [w_uk_pad, w_uv]
    out_specs = [row(MLA_HEADS * LANES), row(MLA_HEADS * MLA_V)]
    out_shape = [jax.ShapeDtypeStruct((n, MLA_HEADS * LANES), BF16),
                 jax.ShapeDtypeStruct((n, MLA_HEADS * MLA_V), BF16)]
    if norm:
        tab = pl.BlockSpec((tm, LANES), lambda i: (_rope_block_index(i, tm), 0))
        in_specs += [tab, tab]
        args += [cos_tab, sin_tab]
        out_specs = [row(MLA_KV_RANK)] + out_specs
        out_shape = [jax.ShapeDtypeStruct((n, MLA_KV_RANK), F32)] + out_shape
    return pl.pallas_call(
        functools.partial(_kv_kernel, norm=norm),
        grid=(n // tm,),
        in_specs=in_specs,
        out_specs=out_specs,
        out_shape=out_shape,
        compiler_params=_cparams("arbitrary"),
        name="mla_kv" if norm else "mla_kv_cache",
    )(*args)


def _attn_kernel(q_ref, k_ref, v_ref, o_ref, *, heads):
    scale = (MLA_NOPE + MLA_ROPE) ** -0.5
    scores = [_dot_nt(q_ref[:, j * LANES:(j + 1) * LANES], k_ref[:, j * LANES:(j + 1) * LANES]) * scale
              for j in range(heads)]
    probs, sums = [], []
    for s in scores:
        p = jnp.exp(s - jnp.max(s, axis=1, keepdims=True))
        sums.append(jnp.sum(p, axis=1, keepdims=True))
        probs.append(p.astype(BF16))
    pvs = [_dot(probs[j], v_ref[:, j * MLA_V:(j + 1) * MLA_V]) for j in range(heads)]
    o_ref[...] = jnp.concatenate([pvs[j] / sums[j] for j in range(heads)], axis=1)


def _attention(q, k, v, *, B, Tq, Tk, tq, q_row0, heads):
    nq = Tq // tq
    blk0 = q_row0 // tq
    return pl.pallas_call(
        functools.partial(_attn_kernel, heads=heads),
        grid=(B, MLA_HEADS // heads, nq),
        in_specs=[pl.BlockSpec((tq, heads * LANES), lambda b, hp, i: (blk0 + b * nq + i, hp)),
                  pl.BlockSpec((Tk, heads * LANES), lambda b, hp, i: (b, hp)),
                  pl.BlockSpec((Tk, heads * MLA_V), lambda b, hp, i: (b, hp))],
        out_specs=pl.BlockSpec((tq, heads * MLA_V), lambda b, hp, i: (b * nq + i, hp)),
        out_shape=jax.ShapeDtypeStruct((B * Tq, MLA_HEADS * MLA_V), F32),
        compiler_params=_cparams("arbitrary", "arbitrary", "arbitrary"),
        name="attention",
    )(q, k, v)


def _outproj_kernel(*refs, odd, split_x, tm):
    refs = list(refs)
    a1 = _take_tile(refs, True, tm)
    a2 = _take_tile(refs, True, tm)
    x = _take_tile(refs, split_x, tm)
    if odd:
        z_ref, gn_ref = refs.pop(0), refs.pop(0)
        a1 = _rms(a1 * _silu(z_ref[...]), gn_ref[...])
    w1_ref, w2_ref, g1_ref, o_ref = refs
    out = _bdot(a1, w1_ref[...]) + _bdot(a2, w2_ref[...])
    o_ref[...] = x + g1_ref[...] * out


def _outproj(a1, a2, w1, w2, x, mod4, layer, z=None, gnorm=None):
    odd = z is not None
    tm = ROW_TILE
    half = w1.shape[0]
    full = lambda a: pl.BlockSpec(a.shape, lambda i: (0, 0))
    in_specs, args = [], []
    for operand in (a1, a2, x):
        specs, arrays = _token_operand(operand, tm)
        in_specs += specs
        args += arrays
    if odd:
        in_specs += [pl.BlockSpec((tm, half), lambda i: (i, 0)), pl.BlockSpec((1, half), lambda i: (0, 0))]
        args += [z, gnorm.reshape(1, half)]
    in_specs += [full(w1), full(w2),
                 pl.BlockSpec((None, None, 1, D_MODEL), lambda i: (layer, _mod_index(i, tm), 0, 2))]
    args += [w1, w2, mod4]
    return pl.pallas_call(
        functools.partial(_outproj_kernel, odd=odd, split_x=isinstance(x, tuple), tm=tm),
        grid=(N_TOK // tm,),
        in_specs=in_specs,
        out_specs=pl.BlockSpec((tm, D_MODEL), lambda i: (i, 0)),
        out_shape=jax.ShapeDtypeStruct((N_TOK, D_MODEL), F32),
        compiler_params=_cparams("arbitrary"),
        name="outproj",
    )(*args)


def _conv_silu(src_ref, dst_ref, pad_ref, w_ref, b_ref, T, NS):
    width = src_ref.shape[1]
    zeros = jnp.zeros((CONV_HALO, width), F32)
    pad_ref[pl.ds(0, CONV_HALO), :] = zeros
    pad_ref[pl.ds(CONV_HALO + T, CONV_HALO), :] = zeros
    w = w_ref[...]
    bias = b_ref[...]
    for s in range(NS):
        pad_ref[pl.ds(CONV_HALO, T), :] = src_ref[pl.ds(s * T, T), :]
        for r0 in range(0, T, CONV_ROWS):
            acc = bias
            for j in range(CONV_W):
                start = r0 + CONV_HALO + j - CONV_W // 2
                acc = acc + w[j:j + 1, :] * pad_ref[pl.ds(start, CONV_ROWS), :]
            dst_ref[pl.ds(s * T + r0, CONV_ROWS), :] = _silu(acc)


def _ssd_kernel(*refs, T, NS, has_init, emit_state):
    it = iter(refs)
    x_ref, b_ref, c_ref = next(it), next(it), next(it)
    wx_ref, wb_ref, wc_ref = next(it), next(it), next(it)
    bx_ref, bb_ref, bc_ref = next(it), next(it), next(it)
    misc_ref, dtb_ref, alog_ref, dskip_ref = next(it), next(it), next(it), next(it)
    h0_ref = next(it) if has_init else None
    y_ref = next(it)
    hout_ref = next(it) if emit_state else None
    (pad_scr, xc_scr, bc_scr, cc_scr, yb_scr, hs_scr, yi_scr, ea_scr, upd_scr, eal_scr) = (
        next(it) for _ in range(10))

    p = pl.program_id(1)
    nc = T // CHUNK

    _conv_silu(x_ref, xc_scr, pad_scr, wx_ref, bx_ref, T, NS)
    _conv_silu(b_ref, bc_scr, pad_scr, wb_ref, bb_ref, T, NS)
    _conv_silu(c_ref, cc_scr, pad_scr, wc_ref, bc_ref, T, NS)

    for s in range(NS):
        for d in range(2):
            if has_init:
                hs_scr[2 * s + d] = h0_ref[s, d].reshape(2 * SSD_P, SSD_N).T
            else:
                hs_scr[2 * s + d] = jnp.zeros((SSD_N, 2 * SSD_P), F32)

    dtb = dtb_ref[...]
    neg_a = -jnp.exp(alog_ref[...])
    hms = [_half_masks(False, False), _half_masks(True, True)]
    hi_row = lax.broadcasted_iota(jnp.int32, (1, LANES), 1) >= CHUNK

    def pre_block(chunks):
        chains = []
        for c in chunks:
            rows = _chunk_rows(c)
            bc = bc_scr[rows, :].astype(BF16)
            cb2 = _dot_nt(cc_scr[rows, :].astype(BF16), jnp.concatenate([bc, bc], axis=0))
            chains.append(dict(c=c, rows=rows, bc=bc, cb2=cb2))
        for ch in chains:
            rows = ch["rows"]
            xc = xc_scr[rows, :]
            dts = _softplus(misc_ref[rows, :] + dtb)
            adt = dts * neg_a
            ch["x2m"] = _blockdiag(hms[0], xc)
            ch["g2"], ch["xw"] = [], []
            for d in range(2):
                hm = hms[d]
                dt_h = [_colsel(dts, d * SSD_HEADS + 2 * p + hh) for hh in range(2)]
                a_h = [_colsel(adt, d * SSD_HEADS + 2 * p + hh) for hh in range(2)]
                cum0, cum1, cum_rows = _half_cum(hm, a_h[0], a_h[1])
                seg2 = jnp.exp(jnp.where(hm["incl"], _pack(hm, cum0, cum1) - cum_rows, NEG_INF))
                dt_rows = _rows(hm, _pack(hm, dt_h[0], dt_h[1]))
                ch["g2"].append((ch["cb2"] * seg2 * dt_rows).astype(BF16))
                al = [jnp.sum(a, axis=0, keepdims=True) for a in a_h]
                wgt = [jnp.exp(al[hh] - cum) * dt_h[hh] for hh, cum in enumerate((cum0, cum1))]
                ch["xw"].append((xc * _pack(hm, wgt[0], wgt[1])).astype(BF16))
                ea_scr[d, rows, :] = _pack(hm, jnp.exp(cum0), jnp.exp(cum1))
                eal_scr[d, ch["c"]] = jnp.where(hi_row, jnp.exp(al[1]), jnp.exp(al[0]))
        for ch in chains:
            for d in range(2):
                yi_scr[d, ch["rows"], :] = _dot(ch["g2"][d], ch["x2m"])
                upd_scr[d, ch["c"]] = _dot_tn(ch["bc"], ch["xw"][d])

    def step_block(chunks):
        states = [hs_scr[j] for j in range(2 * NS)]
        pairs = []
        for c in chunks:
            for s in range(NS):
                pair = []
                for d in range(2):
                    cc = s * nc + (c if d == 0 else nc - 1 - c)
                    pair.append(dict(rows=_chunk_rows(cc), hs=states[2 * s + d].astype(BF16)))
                    states[2 * s + d] = eal_scr[d, cc] * states[2 * s + d] + upd_scr[d, cc]
                pairs.append(pair)
        for j in range(2 * NS):
            hs_scr[j] = states[j]
        for pair in pairs:
            lhs = jnp.concatenate([cc_scr[pair[d]["rows"], :].astype(BF16) for d in range(2)], axis=0)
            rhs = jnp.concatenate([pair[d]["hs"] for d in range(2)], axis=1)
            pair.append(_dot(lhs, rhs))
        for pair in pairs:
            for d, dst in enumerate((y_ref, yb_scr)):
                rows = pair[d]["rows"]
                ch = pair[2][d * CHUNK:(d + 1) * CHUNK, d * LANES:(d + 1) * LANES]
                dst[rows, :] = yi_scr[d, rows, :] + ea_scr[d, rows, :] * ch

    _block_loop(NS * nc, pre_block, PRE_CHUNKS)
    _block_loop(nc, step_block, STEP_CHUNKS)

    y_ref[...] = y_ref[...] + yb_scr[...] + dskip_ref[...] * xc_scr[...]
    if emit_state:
        for s in range(NS):
            for d in range(2):
                hout_ref[s, d] = hs_scr[2 * s + d].T.reshape(2, SSD_P, SSD_N)


def _lane_row(v):
    return jnp.pad(v.astype(F32), (0, LANES - v.shape[0])).reshape(1, LANES)


def _ssd(xbc, misc, conv_w, conv_b, dt_bias, a_log, d_skip, *, prompt, init=None):
    T, B, NS, blk0 = _seq_geometry(prompt)
    R = NS * T
    nc = R // CHUNK
    has_init = init is not None
    n_pairs = SSD_HEADS // 2
    pairs_per_group = n_pairs // SSD_GROUPS
    xb = SSD_HEADS * SSD_P // LANES
    cb = xb + SSD_GROUPS * SSD_N // LANES
    colx = lambda b, p: (blk0 + b, p)
    colb = lambda b, p: (blk0 + b, xb + p // pairs_per_group)
    colc = lambda b, p: (blk0 + b, cb + p // pairs_per_group)
    wsel = lambda f: (lambda b, p: (0, f(b, p)[1]))
    row128 = pl.BlockSpec((1, LANES), lambda b, p: (0, 0))
    in_specs = [pl.BlockSpec((R, LANES), colx), pl.BlockSpec((R, LANES), colb), pl.BlockSpec((R, LANES), colc),
                pl.BlockSpec((CONV_W, LANES), wsel(colx)), pl.BlockSpec((CONV_W, LANES), wsel(colb)),
                pl.BlockSpec((CONV_W, LANES), wsel(colc)),
                pl.BlockSpec((1, LANES), wsel(colx)), pl.BlockSpec((1, LANES), wsel(colb)),
                pl.BlockSpec((1, LANES), wsel(colc)),
                pl.BlockSpec((R, LANES), lambda b, p: (blk0 + b, 0)), row128, row128,
                pl.BlockSpec((1, LANES), lambda b, p: (0, p))]
    cb2 = conv_b.reshape(1, -1)
    args = [xbc, xbc, xbc, conv_w, conv_w, conv_w, cb2, cb2, cb2, misc,
            _lane_row(dt_bias.reshape(-1)), _lane_row(a_log.reshape(-1)),
            jnp.repeat(d_skip, SSD_P).reshape(1, SSD_HEADS * SSD_P)]
    state_spec = pl.BlockSpec((NS, 2, 2, SSD_P, SSD_N), lambda b, p: (b, 0, p, 0, 0))
    if has_init:
        in_specs.append(state_spec)
        args.append(init)
    out_specs = [pl.BlockSpec((R, LANES), lambda b, p: (b, p))]
    out_shape = [jax.ShapeDtypeStruct((B * T, SSD_HEADS * SSD_P), F32)]
    if prompt:
        out_specs.append(state_spec)
        out_shape.append(jax.ShapeDtypeStruct((B, 2, SSD_HEADS, SSD_P, SSD_N), F32))
    return pl.pallas_call(
        functools.partial(_ssd_kernel, T=T, NS=NS, has_init=has_init, emit_state=prompt),
        grid=(B // NS, n_pairs),
        in_specs=in_specs,
        out_specs=out_specs,
        out_shape=out_shape,
        scratch_shapes=[pltpu.VMEM((T + 2 * CONV_HALO, LANES), F32)]
        + [pltpu.VMEM((R, LANES), F32) for _ in range(4)]
        + [pltpu.VMEM((2 * NS, SSD_N, 2 * SSD_P), F32),
           pltpu.VMEM((2, R, LANES), F32),
           pltpu.VMEM((2, R, LANES), F32),
           pltpu.VMEM((2, nc, SSD_N, 2 * SSD_P), F32),
           pltpu.VMEM((2, nc, 1, LANES), F32)],
        compiler_params=_cparams("arbitrary", "arbitrary"),
        name="ssd_prompt" if prompt else "ssd_sample",
    )(*args)


def _tri_inverse(hm, nmat, eye_f):
    levels = int(np.log2(CHUNK))

    def off(level):
        same_big = lax.shift_right_logical(hm["t"], level) == lax.shift_right_logical(hm["s"], level)
        same_small = lax.shift_right_logical(hm["t"], level - 1) == lax.shift_right_logical(hm["s"], level - 1)
        return jnp.where(same_big & jnp.logical_not(same_small), nmat, 0.0)

    state = dict(dinv=eye_f - off(1))

    def first(level):
        def run():
            state["t1"] = _dot(state["dinv"].astype(BF16), _blockdiag(hm, off(level)))
        return run

    def second():
        state["dinv"] = state["dinv"] - _dot(state["t1"].astype(BF16), _blockdiag(hm, state["dinv"]))

    stages = []
    for level in range(2, levels + 1):
        stages += [first(level), second]
    return state, stages


def _gdn_kernel(*refs, T, NS, has_init, emit_state):
    it = iter(refs)
    q_ref, k_ref, v_ref = next(it), next(it), next(it)
    wq_ref, wk_ref, wv_ref = next(it), next(it), next(it)
    bq_ref, bk_ref, bv_ref = next(it), next(it), next(it)
    misc_ref, dtb_ref, alog_ref, z_ref, gn_ref = (next(it) for _ in range(5))
    s0_ref = next(it) if has_init else None
    o_ref = next(it)
    sout_ref = next(it) if emit_state else None
    (pad_scr, qc_scr, kc_scr, vc_scr, ob_scr, s_scr,
     u0_scr, wq_scr, kcf_scr, p2_scr, egl_scr) = (next(it) for _ in range(11))

    h = pl.program_id(1)
    nc = T // CHUNK

    _conv_silu(q_ref, qc_scr, pad_scr, wq_ref, bq_ref, T, NS)
    _conv_silu(k_ref, kc_scr, pad_scr, wk_ref, bk_ref, T, NS)
    _conv_silu(v_ref, vc_scr, pad_scr, wv_ref, bv_ref, T, NS)
    q = qc_scr[...]
    qc_scr[...] = q * lax.rsqrt(jnp.sum(q * q, axis=-1, keepdims=True) + EPS) * (GDN_DK ** -0.5)
    k = kc_scr[...]
    kc_scr[...] = k * lax.rsqrt(jnp.sum(k * k, axis=-1, keepdims=True) + EPS)

    for s in range(NS):
        for d in range(2):
            if has_init:
                s_scr[2 * s + d] = s0_ref[s, d]
            else:
                s_scr[2 * s + d] = jnp.zeros((GDN_DK, GDN_DV), F32)

    dtb = dtb_ref[...]
    neg_a = -jnp.exp(alog_ref[...])
    hm = _half_masks(False, True)
    eye_f = hm["eye"].astype(F32)

    def pre_gates(c):
        rows = _chunk_rows(c)
        kc = kc_scr[rows, :]
        qc = qc_scr[rows, :]
        mi = misc_ref[rows, :]
        gdec = neg_a * _softplus(mi + dtb)
        g_src = [_colsel(gdec, 2 * SSD_HEADS + d * GDN_HEADS + h) for d in range(2)]
        beta = [jax.nn.sigmoid(_colsel(mi, 2 * SSD_HEADS + 2 * GDN_HEADS + d * GDN_HEADS + h)) for d in range(2)]
        gc = [None, None]
        gc[0], gc[1], gc_rows = _half_cum(hm, g_src[0], g_src[1])
        diff = _pack(hm, gc[0], gc[1]) - gc_rows
        beta_rows = _rows(hm, _pack(hm, beta[0], beta[1]))
        kb = kc.astype(BF16)
        ek = []
        for d in range(2):
            egc = jnp.exp(gc[d])
            gl = jnp.sum(g_src[d], axis=0, keepdims=True)
            ek.append((egc * kc).astype(BF16))
            wq_scr[d, c, CHUNK:, :] = (egc * qc).astype(BF16)
            kcf_scr[d, rows, :] = (jnp.exp(gl - gc[d]) * beta[d] * kc).astype(BF16)
            egl_scr[d, c] = _lanes(jnp.exp(gl))
        return dict(c=c, rows=rows, kb=kb, qb=qc.astype(BF16), ek=ek,
                    dec_s=jnp.exp(jnp.where(hm["strict"], diff, NEG_INF)) * beta_rows,
                    dec_i=jnp.exp(jnp.where(hm["incl"], diff, NEG_INF)) * beta_rows)

    def pre_block(chunks):
        chains = [pre_gates(c) for c in chunks]
        for ch in chains:
            kb = ch["kb"]
            ch["kq"] = _dot_nt(jnp.concatenate([kb, ch["qb"]], axis=0), jnp.concatenate([kb, kb], axis=0))
        for ch in chains:
            p2_scr[ch["rows"], :] = (ch["dec_i"] * ch["kq"][CHUNK:]).astype(BF16)
            ch["inv"], ch["stages"] = _tri_inverse(hm, ch["dec_s"] * ch["kq"][:CHUNK], eye_f)
        for i in range(len(chains[0]["stages"])):
            for ch in chains:
                ch["stages"][i]()
        for ch in chains:
            vb = vc_scr[ch["rows"], :].astype(BF16)
            zero = jnp.zeros_like(vb)
            rhs = jnp.concatenate([jnp.concatenate([vb, ch["ek"][0], zero, zero], axis=1),
                                   jnp.concatenate([zero, zero, vb, ch["ek"][1]], axis=1)], axis=0)
            uw = _dot(ch["inv"]["dinv"].astype(BF16), rhs)
            for d in range(2):
                u0_scr[d, ch["rows"], :] = uw[:, 2 * d * LANES:(2 * d + 1) * LANES]
                wq_scr[d, ch["c"], :CHUNK, :] = uw[:, (2 * d + 1) * LANES:(2 * d + 2) * LANES].astype(BF16)

    def step_block(chunks):
        states = [s_scr[j] for j in range(2 * NS)]
        for c in chunks:
            work = []
            for s in range(NS):
                cs = [s * nc + c, s * nc + nc - 1 - c]
                lhs = jnp.concatenate([wq_scr[d, cs[d]] for d in range(2)], axis=0)
                rhs = jnp.concatenate([states[2 * s + d].astype(BF16) for d in range(2)], axis=1)
                work.append(dict(s=s, cs=cs, rows=[_chunk_rows(cc) for cc in cs], a=_dot(lhs, rhs)))
            for w in work:
                rows, ub, w["qs"] = w["rows"], [], []
                for d in range(2):
                    blk = w["a"][2 * d * CHUNK:(2 * d + 2) * CHUNK, d * LANES:(d + 1) * LANES]
                    ub.append((u0_scr[d, rows[d], :] - blk[:CHUNK]).astype(BF16))
                    w["qs"].append(blk[CHUNK:])
                zero = jnp.zeros_like(ub[0])
                w["pu"] = _dot(jnp.concatenate([p2_scr[rows[d], :] for d in range(2)], axis=0),
                               jnp.concatenate([jnp.concatenate([ub[0], zero], axis=1),
                                                jnp.concatenate([zero, ub[1]], axis=1)], axis=0))
                w["ktu"] = _dot_tn(jnp.concatenate([kcf_scr[d, rows[d], :] for d in range(2)], axis=1),
                                   jnp.concatenate(ub, axis=1))
            for w in work:
                for d, dst in enumerate((o_ref, ob_scr)):
                    j = 2 * w["s"] + d
                    dst[w["rows"][d], :] = w["qs"][d] + w["pu"][d * CHUNK:(d + 1) * CHUNK, d * LANES:(d + 1) * LANES]
                    states[j] = (egl_scr[d, w["cs"][d]] * states[j]
                                 + w["ktu"][d * GDN_DK:(d + 1) * GDN_DK, d * GDN_DV:(d + 1) * GDN_DV])
        for j in range(2 * NS):
            s_scr[j] = states[j]

    _block_loop(NS * nc, pre_block, PRE_CHUNKS)
    _block_loop(nc, step_block, STEP_CHUNKS)

    og = o_ref[...] + ob_scr[...]
    o_ref[...] = _rms(og, gn_ref[...]) * _silu(z_ref[...])
    if emit_state:
        for s in range(NS):
            for d in range(2):
                sout_ref[s, d] = s_scr[2 * s + d]


def _gdn(qkv, misc, zg, conv_w, conv_b, dt_row, alog_row, gnorm, *, prompt, init=None):
    T, B, NS, blk0 = _seq_geometry(prompt, GDN_PROMPT_SEQS_PER_STEP, GDN_SAMPLE_SEQS_PER_STEP)
    R = NS * T
    nc = R // CHUNK
    has_init = init is not None
    col = lambda j: (lambda b, h: (blk0 + b, j * GDN_HEADS + h))
    wsel = lambda j: (lambda b, h: (0, j * GDN_HEADS + h))
    row128 = pl.BlockSpec((1, LANES), lambda b, h: (0, 0))
    in_specs = [pl.BlockSpec((R, LANES), col(j)) for j in range(3)]
    in_specs += [pl.BlockSpec((CONV_W, LANES), wsel(j)) for j in range(3)]
    in_specs += [pl.BlockSpec((1, LANES), wsel(j)) for j in range(3)]
    in_specs += [pl.BlockSpec((R, LANES), lambda b, h: (blk0 + b, 0)), row128, row128,
                 pl.BlockSpec((R, LANES), lambda b, h: (blk0 + b, h)), row128]
    cb2 = conv_b.reshape(1, -1)
    args = [qkv, qkv, qkv, conv_w, conv_w, conv_w, cb2, cb2, cb2, misc, dt_row, alog_row, zg,
            gnorm.reshape(1, GDN_DV)]
    state_spec = pl.BlockSpec((NS, 2, None, GDN_DK, GDN_DV), lambda b, h: (b, 0, h, 0, 0))
    if has_init:
        in_specs.append(state_spec)
        args.append(init)
    out_specs = [pl.BlockSpec((R, LANES), lambda b, h: (b, h))]
    out_shape = [jax.ShapeDtypeStruct((B * T, GDN_HEADS * GDN_DV), F32)]
    if prompt:
        out_specs.append(state_spec)
        out_shape.append(jax.ShapeDtypeStruct((B, 2, GDN_HEADS, GDN_DK, GDN_DV), F32))
    return pl.pallas_call(
        functools.partial(_gdn_kernel, T=T, NS=NS, has_init=has_init, emit_state=prompt),
        grid=(B // NS, GDN_HEADS),
        in_specs=in_specs,
        out_specs=out_specs,
        out_shape=out_shape,
        scratch_shapes=[pltpu.VMEM((T + 2 * CONV_HALO, LANES), F32)]
        + [pltpu.VMEM((R, LANES), F32) for _ in range(4)]
        + [pltpu.VMEM((2 * NS, GDN_DK, GDN_DV), F32),
           pltpu.VMEM((2, R, GDN_DV), F32),
           pltpu.VMEM((2, nc, 2 * CHUNK, GDN_DK), BF16),
           pltpu.VMEM((2, R, GDN_DK), BF16),
           pltpu.VMEM((R, LANES), BF16),
           pltpu.VMEM((2, nc, 1, LANES), F32)],
        compiler_params=_cparams("arbitrary", "arbitrary"),
        name="gdn_prompt" if prompt else "gdn_sample",
    )(*args)


def _router_kernel(x_ref, g_ref, mod_ref, whi_ref, wlo_ref, b_ref, h_ref, ri_ref, w_ref, cnt_ref, count_scr):
    @pl.when(pl.program_id(0) == 0)
    def _():
        count_scr[...] = jnp.zeros_like(count_scr)

    mod = mod_ref[...]
    h = _rms(x_ref[...], g_ref[...]) * (1.0 + mod[:, D_MODEL:2 * D_MODEL]) + mod[:, :D_MODEL]
    hb = h.astype(BF16)
    h_ref[...] = h
    hl = (h - hb.astype(F32)).astype(BF16)
    whi = whi_ref[...]
    logit = _dot(hb, whi) + (_dot(hl, whi) + _dot(hb, wlo_ref[...])) + b_ref[...]
    lane = lax.broadcasted_iota(jnp.int32, logit.shape, 1)
    is_group = (lane >= MOE_EXPERTS) & (lane < MOE_EXPERTS + MOE_GROUPS)
    glog = jnp.where(is_group, logit, NEG_INF)
    gmax = jnp.max(glog, axis=1, keepdims=True)
    gsel = jnp.min(jnp.where(glog == gmax, lane, LANES), axis=1, keepdims=True) - MOE_EXPERTS
    gw = 1.0 / jnp.sum(jnp.exp(glog - gmax), axis=1, keepdims=True)
    lo = gsel * MOE_PER_GROUP
    in_group = (lane >= lo) & (lane < lo + MOE_PER_GROUP)
    elog = jnp.where(in_group, logit, NEG_INF)
    v1 = jnp.max(elog, axis=1, keepdims=True)
    i1 = jnp.min(jnp.where(elog == v1, lane, LANES), axis=1, keepdims=True)
    elog2 = jnp.where(lane == i1, NEG_INF, elog)
    v2 = jnp.max(elog2, axis=1, keepdims=True)
    i2 = jnp.min(jnp.where(elog2 == v2, lane, LANES), axis=1, keepdims=True)
    e2 = jnp.exp(v2 - v1)
    w1 = gw / (1.0 + e2)
    w2 = gw * e2 / (1.0 + e2)
    w_ref[...] = jnp.where(lane == 0, w1, jnp.where(lane == 1, w2, 0.0))
    tm = logit.shape[0]
    onehot = jnp.where((lane == i1) | (lane == i2), 1.0, 0.0)
    t_i = lax.broadcasted_iota(jnp.int32, (tm, tm), 0)
    s_i = lax.broadcasted_iota(jnp.int32, (tm, tm), 1)
    earlier = jnp.where(s_i < t_i, 1.0, 0.0).astype(BF16)
    before = _dot(earlier, onehot.astype(BF16)) + count_scr[...]
    r1 = jnp.sum(jnp.where(lane == i1, before, 0.0), axis=1, keepdims=True).astype(jnp.int32)
    r2 = jnp.sum(jnp.where(lane == i2, before, 0.0), axis=1, keepdims=True).astype(jnp.int32)
    ri_ref[...] = jnp.where(lane == 0, i1, jnp.where(lane == 1, i2, jnp.where(lane == 2, r1, jnp.where(lane == 3, r2, 0))))
    count_scr[...] += jnp.sum(onehot, axis=0, keepdims=True)
    cnt_ref[...] = count_scr[...]


def _router(x, gain, mod4, layer, w_route, b_route):
    tm = ROW_TILE
    w_hi = w_route.astype(BF16)
    w_lo = (w_route - w_hi.astype(F32)).astype(BF16)
    return pl.pallas_call(
        _router_kernel,
        grid=(N_TOK // tm,),
        in_specs=[pl.BlockSpec((tm, D_MODEL), lambda i: (i, 0)),
                  pl.BlockSpec((1, D_MODEL), lambda i: (0, 0)),
                  pl.BlockSpec((None, None, 1, 3 * D_MODEL), lambda i: (layer, _mod_index(i, tm), 0, 1)),
                  pl.BlockSpec((D_MODEL, LANES), lambda i: (0, 0)),
                  pl.BlockSpec((D_MODEL, LANES), lambda i: (0, 0)),
                  pl.BlockSpec((1, LANES), lambda i: (0, 0))],
        out_specs=[pl.BlockSpec((tm, D_MODEL), lambda i: (i, 0)),
                   pl.BlockSpec((tm, LANES), lambda i: (i, 0)),
                   pl.BlockSpec((tm, LANES), lambda i: (i, 0)),
                   pl.BlockSpec((1, LANES), lambda i: (0, 0))],
        out_shape=[jax.ShapeDtypeStruct((N_TOK, D_MODEL), F32),
                   jax.ShapeDtypeStruct((N_TOK, LANES), jnp.int32),
                   jax.ShapeDtypeStruct((N_TOK, LANES), F32),
                   jax.ShapeDtypeStruct((1, LANES), F32)],
        scratch_shapes=[pltpu.VMEM((1, LANES), F32)],
        compiler_params=_cparams("arbitrary"),
        name="router",
    )(x, gain.reshape(1, D_MODEL), mod4, w_hi, w_lo, b_route)


N_ASSIGN = 2 * N_TOK
MOE_TILE = 512
MOE_TILES = N_ASSIGN // MOE_TILE
MOE_SEGMENTS = MOE_TILES + MOE_EXPERTS - 1
DISPATCH_ROWS = 1024
COMBINE_ROWS = 256


def _routing_tables(ri, cnt):
    counts = cnt[0, :MOE_EXPERTS].astype(jnp.int32)
    cum_end = jnp.cumsum(counts)
    cum_start = cum_end - counts
    where = (ri[:, 0:2].reshape(N_ASSIGN), ri[:, 2:4].reshape(N_ASSIGN), cum_start)
    tile_start = jnp.arange(MOE_TILES, dtype=jnp.int32) * MOE_TILE
    expert_start = cum_start[1:]
    tile_rank = jnp.arange(MOE_TILES, dtype=jnp.int32) + jnp.sum(
        (expert_start[None, :] < tile_start[:, None]).astype(jnp.int32), axis=1)
    expert_rank = jnp.arange(MOE_EXPERTS - 1, dtype=jnp.int32) + jnp.minimum(expert_start // MOE_TILE + 1, MOE_TILES)
    values = jnp.concatenate([tile_start, expert_start])
    ranks = jnp.concatenate([tile_rank, expert_rank])
    slot = jnp.arange(MOE_SEGMENTS, dtype=jnp.int32)
    cuts = jnp.sum(jnp.where(ranks[None, :] == slot[:, None], values[None, :], 0), axis=1)
    seg_lo = cuts
    seg_hi = jnp.concatenate([cuts[1:], jnp.full((1,), N_ASSIGN, jnp.int32)])
    seg_tile = jnp.minimum(seg_lo // MOE_TILE, MOE_TILES - 1)
    seg_expert = jnp.minimum(jnp.sum((cum_end[None, :] <= seg_lo[:, None]).astype(jnp.int32), axis=1),
                             MOE_EXPERTS - 1)
    seg_first = jnp.concatenate([jnp.ones((1,), jnp.int32), (seg_tile[1:] != seg_tile[:-1]).astype(jnp.int32)])
    return where, (seg_tile, seg_expert, seg_lo, seg_hi, seg_first)


def _row_copy(src_ref, src_row, dst_ref, dst_row, sem):
    return pltpu.make_async_copy(src_ref.at[pl.ds(src_row, 1), :], dst_ref.at[pl.ds(dst_row, 1), :], sem)


def _dispatch_kernel(expert_ref, rank_ref, start_ref, h_ref, xs_ref, sem):
    base = pl.program_id(0) * DISPATCH_ROWS

    def start(t, carry):
        for k in range(2):
            a = 2 * (base + t) + k
            _row_copy(h_ref, t, xs_ref, start_ref[expert_ref[a]] + rank_ref[a], sem).start()
        return carry

    def wait(t, carry):
        for k in range(2):
            _row_copy(h_ref, 0, xs_ref, 0, sem).wait()
        return carry

    lax.fori_loop(0, DISPATCH_ROWS, start, 0, unroll=4)
    lax.fori_loop(0, DISPATCH_ROWS, wait, 0, unroll=4)


def _dispatch(where, h):
    tm = DISPATCH_ROWS
    return pl.pallas_call(
        _dispatch_kernel,
        grid_spec=pltpu.PrefetchScalarGridSpec(
            num_scalar_prefetch=3,
            grid=(N_TOK // tm,),
            in_specs=[pl.BlockSpec((tm, D_MODEL), lambda i, e, r, s: (i, 0))],
            out_specs=pl.BlockSpec(memory_space=pl.ANY),
            scratch_shapes=[pltpu.SemaphoreType.DMA(())]),
        out_shape=jax.ShapeDtypeStruct((N_ASSIGN, D_MODEL), F32),
        compiler_params=_cparams("arbitrary"),
        name="moe_dispatch",
    )(*where, h)


def _experts_kernel(tile_ref, expert_ref, lo_ref, hi_ref, first_ref, xs_ref, wg_ref, wu_ref, wd_ref, o_ref):
    p = pl.program_id(0)
    lo, hi = lo_ref[p], hi_ref[p]

    @pl.when(first_ref[p] == 1)
    def _():
        o_ref[...] = jnp.zeros_like(o_ref)

    @pl.when(hi > lo)
    def _():
        x = xs_ref[...].astype(BF16)
        hg = _dot(x, wg_ref[...].astype(BF16))
        hu = _dot(x, wu_ref[...].astype(BF16))
        row = tile_ref[p] * MOE_TILE + lax.broadcasted_iota(jnp.int32, (MOE_TILE, 1), 0)
        act = jnp.where((row >= lo) & (row < hi), _silu(hg) * hu, 0.0)
        o_ref[...] += _bdot(act, wd_ref[...])


def _experts(tables, xs, wg, wu, wd, layer):
    weight = lambda shape: pl.BlockSpec((None, None) + shape, lambda p, tile, expert, lo, hi, first: (layer, expert[p], 0, 0))
    rows = pl.BlockSpec((MOE_TILE, D_MODEL), lambda p, tile, expert, lo, hi, first: (tile[p], 0))
    return pl.pallas_call(
        _experts_kernel,
        grid_spec=pltpu.PrefetchScalarGridSpec(
            num_scalar_prefetch=5,
            grid=(MOE_SEGMENTS,),
            in_specs=[rows, weight((D_MODEL, MOE_FF)), weight((D_MODEL, MOE_FF)), weight((MOE_FF, D_MODEL))],
            out_specs=rows),
        out_shape=jax.ShapeDtypeStruct((N_ASSIGN, D_MODEL), F32),
        compiler_params=_cparams("arbitrary"),
        name="moe_experts",
    )(*tables, xs, wg, wu, wd)


def _combine_kernel(expert_ref, rank_ref, start_ref, ys_ref, w_ref, x_ref, mod_ref, gf_ref, o_ref, buf, sem, *,
                    final, tile0, n_tiles):
    i = pl.program_id(0)
    slot = i % 2

    def gather(tile, into):
        base = (tile0 + tile) * COMBINE_ROWS

        def start(t, carry):
            for k in range(2):
                a = 2 * (base + t) + k
                _row_copy(ys_ref, start_ref[expert_ref[a]] + rank_ref[a], buf.at[into, k], t, sem.at[into]).start()
            return carry

        lax.fori_loop(0, COMBINE_ROWS, start, 0, unroll=4)

    @pl.when(i == 0)
    def _():
        gather(0, 0)

    @pl.when(i + 1 < n_tiles)
    def _():
        gather(i + 1, 1 - slot)

    def wait(t, carry):
        for k in range(2):
            _row_copy(ys_ref, 0, buf.at[slot, k], 0, sem.at[slot]).wait()
        return carry

    lax.fori_loop(0, COMBINE_ROWS, wait, 0, unroll=4)
    w = w_ref[...]
    y = x_ref[...] + mod_ref[...] * (w[:, 0:1] * buf[slot, 0] + w[:, 1:2] * buf[slot, 1])
    if final:
        y = _rms(y, gf_ref[...])
    o_ref[...] = y


def _combine(where, ys, wts, x, mod4, layer, norm_final, final, tile0, n_tiles):
    tm = COMBINE_ROWS
    return pl.pallas_call(
        functools.partial(_combine_kernel, final=final, tile0=tile0, n_tiles=n_tiles),
        grid_spec=pltpu.PrefetchScalarGridSpec(
            num_scalar_prefetch=3,
            grid=(n_tiles,),
            in_specs=[pl.BlockSpec(memory_space=pl.ANY),
                      pl.BlockSpec((tm, LANES), lambda i, e, r, s: (tile0 + i, 0)),
                      pl.BlockSpec((tm, D_MODEL), lambda i, e, r, s: (tile0 + i, 0)),
                      pl.BlockSpec((None, None, 1, D_MODEL),
                                   lambda i, e, r, s: (layer, _mod_index(tile0 + i, tm), 0, 5)),
                      pl.BlockSpec((1, D_MODEL), lambda i, e, r, s: (0, 0))],
            out_specs=pl.BlockSpec((tm, D_MODEL), lambda i, e, r, s: (i, 0)),
            scratch_shapes=[pltpu.VMEM((2, 2, tm, D_MODEL), F32), pltpu.SemaphoreType.DMA((2,))]),
        out_shape=jax.ShapeDtypeStruct((n_tiles * tm, D_MODEL), F32),
        compiler_params=_cparams("arbitrary"),
        name="moe_combine",
    )(*where, ys, wts, x, mod4, norm_final.reshape(1, D_MODEL))


def _rope_tables():
    t = jnp.arange(DEC_SEQ)
    pos = jnp.stack([(t // GRID_W).astype(F32), (t % GRID_W).astype(F32)], axis=1)
    nf = MLA_ROPE // 4
    inv = ROPE_BASE ** (-jnp.arange(nf, dtype=F32) / nf)
    j = jnp.arange(MLA_ROPE)
    ang = pos[:, j // (2 * nf)] * inv[j % nf][None, :]
    sign = jnp.where((j % (2 * nf)) < nf, -1.0, 1.0)
    cos = jnp.pad(jnp.cos(ang), ((0, 0), (MLA_NOPE, LANES - MLA_NOPE - MLA_ROPE)), constant_values=1.0)
    sin = jnp.pad(jnp.sin(ang) * sign, ((0, 0), (MLA_NOPE, LANES - MLA_NOPE - MLA_ROPE)))
    cos = jnp.concatenate([jnp.ones((ROW_TILE, LANES), F32), cos], axis=0)
    sin = jnp.concatenate([jnp.zeros((ROW_TILE, LANES), F32), sin], axis=0)
    return cos, sin


def _pad_heads(w, n_heads, width, lo, hi):
    k = w.shape[0]
    w = w.reshape(k, n_heads, width)[:, :, lo:hi]
    w = jnp.pad(w, ((0, 0), (0, 0), (0, LANES - (hi - lo))))
    return w.reshape(k, n_heads * LANES)


def kernel(x_prompt, x_sample, c, cache_mla_kv, cache_mla_krope, state_mlstm_C, state_mlstm_n, state_mlstm_m, state_ssd, state_gdn, c_ctx, ada_w, ada_b, norm_mix, norm_ffn, w_in_even, ml_i_bias, ml_f_bias, ml_norm, mla_q_norm, mla_w_uq, mla_kv_norm, mla_w_ukv, w_out_even, w_in_odd, ssd_conv_w, ssd_conv_b, ssd_dt_bias, ssd_A_log, ssd_D, ssd_norm, gdn_conv_w, gdn_conv_b, gdn_dt_bias, gdn_A_log, gdn_norm, w_out_odd, moe_w_group, moe_b_group, moe_w_expert, moe_b_expert, moe_w_gate, moe_w_up, moe_w_down, norm_final):
    x = (x_prompt.reshape(N_PROMPT, D_MODEL), x_sample.reshape(N_SAMPLE, D_MODEL))
    cond = jnp.concatenate([c_ctx[None, :], c, jnp.zeros((N_COND - 1 - DEC_BATCH, D_MODEL), F32)], axis=0)
    mod4 = _ada(cond, ada_w, ada_b).reshape(DEPTH, N_COND, 1, 6 * D_MODEL)
    cos_tab, sin_tab = _rope_tables()

    def moe_layer(x, layer, final):
        w_route = jnp.concatenate([moe_w_expert[layer], moe_w_group[layer]], axis=1)
        w_route = jnp.pad(w_route, ((0, 0), (0, LANES - MOE_EXPERTS - MOE_GROUPS)))
        b_route = _lane_row(jnp.concatenate([moe_b_expert[layer], moe_b_group[layer]]))
        h, ri, wts, cnt = _router(x, norm_ffn[layer], mod4, layer, w_route, b_route)
        where, tables = _routing_tables(ri, cnt)
        ys = _experts(tables, _dispatch(where, h), moe_w_gate, moe_w_up, moe_w_down, layer)
        combine = functools.partial(_combine, where, ys, wts, x, mod4, layer, norm_final, final)
        if not final:
            return combine(0, N_TOK // COMBINE_ROWS)
        p_tiles = N_PROMPT // COMBINE_ROWS
        return combine(0, p_tiles), combine(p_tiles, N_SAMPLE // COMBINE_ROWS)

    e = 0
    w = w_in_even[e]
    off = np.cumsum([0, 4 * ML_HEADS * ML_DK, 2 * ML_HEADS, 2 * ML_HEADS, MLA_Q_RANK, MLA_KV_RANK, MLA_ROPE])
    w_misc = jnp.concatenate([w[:, off[1]:off[3]],
                              jnp.zeros((D_MODEL, MLA_NOPE - 4 * ML_HEADS), F32),
                              w[:, off[5]:off[6]],
                              jnp.zeros((D_MODEL, LANES - MLA_NOPE - MLA_ROPE), F32)], axis=1)
    weights = [w[:, :off[1]].astype(BF16), w[:, off[3]:off[4]].astype(BF16),
               w[:, off[4]:off[5]].astype(BF16), w_misc.astype(BF16)]
    qkvo, cq, ckv, misc0 = _inproj(x, norm_mix[0], mod4, 0, weights)

    hm_p, st_c, st_n, st_m = _mlstm(qkvo, misc0, ml_i_bias[e], ml_f_bias[e], ml_norm[e], prompt=True)
    (hm_s,) = _mlstm(qkvo, misc0, ml_i_bias[e], ml_f_bias[e], ml_norm[e], prompt=False,
                     init=(state_mlstm_C[:, e], state_mlstm_n[:, e], state_mlstm_m[:, e]))

    dq = MLA_NOPE + MLA_ROPE
    w_uq_pad = _pad_heads(mla_w_uq[e], MLA_HEADS, dq, 0, dq).astype(BF16)
    w_uk_pad = _pad_heads(mla_w_ukv[e], MLA_HEADS, MLA_NOPE + MLA_V, 0, MLA_NOPE).astype(BF16)
    w_uv = mla_w_ukv[e].reshape(MLA_KV_RANK, MLA_HEADS, MLA_NOPE + MLA_V)[:, :, MLA_NOPE:]
    w_uv = w_uv.reshape(MLA_KV_RANK, MLA_HEADS * MLA_V).astype(BF16)
    q_cat = _mla_q(cq, mla_q_norm[e], w_uq_pad, cos_tab, sin_tab)
    ckv_n, k_cat, v_all = _mla_kv(ckv, misc0, w_uk_pad, w_uv, mla_kv_norm[e], cos_tab, sin_tab)
    cache_kpe = jnp.pad(cache_mla_krope[:, e].reshape(DEC_BATCH * PAST_LEN, MLA_ROPE),
                        ((0, 0), (MLA_NOPE, LANES - MLA_NOPE - MLA_ROPE)))
    k_cache, v_cache = _mla_kv(cache_mla_kv[:, e].reshape(DEC_BATCH * PAST_LEN, MLA_KV_RANK), cache_kpe,
                               w_uk_pad, w_uv)
    att_p = _attention(q_cat, k_cat, v_all, B=BATCH, Tq=SEQ, Tk=SEQ, tq=SEQ, q_row0=0, heads=MLA_HEADS)

    def with_cache(cache, new):
        width = new.shape[1]
        both = jnp.concatenate([cache.reshape(DEC_BATCH, PAST_LEN, width),
                                new[N_PROMPT:].reshape(DEC_BATCH, DEC_SEQ, width)], axis=1)
        return both.reshape(DEC_BATCH * (PAST_LEN + DEC_SEQ), width)

    att_s = _attention(q_cat, with_cache(k_cache, k_cat), with_cache(v_cache, v_all), B=DEC_BATCH, Tq=DEC_SEQ,
                       Tk=PAST_LEN + DEC_SEQ, tq=256, q_row0=N_PROMPT, heads=2)
    wo = w_out_even[e].astype(BF16)
    x = _outproj((hm_p, hm_s), (att_p, att_s), wo[:ML_HEADS * ML_DV], wo[ML_HEADS * ML_DV:], x, mod4, 0)
    x = moe_layer(x, 0, final=False)

    oi = 0
    w = w_in_odd[oi]
    ssd_w = SSD_HEADS * SSD_P
    ssd_cc = ssd_w + 2 * SSD_GROUPS * SSD_N
    gdn_w = GDN_HEADS * GDN_DK
    off = np.cumsum([0, ssd_w, ssd_cc, 2 * SSD_HEADS, 3 * gdn_w, gdn_w, 2 * GDN_HEADS, 2 * GDN_HEADS])
    w_misc = jnp.concatenate([w[:, off[2]:off[3]], w[:, off[5]:off[7]],
                              jnp.zeros((D_MODEL, LANES - 2 * SSD_HEADS - 4 * GDN_HEADS), F32)], axis=1)
    weights = [w[:, off[0]:off[1]].astype(BF16), w[:, off[1]:off[2]].astype(BF16),
               w[:, off[3]:off[4]].astype(BF16), w[:, off[4]:off[5]].astype(BF16), w_misc.astype(BF16)]
    z_s, xbc, qkv_g, z_g, misc = _inproj(x, norm_mix[1], mod4, 1, weights)

    ssd_args = (xbc, misc, ssd_conv_w[oi], ssd_conv_b[oi], ssd_dt_bias[oi], ssd_A_log[oi], ssd_D[oi])
    ys_p, st_ssd = _ssd(*ssd_args, prompt=True)
    (ys_s,) = _ssd(*ssd_args, prompt=False, init=state_ssd[:, oi])

    lo = 2 * SSD_HEADS
    gdn_dt_row = jnp.pad(gdn_dt_bias[oi].reshape(-1), (lo, LANES - lo - 2 * GDN_HEADS)).reshape(1, LANES)
    gdn_alog_row = jnp.pad(gdn_A_log[oi].reshape(-1), (lo, LANES - lo - 2 * GDN_HEADS)).reshape(1, LANES)
    gdn_args = (qkv_g, misc, z_g, gdn_conv_w[oi], gdn_conv_b[oi], gdn_dt_row, gdn_alog_row, gdn_norm[oi])
    og_p, st_gdn = _gdn(*gdn_args, prompt=True)
    (og_s,) = _gdn(*gdn_args, prompt=False, init=state_gdn[:, oi])

    wo = w_out_odd[oi].astype(BF16)
    x = _outproj((ys_p, ys_s), (og_p, og_s), wo[:ssd_w], wo[ssd_w:], x, mod4, 1, z=z_s, gnorm=ssd_norm[oi])
    y_p, y_s = moe_layer(x, 1, final=True)

    y_prompt = y_p.reshape(BATCH, SEQ, D_MODEL)
    y_sample = y_s.reshape(DEC_BATCH, DEC_SEQ, D_MODEL)
    new_mla_kv = ckv_n[:N_PROMPT].reshape(BATCH, 1, SEQ, MLA_KV_RANK)
    new_mla_krope = misc0[:N_PROMPT, MLA_NOPE:MLA_NOPE + MLA_ROPE].reshape(BATCH, 1, SEQ, MLA_ROPE)
    return (y_prompt, y_sample, new_mla_kv, new_mla_krope, st_c[:, None], st_n.reshape(BATCH, 1, 2, ML_HEADS, ML_DK),
            st_m.reshape(BATCH, 1, 2, ML_HEADS), st_ssd[:, None], st_gdn[:, None])
```

```python
import functools

import numpy as np
import jax
import jax.numpy as jnp
from jax import lax
from jax.experimental import pallas as pl
from jax.experimental.pallas import tpu as pltpu

F32 = jnp.float32
BF16 = jnp.bfloat16

D_MODEL = 1024
BATCH = 32
SEQ = 256
DEPTH = 2
DEC_BATCH = 2
DEC_SEQ = 2048
PAST_LEN = 256
GRID_W = 64
EPS = 1e-6
ML_HEADS = 4
ML_DK = 128
ML_DV = 128
MLA_HEADS = 8
MLA_Q_RANK = 384
MLA_KV_RANK = 256
MLA_NOPE = 64
MLA_ROPE = 32
MLA_V = 64
ROPE_BASE = 10000.0
SSD_HEADS = 8
SSD_P = 64
SSD_GROUPS = 2
SSD_N = 128
GDN_HEADS = 4
GDN_DK = 128
GDN_DV = 128
CONV_W = 5
MOE_GROUPS = 4
MOE_PER_GROUP = 8
MOE_EXPERTS = 32
MOE_FF = 256

N_PROMPT = BATCH * SEQ
N_SAMPLE = DEC_BATCH * DEC_SEQ
N_TOK = N_PROMPT + N_SAMPLE
N_COND = 8

LANES = 128
CHUNK = 64
ROW_TILE = 512
CONV_HALO = 8
CONV_ROWS = 256
VMEM_LIMIT = 56 * 1024 * 1024
PRE_CHUNKS = 8
STEP_CHUNKS = 4
GDN_PROMPT_SEQS_PER_STEP = 8
GDN_SAMPLE_SEQS_PER_STEP = 1

assert LANES == 2 * CHUNK and SSD_P == CHUNK

NEG_INF = float("-inf")


def _cparams(*sem):
    return pltpu.CompilerParams(dimension_semantics=sem, vmem_limit_bytes=VMEM_LIMIT)


def _dot(a, b):
    return jnp.dot(a, b, preferred_element_type=F32)


def _dot_nt(a, b):
    return lax.dot_general(a, b, (((1,), (1,)), ((), ())), preferred_element_type=F32)


def _dot_tn(a, b):
    return lax.dot_general(a, b, (((0,), (0,)), ((), ())), preferred_element_type=F32)


def _bdot(a, b):
    return _dot(a.astype(BF16), b.astype(BF16))


def _rms(x, g):
    return x * lax.rsqrt(jnp.mean(x * x, axis=-1, keepdims=True) + EPS) * g


def _softplus(x):
    return jnp.maximum(x, 0.0) + jnp.log1p(jnp.exp(-jnp.abs(x)))


def _silu(x):
    return x * jax.nn.sigmoid(x)


def _colsel(x, j):
    lane = lax.broadcasted_iota(jnp.int32, x.shape, 1)
    return jnp.sum(jnp.where(lane == j, x, 0.0), axis=1, keepdims=True)


def _lanes(x):
    return jnp.broadcast_to(x, (x.shape[0], LANES))


def _half_masks(rev_lo, rev_hi):
    t = lax.broadcasted_iota(jnp.int32, (CHUNK, LANES), 0)
    lane = lax.broadcasted_iota(jnp.int32, (CHUNK, LANES), 1)
    s = lane & (CHUNK - 1)
    hi = lane >= CHUNK

    def pick(fwd, bwd):
        if rev_lo == rev_hi:
            return bwd if rev_lo else fwd
        on_hi, on_lo = (bwd, fwd) if rev_hi else (fwd, bwd)
        return (hi & on_hi) | (jnp.logical_not(hi) & on_lo)

    return dict(hi=hi, t=t, s=s, eye=(s == t), incl=pick(s <= t, s >= t), incl_t=pick(t <= s, t >= s),
                strict=pick(s < t, s > t))


def _pack(hm, col_lo, col_hi):
    return jnp.where(hm["hi"], col_hi, col_lo)


def _rows(hm, cols):
    return jnp.sum(jnp.where(hm["eye"], cols, 0.0), axis=0, keepdims=True)


def _half_sums(hm, x):
    lo = jnp.sum(jnp.where(hm["hi"], 0.0, x), axis=1, keepdims=True)
    hi = jnp.sum(jnp.where(hm["hi"], x, 0.0), axis=1, keepdims=True)
    return lo, hi


def _half_cum(hm, col_lo, col_hi):
    cols = _pack(hm, col_lo, col_hi)
    cum_lo, cum_hi = _half_sums(hm, jnp.where(hm["incl"], _rows(hm, cols), 0.0))
    cum_rows = jnp.sum(jnp.where(hm["incl_t"], cols, 0.0), axis=0, keepdims=True)
    return cum_lo, cum_hi, cum_rows


def _blockdiag(hm, x):
    return jnp.concatenate([jnp.where(hm["hi"], 0.0, x).astype(BF16),
                            jnp.where(hm["hi"], x, 0.0).astype(BF16)], axis=0)


def _chunk_rows(c):
    if isinstance(c, int):
        return pl.ds(c * CHUNK, CHUNK)
    return pl.ds(pl.multiple_of(c * CHUNK, CHUNK), CHUNK)


def _block_loop(n, body, size):
    if n <= size:
        body(list(range(n)))
        return

    def block(blk, carry):
        body([blk * size + j for j in range(size)])
        return carry

    lax.fori_loop(0, n // size, block, 0)


def _mod_index(i, rows_per_tile):
    p_tiles = N_PROMPT // rows_per_tile
    s_tiles = DEC_SEQ // rows_per_tile
    return jnp.where(i < p_tiles, 0, 1 + (i - p_tiles) // s_tiles)


def _ada_kernel(c_ref, w_ref, b_ref, o_ref):
    c = c_ref[...]
    o_ref[...] = _bdot(_silu(c), w_ref[...]) + b_ref[...]


def _ada(cond, ada_w, ada_b):
    nb = 6
    return pl.pallas_call(
        _ada_kernel,
        grid=(DEPTH, nb),
        in_specs=[pl.BlockSpec((N_COND, D_MODEL), lambda l, j: (0, 0)),
                  pl.BlockSpec((None, D_MODEL, D_MODEL), lambda l, j: (l, 0, j)),
                  pl.BlockSpec((None, 1, D_MODEL), lambda l, j: (l, 0, j))],
        out_specs=pl.BlockSpec((None, N_COND, D_MODEL), lambda l, j: (l, 0, j)),
        out_shape=jax.ShapeDtypeStruct((DEPTH, N_COND, 6 * D_MODEL), F32),
        compiler_params=_cparams("arbitrary", "arbitrary"),
        name="ada",
    )(cond, ada_w, ada_b.reshape(DEPTH, 1, 6 * D_MODEL))


def _token_operand(x, tm):
    if not isinstance(x, tuple):
        return [pl.BlockSpec((tm, x.shape[1]), lambda i: (i, 0))], [x]
    pt = N_PROMPT // tm
    width = x[0].shape[1]
    return ([pl.BlockSpec((tm, width), lambda i: (jnp.minimum(i, pt - 1), 0)),
             pl.BlockSpec((tm, width), lambda i: (jnp.maximum(i - pt, 0), 0))], list(x))


def _take_tile(refs, split, tm):
    if not split:
        return refs.pop(0)[...]
    p_ref, s_ref = refs.pop(0), refs.pop(0)
    return jnp.where(pl.program_id(0) < N_PROMPT // tm, p_ref[...], s_ref[...])


def _inproj_kernel(*refs, n_out, split, tm):
    refs = list(refs)
    x = _take_tile(refs, split, tm)
    g_ref, mod_ref = refs[:2]
    w_refs, o_refs = refs[2:2 + n_out], refs[2 + n_out:]
    mod = mod_ref[...]
    h = _rms(x, g_ref[...]) * (1.0 + mod[:, D_MODEL:2 * D_MODEL]) + mod[:, :D_MODEL]
    hb = h.astype(BF16)
    for w_ref, o_ref in zip(w_refs, o_refs):
        o_ref[...] = _dot(hb, w_ref[...])


def _inproj(x, gain, mod4, layer, weights):
    n_out = len(weights)
    tm = ROW_TILE
    in_specs, args = _token_operand(x, tm)
    in_specs += [pl.BlockSpec((1, D_MODEL), lambda i: (0, 0)),
                 pl.BlockSpec((None, None, 1, 2 * D_MODEL), lambda i: (layer, _mod_index(i, tm), 0, 0))]
    in_specs += [pl.BlockSpec(w.shape, lambda i: (0, 0)) for w in weights]
    return pl.pallas_call(
        functools.partial(_inproj_kernel, n_out=n_out, split=isinstance(x, tuple), tm=tm),
        grid=(N_TOK // tm,),
        in_specs=in_specs,
        out_specs=[pl.BlockSpec((tm, w.shape[1]), lambda i: (i, 0)) for w in weights],
        out_shape=[jax.ShapeDtypeStruct((N_TOK, w.shape[1]), F32) for w in weights],
        compiler_params=_cparams("arbitrary"),
        name="inproj",
    )(*args, gain.reshape(1, D_MODEL), mod4, *weights)


def _mlstm_kernel(*refs, T, NS, has_init, emit_state):
    it = iter(refs)
    ib_ref, fb_ref = next(it), next(it)
    m0_ref = next(it) if has_init else None
    q_ref, k_ref, v_ref, o_ref, misc_ref, gn_ref = (next(it) for _ in range(6))
    c0_ref, n0_ref = (next(it), next(it)) if has_init else (None, None)
    hm_ref = next(it)
    cout_ref, nout_ref, mout_ref = (next(it), next(it), next(it)) if emit_state else (None, None, None)
    (hb_scr, c_scr, n_scr, m_scr, num_scr, st_scr, kv_scr, nl_scr, bl_scr, gm_scr) = (next(it) for _ in range(10))

    b = pl.program_id(0)
    h = pl.program_id(1)
    nc = T // CHUNK
    scale = ML_DK ** -0.5

    for s in range(NS):
        for d in range(2):
            if has_init:
                c_scr[2 * s + d] = c0_ref[s, d]
                n_scr[2 * s + d] = n0_ref[s, d]
                m_scr[2 * s + d] = jnp.full((1, LANES), m0_ref[b * NS + s, d, h], F32)
            else:
                c_scr[2 * s + d] = jnp.zeros((ML_DK, ML_DV), F32)
                n_scr[2 * s + d] = jnp.zeros((1, ML_DK), F32)
                m_scr[2 * s + d] = jnp.zeros((1, LANES), F32)

    hm = _half_masks(False, True)

    def pre_gates(c):
        rows = _chunk_rows(c)
        mi = misc_ref[rows, :]
        li = [_colsel(mi, d * ML_HEADS + h) + ib_ref[d, h] for d in range(2)]
        lf = [-_softplus(-(_colsel(mi, 2 * ML_HEADS + d * ML_HEADS + h) + fb_ref[d, h])) for d in range(2)]
        b_cols = [None, None]
        b_cols[0], b_cols[1], b_rows = _half_cum(hm, lf[0], lf[1])
        li_rows = _rows(hm, _pack(hm, li[0], li[1]))
        dm = jnp.where(hm["incl"], _pack(hm, b_cols[0], b_cols[1]) - b_rows + li_rows, NEG_INF)
        mloc = [jnp.max(jnp.where(hm["hi"], NEG_INF, dm), axis=1, keepdims=True),
                jnp.max(jnp.where(hm["hi"], dm, NEG_INF), axis=1, keepdims=True)]
        kc = k_ref[rows, :]
        kws = []
        for d in range(2):
            bl = jnp.sum(lf[d], axis=0, keepdims=True)
            g = bl - b_cols[d] + li[d]
            gmax = jnp.max(g, axis=0, keepdims=True)
            kw = jnp.exp(g - gmax) * kc
            kws.append(kw.astype(BF16))
            nl_scr[d, c] = jnp.sum(kw, axis=0, keepdims=True)
            bl_scr[d, c] = _lanes(bl)
            gm_scr[d, c] = _lanes(gmax)
            st_scr[d, 0, rows, :] = _lanes(mloc[d])
            st_scr[d, 1, rows, :] = _lanes(b_cols[d])
        return dict(c=c, rows=rows, kc=kc.astype(BF16), kw2=jnp.concatenate(kws, axis=1),
                    e2=jnp.exp(dm - _pack(hm, mloc[0], mloc[1])))

    def pre_block(chunks):
        chains = [pre_gates(c) for c in chunks]
        for ch in chains:
            qc = (q_ref[ch["rows"], :] * scale).astype(BF16)
            ch["vb"] = v_ref[ch["rows"], :].astype(BF16)
            ch["qk2"] = _dot_nt(qc, jnp.concatenate([ch["kc"], ch["kc"]], axis=0))
            kv2 = _dot_tn(ch["kw2"], ch["vb"])
            kv_scr[0, ch["c"]] = kv2[:ML_DK]
            kv_scr[1, ch["c"]] = kv2[ML_DK:]
        for ch in chains:
            s2 = ch["qk2"] * ch["e2"]
            ch["s2"] = s2.astype(BF16)
            dens = _half_sums(hm, s2)
            for d in range(2):
                st_scr[d, 2, ch["rows"], :] = _lanes(dens[d])
        for ch in chains:
            vb = ch["vb"]
            zero = jnp.zeros_like(vb)
            vbd = jnp.concatenate([jnp.concatenate([vb, zero], axis=1),
                                   jnp.concatenate([zero, vb], axis=1)], axis=0)
            num2 = _dot(ch["s2"], vbd)
            num_scr[0, ch["rows"], :] = num2[:, :ML_DV]
            num_scr[1, ch["rows"], :] = num2[:, ML_DV:]

    def step_block(chunks):
        states = [(m_scr[j], c_scr[j], n_scr[j]) for j in range(2 * NS)]
        pairs = []
        for c in chunks:
            for s in range(NS):
                pair = []
                for d in range(2):
                    cc = s * nc + (c if d == 0 else nc - 1 - c)
                    rows = _chunk_rows(cc)
                    m, c_st, n_st = states[2 * s + d]
                    mloc, b_col, den_loc = st_scr[d, 0, rows, :], st_scr[d, 1, rows, :], st_scr[d, 2, rows, :]
                    bl, gmax = bl_scr[d, cc], gm_scr[d, cc]
                    qc = q_ref[rows, :] * scale
                    inter = b_col + m
                    mq = jnp.maximum(inter, mloc)
                    a = jnp.exp(inter - mq)
                    f = jnp.exp(mloc - mq)
                    den = f * den_loc + a * jnp.sum(qc * n_st, axis=1, keepdims=True)
                    pair.append(dict(rows=rows, qc=qc.astype(BF16), c_st=c_st.astype(BF16), a=a, f=f,
                                     inv=1.0 / jnp.maximum(jnp.abs(den), jnp.exp(-mq))))
                    m_new = jnp.maximum(bl + m, gmax)
                    dec = jnp.exp(bl + m - m_new)
                    fk = jnp.exp(gmax - m_new)
                    states[2 * s + d] = (m_new, dec * c_st + fk * kv_scr[d, cc], dec * n_st + fk * nl_scr[d, cc])
                pairs.append(pair)
        for j in range(2 * NS):
            m_scr[j], c_scr[j], n_scr[j] = states[j]
        for pair in pairs:
            lhs = jnp.concatenate([pair[0]["qc"], pair[1]["qc"]], axis=0)
            rhs = jnp.concatenate([pair[0]["c_st"], pair[1]["c_st"]], axis=1)
            pair.append(_dot(lhs, rhs))
        for pair in pairs:
            res = pair[2]
            for d, dst in enumerate((hm_ref, hb_scr)):
                it_ = pair[d]
                qc_c = res[d * CHUNK:(d + 1) * CHUNK, d * ML_DV:(d + 1) * ML_DV]
                dst[it_["rows"], :] = (it_["f"] * num_scr[d, it_["rows"], :] + it_["a"] * qc_c) * it_["inv"]

    _block_loop(NS * nc, pre_block, PRE_CHUNKS)
    _block_loop(nc, step_block, STEP_CHUNKS)

    hs = hm_ref[...] + hb_scr[...]
    hm_ref[...] = _rms(hs, gn_ref[...]) * jax.nn.sigmoid(o_ref[...])
    if emit_state:
        for s in range(NS):
            for d in range(2):
                cout_ref[s, d] = c_scr[2 * s + d]
                nout_ref[s, d] = n_scr[2 * s + d]
                mout_ref[s, d] = m_scr[2 * s + d][:, 0:1]


def _seq_geometry(prompt, prompt_seqs_per_step=1, sample_seqs_per_step=1):
    if prompt:
        return SEQ, BATCH, prompt_seqs_per_step, 0
    return DEC_SEQ, DEC_BATCH, sample_seqs_per_step, N_PROMPT // (sample_seqs_per_step * DEC_SEQ)


def _mlstm(qkvo, misc, i_bias, f_bias, gnorm, *, prompt, init=None):
    T, B, NS, blk0 = _seq_geometry(prompt)
    R = NS * T
    nc = R // CHUNK
    has_init = init is not None
    smem = pl.BlockSpec(memory_space=pltpu.SMEM)

    def col(j):
        return pl.BlockSpec((R, LANES), lambda b, h: (blk0 + b, j * ML_HEADS + h))

    in_specs = [smem, smem]
    args = [i_bias, f_bias]
    if has_init:
        in_specs.append(smem)
        args.append(init[2])
    in_specs += [col(0), col(1), col(2), col(3),
                 pl.BlockSpec((R, LANES), lambda b, h: (blk0 + b, 0)),
                 pl.BlockSpec((None, 1, ML_DV), lambda b, h: (h, 0, 0))]
    args += [qkvo, qkvo, qkvo, qkvo, misc, gnorm.reshape(ML_HEADS, 1, ML_DV)]
    c_spec = pl.BlockSpec((NS, 2, None, ML_DK, ML_DV), lambda b, h: (b, 0, h, 0, 0))
    n_spec = pl.BlockSpec((NS, 2, None, 1, ML_DK), lambda b, h: (b, 0, h, 0, 0))
    if has_init:
        in_specs += [c_spec, n_spec]
        args += [init[0], init[1].reshape(B, 2, ML_HEADS, 1, ML_DK)]
    out_specs = [pl.BlockSpec((R, LANES), lambda b, h: (b, h))]
    out_shape = [jax.ShapeDtypeStruct((B * T, ML_HEADS * ML_DV), F32)]
    if prompt:
        out_specs += [c_spec, n_spec, pl.BlockSpec((NS, 2, None, 1, 1), lambda b, h: (b, 0, h, 0, 0))]
        out_shape += [jax.ShapeDtypeStruct((B, 2, ML_HEADS, ML_DK, ML_DV), F32),
                      jax.ShapeDtypeStruct((B, 2, ML_HEADS, 1, ML_DK), F32),
                      jax.ShapeDtypeStruct((B, 2, ML_HEADS, 1, 1), F32)]
    return pl.pallas_call(
        functools.partial(_mlstm_kernel, T=T, NS=NS, has_init=has_init, emit_state=prompt),
        grid=(B // NS, ML_HEADS),
        in_specs=in_specs,
        out_specs=out_specs,
        out_shape=out_shape,
        scratch_shapes=[pltpu.VMEM((R, ML_DV), F32),
                        pltpu.VMEM((2 * NS, ML_DK, ML_DV), F32),
                        pltpu.VMEM((2 * NS, 1, ML_DK), F32),
                        pltpu.VMEM((2 * NS, 1, LANES), F32),
                        pltpu.VMEM((2, R, ML_DV), F32),
                        pltpu.VMEM((2, 3, R, LANES), F32),
                        pltpu.VMEM((2, nc, ML_DK, ML_DV), F32),
                        pltpu.VMEM((2, nc, 1, ML_DK), F32),
                        pltpu.VMEM((2, nc, 1, LANES), F32),
                        pltpu.VMEM((2, nc, 1, LANES), F32)],
        compiler_params=_cparams("arbitrary", "arbitrary"),
        name="mlstm_prompt" if prompt else "mlstm_sample",
    )(*args)


def _rope(x, cos, sin_signed):
    lane = lax.broadcasted_iota(jnp.int32, x.shape, 1)
    first = (lane & 15) < 8
    partner = jnp.where(first, pltpu.roll(x, LANES - 8, axis=1), pltpu.roll(x, 8, axis=1))
    return x * cos + partner * sin_signed


def _q_kernel(cq_ref, g_ref, w_ref, cos_ref, sin_ref, q_ref):
    cq = _rms(cq_ref[...], g_ref[...])
    y = _bdot(cq, w_ref[...])
    cos, sin = cos_ref[...], sin_ref[...]
    for hd in range(MLA_HEADS):
        sl = slice(hd * LANES, (hd + 1) * LANES)
        q_ref[:, sl] = _rope(y[:, sl], cos, sin).astype(BF16)


def _rope_block_index(i, tm):
    p_tiles = N_PROMPT // tm
    s_tiles = DEC_SEQ // tm
    return jnp.where(i < p_tiles, 0, 1 + (i - p_tiles) % s_tiles)


def _mla_q(cq, gain, w_uq_pad, cos_tab, sin_tab):
    tm = ROW_TILE
    tab = pl.BlockSpec((tm, LANES), lambda i: (_rope_block_index(i, tm), 0))
    return pl.pallas_call(
        _q_kernel,
        grid=(N_TOK // tm,),
        in_specs=[pl.BlockSpec((tm, MLA_Q_RANK), lambda i: (i, 0)),
                  pl.BlockSpec((1, MLA_Q_RANK), lambda i: (0, 0)),
                  pl.BlockSpec(w_uq_pad.shape, lambda i: (0, 0)),
                  tab, tab],
        out_specs=pl.BlockSpec((tm, MLA_HEADS * LANES), lambda i: (i, 0)),
        out_shape=jax.ShapeDtypeStruct((N_TOK, MLA_HEADS * LANES), BF16),
        compiler_params=_cparams("arbitrary"),
        name="mla_q",
    )(cq, gain.reshape(1, MLA_Q_RANK), w_uq_pad, cos_tab, sin_tab)


def _kv_kernel(*refs, norm):
    if norm:
        ckv_ref, kpe_ref, g_ref, wk_ref, wv_ref, cos_ref, sin_ref, ckvn_ref, k_ref, v_ref = refs
        c = _rms(ckv_ref[...], g_ref[...])
        ckvn_ref[...] = c
    else:
        ckv_ref, kpe_ref, wk_ref, wv_ref, k_ref, v_ref = refs
        c = ckv_ref[...]
    kp = kpe_ref[...]
    lane = lax.broadcasted_iota(jnp.int32, kp.shape, 1)
    kp = jnp.where((lane >= MLA_NOPE) & (lane < MLA_NOPE + MLA_ROPE), kp, 0.0)
    if norm:
        kp = _rope(kp, cos_ref[...], sin_ref[...])
    kn = _bdot(c, wk_ref[...])
    for hd in range(MLA_HEADS):
        sl = slice(hd * LANES, (hd + 1) * LANES)
        k_ref[:, sl] = (kn[:, sl] + kp).astype(BF16)
    v_ref[...] = _bdot(c, wv_ref[...]).astype(BF16)


def _mla_kv(ckv, kpe128, w_uk_pad, w_uv, gain=None, cos_tab=None, sin_tab=None):
    norm = gain is not None
    n = ckv.shape[0]
    tm = ROW_TILE
    row = lambda w: pl.BlockSpec((tm, w), lambda i: (i, 0))
    full = lambda a: pl.BlockSpec(a.shape, lambda i: (0, 0))
    in_specs = [row(MLA_KV_RANK), row(LANES)]
    args = [ckv, kpe128]
    if norm:
        in_specs.append(pl.BlockSpec((1, MLA_KV_RANK), lambda i: (0, 0)))
        args.append(gain.reshape(1, MLA_KV_RANK))
    in_specs += [full(w_uk_pad), full(w_uv)]
    args += [w_uk_pad, w_uv]
    out_specs = [row(MLA_HEADS * LANES), row(MLA_HEADS * MLA_V)]
    out_shape = [jax.ShapeDtypeStruct((n, MLA_HEADS * LANES), BF16),
                 jax.ShapeDtypeStruct((n, MLA_HEADS * MLA_V), BF16)]
    if norm:
        tab = pl.BlockSpec((tm, LANES), lambda i: (_rope_block_index(i, tm), 0))
        in_specs += [tab, tab]
        args += [cos_tab, sin_tab]
        out_specs = [row(MLA_KV_RANK)] + out_specs
        out_shape = [jax.ShapeDtypeStruct((n, MLA_KV_RANK), F32)] + out_shape
    return pl.pallas_call(
        functools.partial(_kv_kernel, norm=norm),
        grid=(n // tm,),
        in_specs=in_specs,
        out_specs=out_specs,
        out_shape=out_shape,
        compiler_params=_cparams("arbitrary"),
        name="mla_kv" if norm else "mla_kv_cache",
    )(*args)


def _attn_kernel(q_ref, k_ref, v_ref, o_ref, *, heads):
    scale = (MLA_NOPE + MLA_ROPE) ** -0.5
    scores = [_dot_nt(q_ref[:, j * LANES:(j + 1) * LANES], k_ref[:, j * LANES:(j + 1) * LANES]) * scale
              for j in range(heads)]
    probs, sums = [], []
    for s in scores:
        p = jnp.exp(s - jnp.max(s, axis=1, keepdims=True))
        sums.append(jnp.sum(p, axis=1, keepdims=True))
        probs.append(p.astype(BF16))
    pvs = [_dot(probs[j], v_ref[:, j * MLA_V:(j + 1) * MLA_V]) for j in range(heads)]
    o_ref[...] = jnp.concatenate([pvs[j] / sums[j] for j in range(heads)], axis=1)


def _attention(q, k, v, *, B, Tq, Tk, tq, q_row0, heads):
    nq = Tq // tq
    blk0 = q_row0 // tq
    return pl.pallas_call(
        functools.partial(_attn_kernel, heads=heads),
        grid=(B, MLA_HEADS // heads, nq),
        in_specs=[pl.BlockSpec((tq, heads * LANES), lambda b, hp, i: (blk0 + b * nq + i, hp)),
                  pl.BlockSpec((Tk, heads * LANES), lambda b, hp, i: (b, hp)),
                  pl.BlockSpec((Tk, heads * MLA_V), lambda b, hp, i: (b, hp))],
        out_specs=pl.BlockSpec((tq, heads * MLA_V), lambda b, hp, i: (b * nq + i, hp)),
        out_shape=jax.ShapeDtypeStruct((B * Tq, MLA_HEADS * MLA_V), F32),
        compiler_params=_cparams("arbitrary", "arbitrary", "arbitrary"),
        name="attention",
    )(q, k, v)


def _outproj_kernel(*refs, odd, split_x, tm):
    refs = list(refs)
    a1 = _take_tile(refs, True, tm)
    a2 = _take_tile(refs, True, tm)
    x = _take_tile(refs, split_x, tm)
    if odd:
        z_ref, gn_ref = refs.pop(0), refs.pop(0)
        a1 = _rms(a1 * _silu(z_ref[...]), gn_ref[...])
    w1_ref, w2_ref, g1_ref, o_ref = refs
    out = _bdot(a1, w1_ref[...]) + _bdot(a2, w2_ref[...])
    o_ref[...] = x + g1_ref[...] * out


def _outproj(a1, a2, w1, w2, x, mod4, layer, z=None, gnorm=None):
    odd = z is not None
    tm = ROW_TILE
    half = w1.shape[0]
    full = lambda a: pl.BlockSpec(a.shape, lambda i: (0, 0))
    in_specs, args = [], []
    for operand in (a1, a2, x):
        specs, arrays = _token_operand(operand, tm)
        in_specs += specs
        args += arrays
    if odd:
        in_specs += [pl.BlockSpec((tm, half), lambda i: (i, 0)), pl.BlockSpec((1, half), lambda i: (0, 0))]
        args += [z, gnorm.reshape(1, half)]
    in_specs += [full(w1), full(w2),
                 pl.BlockSpec((None, None, 1, D_MODEL), lambda i: (layer, _mod_index(i, tm), 0, 2))]
    args += [w1, w2, mod4]
    return pl.pallas_call(
        functools.partial(_outproj_kernel, odd=odd, split_x=isinstance(x, tuple), tm=tm),
        grid=(N_TOK // tm,),
        in_specs=in_specs,
        out_specs=pl.BlockSpec((tm, D_MODEL), lambda i: (i, 0)),
        out_shape=jax.ShapeDtypeStruct((N_TOK, D_MODEL), F32),
        compiler_params=_cparams("arbitrary"),
        name="outproj",
    )(*args)


def _conv_silu(src_ref, dst_ref, pad_ref, w_ref, b_ref, T, NS):
    width = src_ref.shape[1]
    zeros = jnp.zeros((CONV_HALO, width), F32)
    pad_ref[pl.ds(0, CONV_HALO), :] = zeros
    pad_ref[pl.ds(CONV_HALO + T, CONV_HALO), :] = zeros
    w = w_ref[...]
    bias = b_ref[...]
    for s in range(NS):
        pad_ref[pl.ds(CONV_HALO, T), :] = src_ref[pl.ds(s * T, T), :]
        for r0 in range(0, T, CONV_ROWS):
            acc = bias
            for j in range(CONV_W):
                start = r0 + CONV_HALO + j - CONV_W // 2
                acc = acc + w[j:j + 1, :] * pad_ref[pl.ds(start, CONV_ROWS), :]
            dst_ref[pl.ds(s * T + r0, CONV_ROWS), :] = _silu(acc)


def _ssd_kernel(*refs, T, NS, has_init, emit_state):
    it = iter(refs)
    x_ref, b_ref, c_ref = next(it), next(it), next(it)
    wx_ref, wb_ref, wc_ref = next(it), next(it), next(it)
    bx_ref, bb_ref, bc_ref = next(it), next(it), next(it)
    misc_ref, dtb_ref, alog_ref, dskip_ref = next(it), next(it), next(it), next(it)
    h0_ref = next(it) if has_init else None
    y_ref = next(it)
    hout_ref = next(it) if emit_state else None
    (pad_scr, xc_scr, bc_scr, cc_scr, yb_scr, hs_scr, yi_scr, ea_scr, upd_scr, eal_scr) = (
        next(it) for _ in range(10))

    p = pl.program_id(1)
    nc = T // CHUNK

    _conv_silu(x_ref, xc_scr, pad_scr, wx_ref, bx_ref, T, NS)
    _conv_silu(b_ref, bc_scr, pad_scr, wb_ref, bb_ref, T, NS)
    _conv_silu(c_ref, cc_scr, pad_scr, wc_ref, bc_ref, T, NS)

    for s in range(NS):
        for d in range(2):
            if has_init:
                hs_scr[2 * s + d] = h0_ref[s, d].reshape(2 * SSD_P, SSD_N).T
            else:
                hs_scr[2 * s + d] = jnp.zeros((SSD_N, 2 * SSD_P), F32)

    dtb = dtb_ref[...]
    neg_a = -jnp.exp(alog_ref[...])
    hms = [_half_masks(False, False), _half_masks(True, True)]
    hi_row = lax.broadcasted_iota(jnp.int32, (1, LANES), 1) >= CHUNK

    def pre_block(chunks):
        chains = []
        for c in chunks:
            rows = _chunk_rows(c)
            bc = bc_scr[rows, :].astype(BF16)
            cb2 = _dot_nt(cc_scr[rows, :].astype(BF16), jnp.concatenate([bc, bc], axis=0))
            chains.append(dict(c=c, rows=rows, bc=bc, cb2=cb2))
        for ch in chains:
            rows = ch["rows"]
            xc = xc_scr[rows, :]
            dts = _softplus(misc_ref[rows, :] + dtb)
            adt = dts * neg_a
            ch["x2m"] = _blockdiag(hms[0], xc)
            ch["g2"], ch["xw"] = [], []
            for d in range(2):
                hm = hms[d]
                dt_h = [_colsel(dts, d * SSD_HEADS + 2 * p + hh) for hh in range(2)]
                a_h = [_colsel(adt, d * SSD_HEADS + 2 * p + hh) for hh in range(2)]
                cum0, cum1, cum_rows = _half_cum(hm, a_h[0], a_h[1])
                seg2 = jnp.exp(jnp.where(hm["incl"], _pack(hm, cum0, cum1) - cum_rows, NEG_INF))
                dt_rows = _rows(hm, _pack(hm, dt_h[0], dt_h[1]))
                ch["g2"].append((ch["cb2"] * seg2 * dt_rows).astype(BF16))
                al = [jnp.sum(a, axis=0, keepdims=True) for a in a_h]
                wgt = [jnp.exp(al[hh] - cum) * dt_h[hh] for hh, cum in enumerate((cum0, cum1))]
                ch["xw"].append((xc * _pack(hm, wgt[0], wgt[1])).astype(BF16))
                ea_scr[d, rows, :] = _pack(hm, jnp.exp(cum0), jnp.exp(cum1))
                eal_scr[d, ch["c"]] = jnp.where(hi_row, jnp.exp(al[1]), jnp.exp(al[0]))
        for ch in chains:
            for d in range(2):
                yi_scr[d, ch["rows"], :] = _dot(ch["g2"][d], ch["x2m"])
                upd_scr[d, ch["c"]] = _dot_tn(ch["bc"], ch["xw"][d])

    def step_block(chunks):
        states = [hs_scr[j] for j in range(2 * NS)]
        pairs = []
        for c in chunks:
            for s in range(NS):
                pair = []
                for d in range(2):
                    cc = s * nc + (c if d == 0 else nc - 1 - c)
                    pair.append(dict(rows=_chunk_rows(cc), hs=states[2 * s + d].astype(BF16)))
                    states[2 * s + d] = eal_scr[d, cc] * states[2 * s + d] + upd_scr[d, cc]
                pairs.append(pair)
        for j in range(2 * NS):
            hs_scr[j] = states[j]
        for pair in pairs:
            lhs = jnp.concatenate([cc_scr[pair[d]["rows"], :].astype(BF16) for d in range(2)], axis=0)
            rhs = jnp.concatenate([pair[d]["hs"] for d in range(2)], axis=1)
            pair.append(_dot(lhs, rhs))
        for pair in pairs:
            for d, dst in enumerate((y_ref, yb_scr)):
                rows = pair[d]["rows"]
                ch = pair[2][d * CHUNK:(d + 1) * CHUNK, d * LANES:(d + 1) * LANES]
                dst[rows, :] = yi_scr[d, rows, :] + ea_scr[d, rows, :] * ch

    _block_loop(NS * nc, pre_block, PRE_CHUNKS)
    _block_loop(nc, step_block, STEP_CHUNKS)

    y_ref[...] = y_ref[...] + yb_scr[...] + dskip_ref[...] * xc_scr[...]
    if emit_state:
        for s in range(NS):
            for d in range(2):
                hout_ref[s, d] = hs_scr[2 * s + d].T.reshape(2, SSD_P, SSD_N)


def _lane_row(v):
    return jnp.pad(v.astype(F32), (0, LANES - v.shape[0])).reshape(1, LANES)


def _ssd(xbc, misc, conv_w, conv_b, dt_bias, a_log, d_skip, *, prompt, init=None):
    T, B, NS, blk0 = _seq_geometry(prompt)
    R = NS * T
    nc = R // CHUNK
    has_init = init is not None
    n_pairs = SSD_HEADS // 2
    pairs_per_group = n_pairs // SSD_GROUPS
    xb = SSD_HEADS * SSD_P // LANES
    cb = xb + SSD_GROUPS * SSD_N // LANES
    colx = lambda b, p: (blk0 + b, p)
    colb = lambda b, p: (blk0 + b, xb + p // pairs_per_group)
    colc = lambda b, p: (blk0 + b, cb + p // pairs_per_group)
    wsel = lambda f: (lambda b, p: (0, f(b, p)[1]))
    row128 = pl.BlockSpec((1, LANES), lambda b, p: (0, 0))
    in_specs = [pl.BlockSpec((R, LANES), colx), pl.BlockSpec((R, LANES), colb), pl.BlockSpec((R, LANES), colc),
                pl.BlockSpec((CONV_W, LANES), wsel(colx)), pl.BlockSpec((CONV_W, LANES), wsel(colb)),
                pl.BlockSpec((CONV_W, LANES), wsel(colc)),
                pl.BlockSpec((1, LANES), wsel(colx)), pl.BlockSpec((1, LANES), wsel(colb)),
                pl.BlockSpec((1, LANES), wsel(colc)),
                pl.BlockSpec((R, LANES), lambda b, p: (blk0 + b, 0)), row128, row128,
                pl.BlockSpec((1, LANES), lambda b, p: (0, p))]
    cb2 = conv_b.reshape(1, -1)
    args = [xbc, xbc, xbc, conv_w, conv_w, conv_w, cb2, cb2, cb2, misc,
            _lane_row(dt_bias.reshape(-1)), _lane_row(a_log.reshape(-1)),
            jnp.repeat(d_skip, SSD_P).reshape(1, SSD_HEADS * SSD_P)]
    state_spec = pl.BlockSpec((NS, 2, 2, SSD_P, SSD_N), lambda b, p: (b, 0, p, 0, 0))
    if has_init:
        in_specs.append(state_spec)
        args.append(init)
    out_specs = [pl.BlockSpec((R, LANES), lambda b, p: (b, p))]
    out_shape = [jax.ShapeDtypeStruct((B * T, SSD_HEADS * SSD_P), F32)]
    if prompt:
        out_specs.append(state_spec)
        out_shape.append(jax.ShapeDtypeStruct((B, 2, SSD_HEADS, SSD_P, SSD_N), F32))
    return pl.pallas_call(
        functools.partial(_ssd_kernel, T=T, NS=NS, has_init=has_init, emit_state=prompt),
        grid=(B // NS, n_pairs),
        in_specs=in_specs,
        out_specs=out_specs,
        out_shape=out_shape,
        scratch_shapes=[pltpu.VMEM((T + 2 * CONV_HALO, LANES), F32)]
        + [pltpu.VMEM((R, LANES), F32) for _ in range(4)]
        + [pltpu.VMEM((2 * NS, SSD_N, 2 * SSD_P), F32),
           pltpu.VMEM((2, R, LANES), F32),
           pltpu.VMEM((2, R, LANES), F32),
           pltpu.VMEM((2, nc, SSD_N, 2 * SSD_P), F32),
           pltpu.VMEM((2, nc, 1, LANES), F32)],
        compiler_params=_cparams("arbitrary", "arbitrary"),
        name="ssd_prompt" if prompt else "ssd_sample",
    )(*args)


def _tri_inverse(hm, nmat, eye_f):
    levels = int(np.log2(CHUNK))

    def off(level):
        same_big = lax.shift_right_logical(hm["t"], level) == lax.shift_right_logical(hm["s"], level)
        same_small = lax.shift_right_logical(hm["t"], level - 1) == lax.shift_right_logical(hm["s"], level - 1)
        return jnp.where(same_big & jnp.logical_not(same_small), nmat, 0.0)

    state = dict(dinv=eye_f - off(1))

    def first(level):
        def run():
            state["t1"] = _dot(state["dinv"].astype(BF16), _blockdiag(hm, off(level)))
        return run

    def second():
        state["dinv"] = state["dinv"] - _dot(state["t1"].astype(BF16), _blockdiag(hm, state["dinv"]))

    stages = []
    for level in range(2, levels + 1):
        stages += [first(level), second]
    return state, stages


def _gdn_kernel(*refs, T, NS, has_init, emit_state):
    it = iter(refs)
    q_ref, k_ref, v_ref = next(it), next(it), next(it)
    wq_ref, wk_ref, wv_ref = next(it), next(it), next(it)
    bq_ref, bk_ref, bv_ref = next(it), next(it), next(it)
    misc_ref, dtb_ref, alog_ref, z_ref, gn_ref = (next(it) for _ in range(5))
    s0_ref = next(it) if has_init else None
    o_ref = next(it)
    sout_ref = next(it) if emit_state else None
    (pad_scr, qc_scr, kc_scr, vc_scr, ob_scr, s_scr,
     u0_scr, wq_scr, kcf_scr, p2_scr, egl_scr) = (next(it) for _ in range(11))

    h = pl.program_id(1)
    nc = T // CHUNK

    _conv_silu(q_ref, qc_scr, pad_scr, wq_ref, bq_ref, T, NS)
    _conv_silu(k_ref, kc_scr, pad_scr, wk_ref, bk_ref, T, NS)
    _conv_silu(v_ref, vc_scr, pad_scr, wv_ref, bv_ref, T, NS)
    q = qc_scr[...]
    qc_scr[...] = q * lax.rsqrt(jnp.sum(q * q, axis=-1, keepdims=True) + EPS) * (GDN_DK ** -0.5)
    k = kc_scr[...]
    kc_scr[...] = k * lax.rsqrt(jnp.sum(k * k, axis=-1, keepdims=True) + EPS)

    for s in range(NS):
        for d in range(2):
            if has_init:
                s_scr[2 * s + d] = s0_ref[s, d]
            else:
                s_scr[2 * s + d] = jnp.zeros((GDN_DK, GDN_DV), F32)

    dtb = dtb_ref[...]
    neg_a = -jnp.exp(alog_ref[...])
    hm = _half_masks(False, True)
    eye_f = hm["eye"].astype(F32)

    def pre_gates(c):
        rows = _chunk_rows(c)
        kc = kc_scr[rows, :]
        qc = qc_scr[rows, :]
        mi = misc_ref[rows, :]
        gdec = neg_a * _softplus(mi + dtb)
        g_src = [_colsel(gdec, 2 * SSD_HEADS + d * GDN_HEADS + h) for d in range(2)]
        beta = [jax.nn.sigmoid(_colsel(mi, 2 * SSD_HEADS + 2 * GDN_HEADS + d * GDN_HEADS + h)) for d in range(2)]
        gc = [None, None]
        gc[0], gc[1], gc_rows = _half_cum(hm, g_src[0], g_src[1])
        diff = _pack(hm, gc[0], gc[1]) - gc_rows
        beta_rows = _rows(hm, _pack(hm, beta[0], beta[1]))
        kb = kc.astype(BF16)
        ek = []
        for d in range(2):
            egc = jnp.exp(gc[d])
            gl = jnp.sum(g_src[d], axis=0, keepdims=True)
            ek.append((egc * kc).astype(BF16))
            wq_scr[d, c, CHUNK:, :] = (egc * qc).astype(BF16)
            kcf_scr[d, rows, :] = (jnp.exp(gl - gc[d]) * beta[d] * kc).astype(BF16)
            egl_scr[d, c] = _lanes(jnp.exp(gl))
        return dict(c=c, rows=rows, kb=kb, qb=qc.astype(BF16), ek=ek,
                    dec_s=jnp.exp(jnp.where(hm["strict"], diff, NEG_INF)) * beta_rows,
                    dec_i=jnp.exp(jnp.where(hm["incl"], diff, NEG_INF)) * beta_rows)

    def pre_block(chunks):
        chains = [pre_gates(c) for c in chunks]
        for ch in chains:
            kb = ch["kb"]
            ch["kq"] = _dot_nt(jnp.concatenate([kb, ch["qb"]], axis=0), jnp.concatenate([kb, kb], axis=0))
        for ch in chains:
            p2_scr[ch["rows"], :] = (ch["dec_i"] * ch["kq"][CHUNK:]).astype(BF16)
            ch["inv"], ch["stages"] = _tri_inverse(hm, ch["dec_s"] * ch["kq"][:CHUNK], eye_f)
        for i in range(len(chains[0]["stages"])):
            for ch in chains:
                ch["stages"][i]()
        for ch in chains:
            vb = vc_scr[ch["rows"], :].astype(BF16)
            zero = jnp.zeros_like(vb)
            rhs = jnp.concatenate([jnp.concatenate([vb, ch["ek"][0], zero, zero], axis=1),
                                   jnp.concatenate([zero, zero, vb, ch["ek"][1]], axis=1)], axis=0)
            uw = _dot(ch["inv"]["dinv"].astype(BF16), rhs)
            for d in range(2):
                u0_scr[d, ch["rows"], :] = uw[:, 2 * d * LANES:(2 * d + 1) * LANES]
                wq_scr[d, ch["c"], :CHUNK, :] = uw[:, (2 * d + 1) * LANES:(2 * d + 2) * LANES].astype(BF16)

    def step_block(chunks):
        states = [s_scr[j] for j in range(2 * NS)]
        for c in chunks:
            work = []
            for s in range(NS):
                cs = [s * nc + c, s * nc + nc - 1 - c]
                lhs = jnp.concatenate([wq_scr[d, cs[d]] for d in range(2)], axis=0)
                rhs = jnp.concatenate([states[2 * s + d].astype(BF16) for d in range(2)], axis=1)
                work.append(dict(s=s, cs=cs, rows=[_chunk_rows(cc) for cc in cs], a=_dot(lhs, rhs)))
            for w in work:
                rows, ub, w["qs"] = w["rows"], [], []
                for d in range(2):
                    blk = w["a"][2 * d * CHUNK:(2 * d + 2) * CHUNK, d * LANES:(d + 1) * LANES]
                    ub.append((u0_scr[d, rows[d], :] - blk[:CHUNK]).astype(BF16))
                    w["qs"].append(blk[CHUNK:])
                zero = jnp.zeros_like(ub[0])
                w["pu"] = _dot(jnp.concatenate([p2_scr[rows[d], :] for d in range(2)], axis=0),
                               jnp.concatenate([jnp.concatenate([ub[0], zero], axis=1),
                                                jnp.concatenate([zero, ub[1]], axis=1)], axis=0))
                w["ktu"] = _dot_tn(jnp.concatenate([kcf_scr[d, rows[d], :] for d in range(2)], axis=1),
                                   jnp.concatenate(ub, axis=1))
            for w in work:
                for d, dst in enumerate((o_ref, ob_scr)):
                    j = 2 * w["s"] + d
                    dst[w["rows"][d], :] = w["qs"][d] + w["pu"][d * CHUNK:(d + 1) * CHUNK, d * LANES:(d + 1) * LANES]
                    states[j] = (egl_scr[d, w["cs"][d]] * states[j]
                                 + w["ktu"][d * GDN_DK:(d + 1) * GDN_DK, d * GDN_DV:(d + 1) * GDN_DV])
        for j in range(2 * NS):
            s_scr[j] = states[j]

    _block_loop(NS * nc, pre_block, PRE_CHUNKS)
    _block_loop(nc, step_block, STEP_CHUNKS)

    og = o_ref[...] + ob_scr[...]
    o_ref[...] = _rms(og, gn_ref[...]) * _silu(z_ref[...])
    if emit_state:
        for s in range(NS):
            for d in range(2):
                sout_ref[s, d] = s_scr[2 * s + d]


def _gdn(qkv, misc, zg, conv_w, conv_b, dt_row, alog_row, gnorm, *, prompt, init=None):
    T, B, NS, blk0 = _seq_geometry(prompt, GDN_PROMPT_SEQS_PER_STEP, GDN_SAMPLE_SEQS_PER_STEP)
    R = NS * T
    nc = R // CHUNK
    has_init = init is not None
    col = lambda j: (lambda b, h: (blk0 + b, j * GDN_HEADS + h))
    wsel = lambda j: (lambda b, h: (0, j * GDN_HEADS + h))
    row128 = pl.BlockSpec((1, LANES), lambda b, h: (0, 0))
    in_specs = [pl.BlockSpec((R, LANES), col(j)) for j in range(3)]
    in_specs += [pl.BlockSpec((CONV_W, LANES), wsel(j)) for j in range(3)]
    in_specs += [pl.BlockSpec((1, LANES), wsel(j)) for j in range(3)]
    in_specs += [pl.BlockSpec((R, LANES), lambda b, h: (blk0 + b, 0)), row128, row128,
                 pl.BlockSpec((R, LANES), lambda b, h: (blk0 + b, h)), row128]
    cb2 = conv_b.reshape(1, -1)
    args = [qkv, qkv, qkv, conv_w, conv_w, conv_w, cb2, cb2, cb2, misc, dt_row, alog_row, zg,
            gnorm.reshape(1, GDN_DV)]
    state_spec = pl.BlockSpec((NS, 2, None, GDN_DK, GDN_DV), lambda b, h: (b, 0, h, 0, 0))
    if has_init:
        in_specs.append(state_spec)
        args.append(init)
    out_specs = [pl.BlockSpec((R, LANES), lambda b, h: (b, h))]
    out_shape = [jax.ShapeDtypeStruct((B * T, GDN_HEADS * GDN_DV), F32)]
    if prompt:
        out_specs.append(state_spec)
        out_shape.append(jax.ShapeDtypeStruct((B, 2, GDN_HEADS, GDN_DK, GDN_DV), F32))
    return pl.pallas_call(
        functools.partial(_gdn_kernel, T=T, NS=NS, has_init=has_init, emit_state=prompt),
        grid=(B // NS, GDN_HEADS),
        in_specs=in_specs,
        out_specs=out_specs,
        out_shape=out_shape,
        scratch_shapes=[pltpu.VMEM((T + 2 * CONV_HALO, LANES), F32)]
        + [pltpu.VMEM((R, LANES), F32) for _ in range(4)]
        + [pltpu.VMEM((2 * NS, GDN_DK, GDN_DV), F32),
           pltpu.VMEM((2, R, GDN_DV), F32),
           pltpu.VMEM((2, nc, 2 * CHUNK, GDN_DK), BF16),
           pltpu.VMEM((2, R, GDN_DK), BF16),
           pltpu.VMEM((R, LANES), BF16),
           pltpu.VMEM((2, nc, 1, LANES), F32)],
        compiler_params=_cparams("arbitrary", "arbitrary"),
        name="gdn_prompt" if prompt else "gdn_sample",
    )(*args)


def _router_kernel(x_ref, g_ref, mod_ref, whi_ref, wlo_ref, b_ref, h_ref, ri_ref, w_ref, cnt_ref, count_scr):
    @pl.when(pl.program_id(0) == 0)
    def _():
        count_scr[...] = jnp.zeros_like(count_scr)

    mod = mod_ref[...]
    h = _rms(x_ref[...], g_ref[...]) * (1.0 + mod[:, D_MODEL:2 * D_MODEL]) + mod[:, :D_MODEL]
    hb = h.astype(BF16)
    h_ref[...] = h
    hl = (h - hb.astype(F32)).astype(BF16)
    whi = whi_ref[...]
    logit = _dot(hb, whi) + (_dot(hl, whi) + _dot(hb, wlo_ref[...])) + b_ref[...]
    lane = lax.broadcasted_iota(jnp.int32, logit.shape, 1)
    is_group = (lane >= MOE_EXPERTS) & (lane < MOE_EXPERTS + MOE_GROUPS)
    glog = jnp.where(is_group, logit, NEG_INF)
    gmax = jnp.max(glog, axis=1, keepdims=True)
    gsel = jnp.min(jnp.where(glog == gmax, lane, LANES), axis=1, keepdims=True) - MOE_EXPERTS
    gw = 1.0 / jnp.sum(jnp.exp(glog - gmax), axis=1, keepdims=True)
    lo = gsel * MOE_PER_GROUP
    in_group = (lane >= lo) & (lane < lo + MOE_PER_GROUP)
    elog = jnp.where(in_group, logit, NEG_INF)
    v1 = jnp.max(elog, axis=1, keepdims=True)
    i1 = jnp.min(jnp.where(elog == v1, lane, LANES), axis=1, keepdims=True)
    elog2 = jnp.where(lane == i1, NEG_INF, elog)
    v2 = jnp.max(elog2, axis=1, keepdims=True)
    i2 = jnp.min(jnp.where(elog2 == v2, lane, LANES), axis=1, keepdims=True)
    e2 = jnp.exp(v2 - v1)
    w1 = gw / (1.0 + e2)
    w2 = gw * e2 / (1.0 + e2)
    w_ref[...] = jnp.where(lane == 0, w1, jnp.where(lane == 1, w2, 0.0))
    tm = logit.shape[0]
    onehot = jnp.where((lane == i1) | (lane == i2), 1.0, 0.0)
    t_i = lax.broadcasted_iota(jnp.int32, (tm, tm), 0)
    s_i = lax.broadcasted_iota(jnp.int32, (tm, tm), 1)
    earlier = jnp.where(s_i < t_i, 1.0, 0.0).astype(BF16)
    before = _dot(earlier, onehot.astype(BF16)) + count_scr[...]
    r1 = jnp.sum(jnp.where(lane == i1, before, 0.0), axis=1, keepdims=True).astype(jnp.int32)
    r2 = jnp.sum(jnp.where(lane == i2, before, 0.0), axis=1, keepdims=True).astype(jnp.int32)
    ri_ref[...] = jnp.where(lane == 0, i1, jnp.where(lane == 1, i2, jnp.where(lane == 2, r1, jnp.where(lane == 3, r2, 0))))
    count_scr[...] += jnp.sum(onehot, axis=0, keepdims=True)
    cnt_ref[...] = count_scr[...]


def _router(x, gain, mod4, layer, w_route, b_route):
    tm = ROW_TILE
    w_hi = w_route.astype(BF16)
    w_lo = (w_route - w_hi.astype(F32)).astype(BF16)
    return pl.pallas_call(
        _router_kernel,
        grid=(N_TOK // tm,),
        in_specs=[pl.BlockSpec((tm, D_MODEL), lambda i: (i, 0)),
                  pl.BlockSpec((1, D_MODEL), lambda i: (0, 0)),
                  pl.BlockSpec((None, None, 1, 3 * D_MODEL), lambda i: (layer, _mod_index(i, tm), 0, 1)),
                  pl.BlockSpec((D_MODEL, LANES), lambda i: (0, 0)),
                  pl.BlockSpec((D_MODEL, LANES), lambda i: (0, 0)),
                  pl.BlockSpec((1, LANES), lambda i: (0, 0))],
        out_specs=[pl.BlockSpec((tm, D_MODEL), lambda i: (i, 0)),
                   pl.BlockSpec((tm, LANES), lambda i: (i, 0)),
                   pl.BlockSpec((tm, LANES), lambda i: (i, 0)),
                   pl.BlockSpec((1, LANES), lambda i: (0, 0))],
        out_shape=[jax.ShapeDtypeStruct((N_TOK, D_MODEL), F32),
                   jax.ShapeDtypeStruct((N_TOK, LANES), jnp.int32),
                   jax.ShapeDtypeStruct((N_TOK, LANES), F32),
                   jax.ShapeDtypeStruct((1, LANES), F32)],
        scratch_shapes=[pltpu.VMEM((1, LANES), F32)],
        compiler_params=_cparams("arbitrary"),
        name="router",
    )(x, gain.reshape(1, D_MODEL), mod4, w_hi, w_lo, b_route)


N_ASSIGN = 2 * N_TOK
MOE_TILE = 512
MOE_TILES = N_ASSIGN // MOE_TILE
MOE_SEGMENTS = MOE_TILES + MOE_EXPERTS - 1
DISPATCH_ROWS = 1024
COMBINE_ROWS = 256


def _routing_tables(ri, cnt):
    counts = cnt[0, :MOE_EXPERTS].astype(jnp.int32)
    cum_end = jnp.cumsum(counts)
    cum_start = cum_end - counts
    where = (ri[:, 0:2].reshape(N_ASSIGN), ri[:, 2:4].reshape(N_ASSIGN), cum_start)
    tile_start = jnp.arange(MOE_TILES, dtype=jnp.int32) * MOE_TILE
    expert_start = cum_start[1:]
    tile_rank = jnp.arange(MOE_TILES, dtype=jnp.int32) + jnp.sum(
        (expert_start[None, :] < tile_start[:, None]).astype(jnp.int32), axis=1)
    expert_rank = jnp.arange(MOE_EXPERTS - 1, dtype=jnp.int32) + jnp.minimum(expert_start // MOE_TILE + 1, MOE_TILES)
    values = jnp.concatenate([tile_start, expert_start])
    ranks = jnp.concatenate([tile_rank, expert_rank])
    slot = jnp.arange(MOE_SEGMENTS, dtype=jnp.int32)
    cuts = jnp.sum(jnp.where(ranks[None, :] == slot[:, None], values[None, :], 0), axis=1)
    seg_lo = cuts
    seg_hi = jnp.concatenate([cuts[1:], jnp.full((1,), N_ASSIGN, jnp.int32)])
    seg_tile = jnp.minimum(seg_lo // MOE_TILE, MOE_TILES - 1)
    seg_expert = jnp.minimum(jnp.sum((cum_end[None, :] <= seg_lo[:, None]).astype(jnp.int32), axis=1),
                             MOE_EXPERTS - 1)
    seg_first = jnp.concatenate([jnp.ones((1,), jnp.int32), (seg_tile[1:] != seg_tile[:-1]).astype(jnp.int32)])
    return where, (seg_tile, seg_expert, seg_lo, seg_hi, seg_first)


def _row_copy(src_ref, src_row, dst_ref, dst_row, sem):
    return pltpu.make_async_copy(src_ref.at[pl.ds(src_row, 1), :], dst_ref.at[pl.ds(dst_row, 1), :], sem)


def _dispatch_kernel(expert_ref, rank_ref, start_ref, h_ref, xs_ref, sem):
    base = pl.program_id(0) * DISPATCH_ROWS

    def start(t, carry):
        for k in range(2):
            a = 2 * (base + t) + k
            _row_copy(h_ref, t, xs_ref, start_ref[expert_ref[a]] + rank_ref[a], sem).start(priority=k)
        return carry

    def wait(t, carry):
        for k in range(2):
            _row_copy(h_ref, 0, xs_ref, 0, sem).wait()
        return carry

    lax.fori_loop(0, DISPATCH_ROWS, start, 0, unroll=4)
    lax.fori_loop(0, DISPATCH_ROWS, wait, 0, unroll=4)


def _dispatch(where, h):
    tm = DISPATCH_ROWS
    return pl.pallas_call(
        _dispatch_kernel,
        grid_spec=pltpu.PrefetchScalarGridSpec(
            num_scalar_prefetch=3,
            grid=(N_TOK // tm,),
            in_specs=[pl.BlockSpec((tm, D_MODEL), lambda i, e, r, s: (i, 0))],
            out_specs=pl.BlockSpec(memory_space=pl.ANY),
            scratch_shapes=[pltpu.SemaphoreType.DMA(())]),
        out_shape=jax.ShapeDtypeStruct((N_ASSIGN, D_MODEL), F32),
        compiler_params=_cparams("arbitrary"),
        name="moe_dispatch",
    )(*where, h)


def _experts_kernel(tile_ref, expert_ref, lo_ref, hi_ref, first_ref, xs_ref, wg_ref, wu_ref, wd_ref, o_ref):
    p = pl.program_id(0)
    lo, hi = lo_ref[p], hi_ref[p]

    @pl.when(first_ref[p] == 1)
    def _():
        o_ref[...] = jnp.zeros_like(o_ref)

    @pl.when(hi > lo)
    def _():
        x = xs_ref[...].astype(BF16)
        hg = _dot(x, wg_ref[...].astype(BF16))
        hu = _dot(x, wu_ref[...].astype(BF16))
        row = tile_ref[p] * MOE_TILE + lax.broadcasted_iota(jnp.int32, (MOE_TILE, 1), 0)
        act = jnp.where((row >= lo) & (row < hi), _silu(hg) * hu, 0.0)
        o_ref[...] += _bdot(act, wd_ref[...])


def _experts(tables, xs, wg, wu, wd, layer):
    weight = lambda shape: pl.BlockSpec((None, None) + shape, lambda p, tile, expert, lo, hi, first: (layer, expert[p], 0, 0))
    rows = pl.BlockSpec((MOE_TILE, D_MODEL), lambda p, tile, expert, lo, hi, first: (tile[p], 0))
    return pl.pallas_call(
        _experts_kernel,
        grid_spec=pltpu.PrefetchScalarGridSpec(
            num_scalar_prefetch=5,
            grid=(MOE_SEGMENTS,),
            in_specs=[rows, weight((D_MODEL, MOE_FF)), weight((D_MODEL, MOE_FF)), weight((MOE_FF, D_MODEL))],
            out_specs=rows),
        out_shape=jax.ShapeDtypeStruct((N_ASSIGN, D_MODEL), F32),
        compiler_params=_cparams("arbitrary"),
        name="moe_experts",
    )(*tables, xs, wg, wu, wd)


def _combine_kernel(expert_ref, rank_ref, start_ref, ys_ref, w_ref, x_ref, mod_ref, gf_ref, o_ref, buf, sem, *,
                    final, tile0, n_tiles):
    i = pl.program_id(0)
    slot = i % 2

    def gather(tile, into):
        base = (tile0 + tile) * COMBINE_ROWS

        def start(t, carry):
            for k in range(2):
                a = 2 * (base + t) + k
                _row_copy(ys_ref, start_ref[expert_ref[a]] + rank_ref[a], buf.at[into, k], t,
                          sem.at[into]).start(priority=k)
            return carry

        lax.fori_loop(0, COMBINE_ROWS, start, 0, unroll=4)

    @pl.when(i == 0)
    def _():
        gather(0, 0)

    @pl.when(i + 1 < n_tiles)
    def _():
        gather(i + 1, 1 - slot)

    def wait(t, carry):
        for k in range(2):
            _row_copy(ys_ref, 0, buf.at[slot, k], 0, sem.at[slot]).wait()
        return carry

    lax.fori_loop(0, COMBINE_ROWS, wait, 0, unroll=4)
    w = w_ref[...]
    y = x_ref[...] + mod_ref[...] * (w[:, 0:1] * buf[slot, 0] + w[:, 1:2] * buf[slot, 1])
    if final:
        y = _rms(y, gf_ref[...])
    o_ref[...] = y


def _combine(where, ys, wts, x, mod4, layer, norm_final, final, tile0, n_tiles):
    tm = COMBINE_ROWS
    return pl.pallas_call(
        functools.partial(_combine_kernel, final=final, tile0=tile0, n_tiles=n_tiles),
        grid_spec=pltpu.PrefetchScalarGridSpec(
            num_scalar_prefetch=3,
            grid=(n_tiles,),
            in_specs=[pl.BlockSpec(memory_space=pl.ANY),
                      pl.BlockSpec((tm, LANES), lambda i, e, r, s: (tile0 + i, 0)),
                      pl.BlockSpec((tm, D_MODEL), lambda i, e, r, s: (tile0 + i, 0)),
                      pl.BlockSpec((None, None, 1, D_MODEL),
                                   lambda i, e, r, s: (layer, _mod_index(tile0 + i, tm), 0, 5)),
                      pl.BlockSpec((1, D_MODEL), lambda i, e, r, s: (0, 0))],
            out_specs=pl.BlockSpec((tm, D_MODEL), lambda i, e, r, s: (i, 0)),
            scratch_shapes=[pltpu.VMEM((2, 2, tm, D_MODEL), F32), pltpu.SemaphoreType.DMA((2,))]),
        out_shape=jax.ShapeDtypeStruct((n_tiles * tm, D_MODEL), F32),
        compiler_params=_cparams("arbitrary"),
        name="moe_combine",
    )(*where, ys, wts, x, mod4, norm_final.reshape(1, D_MODEL))


def _rope_tables():
    t = jnp.arange(DEC_SEQ)
    pos = jnp.stack([(t // GRID_W).astype(F32), (t % GRID_W).astype(F32)], axis=1)
    nf = MLA_ROPE // 4
    inv = ROPE_BASE ** (-jnp.arange(nf, dtype=F32) / nf)
    j = jnp.arange(MLA_ROPE)
    ang = pos[:, j // (2 * nf)] * inv[j % nf][None, :]
    sign = jnp.where((j % (2 * nf)) < nf, -1.0, 1.0)
    cos = jnp.pad(jnp.cos(ang), ((0, 0), (MLA_NOPE, LANES - MLA_NOPE - MLA_ROPE)), constant_values=1.0)
    sin = jnp.pad(jnp.sin(ang) * sign, ((0, 0), (MLA_NOPE, LANES - MLA_NOPE - MLA_ROPE)))
    cos = jnp.concatenate([jnp.ones((ROW_TILE, LANES), F32), cos], axis=0)
    sin = jnp.concatenate([jnp.zeros((ROW_TILE, LANES), F32), sin], axis=0)
    return cos, sin


def _pad_heads(w, n_heads, width, lo, hi):
    k = w.shape[0]
    w = w.reshape(k, n_heads, width)[:, :, lo:hi]
    w = jnp.pad(w, ((0, 0), (0, 0), (0, LANES - (hi - lo))))
    return w.reshape(k, n_heads * LANES)


def kernel(x_prompt, x_sample, c, cache_mla_kv, cache_mla_krope, state_mlstm_C, state_mlstm_n, state_mlstm_m, state_ssd, state_gdn, c_ctx, ada_w, ada_b, norm_mix, norm_ffn, w_in_even, ml_i_bias, ml_f_bias, ml_norm, mla_q_norm, mla_w_uq, mla_kv_norm, mla_w_ukv, w_out_even, w_in_odd, ssd_conv_w, ssd_conv_b, ssd_dt_bias, ssd_A_log, ssd_D, ssd_norm, gdn_conv_w, gdn_conv_b, gdn_dt_bias, gdn_A_log, gdn_norm, w_out_odd, moe_w_group, moe_b_group, moe_w_expert, moe_b_expert, moe_w_gate, moe_w_up, moe_w_down, norm_final):
    x = (x_prompt.reshape(N_PROMPT, D_MODEL), x_sample.reshape(N_SAMPLE, D_MODEL))
    cond = jnp.concatenate([c_ctx[None, :], c, jnp.zeros((N_COND - 1 - DEC_BATCH, D_MODEL), F32)], axis=0)
    mod4 = _ada(cond, ada_w, ada_b).reshape(DEPTH, N_COND, 1, 6 * D_MODEL)
    cos_tab, sin_tab = _rope_tables()

    def moe_layer(x, layer, final):
        w_route = jnp.concatenate([moe_w_expert[layer], moe_w_group[layer]], axis=1)
        w_route = jnp.pad(w_route, ((0, 0), (0, LANES - MOE_EXPERTS - MOE_GROUPS)))
        b_route = _lane_row(jnp.concatenate([moe_b_expert[layer], moe_b_group[layer]]))
        h, ri, wts, cnt = _router(x, norm_ffn[layer], mod4, layer, w_route, b_route)
        where, tables = _routing_tables(ri, cnt)
        ys = _experts(tables, _dispatch(where, h), moe_w_gate, moe_w_up, moe_w_down, layer)
        combine = functools.partial(_combine, where, ys, wts, x, mod4, layer, norm_final, final)
        if not final:
            return combine(0, N_TOK // COMBINE_ROWS)
        p_tiles = N_PROMPT // COMBINE_ROWS
        return combine(0, p_tiles), combine(p_tiles, N_SAMPLE // COMBINE_ROWS)

    e = 0
    w = w_in_even[e]
    off = np.cumsum([0, 4 * ML_HEADS * ML_DK, 2 * ML_HEADS, 2 * ML_HEADS, MLA_Q_RANK, MLA_KV_RANK, MLA_ROPE])
    w_misc = jnp.concatenate([w[:, off[1]:off[3]],
                              jnp.zeros((D_MODEL, MLA_NOPE - 4 * ML_HEADS), F32),
                              w[:, off[5]:off[6]],
                              jnp.zeros((D_MODEL, LANES - MLA_NOPE - MLA_ROPE), F32)], axis=1)
    weights = [w[:, :off[1]].astype(BF16), w[:, off[3]:off[4]].astype(BF16),
               w[:, off[4]:off[5]].astype(BF16), w_misc.astype(BF16)]
    qkvo, cq, ckv, misc0 = _inproj(x, norm_mix[0], mod4, 0, weights)

    hm_p, st_c, st_n, st_m = _mlstm(qkvo, misc0, ml_i_bias[e], ml_f_bias[e], ml_norm[e], prompt=True)
    (hm_s,) = _mlstm(qkvo, misc0, ml_i_bias[e], ml_f_bias[e], ml_norm[e], prompt=False,
                     init=(state_mlstm_C[:, e], state_mlstm_n[:, e], state_mlstm_m[:, e]))

    dq = MLA_NOPE + MLA_ROPE
    w_uq_pad = _pad_heads(mla_w_uq[e], MLA_HEADS, dq, 0, dq).astype(BF16)
    w_uk_pad = _pad_heads(mla_w_ukv[e], MLA_HEADS, MLA_NOPE + MLA_V, 0, MLA_NOPE).astype(BF16)
    w_uv = mla_w_ukv[e].reshape(MLA_KV_RANK, MLA_HEADS, MLA_NOPE + MLA_V)[:, :, MLA_NOPE:]
    w_uv = w_uv.reshape(MLA_KV_RANK, MLA_HEADS * MLA_V).astype(BF16)
    q_cat = _mla_q(cq, mla_q_norm[e], w_uq_pad, cos_tab, sin_tab)
    ckv_n, k_cat, v_all = _mla_kv(ckv, misc0, w_uk_pad, w_uv, mla_kv_norm[e], cos_tab, sin_tab)
    cache_kpe = jnp.pad(cache_mla_krope[:, e].reshape(DEC_BATCH * PAST_LEN, MLA_ROPE),
                        ((0, 0), (MLA_NOPE, LANES - MLA_NOPE - MLA_ROPE)))
    k_cache, v_cache = _mla_kv(cache_mla_kv[:, e].reshape(DEC_BATCH * PAST_LEN, MLA_KV_RANK), cache_kpe,
                               w_uk_pad, w_uv)
    att_p = _attention(q_cat, k_cat, v_all, B=BATCH, Tq=SEQ, Tk=SEQ, tq=SEQ, q_row0=0, heads=MLA_HEADS)

    def with_cache(cache, new):
        width = new.shape[1]
        both = jnp.concatenate([cache.reshape(DEC_BATCH, PAST_LEN, width),
                                new[N_PROMPT:].reshape(DEC_BATCH, DEC_SEQ, width)], axis=1)
        return both.reshape(DEC_BATCH * (PAST_LEN + DEC_SEQ), width)

    att_s = _attention(q_cat, with_cache(k_cache, k_cat), with_cache(v_cache, v_all), B=DEC_BATCH, Tq=DEC_SEQ,
                       Tk=PAST_LEN + DEC_SEQ, tq=256, q_row0=N_PROMPT, heads=2)
    wo = w_out_even[e].astype(BF16)
    x = _outproj((hm_p, hm_s), (att_p, att_s), wo[:ML_HEADS * ML_DV], wo[ML_HEADS * ML_DV:], x, mod4, 0)
    x = moe_layer(x, 0, final=False)

    oi = 0
    w = w_in_odd[oi]
    ssd_w = SSD_HEADS * SSD_P
    ssd_cc = ssd_w + 2 * SSD_GROUPS * SSD_N
    gdn_w = GDN_HEADS * GDN_DK
    off = np.cumsum([0, ssd_w, ssd_cc, 2 * SSD_HEADS, 3 * gdn_w, gdn_w, 2 * GDN_HEADS, 2 * GDN_HEADS])
    w_misc = jnp.concatenate([w[:, off[2]:off[3]], w[:, off[5]:off[7]],
                              jnp.zeros((D_MODEL, LANES - 2 * SSD_HEADS - 4 * GDN_HEADS), F32)], axis=1)
    weights = [w[:, off[0]:off[1]].astype(BF16), w[:, off[1]:off[2]].astype(BF16),
               w[:, off[3]:off[4]].astype(BF16), w[:, off[4]:off[5]].astype(BF16), w_misc.astype(BF16)]
    z_s, xbc, qkv_g, z_g, misc = _inproj(x, norm_mix[1], mod4, 1, weights)

    ssd_args = (xbc, misc, ssd_conv_w[oi], ssd_conv_b[oi], ssd_dt_bias[oi], ssd_A_log[oi], ssd_D[oi])
    ys_p, st_ssd = _ssd(*ssd_args, prompt=True)
    (ys_s,) = _ssd(*ssd_args, prompt=False, init=state_ssd[:, oi])

    lo = 2 * SSD_HEADS
    gdn_dt_row = jnp.pad(gdn_dt_bias[oi].reshape(-1), (lo, LANES - lo - 2 * GDN_HEADS)).reshape(1, LANES)
    gdn_alog_row = jnp.pad(gdn_A_log[oi].reshape(-1), (lo, LANES - lo - 2 * GDN_HEADS)).reshape(1, LANES)
    gdn_args = (qkv_g, misc, z_g, gdn_conv_w[oi], gdn_conv_b[oi], gdn_dt_row, gdn_alog_row, gdn_norm[oi])
    og_p, st_gdn = _gdn(*gdn_args, prompt=True)
    (og_s,) = _gdn(*gdn_args, prompt=False, init=state_gdn[:, oi])

    wo = w_out_odd[oi].astype(BF16)
    x = _outproj((ys_p, ys_s), (og_p, og_s), wo[:ssd_w], wo[ssd_w:], x, mod4, 1, z=z_s, gnorm=ssd_norm[oi])
    y_p, y_s = moe_layer(x, 1, final=True)

    y_prompt = y_p.reshape(BATCH, SEQ, D_MODEL)
    y_sample = y_s.reshape(DEC_BATCH, DEC_SEQ, D_MODEL)
    new_mla_kv = ckv_n[:N_PROMPT].reshape(BATCH, 1, SEQ, MLA_KV_RANK)
    new_mla_krope = misc0[:N_PROMPT, MLA_NOPE:MLA_NOPE + MLA_ROPE].reshape(BATCH, 1, SEQ, MLA_ROPE)
    return (y_prompt, y_sample, new_mla_kv, new_mla_krope, st_c[:, None], st_n.reshape(BATCH, 1, 2, ML_HEADS, ML_DK),
            st_m.reshape(BATCH, 1, 2, ML_HEADS), st_ssd[:, None], st_gdn[:, None])
```

```python
import functools

import numpy as np
import jax
import jax.numpy as jnp
from jax import lax
from jax.experimental import pallas as pl
from jax.experimental.pallas import tpu as pltpu

F32 = jnp.float32
BF16 = jnp.bfloat16

D_MODEL = 1024
BATCH = 32
SEQ = 256
DEPTH = 2
DEC_BATCH = 2
DEC_SEQ = 2048
PAST_LEN = 256
GRID_W = 64
EPS = 1e-6
ML_HEADS = 4
ML_DK = 128
ML_DV = 128
MLA_HEADS = 8
MLA_Q_RANK = 384
MLA_KV_RANK = 256
MLA_NOPE = 64
MLA_ROPE = 32
MLA_V = 64
ROPE_BASE = 10000.0
SSD_HEADS = 8
SSD_P = 64
SSD_GROUPS = 2
SSD_N = 128
GDN_HEADS = 4
GDN_DK = 128
GDN_DV = 128
CONV_W = 5
MOE_GROUPS = 4
MOE_PER_GROUP = 8
MOE_EXPERTS = 32
MOE_FF = 256

N_PROMPT = BATCH * SEQ
N_SAMPLE = DEC_BATCH * DEC_SEQ
N_TOK = N_PROMPT + N_SAMPLE
N_COND = 8

LANES = 128
CHUNK = 64
ROW_TILE = 512
CONV_HALO = 8
CONV_ROWS = 256
VMEM_LIMIT = 56 * 1024 * 1024
PRE_CHUNKS = 8
STEP_CHUNKS = 4
GDN_PROMPT_SEQS_PER_STEP = 8
GDN_SAMPLE_SEQS_PER_STEP = 1

assert LANES == 2 * CHUNK and SSD_P == CHUNK

NEG_INF = float("-inf")


def _cparams(*sem):
    return pltpu.CompilerParams(dimension_semantics=sem, vmem_limit_bytes=VMEM_LIMIT)


def _dot(a, b):
    return jnp.dot(a, b, preferred_element_type=F32)


def _dot_nt(a, b):
    return lax.dot_general(a, b, (((1,), (1,)), ((), ())), preferred_element_type=F32)


def _dot_tn(a, b):
    return lax.dot_general(a, b, (((0,), (0,)), ((), ())), preferred_element_type=F32)


def _bdot(a, b):
    return _dot(a.astype(BF16), b.astype(BF16))


def _rms(x, g):
    return x * lax.rsqrt(jnp.mean(x * x, axis=-1, keepdims=True) + EPS) * g


def _softplus(x):
    return jnp.maximum(x, 0.0) + jnp.log1p(jnp.exp(-jnp.abs(x)))


def _silu(x):
    return x * jax.nn.sigmoid(x)


def _colsel(x, j):
    lane = lax.broadcasted_iota(jnp.int32, x.shape, 1)
    return jnp.sum(jnp.where(lane == j, x, 0.0), axis=1, keepdims=True)


def _lanes(x):
    return jnp.broadcast_to(x, (x.shape[0], LANES))


def _half_masks(rev_lo, rev_hi):
    t = lax.broadcasted_iota(jnp.int32, (CHUNK, LANES), 0)
    lane = lax.broadcasted_iota(jnp.int32, (CHUNK, LANES), 1)
    s = lane & (CHUNK - 1)
    hi = lane >= CHUNK

    def pick(fwd, bwd):
        if rev_lo == rev_hi:
            return bwd if rev_lo else fwd
        on_hi, on_lo = (bwd, fwd) if rev_hi else (fwd, bwd)
        return (hi & on_hi) | (jnp.logical_not(hi) & on_lo)

    return dict(hi=hi, t=t, s=s, eye=(s == t), incl=pick(s <= t, s >= t), incl_t=pick(t <= s, t >= s),
                strict=pick(s < t, s > t))


def _pack(hm, col_lo, col_hi):
    return jnp.where(hm["hi"], col_hi, col_lo)


def _rows(hm, cols):
    return jnp.sum(jnp.where(hm["eye"], cols, 0.0), axis=0, keepdims=True)


def _half_sums(hm, x):
    lo = jnp.sum(jnp.where(hm["hi"], 0.0, x), axis=1, keepdims=True)
    hi = jnp.sum(jnp.where(hm["hi"], x, 0.0), axis=1, keepdims=True)
    return lo, hi


def _half_cum(hm, col_lo, col_hi):
    cols = _pack(hm, col_lo, col_hi)
    cum_lo, cum_hi = _half_sums(hm, jnp.where(hm["incl"], _rows(hm, cols), 0.0))
    cum_rows = jnp.sum(jnp.where(hm["incl_t"], cols, 0.0), axis=0, keepdims=True)
    return cum_lo, cum_hi, cum_rows


def _blockdiag(hm, x):
    return jnp.concatenate([jnp.where(hm["hi"], 0.0, x).astype(BF16),
                            jnp.where(hm["hi"], x, 0.0).astype(BF16)], axis=0)


def _chunk_rows(c):
    if isinstance(c, int):
        return pl.ds(c * CHUNK, CHUNK)
    return pl.ds(pl.multiple_of(c * CHUNK, CHUNK), CHUNK)


def _block_loop(n, body, size):
    if n <= size:
        body(list(range(n)))
        return

    def block(blk, carry):
        body([blk * size + j for j in range(size)])
        return carry

    lax.fori_loop(0, n // size, block, 0)


def _mod_index(i, rows_per_tile):
    p_tiles = N_PROMPT // rows_per_tile
    s_tiles = DEC_SEQ // rows_per_tile
    return jnp.where(i < p_tiles, 0, 1 + (i - p_tiles) // s_tiles)


def _ada_kernel(c_ref, w_ref, b_ref, o_ref):
    c = c_ref[...]
    o_ref[...] = _bdot(_silu(c), w_ref[...]) + b_ref[...]


def _ada(cond, ada_w, ada_b):
    nb = 6
    return pl.pallas_call(
        _ada_kernel,
        grid=(DEPTH, nb),
        in_specs=[pl.BlockSpec((N_COND, D_MODEL), lambda l, j: (0, 0)),
                  pl.BlockSpec((None, D_MODEL, D_MODEL), lambda l, j: (l, 0, j)),
                  pl.BlockSpec((None, 1, D_MODEL), lambda l, j: (l, 0, j))],
        out_specs=pl.BlockSpec((None, N_COND, D_MODEL), lambda l, j: (l, 0, j)),
        out_shape=jax.ShapeDtypeStruct((DEPTH, N_COND, 6 * D_MODEL), F32),
        compiler_params=_cparams("arbitrary", "arbitrary"),
        name="ada",
    )(cond, ada_w, ada_b.reshape(DEPTH, 1, 6 * D_MODEL))


def _token_operand(x, tm):
    if not isinstance(x, tuple):
        return [pl.BlockSpec((tm, x.shape[1]), lambda i: (i, 0))], [x]
    pt = N_PROMPT // tm
    width = x[0].shape[1]
    return ([pl.BlockSpec((tm, width), lambda i: (jnp.minimum(i, pt - 1), 0)),
             pl.BlockSpec((tm, width), lambda i: (jnp.maximum(i - pt, 0), 0))], list(x))


def _take_tile(refs, split, tm):
    if not split:
        return refs.pop(0)[...]
    p_ref, s_ref = refs.pop(0), refs.pop(0)
    return jnp.where(pl.program_id(0) < N_PROMPT // tm, p_ref[...], s_ref[...])


def _inproj_kernel(*refs, n_out, split, tm, mla):
    refs = list(refs)
    x = _take_tile(refs, split, tm)
    g_ref, mod_ref = refs[:2]
    w_refs, rest = refs[2:2 + n_out], refs[2 + n_out:]
    mod = mod_ref[...]
    h = _rms(x, g_ref[...]) * (1.0 + mod[:, D_MODEL:2 * D_MODEL]) + mod[:, :D_MODEL]
    hb = h.astype(BF16)
    ys = [_dot(hb, w_ref[...]) for w_ref in w_refs]
    if not mla:
        for y, o_ref in zip(ys, rest):
            o_ref[...] = y
        return
    (gq_ref, wuq_ref, gkv_ref, wuk_ref, wuv_ref, cos_ref, sin_ref,
     qkvo_ref, misc_ref, q_ref, ckvn_ref, k_ref, v_ref) = rest
    y_qkvo, y_cq, y_ckv, y_misc = ys
    qkvo_ref[...] = y_qkvo
    misc_ref[...] = y_misc
    cos, sin = cos_ref[...], sin_ref[...]
    _q_tile(y_cq, gq_ref, wuq_ref, cos, sin, q_ref)
    c = _rms(y_ckv, gkv_ref[...])
    ckvn_ref[...] = c
    _kv_tile(c, _rope(_rope_lanes(y_misc), cos, sin), wuk_ref, wuv_ref, k_ref, v_ref)


def _inproj(x, gain, mod4, layer, weights, mla=None):
    n_out = len(weights)
    tm = ROW_TILE
    rows = lambda width: pl.BlockSpec((tm, width), lambda i: (i, 0))
    full = lambda a: pl.BlockSpec(a.shape, lambda i: (0, 0))
    in_specs, args = _token_operand(x, tm)
    in_specs += [pl.BlockSpec((1, D_MODEL), lambda i: (0, 0)),
                 pl.BlockSpec((None, None, 1, 2 * D_MODEL), lambda i: (layer, _mod_index(i, tm), 0, 0))]
    in_specs += [full(w) for w in weights]
    args += [gain.reshape(1, D_MODEL), mod4, *weights]
    out_specs = [rows(w.shape[1]) for w in weights]
    out_shape = [jax.ShapeDtypeStruct((N_TOK, w.shape[1]), F32) for w in weights]
    if mla is not None:
        gq, w_uq, gkv, w_uk, w_uv, cos_tab, sin_tab = mla
        tab = pl.BlockSpec((tm, LANES), lambda i: (_rope_block_index(i, tm), 0))
        in_specs += [pl.BlockSpec((1, MLA_Q_RANK), lambda i: (0, 0)), full(w_uq),
                     pl.BlockSpec((1, MLA_KV_RANK), lambda i: (0, 0)), full(w_uk), full(w_uv), tab, tab]
        args += [gq.reshape(1, MLA_Q_RANK), w_uq, gkv.reshape(1, MLA_KV_RANK), w_uk, w_uv, cos_tab, sin_tab]
        widths = [(weights[0].shape[1], F32), (LANES, F32), (MLA_HEADS * LANES, BF16), (MLA_KV_RANK, F32),
                  (MLA_HEADS * LANES, BF16), (MLA_HEADS * MLA_V, BF16)]
        out_specs = [rows(width) for width, _ in widths]
        out_shape = [jax.ShapeDtypeStruct((N_TOK, width), dtype) for width, dtype in widths]
    return pl.pallas_call(
        functools.partial(_inproj_kernel, n_out=n_out, split=isinstance(x, tuple), tm=tm, mla=mla is not None),
        grid=(N_TOK // tm,),
        in_specs=in_specs,
        out_specs=out_specs,
        out_shape=out_shape,
        compiler_params=_cparams("arbitrary"),
        name="inproj",
    )(*args)


def _mlstm_kernel(*refs, T, NS, has_init, emit_state):
    it = iter(refs)
    ib_ref, fb_ref = next(it), next(it)
    m0_ref = next(it) if has_init else None
    q_ref, k_ref, v_ref, o_ref, misc_ref, gn_ref = (next(it) for _ in range(6))
    c0_ref, n0_ref = (next(it), next(it)) if has_init else (None, None)
    hm_ref = next(it)
    cout_ref, nout_ref, mout_ref = (next(it), next(it), next(it)) if emit_state else (None, None, None)
    (hb_scr, c_scr, n_scr, m_scr, num_scr, st_scr, kv_scr, nl_scr, bl_scr, gm_scr) = (next(it) for _ in range(10))

    b = pl.program_id(0)
    h = pl.program_id(1)
    nc = T // CHUNK
    scale = ML_DK ** -0.5

    for s in range(NS):
        for d in range(2):
            if has_init:
                c_scr[2 * s + d] = c0_ref[s, d]
                n_scr[2 * s + d] = n0_ref[s, d]
                m_scr[2 * s + d] = jnp.full((1, LANES), m0_ref[b * NS + s, d, h], F32)
            else:
                c_scr[2 * s + d] = jnp.zeros((ML_DK, ML_DV), F32)
                n_scr[2 * s + d] = jnp.zeros((1, ML_DK), F32)
                m_scr[2 * s + d] = jnp.zeros((1, LANES), F32)

    hm = _half_masks(False, True)

    def pre_gates(c):
        rows = _chunk_rows(c)
        mi = misc_ref[rows, :]
        li = [_colsel(mi, d * ML_HEADS + h) + ib_ref[d, h] for d in range(2)]
        lf = [-_softplus(-(_colsel(mi, 2 * ML_HEADS + d * ML_HEADS + h) + fb_ref[d, h])) for d in range(2)]
        b_cols = [None, None]
        b_cols[0], b_cols[1], b_rows = _half_cum(hm, lf[0], lf[1])
        li_rows = _rows(hm, _pack(hm, li[0], li[1]))
        dm = jnp.where(hm["incl"], _pack(hm, b_cols[0], b_cols[1]) - b_rows + li_rows, NEG_INF)
        mloc = [jnp.max(jnp.where(hm["hi"], NEG_INF, dm), axis=1, keepdims=True),
                jnp.max(jnp.where(hm["hi"], dm, NEG_INF), axis=1, keepdims=True)]
        kc = k_ref[rows, :]
        kws = []
        for d in range(2):
            bl = jnp.sum(lf[d], axis=0, keepdims=True)
            g = bl - b_cols[d] + li[d]
            gmax = jnp.max(g, axis=0, keepdims=True)
            kw = jnp.exp(g - gmax) * kc
            kws.append(kw.astype(BF16))
            nl_scr[d, c] = jnp.sum(kw, axis=0, keepdims=True)
            bl_scr[d, c] = _lanes(bl)
            gm_scr[d, c] = _lanes(gmax)
            st_scr[d, 0, rows, :] = _lanes(mloc[d])
            st_scr[d, 1, rows, :] = _lanes(b_cols[d])
        return dict(c=c, rows=rows, kc=kc.astype(BF16), kw2=jnp.concatenate(kws, axis=1),
                    e2=jnp.exp(dm - _pack(hm, mloc[0], mloc[1])))

    def pre_block(chunks):
        chains = [pre_gates(c) for c in chunks]
        for ch in chains:
            qc = (q_ref[ch["rows"], :] * scale).astype(BF16)
            ch["vb"] = v_ref[ch["rows"], :].astype(BF16)
            ch["qk2"] = _dot_nt(qc, jnp.concatenate([ch["kc"], ch["kc"]], axis=0))
            kv2 = _dot_tn(ch["kw2"], ch["vb"])
            kv_scr[0, ch["c"]] = kv2[:ML_DK]
            kv_scr[1, ch["c"]] = kv2[ML_DK:]
        for ch in chains:
            s2 = ch["qk2"] * ch["e2"]
            ch["s2"] = s2.astype(BF16)
            dens = _half_sums(hm, s2)
            for d in range(2):
                st_scr[d, 2, ch["rows"], :] = _lanes(dens[d])
        for ch in chains:
            vb = ch["vb"]
            zero = jnp.zeros_like(vb)
            vbd = jnp.concatenate([jnp.concatenate([vb, zero], axis=1),
                                   jnp.concatenate([zero, vb], axis=1)], axis=0)
            num2 = _dot(ch["s2"], vbd)
            num_scr[0, ch["rows"], :] = num2[:, :ML_DV]
            num_scr[1, ch["rows"], :] = num2[:, ML_DV:]

    def step_block(chunks):
        states = [(m_scr[j], c_scr[j], n_scr[j]) for j in range(2 * NS)]
        pairs = []
        for c in chunks:
            for s in range(NS):
                pair = []
                for d in range(2):
                    cc = s * nc + (c if d == 0 else nc - 1 - c)
                    rows = _chunk_rows(cc)
                    m, c_st, n_st = states[2 * s + d]
                    mloc, b_col, den_loc = st_scr[d, 0, rows, :], st_scr[d, 1, rows, :], st_scr[d, 2, rows, :]
                    bl, gmax = bl_scr[d, cc], gm_scr[d, cc]
                    qc = q_ref[rows, :] * scale
                    inter = b_col + m
                    mq = jnp.maximum(inter, mloc)
                    a = jnp.exp(inter - mq)
                    f = jnp.exp(mloc - mq)
                    den = f * den_loc + a * jnp.sum(qc * n_st, axis=1, keepdims=True)
                    pair.append(dict(rows=rows, qc=qc.astype(BF16), c_st=c_st.astype(BF16), a=a, f=f,
                                     inv=1.0 / jnp.maximum(jnp.abs(den), jnp.exp(-mq))))
                    m_new = jnp.maximum(bl + m, gmax)
                    dec = jnp.exp(bl + m - m_new)
                    fk = jnp.exp(gmax - m_new)
                    states[2 * s + d] = (m_new, dec * c_st + fk * kv_scr[d, cc], dec * n_st + fk * nl_scr[d, cc])
                pairs.append(pair)
        for j in range(2 * NS):
            m_scr[j], c_scr[j], n_scr[j] = states[j]
        for pair in pairs:
            lhs = jnp.concatenate([pair[0]["qc"], pair[1]["qc"]], axis=0)
            rhs = jnp.concatenate([pair[0]["c_st"], pair[1]["c_st"]], axis=1)
            pair.append(_dot(lhs, rhs))
        for pair in pairs:
            res = pair[2]
            for d, dst in enumerate((hm_ref, hb_scr)):
                it_ = pair[d]
                qc_c = res[d * CHUNK:(d + 1) * CHUNK, d * ML_DV:(d + 1) * ML_DV]
                dst[it_["rows"], :] = (it_["f"] * num_scr[d, it_["rows"], :] + it_["a"] * qc_c) * it_["inv"]

    _block_loop(NS * nc, pre_block, PRE_CHUNKS)
    _block_loop(nc, step_block, STEP_CHUNKS)

    hs = hm_ref[...] + hb_scr[...]
    hm_ref[...] = _rms(hs, gn_ref[...]) * jax.nn.sigmoid(o_ref[...])
    if emit_state:
        for s in range(NS):
            for d in range(2):
                cout_ref[s, d] = c_scr[2 * s + d]
                nout_ref[s, d] = n_scr[2 * s + d]
                mout_ref[s, d] = m_scr[2 * s + d][:, 0:1]


def _seq_geometry(prompt, prompt_seqs_per_step=1, sample_seqs_per_step=1):
    if prompt:
        return SEQ, BATCH, prompt_seqs_per_step, 0
    return DEC_SEQ, DEC_BATCH, sample_seqs_per_step, N_PROMPT // (sample_seqs_per_step * DEC_SEQ)


def _mlstm(qkvo, misc, i_bias, f_bias, gnorm, *, prompt, init=None):
    T, B, NS, blk0 = _seq_geometry(prompt)
    R = NS * T
    nc = R // CHUNK
    has_init = init is not None
    smem = pl.BlockSpec(memory_space=pltpu.SMEM)

    def col(j):
        return pl.BlockSpec((R, LANES), lambda b, h: (blk0 + b, j * ML_HEADS + h))

    in_specs = [smem, smem]
    args = [i_bias, f_bias]
    if has_init:
        in_specs.append(smem)
        args.append(init[2])
    in_specs += [col(0), col(1), col(2), col(3),
                 pl.BlockSpec((R, LANES), lambda b, h: (blk0 + b, 0)),
                 pl.BlockSpec((None, 1, ML_DV), lambda b, h: (h, 0, 0))]
    args += [qkvo, qkvo, qkvo, qkvo, misc, gnorm.reshape(ML_HEADS, 1, ML_DV)]
    c_spec = pl.BlockSpec((NS, 2, None, ML_DK, ML_DV), lambda b, h: (b, 0, h, 0, 0))
    n_spec = pl.BlockSpec((NS, 2, None, 1, ML_DK), lambda b, h: (b, 0, h, 0, 0))
    if has_init:
        in_specs += [c_spec, n_spec]
        args += [init[0], init[1].reshape(B, 2, ML_HEADS, 1, ML_DK)]
    out_specs = [pl.BlockSpec((R, LANES), lambda b, h: (b, h))]
    out_shape = [jax.ShapeDtypeStruct((B * T, ML_HEADS * ML_DV), F32)]
    if prompt:
        out_specs += [c_spec, n_spec, pl.BlockSpec((NS, 2, None, 1, 1), lambda b, h: (b, 0, h, 0, 0))]
        out_shape += [jax.ShapeDtypeStruct((B, 2, ML_HEADS, ML_DK, ML_DV), F32),
                      jax.ShapeDtypeStruct((B, 2, ML_HEADS, 1, ML_DK), F32),
                      jax.ShapeDtypeStruct((B, 2, ML_HEADS, 1, 1), F32)]
    return pl.pallas_call(
        functools.partial(_mlstm_kernel, T=T, NS=NS, has_init=has_init, emit_state=prompt),
        grid=(B // NS, ML_HEADS),
        in_specs=in_specs,
        out_specs=out_specs,
        out_shape=out_shape,
        scratch_shapes=[pltpu.VMEM((R, ML_DV), F32),
                        pltpu.VMEM((2 * NS, ML_DK, ML_DV), F32),
                        pltpu.VMEM((2 * NS, 1, ML_DK), F32),
                        pltpu.VMEM((2 * NS, 1, LANES), F32),
                        pltpu.VMEM((2, R, ML_DV), F32),
                        pltpu.VMEM((2, 3, R, LANES), F32),
                        pltpu.VMEM((2, nc, ML_DK, ML_DV), F32),
                        pltpu.VMEM((2, nc, 1, ML_DK), F32),
                        pltpu.VMEM((2, nc, 1, LANES), F32),
                        pltpu.VMEM((2, nc, 1, LANES), F32)],
        compiler_params=_cparams("arbitrary", "arbitrary"),
        name="mlstm_prompt" if prompt else "mlstm_sample",
    )(*args)


def _rope(x, cos, sin_signed):
    lane = lax.broadcasted_iota(jnp.int32, x.shape, 1)
    first = (lane & 15) < 8
    partner = jnp.where(first, pltpu.roll(x, LANES - 8, axis=1), pltpu.roll(x, 8, axis=1))
    return x * cos + partner * sin_signed


def _q_tile(cq, g_ref, w_ref, cos, sin, q_ref):
    y = _bdot(_rms(cq, g_ref[...]), w_ref[...])
    for hd in range(MLA_HEADS):
        sl = slice(hd * LANES, (hd + 1) * LANES)
        q_ref[:, sl] = _rope(y[:, sl], cos, sin).astype(BF16)


def _rope_lanes(kp):
    lane = lax.broadcasted_iota(jnp.int32, kp.shape, 1)
    return jnp.where((lane >= MLA_NOPE) & (lane < MLA_NOPE + MLA_ROPE), kp, 0.0)


def _kv_tile(c, kp, wk_ref, wv_ref, k_ref, v_ref):
    kn = _bdot(c, wk_ref[...])
    for hd in range(MLA_HEADS):
        sl = slice(hd * LANES, (hd + 1) * LANES)
        k_ref[:, sl] = (kn[:, sl] + kp).astype(BF16)
    v_ref[...] = _bdot(c, wv_ref[...]).astype(BF16)


def _rope_block_index(i, tm):
    p_tiles = N_PROMPT // tm
    s_tiles = DEC_SEQ // tm
    return jnp.where(i < p_tiles, 0, 1 + (i - p_tiles) % s_tiles)


def _kv_cache_kernel(ckv_ref, kpe_ref, wk_ref, wv_ref, k_ref, v_ref):
    _kv_tile(ckv_ref[...], _rope_lanes(kpe_ref[...]), wk_ref, wv_ref, k_ref, v_ref)


def _mla_kv_cache(ckv, kpe128, w_uk_pad, w_uv):
    n = ckv.shape[0]
    tm = ROW_TILE
    row = lambda w: pl.BlockSpec((tm, w), lambda i: (i, 0))
    full = lambda a: pl.BlockSpec(a.shape, lambda i: (0, 0))
    in_specs = [row(MLA_KV_RANK), row(LANES), full(w_uk_pad), full(w_uv)]
    args = [ckv, kpe128, w_uk_pad, w_uv]
    out_specs = [row(MLA_HEADS * LANES), row(MLA_HEADS * MLA_V)]
    out_shape = [jax.ShapeDtypeStruct((n, MLA_HEADS * LANES), BF16),
                 jax.ShapeDtypeStruct((n, MLA_HEADS * MLA_V), BF16)]
    return pl.pallas_call(
        _kv_cache_kernel,
        grid=(n // tm,),
        in_specs=in_specs,
        out_specs=out_specs,
        out_shape=out_shape,
        compiler_params=_cparams("arbitrary"),
        name="mla_kv_cache",
    )(*args)


def _attn_kernel(q_ref, k_ref, v_ref, o_ref, *, heads):
    scale = (MLA_NOPE + MLA_ROPE) ** -0.5
    scores = [_dot_nt(q_ref[:, j * LANES:(j + 1) * LANES], k_ref[:, j * LANES:(j + 1) * LANES]) * scale
              for j in range(heads)]
    probs, sums = [], []
    for s in scores:
        p = jnp.exp(s - jnp.max(s, axis=1, keepdims=True))
        sums.append(jnp.sum(p, axis=1, keepdims=True))
        probs.append(p.astype(BF16))
    pvs = [_dot(probs[j], v_ref[:, j * MLA_V:(j + 1) * MLA_V]) for j in range(heads)]
    o_ref[...] = jnp.concatenate([pvs[j] / sums[j] for j in range(heads)], axis=1)


def _attention(q, k, v, *, B, Tq, Tk, tq, q_row0, heads):
    nq = Tq // tq
    blk0 = q_row0 // tq
    return pl.pallas_call(
        functools.partial(_attn_kernel, heads=heads),
        grid=(B, MLA_HEADS // heads, nq),
        in_specs=[pl.BlockSpec((tq, heads * LANES), lambda b, hp, i: (blk0 + b * nq + i, hp)),
                  pl.BlockSpec((Tk, heads * LANES), lambda b, hp, i: (b, hp)),
                  pl.BlockSpec((Tk, heads * MLA_V), lambda b, hp, i: (b, hp))],
        out_specs=pl.BlockSpec((tq, heads * MLA_V), lambda b, hp, i: (b * nq + i, hp)),
        out_shape=jax.ShapeDtypeStruct((B * Tq, MLA_HEADS * MLA_V), F32),
        compiler_params=_cparams("arbitrary", "arbitrary", "arbitrary"),
        name="attention",
    )(q, k, v)


def _outproj_kernel(*refs, odd, split_x, tm):
    refs = list(refs)
    a1 = _take_tile(refs, True, tm)
    a2 = _take_tile(refs, True, tm)
    x = _take_tile(refs, split_x, tm)
    if odd:
        z_ref, gn_ref = refs.pop(0), refs.pop(0)
        a1 = _rms(a1 * _silu(z_ref[...]), gn_ref[...])
    (w1_ref, w2_ref, g1_ref), route_in, o_ref, route_out = refs[:3], refs[3:8], refs[8], refs[9:]
    out = _bdot(a1, w1_ref[...]) + _bdot(a2, w2_ref[...])
    x1 = x + g1_ref[...] * out
    o_ref[...] = x1
    _route_tile(x1, *route_in, *route_out)


def _outproj(a1, a2, w1, w2, x, mod4, layer, gain_ffn, w_route, b_route, z=None, gnorm=None):
    odd = z is not None
    tm = ROW_TILE
    half = w1.shape[0]
    full = lambda a: pl.BlockSpec(a.shape, lambda i: (0, 0))
    rows = lambda width: pl.BlockSpec((tm, width), lambda i: (i, 0))
    in_specs, args = [], []
    for operand in (a1, a2, x):
        specs, arrays = _token_operand(operand, tm)
        in_specs += specs
        args += arrays
    if odd:
        in_specs += [rows(half), pl.BlockSpec((1, half), lambda i: (0, 0))]
        args += [z, gnorm.reshape(1, half)]
    w_hi = w_route.astype(BF16)
    w_lo = (w_route - w_hi.astype(F32)).astype(BF16)
    in_specs += [full(w1), full(w2),
                 pl.BlockSpec((None, None, 1, D_MODEL), lambda i: (layer, _mod_index(i, tm), 0, 2)),
                 pl.BlockSpec((1, D_MODEL), lambda i: (0, 0)),
                 pl.BlockSpec((None, None, 1, 3 * D_MODEL), lambda i: (layer, _mod_index(i, tm), 0, 1)),
                 full(w_hi), full(w_lo), pl.BlockSpec((1, LANES), lambda i: (0, 0))]
    args += [w1, w2, mod4, gain_ffn.reshape(1, D_MODEL), mod4, w_hi, w_lo, b_route]
    return pl.pallas_call(
        functools.partial(_outproj_kernel, odd=odd, split_x=isinstance(x, tuple), tm=tm),
        grid=(N_TOK // tm,),
        in_specs=in_specs,
        out_specs=[rows(D_MODEL), rows(D_MODEL), rows(LANES), rows(LANES), pl.BlockSpec((1, LANES), lambda i: (0, 0))],
        out_shape=[jax.ShapeDtypeStruct((N_TOK, D_MODEL), F32),
                   jax.ShapeDtypeStruct((N_TOK, D_MODEL), F32),
                   jax.ShapeDtypeStruct((N_TOK, LANES), jnp.int32),
                   jax.ShapeDtypeStruct((N_TOK, LANES), F32),
                   jax.ShapeDtypeStruct((1, LANES), F32)],
        scratch_shapes=[pltpu.VMEM((1, LANES), F32)],
        compiler_params=_cparams("arbitrary"),
        name="outproj_router",
    )(*args)


def _conv_silu(src_ref, dst_ref, pad_ref, w_ref, b_ref, T, NS):
    width = src_ref.shape[1]
    zeros = jnp.zeros((CONV_HALO, width), F32)
    pad_ref[pl.ds(0, CONV_HALO), :] = zeros
    pad_ref[pl.ds(CONV_HALO + T, CONV_HALO), :] = zeros
    w = w_ref[...]
    bias = b_ref[...]
    for s in range(NS):
        pad_ref[pl.ds(CONV_HALO, T), :] = src_ref[pl.ds(s * T, T), :]
        for r0 in range(0, T, CONV_ROWS):
            acc = bias
            for j in range(CONV_W):
                start = r0 + CONV_HALO + j - CONV_W // 2
                acc = acc + w[j:j + 1, :] * pad_ref[pl.ds(start, CONV_ROWS), :]
            dst_ref[pl.ds(s * T + r0, CONV_ROWS), :] = _silu(acc)


def _ssd_kernel(*refs, T, NS, has_init, emit_state):
    it = iter(refs)
    x_ref, b_ref, c_ref = next(it), next(it), next(it)
    wx_ref, wb_ref, wc_ref = next(it), next(it), next(it)
    bx_ref, bb_ref, bc_ref = next(it), next(it), next(it)
    misc_ref, dtb_ref, alog_ref, dskip_ref = next(it), next(it), next(it), next(it)
    h0_ref = next(it) if has_init else None
    y_ref = next(it)
    hout_ref = next(it) if emit_state else None
    (pad_scr, xc_scr, bc_scr, cc_scr, yb_scr, hs_scr, yi_scr, ea_scr, upd_scr, eal_scr) = (
        next(it) for _ in range(10))

    p = pl.program_id(1)
    nc = T // CHUNK

    _conv_silu(x_ref, xc_scr, pad_scr, wx_ref, bx_ref, T, NS)
    _conv_silu(b_ref, bc_scr, pad_scr, wb_ref, bb_ref, T, NS)
    _conv_silu(c_ref, cc_scr, pad_scr, wc_ref, bc_ref, T, NS)

    for s in range(NS):
        for d in range(2):
            if has_init:
                hs_scr[2 * s + d] = h0_ref[s, d].reshape(2 * SSD_P, SSD_N).T
            else:
                hs_scr[2 * s + d] = jnp.zeros((SSD_N, 2 * SSD_P), F32)

    dtb = dtb_ref[...]
    neg_a = -jnp.exp(alog_ref[...])
    hms = [_half_masks(False, False), _half_masks(True, True)]
    hi_row = lax.broadcasted_iota(jnp.int32, (1, LANES), 1) >= CHUNK

    def pre_block(chunks):
        chains = []
        for c in chunks:
            rows = _chunk_rows(c)
            bc = bc_scr[rows, :].astype(BF16)
            cb2 = _dot_nt(cc_scr[rows, :].astype(BF16), jnp.concatenate([bc, bc], axis=0))
            chains.append(dict(c=c, rows=rows, bc=bc, cb2=cb2))
        for ch in chains:
            rows = ch["rows"]
            xc = xc_scr[rows, :]
            dts = _softplus(misc_ref[rows, :] + dtb)
            adt = dts * neg_a
            ch["x2m"] = _blockdiag(hms[0], xc)
            ch["g2"], ch["xw"] = [], []
            for d in range(2):
                hm = hms[d]
                dt_h = [_colsel(dts, d * SSD_HEADS + 2 * p + hh) for hh in range(2)]
                a_h = [_colsel(adt, d * SSD_HEADS + 2 * p + hh) for hh in range(2)]
                cum0, cum1, cum_rows = _half_cum(hm, a_h[0], a_h[1])
                seg2 = jnp.exp(jnp.where(hm["incl"], _pack(hm, cum0, cum1) - cum_rows, NEG_INF))
                dt_rows = _rows(hm, _pack(hm, dt_h[0], dt_h[1]))
                ch["g2"].append((ch["cb2"] * seg2 * dt_rows).astype(BF16))
                al = [jnp.sum(a, axis=0, keepdims=True) for a in a_h]
                wgt = [jnp.exp(al[hh] - cum) * dt_h[hh] for hh, cum in enumerate((cum0, cum1))]
                ch["xw"].append((xc * _pack(hm, wgt[0], wgt[1])).astype(BF16))
                ea_scr[d, rows, :] = _pack(hm, jnp.exp(cum0), jnp.exp(cum1))
                eal_scr[d, ch["c"]] = jnp.where(hi_row, jnp.exp(al[1]), jnp.exp(al[0]))
        for ch in chains:
            for d in range(2):
                yi_scr[d, ch["rows"], :] = _dot(ch["g2"][d], ch["x2m"])
                upd_scr[d, ch["c"]] = _dot_tn(ch["bc"], ch["xw"][d])

    def step_block(chunks):
        states = [hs_scr[j] for j in range(2 * NS)]
        pairs = []
        for c in chunks:
            for s in range(NS):
                pair = []
                for d in range(2):
                    cc = s * nc + (c if d == 0 else nc - 1 - c)
                    pair.append(dict(rows=_chunk_rows(cc), hs=states[2 * s + d].astype(BF16)))
                    states[2 * s + d] = eal_scr[d, cc] * states[2 * s + d] + upd_scr[d, cc]
                pairs.append(pair)
        for j in range(2 * NS):
            hs_scr[j] = states[j]
        for pair in pairs:
            lhs = jnp.concatenate([cc_scr[pair[d]["rows"], :].astype(BF16) for d in range(2)], axis=0)
            rhs = jnp.concatenate([pair[d]["hs"] for d in range(2)], axis=1)
            pair.append(_dot(lhs, rhs))
        for pair in pairs:
            for d, dst in enumerate((y_ref, yb_scr)):
                rows = pair[d]["rows"]
                ch = pair[2][d * CHUNK:(d + 1) * CHUNK, d * LANES:(d + 1) * LANES]
                dst[rows, :] = yi_scr[d, rows, :] + ea_scr[d, rows, :] * ch

    _block_loop(NS * nc, pre_block, PRE_CHUNKS)
    _block_loop(nc, step_block, STEP_CHUNKS)

    y_ref[...] = y_ref[...] + yb_scr[...] + dskip_ref[...] * xc_scr[...]
    if emit_state:
        for s in range(NS):
            for d in range(2):
                hout_ref[s, d] = hs_scr[2 * s + d].T.reshape(2, SSD_P, SSD_N)


def _lane_row(v):
    return jnp.pad(v.astype(F32), (0, LANES - v.shape[0])).reshape(1, LANES)


def _ssd(xbc, misc, conv_w, conv_b, dt_bias, a_log, d_skip, *, prompt, init=None):
    T, B, NS, blk0 = _seq_geometry(prompt)
    R = NS * T
    nc = R // CHUNK
    has_init = init is not None
    n_pairs = SSD_HEADS // 2
    pairs_per_group = n_pairs // SSD_GROUPS
    xb = SSD_HEADS * SSD_P // LANES
    cb = xb + SSD_GROUPS * SSD_N // LANES
    colx = lambda b, p: (blk0 + b, p)
    colb = lambda b, p: (blk0 + b, xb + p // pairs_per_group)
    colc = lambda b, p: (blk0 + b, cb + p // pairs_per_group)
    wsel = lambda f: (lambda b, p: (0, f(b, p)[1]))
    row128 = pl.BlockSpec((1, LANES), lambda b, p: (0, 0))
    in_specs = [pl.BlockSpec((R, LANES), colx), pl.BlockSpec((R, LANES), colb), pl.BlockSpec((R, LANES), colc),
                pl.BlockSpec((CONV_W, LANES), wsel(colx)), pl.BlockSpec((CONV_W, LANES), wsel(colb)),
                pl.BlockSpec((CONV_W, LANES), wsel(colc)),
                pl.BlockSpec((1, LANES), wsel(colx)), pl.BlockSpec((1, LANES), wsel(colb)),
                pl.BlockSpec((1, LANES), wsel(colc)),
                pl.BlockSpec((R, LANES), lambda b, p: (blk0 + b, 0)), row128, row128,
                pl.BlockSpec((1, LANES), lambda b, p: (0, p))]
    cb2 = conv_b.reshape(1, -1)
    args = [xbc, xbc, xbc, conv_w, conv_w, conv_w, cb2, cb2, cb2, misc,
            _lane_row(dt_bias.reshape(-1)), _lane_row(a_log.reshape(-1)),
            jnp.repeat(d_skip, SSD_P).reshape(1, SSD_HEADS * SSD_P)]
    state_spec = pl.BlockSpec((NS, 2, 2, SSD_P, SSD_N), lambda b, p: (b, 0, p, 0, 0))
    if has_init:
        in_specs.append(state_spec)
        args.append(init)
    out_specs = [pl.BlockSpec((R, LANES), lambda b, p: (b, p))]
    out_shape = [jax.ShapeDtypeStruct((B * T, SSD_HEADS * SSD_P), F32)]
    if prompt:
        out_specs.append(state_spec)
        out_shape.append(jax.ShapeDtypeStruct((B, 2, SSD_HEADS, SSD_P, SSD_N), F32))
    return pl.pallas_call(
        functools.partial(_ssd_kernel, T=T, NS=NS, has_init=has_init, emit_state=prompt),
        grid=(B // NS, n_pairs),
        in_specs=in_specs,
        out_specs=out_specs,
        out_shape=out_shape,
        scratch_shapes=[pltpu.VMEM((T + 2 * CONV_HALO, LANES), F32)]
        + [pltpu.VMEM((R, LANES), F32) for _ in range(4)]
        + [pltpu.VMEM((2 * NS, SSD_N, 2 * SSD_P), F32),
           pltpu.VMEM((2, R, LANES), F32),
           pltpu.VMEM((2, R, LANES), F32),
           pltpu.VMEM((2, nc, SSD_N, 2 * SSD_P), F32),
           pltpu.VMEM((2, nc, 1, LANES), F32)],
        compiler_params=_cparams("arbitrary", "arbitrary"),
        name="ssd_prompt" if prompt else "ssd_sample",
    )(*args)


def _tri_inverse(hm, nmat, eye_f):
    levels = int(np.log2(CHUNK))

    def off(level):
        same_big = lax.shift_right_logical(hm["t"], level) == lax.shift_right_logical(hm["s"], level)
        same_small = lax.shift_right_logical(hm["t"], level - 1) == lax.shift_right_logical(hm["s"], level - 1)
        return jnp.where(same_big & jnp.logical_not(same_small), nmat, 0.0)

    state = dict(dinv=eye_f - off(1))

    def first(level):
        def run():
            state["t1"] = _dot(state["dinv"].astype(BF16), _blockdiag(hm, off(level)))
        return run

    def second():
        state["dinv"] = state["dinv"] - _dot(state["t1"].astype(BF16), _blockdiag(hm, state["dinv"]))

    stages = []
    for level in range(2, levels + 1):
        stages += [first(level), second]
    return state, stages


def _gdn_kernel(*refs, T, NS, has_init, emit_state):
    it = iter(refs)
    q_ref, k_ref, v_ref = next(it), next(it), next(it)
    wq_ref, wk_ref, wv_ref = next(it), next(it), next(it)
    bq_ref, bk_ref, bv_ref = next(it), next(it), next(it)
    misc_ref, dtb_ref, alog_ref, z_ref, gn_ref = (next(it) for _ in range(5))
    s0_ref = next(it) if has_init else None
    o_ref = next(it)
    sout_ref = next(it) if emit_state else None
    (pad_scr, qc_scr, kc_scr, vc_scr, ob_scr, s_scr,
     u0_scr, wq_scr, kcf_scr, p2_scr, egl_scr) = (next(it) for _ in range(11))

    h = pl.program_id(1)
    nc = T // CHUNK

    _conv_silu(q_ref, qc_scr, pad_scr, wq_ref, bq_ref, T, NS)
    _conv_silu(k_ref, kc_scr, pad_scr, wk_ref, bk_ref, T, NS)
    _conv_silu(v_ref, vc_scr, pad_scr, wv_ref, bv_ref, T, NS)
    for r0 in range(0, NS * T, CONV_ROWS):
        rows = pl.ds(r0, CONV_ROWS)
        q = qc_scr[rows, :]
        qc_scr[rows, :] = q * lax.rsqrt(jnp.sum(q * q, axis=-1, keepdims=True) + EPS) * (GDN_DK ** -0.5)
        k = kc_scr[rows, :]
        kc_scr[rows, :] = k * lax.rsqrt(jnp.sum(k * k, axis=-1, keepdims=True) + EPS)

    for s in range(NS):
        for d in range(2):
            if has_init:
                s_scr[2 * s + d] = s0_ref[s, d]
            else:
                s_scr[2 * s + d] = jnp.zeros((GDN_DK, GDN_DV), F32)

    dtb = dtb_ref[...]
    neg_a = -jnp.exp(alog_ref[...])
    hm = _half_masks(False, True)
    eye_f = hm["eye"].astype(F32)

    def pre_gates(c):
        rows = _chunk_rows(c)
        kc = kc_scr[rows, :]
        qc = qc_scr[rows, :]
        mi = misc_ref[rows, :]
        gdec = neg_a * _softplus(mi + dtb)
        g_src = [_colsel(gdec, 2 * SSD_HEADS + d * GDN_HEADS + h) for d in range(2)]
        beta = [jax.nn.sigmoid(_colsel(mi, 2 * SSD_HEADS + 2 * GDN_HEADS + d * GDN_HEADS + h)) for d in range(2)]
        gc = [None, None]
        gc[0], gc[1], gc_rows = _half_cum(hm, g_src[0], g_src[1])
        diff = _pack(hm, gc[0], gc[1]) - gc_rows
        beta_rows = _rows(hm, _pack(hm, beta[0], beta[1]))
        kb = kc.astype(BF16)
        ek = []
        for d in range(2):
            egc = jnp.exp(gc[d])
            gl = jnp.sum(g_src[d], axis=0, keepdims=True)
            ek.append((egc * kc).astype(BF16))
            wq_scr[d, c, CHUNK:, :] = (egc * qc).astype(BF16)
            kcf_scr[d, rows, :] = (jnp.exp(gl - gc[d]) * beta[d] * kc).astype(BF16)
            egl_scr[d, c] = _lanes(jnp.exp(gl))
        return dict(c=c, rows=rows, kb=kb, qb=qc.astype(BF16), ek=ek,
                    dec_s=jnp.exp(jnp.where(hm["strict"], diff, NEG_INF)) * beta_rows,
                    dec_i=jnp.exp(jnp.where(hm["incl"], diff, NEG_INF)) * beta_rows)

    def pre_block(chunks):
        chains = [pre_gates(c) for c in chunks]
        for ch in chains:
            kb = ch["kb"]
            ch["kq"] = _dot_nt(jnp.concatenate([kb, ch["qb"]], axis=0), jnp.concatenate([kb, kb], axis=0))
        for ch in chains:
            p2_scr[ch["rows"], :] = (ch["dec_i"] * ch["kq"][CHUNK:]).astype(BF16)
            ch["inv"], ch["stages"] = _tri_inverse(hm, ch["dec_s"] * ch["kq"][:CHUNK], eye_f)
        for i in range(len(chains[0]["stages"])):
            for ch in chains:
                ch["stages"][i]()
        for ch in chains:
            vb = vc_scr[ch["rows"], :].astype(BF16)
            zero = jnp.zeros_like(vb)
            rhs = jnp.concatenate([jnp.concatenate([vb, ch["ek"][0], zero, zero], axis=1),
                                   jnp.concatenate([zero, zero, vb, ch["ek"][1]], axis=1)], axis=0)
            uw = _dot(ch["inv"]["dinv"].astype(BF16), rhs)
            for d in range(2):
                u0_scr[d, ch["rows"], :] = uw[:, 2 * d * LANES:(2 * d + 1) * LANES]
                wq_scr[d, ch["c"], :CHUNK, :] = uw[:, (2 * d + 1) * LANES:(2 * d + 2) * LANES].astype(BF16)

    def step_block(chunks):
        states = [s_scr[j] for j in range(2 * NS)]
        for c in chunks:
            work = []
            for s in range(NS):
                cs = [s * nc + c, s * nc + nc - 1 - c]
                lhs = jnp.concatenate([wq_scr[d, cs[d]] for d in range(2)], axis=0)
                rhs = jnp.concatenate([states[2 * s + d].astype(BF16) for d in range(2)], axis=1)
                work.append(dict(s=s, cs=cs, rows=[_chunk_rows(cc) for cc in cs], a=_dot(lhs, rhs)))
            for w in work:
                rows, ub, w["qs"] = w["rows"], [], []
                for d in range(2):
                    blk = w["a"][2 * d * CHUNK:(2 * d + 2) * CHUNK, d * LANES:(d + 1) * LANES]
                    ub.append((u0_scr[d, rows[d], :] - blk[:CHUNK]).astype(BF16))
                    w["qs"].append(blk[CHUNK:])
                zero = jnp.zeros_like(ub[0])
                w["pu"] = _dot(jnp.concatenate([p2_scr[rows[d], :] for d in range(2)], axis=0),
                               jnp.concatenate([jnp.concatenate([ub[0], zero], axis=1),
                                                jnp.concatenate([zero, ub[1]], axis=1)], axis=0))
                w["ktu"] = _dot_tn(jnp.concatenate([kcf_scr[d, rows[d], :] for d in range(2)], axis=1),
                                   jnp.concatenate(ub, axis=1))
            for w in work:
                for d, dst in enumerate((o_ref, ob_scr)):
                    j = 2 * w["s"] + d
                    dst[w["rows"][d], :] = w["qs"][d] + w["pu"][d * CHUNK:(d + 1) * CHUNK, d * LANES:(d + 1) * LANES]
                    states[j] = (egl_scr[d, w["cs"][d]] * states[j]
                                 + w["ktu"][d * GDN_DK:(d + 1) * GDN_DK, d * GDN_DV:(d + 1) * GDN_DV])
        for j in range(2 * NS):
            s_scr[j] = states[j]

    _block_loop(NS * nc, pre_block, PRE_CHUNKS)
    _block_loop(nc, step_block, STEP_CHUNKS)

    for r0 in range(0, NS * T, CONV_ROWS):
        rows = pl.ds(r0, CONV_ROWS)
        og = o_ref[rows, :] + ob_scr[rows, :]
        o_ref[rows, :] = _rms(og, gn_ref[...]) * _silu(z_ref[rows, :])
    if emit_state:
        for s in range(NS):
            for d in range(2):
                sout_ref[s, d] = s_scr[2 * s + d]


def _gdn(qkv, misc, zg, conv_w, conv_b, dt_row, alog_row, gnorm, *, prompt, init=None):
    T, B, NS, blk0 = _seq_geometry(prompt, GDN_PROMPT_SEQS_PER_STEP, GDN_SAMPLE_SEQS_PER_STEP)
    R = NS * T
    nc = R // CHUNK
    has_init = init is not None
    col = lambda j: (lambda b, h: (blk0 + b, j * GDN_HEADS + h))
    wsel = lambda j: (lambda b, h: (0, j * GDN_HEADS + h))
    row128 = pl.BlockSpec((1, LANES), lambda b, h: (0, 0))
    in_specs = [pl.BlockSpec((R, LANES), col(j)) for j in range(3)]
    in_specs += [pl.BlockSpec((CONV_W, LANES), wsel(j)) for j in range(3)]
    in_specs += [pl.BlockSpec((1, LANES), wsel(j)) for j in range(3)]
    in_specs += [pl.BlockSpec((R, LANES), lambda b, h: (blk0 + b, 0)), row128, row128,
                 pl.BlockSpec((R, LANES), lambda b, h: (blk0 + b, h)), row128]
    cb2 = conv_b.reshape(1, -1)
    args = [qkv, qkv, qkv, conv_w, conv_w, conv_w, cb2, cb2, cb2, misc, dt_row, alog_row, zg,
            gnorm.reshape(1, GDN_DV)]
    state_spec = pl.BlockSpec((NS, 2, None, GDN_DK, GDN_DV), lambda b, h: (b, 0, h, 0, 0))
    if has_init:
        in_specs.append(state_spec)
        args.append(init)
    out_specs = [pl.BlockSpec((R, LANES), lambda b, h: (b, h))]
    out_shape = [jax.ShapeDtypeStruct((B * T, GDN_HEADS * GDN_DV), F32)]
    if prompt:
        out_specs.append(state_spec)
        out_shape.append(jax.ShapeDtypeStruct((B, 2, GDN_HEADS, GDN_DK, GDN_DV), F32))
    return pl.pallas_call(
        functools.partial(_gdn_kernel, T=T, NS=NS, has_init=has_init, emit_state=prompt),
        grid=(B // NS, GDN_HEADS),
        in_specs=in_specs,
        out_specs=out_specs,
        out_shape=out_shape,
        scratch_shapes=[pltpu.VMEM((T + 2 * CONV_HALO, LANES), F32)]
        + [pltpu.VMEM((R, LANES), F32) for _ in range(4)]
        + [pltpu.VMEM((2 * NS, GDN_DK, GDN_DV), F32),
           pltpu.VMEM((2, R, GDN_DV), F32),
           pltpu.VMEM((2, nc, 2 * CHUNK, GDN_DK), BF16),
           pltpu.VMEM((2, R, GDN_DK), BF16),
           pltpu.VMEM((R, LANES), BF16),
           pltpu.VMEM((2, nc, 1, LANES), F32)],
        compiler_params=_cparams("arbitrary", "arbitrary"),
        name="gdn_prompt" if prompt else "gdn_sample",
    )(*args)


def _route_tile(x, g_ref, mod_ref, whi_ref, wlo_ref, b_ref, h_ref, ri_ref, w_ref, cnt_ref, count_scr):
    @pl.when(pl.program_id(0) == 0)
    def _():
        count_scr[...] = jnp.zeros_like(count_scr)

    mod = mod_ref[...]
    h = _rms(x, g_ref[...]) * (1.0 + mod[:, D_MODEL:2 * D_MODEL]) + mod[:, :D_MODEL]
    hb = h.astype(BF16)
    h_ref[...] = h
    hl = (h - hb.astype(F32)).astype(BF16)
    whi = whi_ref[...]
    logit = _dot(hb, whi) + (_dot(hl, whi) + _dot(hb, wlo_ref[...])) + b_ref[...]
    lane = lax.broadcasted_iota(jnp.int32, logit.shape, 1)
    is_group = (lane >= MOE_EXPERTS) & (lane < MOE_EXPERTS + MOE_GROUPS)
    glog = jnp.where(is_group, logit, NEG_INF)
    gmax = jnp.max(glog, axis=1, keepdims=True)
    gsel = jnp.min(jnp.where(glog == gmax, lane, LANES), axis=1, keepdims=True) - MOE_EXPERTS
    gw = 1.0 / jnp.sum(jnp.exp(glog - gmax), axis=1, keepdims=True)
    lo = gsel * MOE_PER_GROUP
    in_group = (lane >= lo) & (lane < lo + MOE_PER_GROUP)
    elog = jnp.where(in_group, logit, NEG_INF)
    v1 = jnp.max(elog, axis=1, keepdims=True)
    i1 = jnp.min(jnp.where(elog == v1, lane, LANES), axis=1, keepdims=True)
    elog2 = jnp.where(lane == i1, NEG_INF, elog)
    v2 = jnp.max(elog2, axis=1, keepdims=True)
    i2 = jnp.min(jnp.where(elog2 == v2, lane, LANES), axis=1, keepdims=True)
    e2 = jnp.exp(v2 - v1)
    w1 = gw / (1.0 + e2)
    w2 = gw * e2 / (1.0 + e2)
    w_ref[...] = jnp.where(lane == 0, w1, jnp.where(lane == 1, w2, 0.0))
    tm = logit.shape[0]
    onehot = jnp.where((lane == i1) | (lane == i2), 1.0, 0.0)
    t_i = lax.broadcasted_iota(jnp.int32, (tm, tm), 0)
    s_i = lax.broadcasted_iota(jnp.int32, (tm, tm), 1)
    earlier = jnp.where(s_i < t_i, 1.0, 0.0).astype(BF16)
    before = _dot(earlier, onehot.astype(BF16)) + count_scr[...]
    r1 = jnp.sum(jnp.where(lane == i1, before, 0.0), axis=1, keepdims=True).astype(jnp.int32)
    r2 = jnp.sum(jnp.where(lane == i2, before, 0.0), axis=1, keepdims=True).astype(jnp.int32)
    ri_ref[...] = jnp.where(lane == 0, i1, jnp.where(lane == 1, i2, jnp.where(lane == 2, r1, jnp.where(lane == 3, r2, 0))))
    count_scr[...] += jnp.sum(onehot, axis=0, keepdims=True)
    cnt_ref[...] = count_scr[...]


N_ASSIGN = 2 * N_TOK
MOE_TILE = 512
MOE_TILES = N_ASSIGN // MOE_TILE
MOE_SEGMENTS = MOE_TILES + MOE_EXPERTS - 1
DISPATCH_ROWS = 1024
COMBINE_ROWS = 256


def _routing_tables(ri, cnt):
    counts = cnt[0, :MOE_EXPERTS].astype(jnp.int32)
    cum_end = jnp.cumsum(counts)
    cum_start = cum_end - counts
    where = (ri[:, 0:2].reshape(N_ASSIGN), ri[:, 2:4].reshape(N_ASSIGN), cum_start)
    tile_start = jnp.arange(MOE_TILES, dtype=jnp.int32) * MOE_TILE
    expert_start = cum_start[1:]
    tile_rank = jnp.arange(MOE_TILES, dtype=jnp.int32) + jnp.sum(
        (expert_start[None, :] < tile_start[:, None]).astype(jnp.int32), axis=1)
    expert_rank = jnp.arange(MOE_EXPERTS - 1, dtype=jnp.int32) + jnp.minimum(expert_start // MOE_TILE + 1, MOE_TILES)
    values = jnp.concatenate([tile_start, expert_start])
    ranks = jnp.concatenate([tile_rank, expert_rank])
    slot = jnp.arange(MOE_SEGMENTS, dtype=jnp.int32)
    cuts = jnp.sum(jnp.where(ranks[None, :] == slot[:, None], values[None, :], 0), axis=1)
    seg_lo = cuts
    seg_hi = jnp.concatenate([cuts[1:], jnp.full((1,), N_ASSIGN, jnp.int32)])
    seg_tile = jnp.minimum(seg_lo // MOE_TILE, MOE_TILES - 1)
    seg_expert = jnp.minimum(jnp.sum((cum_end[None, :] <= seg_lo[:, None]).astype(jnp.int32), axis=1),
                             MOE_EXPERTS - 1)
    seg_first = jnp.concatenate([jnp.ones((1,), jnp.int32), (seg_tile[1:] != seg_tile[:-1]).astype(jnp.int32)])
    return where, (seg_tile, seg_expert, seg_lo, seg_hi, seg_first)


def _row_copy(src_ref, src_row, dst_ref, dst_row, sem):
    return pltpu.make_async_copy(src_ref.at[pl.ds(src_row, 1), :], dst_ref.at[pl.ds(dst_row, 1), :], sem)


def _dispatch_kernel(expert_ref, rank_ref, start_ref, h_ref, xs_ref, sem):
    base = pl.program_id(0) * DISPATCH_ROWS

    def start(t, carry):
        for k in range(2):
            a = 2 * (base + t) + k
            _row_copy(h_ref, t, xs_ref, start_ref[expert_ref[a]] + rank_ref[a], sem).start()
        return carry

    def wait(t, carry):
        for k in range(2):
            _row_copy(h_ref, 0, xs_ref, 0, sem).wait()
        return carry

    lax.fori_loop(0, DISPATCH_ROWS, start, 0, unroll=4)
    lax.fori_loop(0, DISPATCH_ROWS, wait, 0, unroll=4)


def _dispatch(where, h):
    tm = DISPATCH_ROWS
    return pl.pallas_call(
        _dispatch_kernel,
        grid_spec=pltpu.PrefetchScalarGridSpec(
            num_scalar_prefetch=3,
            grid=(N_TOK // tm,),
            in_specs=[pl.BlockSpec((tm, D_MODEL), lambda i, e, r, s: (i, 0))],
            out_specs=pl.BlockSpec(memory_space=pl.ANY),
            scratch_shapes=[pltpu.SemaphoreType.DMA(())]),
        out_shape=jax.ShapeDtypeStruct((N_ASSIGN, D_MODEL), F32),
        compiler_params=_cparams("arbitrary"),
        name="moe_dispatch",
    )(*where, h)


def _experts_kernel(tile_ref, expert_ref, lo_ref, hi_ref, first_ref, xs_ref, wg_ref, wu_ref, wd_ref, o_ref):
    p = pl.program_id(0)
    lo, hi = lo_ref[p], hi_ref[p]

    @pl.when(first_ref[p] == 1)
    def _():
        o_ref[...] = jnp.zeros_like(o_ref)

    @pl.when(hi > lo)
    def _():
        x = xs_ref[...].astype(BF16)
        hg = _dot(x, wg_ref[...].astype(BF16))
        hu = _dot(x, wu_ref[...].astype(BF16))
        row = tile_ref[p] * MOE_TILE + lax.broadcasted_iota(jnp.int32, (MOE_TILE, 1), 0)
        act = jnp.where((row >= lo) & (row < hi), _silu(hg) * hu, 0.0)
        o_ref[...] += _bdot(act, wd_ref[...])


def _experts(tables, xs, wg, wu, wd, layer):
    weight = lambda shape: pl.BlockSpec((None, None) + shape, lambda p, tile, expert, lo, hi, first: (layer, expert[p], 0, 0))
    rows = pl.BlockSpec((MOE_TILE, D_MODEL), lambda p, tile, expert, lo, hi, first: (tile[p], 0))
    return pl.pallas_call(
        _experts_kernel,
        grid_spec=pltpu.PrefetchScalarGridSpec(
            num_scalar_prefetch=5,
            grid=(MOE_SEGMENTS,),
            in_specs=[rows, weight((D_MODEL, MOE_FF)), weight((D_MODEL, MOE_FF)), weight((MOE_FF, D_MODEL))],
            out_specs=rows),
        out_shape=jax.ShapeDtypeStruct((N_ASSIGN, D_MODEL), F32),
        compiler_params=_cparams("arbitrary"),
        name="moe_experts",
    )(*tables, xs, wg, wu, wd)


def _combine_kernel(expert_ref, rank_ref, start_ref, ys_ref, w_ref, x_ref, mod_ref, gf_ref, o_ref, buf, sem, *,
                    final, tile0, n_tiles):
    i = pl.program_id(0)
    slot = i % 2

    def gather(tile, into):
        base = (tile0 + tile) * COMBINE_ROWS

        def start(t, carry):
            for k in range(2):
                a = 2 * (base + t) + k
                _row_copy(ys_ref, start_ref[expert_ref[a]] + rank_ref[a], buf.at[into, k], t, sem.at[into]).start()
            return carry

        lax.fori_loop(0, COMBINE_ROWS, start, 0, unroll=4)

    @pl.when(i == 0)
    def _():
        gather(0, 0)

    @pl.when(i + 1 < n_tiles)
    def _():
        gather(i + 1, 1 - slot)

    def wait(t, carry):
        for k in range(2):
            _row_copy(ys_ref, 0, buf.at[slot, k], 0, sem.at[slot]).wait()
        return carry

    lax.fori_loop(0, COMBINE_ROWS, wait, 0, unroll=4)
    w = w_ref[...]
    y = x_ref[...] + mod_ref[...] * (w[:, 0:1] * buf[slot, 0] + w[:, 1:2] * buf[slot, 1])
    if final:
        y = _rms(y, gf_ref[...])
    o_ref[...] = y


def _combine(where, ys, wts, x, mod4, layer, norm_final, final, tile0, n_tiles):
    tm = COMBINE_ROWS
    return pl.pallas_call(
        functools.partial(_combine_kernel, final=final, tile0=tile0, n_tiles=n_tiles),
        grid_spec=pltpu.PrefetchScalarGridSpec(
            num_scalar_prefetch=3,
            grid=(n_tiles,),
            in_specs=[pl.BlockSpec(memory_space=pl.ANY),
                      pl.BlockSpec((tm, LANES), lambda i, e, r, s: (tile0 + i, 0)),
                      pl.BlockSpec((tm, D_MODEL), lambda i, e, r, s: (tile0 + i, 0)),
                      pl.BlockSpec((None, None, 1, D_MODEL),
                                   lambda i, e, r, s: (layer, _mod_index(tile0 + i, tm), 0, 5)),
                      pl.BlockSpec((1, D_MODEL), lambda i, e, r, s: (0, 0))],
            out_specs=pl.BlockSpec((tm, D_MODEL), lambda i, e, r, s: (i, 0)),
            scratch_shapes=[pltpu.VMEM((2, 2, tm, D_MODEL), F32), pltpu.SemaphoreType.DMA((2,))]),
        out_shape=jax.ShapeDtypeStruct((n_tiles * tm, D_MODEL), F32),
        compiler_params=_cparams("arbitrary"),
        name="moe_combine",
    )(*where, ys, wts, x, mod4, norm_final.reshape(1, D_MODEL))


def _rope_tables():
    t = jnp.arange(DEC_SEQ)
    pos = jnp.stack([(t // GRID_W).astype(F32), (t % GRID_W).astype(F32)], axis=1)
    nf = MLA_ROPE // 4
    inv = ROPE_BASE ** (-jnp.arange(nf, dtype=F32) / nf)
    j = jnp.arange(MLA_ROPE)
    ang = pos[:, j // (2 * nf)] * inv[j % nf][None, :]
    sign = jnp.where((j % (2 * nf)) < nf, -1.0, 1.0)
    cos = jnp.pad(jnp.cos(ang), ((0, 0), (MLA_NOPE, LANES - MLA_NOPE - MLA_ROPE)), constant_values=1.0)
    sin = jnp.pad(jnp.sin(ang) * sign, ((0, 0), (MLA_NOPE, LANES - MLA_NOPE - MLA_ROPE)))
    cos = jnp.concatenate([jnp.ones((ROW_TILE, LANES), F32), cos], axis=0)
    sin = jnp.concatenate([jnp.zeros((ROW_TILE, LANES), F32), sin], axis=0)
    return cos, sin


def _pad_heads(w, n_heads, width, lo, hi):
    k = w.shape[0]
    w = w.reshape(k, n_heads, width)[:, :, lo:hi]
    w = jnp.pad(w, ((0, 0), (0, 0), (0, LANES - (hi - lo))))
    return w.reshape(k, n_heads * LANES)


def kernel(x_prompt, x_sample, c, cache_mla_kv, cache_mla_krope, state_mlstm_C, state_mlstm_n, state_mlstm_m, state_ssd, state_gdn, c_ctx, ada_w, ada_b, norm_mix, norm_ffn, w_in_even, ml_i_bias, ml_f_bias, ml_norm, mla_q_norm, mla_w_uq, mla_kv_norm, mla_w_ukv, w_out_even, w_in_odd, ssd_conv_w, ssd_conv_b, ssd_dt_bias, ssd_A_log, ssd_D, ssd_norm, gdn_conv_w, gdn_conv_b, gdn_dt_bias, gdn_A_log, gdn_norm, w_out_odd, moe_w_group, moe_b_group, moe_w_expert, moe_b_expert, moe_w_gate, moe_w_up, moe_w_down, norm_final):
    x = (x_prompt.reshape(N_PROMPT, D_MODEL), x_sample.reshape(N_SAMPLE, D_MODEL))
    cond = jnp.concatenate([c_ctx[None, :], c, jnp.zeros((N_COND - 1 - DEC_BATCH, D_MODEL), F32)], axis=0)
    mod4 = _ada(cond, ada_w, ada_b).reshape(DEPTH, N_COND, 1, 6 * D_MODEL)
    cos_tab, sin_tab = _rope_tables()

    def route_params(layer):
        w_route = jnp.concatenate([moe_w_expert[layer], moe_w_group[layer]], axis=1)
        w_route = jnp.pad(w_route, ((0, 0), (0, LANES - MOE_EXPERTS - MOE_GROUPS)))
        b_route = _lane_row(jnp.concatenate([moe_b_expert[layer], moe_b_group[layer]]))
        return norm_ffn[layer], w_route, b_route

    def moe_layer(mixed, layer, final):
        x, h, ri, wts, cnt = mixed
        where, tables = _routing_tables(ri, cnt)
        ys = _experts(tables, _dispatch(where, h), moe_w_gate, moe_w_up, moe_w_down, layer)
        combine = functools.partial(_combine, where, ys, wts, x, mod4, layer, norm_final, final)
        if not final:
            return combine(0, N_TOK // COMBINE_ROWS)
        p_tiles = N_PROMPT // COMBINE_ROWS
        return combine(0, p_tiles), combine(p_tiles, N_SAMPLE // COMBINE_ROWS)

    e = 0
    w = w_in_even[e]
    off = np.cumsum([0, 4 * ML_HEADS * ML_DK, 2 * ML_HEADS, 2 * ML_HEADS, MLA_Q_RANK, MLA_KV_RANK, MLA_ROPE])
    w_misc = jnp.concatenate([w[:, off[1]:off[3]],
                              jnp.zeros((D_MODEL, MLA_NOPE - 4 * ML_HEADS), F32),
                              w[:, off[5]:off[6]],
                              jnp.zeros((D_MODEL, LANES - MLA_NOPE - MLA_ROPE), F32)], axis=1)
    weights = [w[:, :off[1]].astype(BF16), w[:, off[3]:off[4]].astype(BF16),
               w[:, off[4]:off[5]].astype(BF16), w_misc.astype(BF16)]
    dq = MLA_NOPE + MLA_ROPE
    w_uq_pad = _pad_heads(mla_w_uq[e], MLA_HEADS, dq, 0, dq).astype(BF16)
    w_uk_pad = _pad_heads(mla_w_ukv[e], MLA_HEADS, MLA_NOPE + MLA_V, 0, MLA_NOPE).astype(BF16)
    w_uv = mla_w_ukv[e].reshape(MLA_KV_RANK, MLA_HEADS, MLA_NOPE + MLA_V)[:, :, MLA_NOPE:]
    w_uv = w_uv.reshape(MLA_KV_RANK, MLA_HEADS * MLA_V).astype(BF16)
    qkvo, misc0, q_cat, ckv_n, k_cat, v_all = _inproj(
        x, norm_mix[0], mod4, 0, weights,
        mla=(mla_q_norm[e], w_uq_pad, mla_kv_norm[e], w_uk_pad, w_uv, cos_tab, sin_tab))

    hm_p, st_c, st_n, st_m = _mlstm(qkvo, misc0, ml_i_bias[e], ml_f_bias[e], ml_norm[e], prompt=True)
    (hm_s,) = _mlstm(qkvo, misc0, ml_i_bias[e], ml_f_bias[e], ml_norm[e], prompt=False,
                     init=(state_mlstm_C[:, e], state_mlstm_n[:, e], state_mlstm_m[:, e]))

    cache_kpe = jnp.pad(cache_mla_krope[:, e].reshape(DEC_BATCH * PAST_LEN, MLA_ROPE),
                        ((0, 0), (MLA_NOPE, LANES - MLA_NOPE - MLA_ROPE)))
    k_cache, v_cache = _mla_kv_cache(cache_mla_kv[:, e].reshape(DEC_BATCH * PAST_LEN, MLA_KV_RANK), cache_kpe,
                                     w_uk_pad, w_uv)
    att_p = _attention(q_cat, k_cat, v_all, B=BATCH, Tq=SEQ, Tk=SEQ, tq=SEQ, q_row0=0, heads=MLA_HEADS)

    def with_cache(cache, new):
        width = new.shape[1]
        both = jnp.concatenate([cache.reshape(DEC_BATCH, PAST_LEN, width),
                                new[N_PROMPT:].reshape(DEC_BATCH, DEC_SEQ, width)], axis=1)
        return both.reshape(DEC_BATCH * (PAST_LEN + DEC_SEQ), width)

    att_s = _attention(q_cat, with_cache(k_cache, k_cat), with_cache(v_cache, v_all), B=DEC_BATCH, Tq=DEC_SEQ,
                       Tk=PAST_LEN + DEC_SEQ, tq=256, q_row0=N_PROMPT, heads=2)
    wo = w_out_even[e].astype(BF16)
    mixed = _outproj((hm_p, hm_s), (att_p, att_s), wo[:ML_HEADS * ML_DV], wo[ML_HEADS * ML_DV:], x, mod4, 0,
                     *route_params(0))
    x = moe_layer(mixed, 0, final=False)

    oi = 0
    w = w_in_odd[oi]
    ssd_w = SSD_HEADS * SSD_P
    ssd_cc = ssd_w + 2 * SSD_GROUPS * SSD_N
    gdn_w = GDN_HEADS * GDN_DK
    off = np.cumsum([0, ssd_w, ssd_cc, 2 * SSD_HEADS, 3 * gdn_w, gdn_w, 2 * GDN_HEADS, 2 * GDN_HEADS])
    w_misc = jnp.concatenate([w[:, off[2]:off[3]], w[:, off[5]:off[7]],
                              jnp.zeros((D_MODEL, LANES - 2 * SSD_HEADS - 4 * GDN_HEADS), F32)], axis=1)
    weights = [w[:, off[0]:off[1]].astype(BF16), w[:, off[1]:off[2]].astype(BF16),
               w[:, off[3]:off[4]].astype(BF16), w[:, off[4]:off[5]].astype(BF16), w_misc.astype(BF16)]
    z_s, xbc, qkv_g, z_g, misc = _inproj(x, norm_mix[1], mod4, 1, weights)

    ssd_args = (xbc, misc, ssd_conv_w[oi], ssd_conv_b[oi], ssd_dt_bias[oi], ssd_A_log[oi], ssd_D[oi])
    ys_p, st_ssd = _ssd(*ssd_args, prompt=True)
    (ys_s,) = _ssd(*ssd_args, prompt=False, init=state_ssd[:, oi])

    lo = 2 * SSD_HEADS
    gdn_dt_row = jnp.pad(gdn_dt_bias[oi].reshape(-1), (lo, LANES - lo - 2 * GDN_HEADS)).reshape(1, LANES)
    gdn_alog_row = jnp.pad(gdn_A_log[oi].reshape(-1), (lo, LANES - lo - 2 * GDN_HEADS)).reshape(1, LANES)
    gdn_args = (qkv_g, misc, z_g, gdn_conv_w[oi], gdn_conv_b[oi], gdn_dt_row, gdn_alog_row, gdn_norm[oi])
    og_p, st_gdn = _gdn(*gdn_args, prompt=True)
    (og_s,) = _gdn(*gdn_args, prompt=False, init=state_gdn[:, oi])

    wo = w_out_odd[oi].astype(BF16)
    mixed = _outproj((ys_p, ys_s), (og_p, og_s), wo[:ssd_w], wo[ssd_w:], x, mod4, 1, *route_params(1),
                     z=z_s, gnorm=ssd_norm[oi])
    y_p, y_s = moe_layer(mixed, 1, final=True)

    y_prompt = y_p.reshape(BATCH, SEQ, D_MODEL)
    y_sample = y_s.reshape(DEC_BATCH, DEC_SEQ, D_MODEL)
    new_mla_kv = ckv_n[:N_PROMPT].reshape(BATCH, 1, SEQ, MLA_KV_RANK)
    new_mla_krope = misc0[:N_PROMPT, MLA_NOPE:MLA_NOPE + MLA_ROPE].reshape(BATCH, 1, SEQ, MLA_ROPE)
    return (y_prompt, y_sample, new_mla_kv, new_mla_krope, st_c[:, None], st_n.reshape(BATCH, 1, 2, ML_HEADS, ML_DK),
            st_m.reshape(BATCH, 1, 2, ML_HEADS), st_ssd[:, None], st_gdn[:, None])
```

```python
import functools

import numpy as np
import jax
import jax.numpy as jnp
from jax import lax
from jax.experimental import pallas as pl
from jax.experimental.pallas import tpu as pltpu

F32 = jnp.float32
BF16 = jnp.bfloat16

D_MODEL = 1024
BATCH = 32
SEQ = 256
DEPTH = 2
DEC_BATCH = 2
DEC_SEQ = 2048
PAST_LEN = 256
GRID_W = 64
EPS = 1e-6
ML_HEADS = 4
ML_DK = 128
ML_DV = 128
MLA_HEADS = 8
MLA_Q_RANK = 384
MLA_KV_RANK = 256
MLA_NOPE = 64
MLA_ROPE = 32
MLA_V = 64
ROPE_BASE = 10000.0
SSD_HEADS = 8
SSD_P = 64
SSD_GROUPS = 2
SSD_N = 128
GDN_HEADS = 4
GDN_DK = 128
GDN_DV = 128
CONV_W = 5
MOE_GROUPS = 4
MOE_PER_GROUP = 8
MOE_EXPERTS = 32
MOE_FF = 256

N_PROMPT = BATCH * SEQ
N_SAMPLE = DEC_BATCH * DEC_SEQ
N_TOK = N_PROMPT + N_SAMPLE
N_COND = 8

LANES = 128
CHUNK = 64
ROW_TILE = 512
CONV_HALO = 8
CONV_ROWS = 256
VMEM_LIMIT = 56 * 1024 * 1024
PRE_CHUNKS = 8
STEP_CHUNKS = 4
GDN_PROMPT_SEQS_PER_STEP = 8
GDN_SAMPLE_SEQS_PER_STEP = 1

assert LANES == 2 * CHUNK and SSD_P == CHUNK

NEG_INF = float("-inf")


def _cparams(*sem):
    return pltpu.CompilerParams(dimension_semantics=sem, vmem_limit_bytes=VMEM_LIMIT)


def _dot(a, b):
    return jnp.dot(a, b, preferred_element_type=F32)


def _dot_nt(a, b):
    return lax.dot_general(a, b, (((1,), (1,)), ((), ())), preferred_element_type=F32)


def _dot_tn(a, b):
    return lax.dot_general(a, b, (((0,), (0,)), ((), ())), preferred_element_type=F32)


def _bdot(a, b):
    return _dot(a.astype(BF16), b.astype(BF16))


def _rms(x, g):
    return x * lax.rsqrt(jnp.mean(x * x, axis=-1, keepdims=True) + EPS) * g


def _softplus(x):
    return jnp.maximum(x, 0.0) + jnp.log1p(jnp.exp(-jnp.abs(x)))


def _silu(x):
    return x * jax.nn.sigmoid(x)


def _colsel(x, j):
    lane = lax.broadcasted_iota(jnp.int32, x.shape, 1)
    return jnp.sum(jnp.where(lane == j, x, 0.0), axis=1, keepdims=True)


def _lanes(x):
    return jnp.broadcast_to(x, (x.shape[0], LANES))


def _half_masks(rev_lo, rev_hi):
    t = lax.broadcasted_iota(jnp.int32, (CHUNK, LANES), 0)
    lane = lax.broadcasted_iota(jnp.int32, (CHUNK, LANES), 1)
    s = lane & (CHUNK - 1)
    hi = lane >= CHUNK

    def pick(fwd, bwd):
        if rev_lo == rev_hi:
            return bwd if rev_lo else fwd
        on_hi, on_lo = (bwd, fwd) if rev_hi else (fwd, bwd)
        return (hi & on_hi) | (jnp.logical_not(hi) & on_lo)

    return dict(hi=hi, t=t, s=s, eye=(s == t), incl=pick(s <= t, s >= t), incl_t=pick(t <= s, t >= s),
                strict=pick(s < t, s > t))


def _pack(hm, col_lo, col_hi):
    return jnp.where(hm["hi"], col_hi, col_lo)


def _rows(hm, cols):
    return jnp.sum(jnp.where(hm["eye"], cols, 0.0), axis=0, keepdims=True)


def _half_sums(hm, x):
    lo = jnp.sum(jnp.where(hm["hi"], 0.0, x), axis=1, keepdims=True)
    hi = jnp.sum(jnp.where(hm["hi"], x, 0.0), axis=1, keepdims=True)
    return lo, hi


def _half_cum(hm, col_lo, col_hi):
    cols = _pack(hm, col_lo, col_hi)
    cum_lo, cum_hi = _half_sums(hm, jnp.where(hm["incl"], _rows(hm, cols), 0.0))
    cum_rows = jnp.sum(jnp.where(hm["incl_t"], cols, 0.0), axis=0, keepdims=True)
    return cum_lo, cum_hi, cum_rows


def _blockdiag(hm, x):
    return jnp.concatenate([jnp.where(hm["hi"], 0.0, x).astype(BF16),
                            jnp.where(hm["hi"], x, 0.0).astype(BF16)], axis=0)


def _chunk_rows(c):
    if isinstance(c, int):
        return pl.ds(c * CHUNK, CHUNK)
    return pl.ds(pl.multiple_of(c * CHUNK, CHUNK), CHUNK)


def _block_loop(n, body, size):
    if n <= size:
        body(list(range(n)))
        return

    def block(blk, carry):
        body([blk * size + j for j in range(size)])
        return carry

    lax.fori_loop(0, n // size, block, 0)


def _mod_index(i, rows_per_tile):
    p_tiles = N_PROMPT // rows_per_tile
    s_tiles = DEC_SEQ // rows_per_tile
    return jnp.where(i < p_tiles, 0, 1 + (i - p_tiles) // s_tiles)


def _ada_kernel(c_ref, w_ref, b_ref, o_ref):
    c = c_ref[...]
    o_ref[...] = _bdot(_silu(c), w_ref[...]) + b_ref[...]


def _ada(cond, ada_w, ada_b):
    nb = 6
    return pl.pallas_call(
        _ada_kernel,
        grid=(DEPTH, nb),
        in_specs=[pl.BlockSpec((N_COND, D_MODEL), lambda l, j: (0, 0)),
                  pl.BlockSpec((None, D_MODEL, D_MODEL), lambda l, j: (l, 0, j)),
                  pl.BlockSpec((None, 1, D_MODEL), lambda l, j: (l, 0, j))],
        out_specs=pl.BlockSpec((None, N_COND, D_MODEL), lambda l, j: (l, 0, j)),
        out_shape=jax.ShapeDtypeStruct((DEPTH, N_COND, 6 * D_MODEL), F32),
        compiler_params=_cparams("arbitrary", "arbitrary"),
        name="ada",
    )(cond, ada_w, ada_b.reshape(DEPTH, 1, 6 * D_MODEL))


def _token_operand(x, tm):
    if not isinstance(x, tuple):
        return [pl.BlockSpec((tm, x.shape[1]), lambda i: (i, 0))], [x]
    pt = N_PROMPT // tm
    width = x[0].shape[1]
    return ([pl.BlockSpec((tm, width), lambda i: (jnp.minimum(i, pt - 1), 0)),
             pl.BlockSpec((tm, width), lambda i: (jnp.maximum(i - pt, 0), 0))], list(x))


def _take_tile(refs, split, tm):
    if not split:
        return refs.pop(0)[...]
    p_ref, s_ref = refs.pop(0), refs.pop(0)
    return jnp.where(pl.program_id(0) < N_PROMPT // tm, p_ref[...], s_ref[...])


def _inproj_kernel(*refs, n_out, split, tm, mla):
    refs = list(refs)
    x = _take_tile(refs, split, tm)
    g_ref, mod_ref = refs[:2]
    w_refs, rest = refs[2:2 + n_out], refs[2 + n_out:]
    mod = mod_ref[...]
    h = _rms(x, g_ref[...]) * (1.0 + mod[:, D_MODEL:2 * D_MODEL]) + mod[:, :D_MODEL]
    hb = h.astype(BF16)
    ys = [_dot(hb, w_ref[...]) for w_ref in w_refs]
    if not mla:
        for y, o_ref in zip(ys, rest):
            o_ref[...] = y
        return
    (gq_ref, wuq_ref, gkv_ref, wuk_ref, wuv_ref, cos_ref, sin_ref,
     qkvo_ref, misc_ref, q_ref, ckvn_ref, k_ref, v_ref) = rest
    y_qkvo, y_cq, y_ckv, y_misc = ys
    qkvo_ref[...] = y_qkvo
    misc_ref[...] = y_misc
    cos, sin = cos_ref[...], sin_ref[...]
    _q_tile(y_cq, gq_ref, wuq_ref, cos, sin, q_ref)
    c = _rms(y_ckv, gkv_ref[...])
    ckvn_ref[...] = c
    _kv_tile(c, _rope(_rope_lanes(y_misc), cos, sin), wuk_ref, wuv_ref, k_ref, v_ref)


def _inproj(x, gain, mod4, layer, weights, mla=None):
    n_out = len(weights)
    tm = ROW_TILE
    rows = lambda width: pl.BlockSpec((tm, width), lambda i: (i, 0))
    full = lambda a: pl.BlockSpec(a.shape, lambda i: (0, 0))
    in_specs, args = _token_operand(x, tm)
    in_specs += [pl.BlockSpec((1, D_MODEL), lambda i: (0, 0)),
                 pl.BlockSpec((None, None, 1, 2 * D_MODEL), lambda i: (layer, _mod_index(i, tm), 0, 0))]
    in_specs += [full(w) for w in weights]
    args += [gain.reshape(1, D_MODEL), mod4, *weights]
    out_specs = [rows(w.shape[1]) for w in weights]
    out_shape = [jax.ShapeDtypeStruct((N_TOK, w.shape[1]), F32) for w in weights]
    if mla is not None:
        gq, w_uq, gkv, w_uk, w_uv, cos_tab, sin_tab = mla
        tab = pl.BlockSpec((tm, LANES), lambda i: (_rope_block_index(i, tm), 0))
        in_specs += [pl.BlockSpec((1, MLA_Q_RANK), lambda i: (0, 0)), full(w_uq),
                     pl.BlockSpec((1, MLA_KV_RANK), lambda i: (0, 0)), full(w_uk), full(w_uv), tab, tab]
        args += [gq.reshape(1, MLA_Q_RANK), w_uq, gkv.reshape(1, MLA_KV_RANK), w_uk, w_uv, cos_tab, sin_tab]
        widths = [(weights[0].shape[1], F32), (LANES, F32), (MLA_HEADS * LANES, BF16), (MLA_KV_RANK, F32),
                  (MLA_HEADS * LANES, BF16), (MLA_HEADS * MLA_V, BF16)]
        out_specs = [rows(width) for width, _ in widths]
        out_shape = [jax.ShapeDtypeStruct((N_TOK, width), dtype) for width, dtype in widths]
    return pl.pallas_call(
        functools.partial(_inproj_kernel, n_out=n_out, split=isinstance(x, tuple), tm=tm, mla=mla is not None),
        grid=(N_TOK // tm,),
        in_specs=in_specs,
        out_specs=out_specs,
        out_shape=out_shape,
        compiler_params=_cparams("arbitrary"),
        name="inproj",
    )(*args)


def _mlstm_kernel(*refs, T, NS, has_init, emit_state):
    it = iter(refs)
    ib_ref, fb_ref = next(it), next(it)
    m0_ref = next(it) if has_init else None
    q_ref, k_ref, v_ref, o_ref, misc_ref, gn_ref = (next(it) for _ in range(6))
    c0_ref, n0_ref = (next(it), next(it)) if has_init else (None, None)
    hm_ref = next(it)
    cout_ref, nout_ref, mout_ref = (next(it), next(it), next(it)) if emit_state else (None, None, None)
    (hb_scr, c_scr, n_scr, m_scr, num_scr, st_scr, kv_scr, nl_scr, bl_scr, gm_scr) = (next(it) for _ in range(10))

    b = pl.program_id(0)
    h = pl.program_id(1)
    nc = T // CHUNK
    scale = ML_DK ** -0.5

    for s in range(NS):
        for d in range(2):
            if has_init:
                c_scr[2 * s + d] = c0_ref[s, d]
                n_scr[2 * s + d] = n0_ref[s, d]
                m_scr[2 * s + d] = jnp.full((1, LANES), m0_ref[b * NS + s, d, h], F32)
            else:
                c_scr[2 * s + d] = jnp.zeros((ML_DK, ML_DV), F32)
                n_scr[2 * s + d] = jnp.zeros((1, ML_DK), F32)
                m_scr[2 * s + d] = jnp.zeros((1, LANES), F32)

    hm = _half_masks(False, True)

    def pre_gates(c):
        rows = _chunk_rows(c)
        mi = misc_ref[rows, :]
        li = [_colsel(mi, d * ML_HEADS + h) + ib_ref[d, h] for d in range(2)]
        lf = [-_softplus(-(_colsel(mi, 2 * ML_HEADS + d * ML_HEADS + h) + fb_ref[d, h])) for d in range(2)]
        b_cols = [None, None]
        b_cols[0], b_cols[1], b_rows = _half_cum(hm, lf[0], lf[1])
        li_rows = _rows(hm, _pack(hm, li[0], li[1]))
        dm = jnp.where(hm["incl"], _pack(hm, b_cols[0], b_cols[1]) - b_rows + li_rows, NEG_INF)
        mloc = [jnp.max(jnp.where(hm["hi"], NEG_INF, dm), axis=1, keepdims=True),
                jnp.max(jnp.where(hm["hi"], dm, NEG_INF), axis=1, keepdims=True)]
        kc = k_ref[rows, :]
        kws = []
        for d in range(2):
            bl = jnp.sum(lf[d], axis=0, keepdims=True)
            g = bl - b_cols[d] + li[d]
            gmax = jnp.max(g, axis=0, keepdims=True)
            kw = jnp.exp(g - gmax) * kc
            kws.append(kw.astype(BF16))
            nl_scr[d, c] = jnp.sum(kw, axis=0, keepdims=True)
            bl_scr[d, c] = _lanes(bl)
            gm_scr[d, c] = _lanes(gmax)
            st_scr[d, 0, rows, :] = _lanes(mloc[d])
            st_scr[d, 1, rows, :] = _lanes(b_cols[d])
        return dict(c=c, rows=rows, kc=kc.astype(BF16), kw2=jnp.concatenate(kws, axis=1),
                    e2=jnp.exp(dm - _pack(hm, mloc[0], mloc[1])))

    def pre_block(chunks):
        chains = [pre_gates(c) for c in chunks]
        for ch in chains:
            qc = (q_ref[ch["rows"], :] * scale).astype(BF16)
            ch["vb"] = v_ref[ch["rows"], :].astype(BF16)
            ch["qk2"] = _dot_nt(qc, jnp.concatenate([ch["kc"], ch["kc"]], axis=0))
            kv2 = _dot_tn(ch["kw2"], ch["vb"])
            kv_scr[0, ch["c"]] = kv2[:ML_DK]
            kv_scr[1, ch["c"]] = kv2[ML_DK:]
        for ch in chains:
            s2 = ch["qk2"] * ch["e2"]
            ch["s2"] = s2.astype(BF16)
            dens = _half_sums(hm, s2)
            for d in range(2):
                st_scr[d, 2, ch["rows"], :] = _lanes(dens[d])
        for ch in chains:
            vb = ch["vb"]
            zero = jnp.zeros_like(vb)
            vbd = jnp.concatenate([jnp.concatenate([vb, zero], axis=1),
                                   jnp.concatenate([zero, vb], axis=1)], axis=0)
            num2 = _dot(ch["s2"], vbd)
            num_scr[0, ch["rows"], :] = num2[:, :ML_DV]
            num_scr[1, ch["rows"], :] = num2[:, ML_DV:]

    def step_block(chunks):
        states = [(m_scr[j], c_scr[j], n_scr[j]) for j in range(2 * NS)]
        pairs = []
        for c in chunks:
            for s in range(NS):
                pair = []
                for d in range(2):
                    cc = s * nc + (c if d == 0 else nc - 1 - c)
                    rows = _chunk_rows(cc)
                    m, c_st, n_st = states[2 * s + d]
                    mloc, b_col, den_loc = st_scr[d, 0, rows, :], st_scr[d, 1, rows, :], st_scr[d, 2, rows, :]
                    bl, gmax = bl_scr[d, cc], gm_scr[d, cc]
                    qc = q_ref[rows, :] * scale
                    inter = b_col + m
                    mq = jnp.maximum(inter, mloc)
                    a = jnp.exp(inter - mq)
                    f = jnp.exp(mloc - mq)
                    den = f * den_loc + a * jnp.sum(qc * n_st, axis=1, keepdims=True)
                    pair.append(dict(rows=rows, qc=qc.astype(BF16), c_st=c_st.astype(BF16), a=a, f=f,
                                     inv=1.0 / jnp.maximum(jnp.abs(den), jnp.exp(-mq))))
                    m_new = jnp.maximum(bl + m, gmax)
                    dec = jnp.exp(bl + m - m_new)
                    fk = jnp.exp(gmax - m_new)
                    states[2 * s + d] = (m_new, dec * c_st + fk * kv_scr[d, cc], dec * n_st + fk * nl_scr[d, cc])
                pairs.append(pair)
        for j in range(2 * NS):
            m_scr[j], c_scr[j], n_scr[j] = states[j]
        for pair in pairs:
            lhs = jnp.concatenate([pair[0]["qc"], pair[1]["qc"]], axis=0)
            rhs = jnp.concatenate([pair[0]["c_st"], pair[1]["c_st"]], axis=1)
            pair.append(_dot(lhs, rhs))
        for pair in pairs:
            res = pair[2]
            for d, dst in enumerate((hm_ref, hb_scr)):
                it_ = pair[d]
                qc_c = res[d * CHUNK:(d + 1) * CHUNK, d * ML_DV:(d + 1) * ML_DV]
                dst[it_["rows"], :] = (it_["f"] * num_scr[d, it_["rows"], :] + it_["a"] * qc_c) * it_["inv"]

    _block_loop(NS * nc, pre_block, PRE_CHUNKS)
    _block_loop(nc, step_block, STEP_CHUNKS)

    hs = hm_ref[...] + hb_scr[...]
    hm_ref[...] = _rms(hs, gn_ref[...]) * jax.nn.sigmoid(o_ref[...])
    if emit_state:
        for s in range(NS):
            for d in range(2):
                cout_ref[s, d] = c_scr[2 * s + d]
                nout_ref[s, d] = n_scr[2 * s + d]
                mout_ref[s, d] = m_scr[2 * s + d][:, 0:1]


def _seq_geometry(prompt, prompt_seqs_per_step=1, sample_seqs_per_step=1):
    if prompt:
        return SEQ, BATCH, prompt_seqs_per_step, 0
    return DEC_SEQ, DEC_BATCH, sample_seqs_per_step, N_PROMPT // (sample_seqs_per_step * DEC_SEQ)


def _mlstm(qkvo, misc, i_bias, f_bias, gnorm, *, prompt, init=None):
    T, B, NS, blk0 = _seq_geometry(prompt)
    R = NS * T
    nc = R // CHUNK
    has_init = init is not None
    smem = pl.BlockSpec(memory_space=pltpu.SMEM)

    def col(j):
        return pl.BlockSpec((R, LANES), lambda b, h: (blk0 + b, j * ML_HEADS + h))

    in_specs = [smem, smem]
    args = [i_bias, f_bias]
    if has_init:
        in_specs.append(smem)
        args.append(init[2])
    in_specs += [col(0), col(1), col(2), col(3),
                 pl.BlockSpec((R, LANES), lambda b, h: (blk0 + b, 0)),
                 pl.BlockSpec((None, 1, ML_DV), lambda b, h: (h, 0, 0))]
    args += [qkvo, qkvo, qkvo, qkvo, misc, gnorm.reshape(ML_HEADS, 1, ML_DV)]
    c_spec = pl.BlockSpec((NS, 2, None, ML_DK, ML_DV), lambda b, h: (b, 0, h, 0, 0))
    n_spec = pl.BlockSpec((NS, 2, None, 1, ML_DK), lambda b, h: (b, 0, h, 0, 0))
    if has_init:
        in_specs += [c_spec, n_spec]
        args += [init[0], init[1].reshape(B, 2, ML_HEADS, 1, ML_DK)]
    out_specs = [pl.BlockSpec((R, LANES), lambda b, h: (b, h))]
    out_shape = [jax.ShapeDtypeStruct((B * T, ML_HEADS * ML_DV), F32)]
    if prompt:
        out_specs += [c_spec, n_spec, pl.BlockSpec((NS, 2, None, 1, 1), lambda b, h: (b, 0, h, 0, 0))]
        out_shape += [jax.ShapeDtypeStruct((B, 2, ML_HEADS, ML_DK, ML_DV), F32),
                      jax.ShapeDtypeStruct((B, 2, ML_HEADS, 1, ML_DK), F32),
                      jax.ShapeDtypeStruct((B, 2, ML_HEADS, 1, 1), F32)]
    return pl.pallas_call(
        functools.partial(_mlstm_kernel, T=T, NS=NS, has_init=has_init, emit_state=prompt),
        grid=(B // NS, ML_HEADS),
        in_specs=in_specs,
        out_specs=out_specs,
        out_shape=out_shape,
        scratch_shapes=[pltpu.VMEM((R, ML_DV), F32),
                        pltpu.VMEM((2 * NS, ML_DK, ML_DV), F32),
                        pltpu.VMEM((2 * NS, 1, ML_DK), F32),
                        pltpu.VMEM((2 * NS, 1, LANES), F32),
                        pltpu.VMEM((2, R, ML_DV), F32),
                        pltpu.VMEM((2, 3, R, LANES), F32),
                        pltpu.VMEM((2, nc, ML_DK, ML_DV), F32),
                        pltpu.VMEM((2, nc, 1, ML_DK), F32),
                        pltpu.VMEM((2, nc, 1, LANES), F32),
                        pltpu.VMEM((2, nc, 1, LANES), F32)],
        compiler_params=_cparams("arbitrary", "arbitrary"),
        name="mlstm_prompt" if prompt else "mlstm_sample",
    )(*args)


def _rope(x, cos, sin_signed):
    lane = lax.broadcasted_iota(jnp.int32, x.shape, 1)
    first = (lane & 15) < 8
    partner = jnp.where(first, pltpu.roll(x, LANES - 8, axis=1), pltpu.roll(x, 8, axis=1))
    return x * cos + partner * sin_signed


def _q_tile(cq, g_ref, w_ref, cos, sin, q_ref):
    y = _bdot(_rms(cq, g_ref[...]), w_ref[...])
    for hd in range(MLA_HEADS):
        sl = slice(hd * LANES, (hd + 1) * LANES)
        q_ref[:, sl] = _rope(y[:, sl], cos, sin).astype(BF16)


def _rope_lanes(kp):
    lane = lax.broadcasted_iota(jnp.int32, kp.shape, 1)
    return jnp.where((lane >= MLA_NOPE) & (lane < MLA_NOPE + MLA_ROPE), kp, 0.0)


def _kv_tile(c, kp, wk_ref, wv_ref, k_ref, v_ref):
    kn = _bdot(c, wk_ref[...])
    for hd in range(MLA_HEADS):
        sl = slice(hd * LANES, (hd + 1) * LANES)
        k_ref[:, sl] = (kn[:, sl] + kp).astype(BF16)
    v_ref[...] = _bdot(c, wv_ref[...]).astype(BF16)


def _rope_block_index(i, tm):
    p_tiles = N_PROMPT // tm
    s_tiles = DEC_SEQ // tm
    return jnp.where(i < p_tiles, 0, 1 + (i - p_tiles) % s_tiles)


def _kv_cache_kernel(ckv_ref, kpe_ref, wk_ref, wv_ref, k_ref, v_ref):
    _kv_tile(ckv_ref[...], _rope_lanes(kpe_ref[...]), wk_ref, wv_ref, k_ref, v_ref)


def _mla_kv_cache(ckv, kpe128, w_uk_pad, w_uv):
    n = ckv.shape[0]
    tm = ROW_TILE
    row = lambda w: pl.BlockSpec((tm, w), lambda i: (i, 0))
    full = lambda a: pl.BlockSpec(a.shape, lambda i: (0, 0))
    in_specs = [row(MLA_KV_RANK), row(LANES), full(w_uk_pad), full(w_uv)]
    args = [ckv, kpe128, w_uk_pad, w_uv]
    out_specs = [row(MLA_HEADS * LANES), row(MLA_HEADS * MLA_V)]
    out_shape = [jax.ShapeDtypeStruct((n, MLA_HEADS * LANES), BF16),
                 jax.ShapeDtypeStruct((n, MLA_HEADS * MLA_V), BF16)]
    return pl.pallas_call(
        _kv_cache_kernel,
        grid=(n // tm,),
        in_specs=in_specs,
        out_specs=out_specs,
        out_shape=out_shape,
        compiler_params=_cparams("arbitrary"),
        name="mla_kv_cache",
    )(*args)


def _attn_kernel(q_ref, k_ref, v_ref, o_ref, *, heads):
    scale = (MLA_NOPE + MLA_ROPE) ** -0.5
    scores = [_dot_nt(q_ref[:, j * LANES:(j + 1) * LANES], k_ref[:, j * LANES:(j + 1) * LANES]) * scale
              for j in range(heads)]
    probs, sums = [], []
    for s in scores:
        p = jnp.exp(s - jnp.max(s, axis=1, keepdims=True))
        sums.append(jnp.sum(p, axis=1, keepdims=True))
        probs.append(p.astype(BF16))
    pvs = [_dot(probs[j], v_ref[:, j * MLA_V:(j + 1) * MLA_V]) for j in range(heads)]
    o_ref[...] = jnp.concatenate([pvs[j] / sums[j] for j in range(heads)], axis=1)


def _attention(q, k, v, *, B, Tq, Tk, tq, q_row0, heads):
    nq = Tq // tq
    blk0 = q_row0 // tq
    return pl.pallas_call(
        functools.partial(_attn_kernel, heads=heads),
        grid=(B, MLA_HEADS // heads, nq),
        in_specs=[pl.BlockSpec((tq, heads * LANES), lambda b, hp, i: (blk0 + b * nq + i, hp)),
                  pl.BlockSpec((Tk, heads * LANES), lambda b, hp, i: (b, hp)),
                  pl.BlockSpec((Tk, heads * MLA_V), lambda b, hp, i: (b, hp))],
        out_specs=pl.BlockSpec((tq, heads * MLA_V), lambda b, hp, i: (b * nq + i, hp)),
        out_shape=jax.ShapeDtypeStruct((B * Tq, MLA_HEADS * MLA_V), F32),
        compiler_params=_cparams("arbitrary", "arbitrary", "arbitrary"),
        name="attention",
    )(q, k, v)


def _outproj_kernel(*refs, odd, split_x, tm):
    refs = list(refs)
    a1 = _take_tile(refs, True, tm)
    a2 = _take_tile(refs, True, tm)
    x = _take_tile(refs, split_x, tm)
    if odd:
        z_ref, gn_ref = refs.pop(0), refs.pop(0)
        a1 = _rms(a1 * _silu(z_ref[...]), gn_ref[...])
    (w1_ref, w2_ref, g1_ref), route_in, o_ref, route_out = refs[:3], refs[3:8], refs[8], refs[9:]
    out = _bdot(a1, w1_ref[...]) + _bdot(a2, w2_ref[...])
    x1 = x + g1_ref[...] * out
    o_ref[...] = x1
    _route_tile(x1, *route_in, *route_out)


def _outproj(a1, a2, w1, w2, x, mod4, layer, gain_ffn, w_route, b_route, z=None, gnorm=None):
    odd = z is not None
    tm = ROW_TILE
    half = w1.shape[0]
    full = lambda a: pl.BlockSpec(a.shape, lambda i: (0, 0))
    rows = lambda width: pl.BlockSpec((tm, width), lambda i: (i, 0))
    in_specs, args = [], []
    for operand in (a1, a2, x):
        specs, arrays = _token_operand(operand, tm)
        in_specs += specs
        args += arrays
    if odd:
        in_specs += [rows(half), pl.BlockSpec((1, half), lambda i: (0, 0))]
        args += [z, gnorm.reshape(1, half)]
    w_hi = w_route.astype(BF16)
    w_lo = (w_route - w_hi.astype(F32)).astype(BF16)
    in_specs += [full(w1), full(w2),
                 pl.BlockSpec((None, None, 1, D_MODEL), lambda i: (layer, _mod_index(i, tm), 0, 2)),
                 pl.BlockSpec((1, D_MODEL), lambda i: (0, 0)),
                 pl.BlockSpec((None, None, 1, 3 * D_MODEL), lambda i: (layer, _mod_index(i, tm), 0, 1)),
                 full(w_hi), full(w_lo), pl.BlockSpec((1, LANES), lambda i: (0, 0))]
    args += [w1, w2, mod4, gain_ffn.reshape(1, D_MODEL), mod4, w_hi, w_lo, b_route]
    return pl.pallas_call(
        functools.partial(_outproj_kernel, odd=odd, split_x=isinstance(x, tuple), tm=tm),
        grid=(N_TOK // tm,),
        in_specs=in_specs,
        out_specs=[rows(D_MODEL), rows(D_MODEL), rows(LANES), rows(LANES), pl.BlockSpec((1, LANES), lambda i: (0, 0))],
        out_shape=[jax.ShapeDtypeStruct((N_TOK, D_MODEL), F32),
                   jax.ShapeDtypeStruct((N_TOK, D_MODEL), F32),
                   jax.ShapeDtypeStruct((N_TOK, LANES), jnp.int32),
                   jax.ShapeDtypeStruct((N_TOK, LANES), F32),
                   jax.ShapeDtypeStruct((1, LANES), F32)],
        scratch_shapes=[pltpu.VMEM((1, LANES), F32)],
        compiler_params=_cparams("arbitrary"),
        name="outproj_router",
    )(*args)


def _conv_silu(src_ref, dst_ref, pad_ref, w_ref, b_ref, T, NS):
    width = src_ref.shape[1]
    zeros = jnp.zeros((CONV_HALO, width), F32)
    pad_ref[pl.ds(0, CONV_HALO), :] = zeros
    pad_ref[pl.ds(CONV_HALO + T, CONV_HALO), :] = zeros
    w = w_ref[...]
    bias = b_ref[...]
    for s in range(NS):
        pad_ref[pl.ds(CONV_HALO, T), :] = src_ref[pl.ds(s * T, T), :]
        for r0 in range(0, T, CONV_ROWS):
            acc = bias
            for j in range(CONV_W):
                start = r0 + CONV_HALO + j - CONV_W // 2
                acc = acc + w[j:j + 1, :] * pad_ref[pl.ds(start, CONV_ROWS), :]
            dst_ref[pl.ds(s * T + r0, CONV_ROWS), :] = _silu(acc)


def _ssd_kernel(*refs, T, NS, has_init, emit_state):
    it = iter(refs)
    x_ref, b_ref, c_ref = next(it), next(it), next(it)
    wx_ref, wb_ref, wc_ref = next(it), next(it), next(it)
    bx_ref, bb_ref, bc_ref = next(it), next(it), next(it)
    misc_ref, dtb_ref, alog_ref, dskip_ref = next(it), next(it), next(it), next(it)
    h0_ref = next(it) if has_init else None
    y_ref = next(it)
    hout_ref = next(it) if emit_state else None
    (pad_scr, xc_scr, bc_scr, cc_scr, yb_scr, hs_scr, yi_scr, ea_scr, upd_scr, eal_scr) = (
        next(it) for _ in range(10))

    p = pl.program_id(1)
    nc = T // CHUNK

    _conv_silu(x_ref, xc_scr, pad_scr, wx_ref, bx_ref, T, NS)
    _conv_silu(b_ref, bc_scr, pad_scr, wb_ref, bb_ref, T, NS)
    _conv_silu(c_ref, cc_scr, pad_scr, wc_ref, bc_ref, T, NS)

    for s in range(NS):
        for d in range(2):
            if has_init:
                hs_scr[2 * s + d] = h0_ref[s, d].reshape(2 * SSD_P, SSD_N).T
            else:
                hs_scr[2 * s + d] = jnp.zeros((SSD_N, 2 * SSD_P), F32)

    dtb = dtb_ref[...]
    neg_a = -jnp.exp(alog_ref[...])
    hms = [_half_masks(False, False), _half_masks(True, True)]
    hi_row = lax.broadcasted_iota(jnp.int32, (1, LANES), 1) >= CHUNK

    def pre_block(chunks):
        chains = []
        for c in chunks:
            rows = _chunk_rows(c)
            bc = bc_scr[rows, :].astype(BF16)
            cb2 = _dot_nt(cc_scr[rows, :].astype(BF16), jnp.concatenate([bc, bc], axis=0))
            chains.append(dict(c=c, rows=rows, bc=bc, cb2=cb2))
        for ch in chains:
            rows = ch["rows"]
            xc = xc_scr[rows, :]
            dts = _softplus(misc_ref[rows, :] + dtb)
            adt = dts * neg_a
            ch["x2m"] = _blockdiag(hms[0], xc)
            ch["g2"], ch["xw"] = [], []
            for d in range(2):
                hm = hms[d]
                dt_h = [_colsel(dts, d * SSD_HEADS + 2 * p + hh) for hh in range(2)]
                a_h = [_colsel(adt, d * SSD_HEADS + 2 * p + hh) for hh in range(2)]
                cum0, cum1, cum_rows = _half_cum(hm, a_h[0], a_h[1])
                seg2 = jnp.exp(jnp.where(hm["incl"], _pack(hm, cum0, cum1) - cum_rows, NEG_INF))
                dt_rows = _rows(hm, _pack(hm, dt_h[0], dt_h[1]))
                ch["g2"].append((ch["cb2"] * seg2 * dt_rows).astype(BF16))
                al = [jnp.sum(a, axis=0, keepdims=True) for a in a_h]
                wgt = [jnp.exp(al[hh] - cum) * dt_h[hh] for hh, cum in enumerate((cum0, cum1))]
                ch["xw"].append((xc * _pack(hm, wgt[0], wgt[1])).astype(BF16))
                ea_scr[d, rows, :] = _pack(hm, jnp.exp(cum0), jnp.exp(cum1))
                eal_scr[d, ch["c"]] = jnp.where(hi_row, jnp.exp(al[1]), jnp.exp(al[0]))
        for ch in chains:
            for d in range(2):
                yi_scr[d, ch["rows"], :] = _dot(ch["g2"][d], ch["x2m"])
                upd_scr[d, ch["c"]] = _dot_tn(ch["bc"], ch["xw"][d])

    def step_block(chunks):
        states = [hs_scr[j] for j in range(2 * NS)]
        pairs = []
        for c in chunks:
            for s in range(NS):
                pair = []
                for d in range(2):
                    cc = s * nc + (c if d == 0 else nc - 1 - c)
                    pair.append(dict(rows=_chunk_rows(cc), hs=states[2 * s + d].astype(BF16)))
                    states[2 * s + d] = eal_scr[d, cc] * states[2 * s + d] + upd_scr[d, cc]
                pairs.append(pair)
        for j in range(2 * NS):
            hs_scr[j] = states[j]
        for pair in pairs:
            lhs = jnp.concatenate([cc_scr[pair[d]["rows"], :].astype(BF16) for d in range(2)], axis=0)
            rhs = jnp.concatenate([pair[d]["hs"] for d in range(2)], axis=1)
            pair.append(_dot(lhs, rhs))
        for pair in pairs:
            for d, dst in enumerate((y_ref, yb_scr)):
                rows = pair[d]["rows"]
                ch = pair[2][d * CHUNK:(d + 1) * CHUNK, d * LANES:(d + 1) * LANES]
                dst[rows, :] = yi_scr[d, rows, :] + ea_scr[d, rows, :] * ch

    _block_loop(NS * nc, pre_block, PRE_CHUNKS)
    _block_loop(nc, step_block, STEP_CHUNKS)

    y_ref[...] = y_ref[...] + yb_scr[...] + dskip_ref[...] * xc_scr[...]
    if emit_state:
        for s in range(NS):
            for d in range(2):
                hout_ref[s, d] = hs_scr[2 * s + d].T.reshape(2, SSD_P, SSD_N)


def _lane_row(v):
    return jnp.pad(v.astype(F32), (0, LANES - v.shape[0])).reshape(1, LANES)


def _ssd(xbc, misc, conv_w, conv_b, dt_bias, a_log, d_skip, *, prompt, init=None):
    T, B, NS, blk0 = _seq_geometry(prompt)
    R = NS * T
    nc = R // CHUNK
    has_init = init is not None
    n_pairs = SSD_HEADS // 2
    pairs_per_group = n_pairs // SSD_GROUPS
    xb = SSD_HEADS * SSD_P // LANES
    cb = xb + SSD_GROUPS * SSD_N // LANES
    colx = lambda b, p: (blk0 + b, p)
    colb = lambda b, p: (blk0 + b, xb + p // pairs_per_group)
    colc = lambda b, p: (blk0 + b, cb + p // pairs_per_group)
    wsel = lambda f: (lambda b, p: (0, f(b, p)[1]))
    row128 = pl.BlockSpec((1, LANES), lambda b, p: (0, 0))
    in_specs = [pl.BlockSpec((R, LANES), colx), pl.BlockSpec((R, LANES), colb), pl.BlockSpec((R, LANES), colc),
                pl.BlockSpec((CONV_W, LANES), wsel(colx)), pl.BlockSpec((CONV_W, LANES), wsel(colb)),
                pl.BlockSpec((CONV_W, LANES), wsel(colc)),
                pl.BlockSpec((1, LANES), wsel(colx)), pl.BlockSpec((1, LANES), wsel(colb)),
                pl.BlockSpec((1, LANES), wsel(colc)),
                pl.BlockSpec((R, LANES), lambda b, p: (blk0 + b, 0)), row128, row128,
                pl.BlockSpec((1, LANES), lambda b, p: (0, p))]
    cb2 = conv_b.reshape(1, -1)
    args = [xbc, xbc, xbc, conv_w, conv_w, conv_w, cb2, cb2, cb2, misc,
            _lane_row(dt_bias.reshape(-1)), _lane_row(a_log.reshape(-1)),
            jnp.repeat(d_skip, SSD_P).reshape(1, SSD_HEADS * SSD_P)]
    state_spec = pl.BlockSpec((NS, 2, 2, SSD_P, SSD_N), lambda b, p: (b, 0, p, 0, 0))
    if has_init:
        in_specs.append(state_spec)
        args.append(init)
    out_specs = [pl.BlockSpec((R, LANES), lambda b, p: (b, p))]
    out_shape = [jax.ShapeDtypeStruct((B * T, SSD_HEADS * SSD_P), F32)]
    if prompt:
        out_specs.append(state_spec)
        out_shape.append(jax.ShapeDtypeStruct((B, 2, SSD_HEADS, SSD_P, SSD_N), F32))
    return pl.pallas_call(
        functools.partial(_ssd_kernel, T=T, NS=NS, has_init=has_init, emit_state=prompt),
        grid=(B // NS, n_pairs),
        in_specs=in_specs,
        out_specs=out_specs,
        out_shape=out_shape,
        scratch_shapes=[pltpu.VMEM((T + 2 * CONV_HALO, LANES), F32)]
        + [pltpu.VMEM((R, LANES), F32) for _ in range(4)]
        + [pltpu.VMEM((2 * NS, SSD_N, 2 * SSD_P), F32),
           pltpu.VMEM((2, R, LANES), F32),
           pltpu.VMEM((2, R, LANES), F32),
           pltpu.VMEM((2, nc, SSD_N, 2 * SSD_P), F32),
           pltpu.VMEM((2, nc, 1, LANES), F32)],
        compiler_params=_cparams("arbitrary", "arbitrary"),
        name="ssd_prompt" if prompt else "ssd_sample",
    )(*args)


def _tri_inverse(hm, nmat, eye_f):
    levels = int(np.log2(CHUNK))

    def off(level):
        same_big = lax.shift_right_logical(hm["t"], level) == lax.shift_right_logical(hm["s"], level)
        same_small = lax.shift_right_logical(hm["t"], level - 1) == lax.shift_right_logical(hm["s"], level - 1)
        return jnp.where(same_big & jnp.logical_not(same_small), nmat, 0.0)

    state = dict(dinv=eye_f - off(1))

    def first(level):
        def run():
            state["t1"] = _dot(state["dinv"].astype(BF16), _blockdiag(hm, off(level)))
        return run

    def second():
        state["dinv"] = state["dinv"] - _dot(state["t1"].astype(BF16), _blockdiag(hm, state["dinv"]))

    stages = []
    for level in range(2, levels + 1):
        stages += [first(level), second]
    return state, stages


def _gdn_kernel(*refs, T, NS, has_init, emit_state):
    it = iter(refs)
    q_ref, k_ref, v_ref = next(it), next(it), next(it)
    wq_ref, wk_ref, wv_ref = next(it), next(it), next(it)
    bq_ref, bk_ref, bv_ref = next(it), next(it), next(it)
    misc_ref, dtb_ref, alog_ref, z_ref, gn_ref = (next(it) for _ in range(5))
    s0_ref = next(it) if has_init else None
    o_ref = next(it)
    sout_ref = next(it) if emit_state else None
    (pad_scr, qc_scr, kc_scr, vc_scr, ob_scr, s_scr,
     u0_scr, wq_scr, kcf_scr, p2_scr, egl_scr) = (next(it) for _ in range(11))

    h = pl.program_id(1)
    nc = T // CHUNK

    _conv_silu(q_ref, qc_scr, pad_scr, wq_ref, bq_ref, T, NS)
    _conv_silu(k_ref, kc_scr, pad_scr, wk_ref, bk_ref, T, NS)
    _conv_silu(v_ref, vc_scr, pad_scr, wv_ref, bv_ref, T, NS)
    for r0 in range(0, NS * T, CONV_ROWS):
        rows = pl.ds(r0, CONV_ROWS)
        q = qc_scr[rows, :]
        qc_scr[rows, :] = q * lax.rsqrt(jnp.sum(q * q, axis=-1, keepdims=True) + EPS) * (GDN_DK ** -0.5)
        k = kc_scr[rows, :]
        kc_scr[rows, :] = k * lax.rsqrt(jnp.sum(k * k, axis=-1, keepdims=True) + EPS)

    for s in range(NS):
        for d in range(2):
            if has_init:
                s_scr[2 * s + d] = s0_ref[s, d]
            else:
                s_scr[2 * s + d] = jnp.zeros((GDN_DK, GDN_DV), F32)

    dtb = dtb_ref[...]
    neg_a = -jnp.exp(alog_ref[...])
    hm = _half_masks(False, True)
    eye_f = hm["eye"].astype(F32)

    def pre_gates(c):
        rows = _chunk_rows(c)
        kc = kc_scr[rows, :]
        qc = qc_scr[rows, :]
        mi = misc_ref[rows, :]
        gdec = neg_a * _softplus(mi + dtb)
        g_src = [_colsel(gdec, 2 * SSD_HEADS + d * GDN_HEADS + h) for d in range(2)]
        beta = [jax.nn.sigmoid(_colsel(mi, 2 * SSD_HEADS + 2 * GDN_HEADS + d * GDN_HEADS + h)) for d in range(2)]
        gc = [None, None]
        gc[0], gc[1], gc_rows = _half_cum(hm, g_src[0], g_src[1])
        diff = _pack(hm, gc[0], gc[1]) - gc_rows
        beta_rows = _rows(hm, _pack(hm, beta[0], beta[1]))
        kb = kc.astype(BF16)
        ek = []
        for d in range(2):
            egc = jnp.exp(gc[d])
            gl = jnp.sum(g_src[d], axis=0, keepdims=True)
            ek.append((egc * kc).astype(BF16))
            wq_scr[d, c, CHUNK:, :] = (egc * qc).astype(BF16)
            kcf_scr[d, rows, :] = (jnp.exp(gl - gc[d]) * beta[d] * kc).astype(BF16)
            egl_scr[d, c] = _lanes(jnp.exp(gl))
        return dict(c=c, rows=rows, kb=kb, qb=qc.astype(BF16), ek=ek,
                    dec_s=jnp.exp(jnp.where(hm["strict"], diff, NEG_INF)) * beta_rows,
                    dec_i=jnp.exp(jnp.where(hm["incl"], diff, NEG_INF)) * beta_rows)

    def pre_block(chunks):
        chains = [pre_gates(c) for c in chunks]
        for ch in chains:
            kb = ch["kb"]
            ch["kq"] = _dot_nt(jnp.concatenate([kb, ch["qb"]], axis=0), jnp.concatenate([kb, kb], axis=0))
        for ch in chains:
            p2_scr[ch["rows"], :] = (ch["dec_i"] * ch["kq"][CHUNK:]).astype(BF16)
            ch["inv"], ch["stages"] = _tri_inverse(hm, ch["dec_s"] * ch["kq"][:CHUNK], eye_f)
        for i in range(len(chains[0]["stages"])):
            for ch in chains:
                ch["stages"][i]()
        for ch in chains:
            vb = vc_scr[ch["rows"], :].astype(BF16)
            zero = jnp.zeros_like(vb)
            rhs = jnp.concatenate([jnp.concatenate([vb, ch["ek"][0], zero, zero], axis=1),
                                   jnp.concatenate([zero, zero, vb, ch["ek"][1]], axis=1)], axis=0)
            uw = _dot(ch["inv"]["dinv"].astype(BF16), rhs)
            for d in range(2):
                u0_scr[d, ch["rows"], :] = uw[:, 2 * d * LANES:(2 * d + 1) * LANES]
                wq_scr[d, ch["c"], :CHUNK, :] = uw[:, (2 * d + 1) * LANES:(2 * d + 2) * LANES].astype(BF16)

    def step_block(chunks):
        states = [s_scr[j] for j in range(2 * NS)]
        for c in chunks:
            work = []
            for s in range(NS):
                cs = [s * nc + c, s * nc + nc - 1 - c]
                lhs = jnp.concatenate([wq_scr[d, cs[d]] for d in range(2)], axis=0)
                rhs = jnp.concatenate([states[2 * s + d].astype(BF16) for d in range(2)], axis=1)
                work.append(dict(s=s, cs=cs, rows=[_chunk_rows(cc) for cc in cs], a=_dot(lhs, rhs)))
            for w in work:
                rows, ub, w["qs"] = w["rows"], [], []
                for d in range(2):
                    blk = w["a"][2 * d * CHUNK:(2 * d + 2) * CHUNK, d * LANES:(d + 1) * LANES]
                    ub.append((u0_scr[d, rows[d], :] - blk[:CHUNK]).astype(BF16))
                    w["qs"].append(blk[CHUNK:])
                zero = jnp.zeros_like(ub[0])
                w["pu"] = _dot(jnp.concatenate([p2_scr[rows[d], :] for d in range(2)], axis=0),
                               jnp.concatenate([jnp.concatenate([ub[0], zero], axis=1),
                                                jnp.concatenate([zero, ub[1]], axis=1)], axis=0))
                w["ktu"] = _dot_tn(jnp.concatenate([kcf_scr[d, rows[d], :] for d in range(2)], axis=1),
                                   jnp.concatenate(ub, axis=1))
            for w in work:
                for d, dst in enumerate((o_ref, ob_scr)):
                    j = 2 * w["s"] + d
                    dst[w["rows"][d], :] = w["qs"][d] + w["pu"][d * CHUNK:(d + 1) * CHUNK, d * LANES:(d + 1) * LANES]
                    states[j] = (egl_scr[d, w["cs"][d]] * states[j]
                                 + w["ktu"][d * GDN_DK:(d + 1) * GDN_DK, d * GDN_DV:(d + 1) * GDN_DV])
        for j in range(2 * NS):
            s_scr[j] = states[j]

    _block_loop(NS * nc, pre_block, PRE_CHUNKS)
    _block_loop(nc, step_block, STEP_CHUNKS)

    for r0 in range(0, NS * T, CONV_ROWS):
        rows = pl.ds(r0, CONV_ROWS)
        og = o_ref[rows, :] + ob_scr[rows, :]
        o_ref[rows, :] = _rms(og, gn_ref[...]) * _silu(z_ref[rows, :])
    if emit_state:
        for s in range(NS):
            for d in range(2):
                sout_ref[s, d] = s_scr[2 * s + d]


def _gdn(qkv, misc, zg, conv_w, conv_b, dt_row, alog_row, gnorm, *, prompt, init=None):
    T, B, NS, blk0 = _seq_geometry(prompt, GDN_PROMPT_SEQS_PER_STEP, GDN_SAMPLE_SEQS_PER_STEP)
    R = NS * T
    nc = R // CHUNK
    has_init = init is not None
    col = lambda j: (lambda b, h: (blk0 + b, j * GDN_HEADS + h))
    wsel = lambda j: (lambda b, h: (0, j * GDN_HEADS + h))
    row128 = pl.BlockSpec((1, LANES), lambda b, h: (0, 0))
    in_specs = [pl.BlockSpec((R, LANES), col(j)) for j in range(3)]
    in_specs += [pl.BlockSpec((CONV_W, LANES), wsel(j)) for j in range(3)]
    in_specs += [pl.BlockSpec((1, LANES), wsel(j)) for j in range(3)]
    in_specs += [pl.BlockSpec((R, LANES), lambda b, h: (blk0 + b, 0)), row128, row128,
                 pl.BlockSpec((R, LANES), lambda b, h: (blk0 + b, h)), row128]
    cb2 = conv_b.reshape(1, -1)
    args = [qkv, qkv, qkv, conv_w, conv_w, conv_w, cb2, cb2, cb2, misc, dt_row, alog_row, zg,
            gnorm.reshape(1, GDN_DV)]
    state_spec = pl.BlockSpec((NS, 2, None, GDN_DK, GDN_DV), lambda b, h: (b, 0, h, 0, 0))
    if has_init:
        in_specs.append(state_spec)
        args.append(init)
    out_specs = [pl.BlockSpec((R, LANES), lambda b, h: (b, h))]
    out_shape = [jax.ShapeDtypeStruct((B * T, GDN_HEADS * GDN_DV), F32)]
    if prompt:
        out_specs.append(state_spec)
        out_shape.append(jax.ShapeDtypeStruct((B, 2, GDN_HEADS, GDN_DK, GDN_DV), F32))
    return pl.pallas_call(
        functools.partial(_gdn_kernel, T=T, NS=NS, has_init=has_init, emit_state=prompt),
        grid=(B // NS, GDN_HEADS),
        in_specs=in_specs,
        out_specs=out_specs,
        out_shape=out_shape,
        scratch_shapes=[pltpu.VMEM((T + 2 * CONV_HALO, LANES), F32)]
        + [pltpu.VMEM((R, LANES), F32) for _ in range(4)]
        + [pltpu.VMEM((2 * NS, GDN_DK, GDN_DV), F32),
           pltpu.VMEM((2, R, GDN_DV), F32),
           pltpu.VMEM((2, nc, 2 * CHUNK, GDN_DK), BF16),
           pltpu.VMEM((2, R, GDN_DK), BF16),
           pltpu.VMEM((R, LANES), BF16),
           pltpu.VMEM((2, nc, 1, LANES), F32)],
        compiler_params=_cparams("arbitrary", "arbitrary"),
        name="gdn_prompt" if prompt else "gdn_sample",
    )(*args)


def _route_tile(x, g_ref, mod_ref, whi_ref, wlo_ref, b_ref, h_ref, ri_ref, w_ref, cnt_ref, count_scr):
    @pl.when(pl.program_id(0) == 0)
    def _():
        count_scr[...] = jnp.zeros_like(count_scr)

    mod = mod_ref[...]
    h = _rms(x, g_ref[...]) * (1.0 + mod[:, D_MODEL:2 * D_MODEL]) + mod[:, :D_MODEL]
    hb = h.astype(BF16)
    h_ref[...] = h
    hl = (h - hb.astype(F32)).astype(BF16)
    whi = whi_ref[...]
    logit = _dot(hb, whi) + (_dot(hl, whi) + _dot(hb, wlo_ref[...])) + b_ref[...]
    lane = lax.broadcasted_iota(jnp.int32, logit.shape, 1)
    is_group = (lane >= MOE_EXPERTS) & (lane < MOE_EXPERTS + MOE_GROUPS)
    glog = jnp.where(is_group, logit, NEG_INF)
    gmax = jnp.max(glog, axis=1, keepdims=True)
    gsel = jnp.min(jnp.where(glog == gmax, lane, LANES), axis=1, keepdims=True) - MOE_EXPERTS
    gw = 1.0 / jnp.sum(jnp.exp(glog - gmax), axis=1, keepdims=True)
    lo = gsel * MOE_PER_GROUP
    in_group = (lane >= lo) & (lane < lo + MOE_PER_GROUP)
    elog = jnp.where(in_group, logit, NEG_INF)
    v1 = jnp.max(elog, axis=1, keepdims=True)
    i1 = jnp.min(jnp.where(elog == v1, lane, LANES), axis=1, keepdims=True)
    elog2 = jnp.where(lane == i1, NEG_INF, elog)
    v2 = jnp.max(elog2, axis=1, keepdims=True)
    i2 = jnp.min(jnp.where(elog2 == v2, lane, LANES), axis=1, keepdims=True)
    e2 = jnp.exp(v2 - v1)
    w1 = gw / (1.0 + e2)
    w2 = gw * e2 / (1.0 + e2)
    w_ref[...] = jnp.where(lane == 0, w1, jnp.where(lane == 1, w2, 0.0))
    tm = logit.shape[0]
    onehot = jnp.where((lane == i1) | (lane == i2), 1.0, 0.0)
    t_i = lax.broadcasted_iota(jnp.int32, (tm, tm), 0)
    s_i = lax.broadcasted_iota(jnp.int32, (tm, tm), 1)
    earlier = jnp.where(s_i < t_i, 1.0, 0.0).astype(BF16)
    before = _dot(earlier, onehot.astype(BF16)) + count_scr[...]
    r1 = jnp.sum(jnp.where(lane == i1, before, 0.0), axis=1, keepdims=True).astype(jnp.int32)
    r2 = jnp.sum(jnp.where(lane == i2, before, 0.0), axis=1, keepdims=True).astype(jnp.int32)
    ri_ref[...] = jnp.where(lane == 0, i1, jnp.where(lane == 1, i2, jnp.where(lane == 2, r1, jnp.where(lane == 3, r2, 0))))
    count_scr[...] += jnp.sum(onehot, axis=0, keepdims=True)
    cnt_ref[...] = count_scr[...]


N_ASSIGN = 2 * N_TOK
MOE_TILE = 512
MOE_TILES = N_ASSIGN // MOE_TILE
MOE_SEGMENTS = MOE_TILES + MOE_EXPERTS - 1
DISPATCH_ROWS = 1024
COMBINE_ROWS = 256


def _routing_tables(ri, cnt):
    counts = cnt[0, :MOE_EXPERTS].astype(jnp.int32)
    cum_end = jnp.cumsum(counts)
    cum_start = cum_end - counts
    where = (ri[:, 0:2].reshape(N_ASSIGN), ri[:, 2:4].reshape(N_ASSIGN), cum_start)
    tile_start = jnp.arange(MOE_TILES, dtype=jnp.int32) * MOE_TILE
    expert_start = cum_start[1:]
    tile_rank = jnp.arange(MOE_TILES, dtype=jnp.int32) + jnp.sum(
        (expert_start[None, :] < tile_start[:, None]).astype(jnp.int32), axis=1)
    expert_rank = jnp.arange(MOE_EXPERTS - 1, dtype=jnp.int32) + jnp.minimum(expert_start // MOE_TILE + 1, MOE_TILES)
    values = jnp.concatenate([tile_start, expert_start])
    ranks = jnp.concatenate([tile_rank, expert_rank])
    slot = jnp.arange(MOE_SEGMENTS, dtype=jnp.int32)
    cuts = jnp.sum(jnp.where(ranks[None, :] == slot[:, None], values[None, :], 0), axis=1)
    seg_lo = cuts
    seg_hi = jnp.concatenate([cuts[1:], jnp.full((1,), N_ASSIGN, jnp.int32)])
    seg_tile = jnp.minimum(seg_lo // MOE_TILE, MOE_TILES - 1)
    seg_expert = jnp.minimum(jnp.sum((cum_end[None, :] <= seg_lo[:, None]).astype(jnp.int32), axis=1),
                             MOE_EXPERTS - 1)
    seg_first = jnp.concatenate([jnp.ones((1,), jnp.int32), (seg_tile[1:] != seg_tile[:-1]).astype(jnp.int32)])
    return where, (seg_tile, seg_expert, seg_lo, seg_hi, seg_first)


def _row_copy(src_ref, src_row, dst_ref, dst_row, sem):
    return pltpu.make_async_copy(src_ref.at[pl.ds(src_row, 1), :], dst_ref.at[pl.ds(dst_row, 1), :], sem)


def _dispatch_kernel(expert_ref, rank_ref, start_ref, h_ref, xs_ref, sem):
    base = pl.program_id(0) * DISPATCH_ROWS

    def start(t, carry):
        for k in range(2):
            a = 2 * (base + t) + k
            _row_copy(h_ref, t, xs_ref, start_ref[expert_ref[a]] + rank_ref[a], sem).start()
        return carry

    def wait(t, carry):
        for k in range(2):
            _row_copy(h_ref, 0, xs_ref, 0, sem).wait()
        return carry

    lax.fori_loop(0, DISPATCH_ROWS, start, 0, unroll=4)
    lax.fori_loop(0, DISPATCH_ROWS, wait, 0, unroll=4)


def _dispatch(where, h):
    tm = DISPATCH_ROWS
    return pl.pallas_call(
        _dispatch_kernel,
        grid_spec=pltpu.PrefetchScalarGridSpec(
            num_scalar_prefetch=3,
            grid=(N_TOK // tm,),
            in_specs=[pl.BlockSpec((tm, D_MODEL), lambda i, e, r, s: (i, 0))],
            out_specs=pl.BlockSpec(memory_space=pl.ANY),
            scratch_shapes=[pltpu.SemaphoreType.DMA(())]),
        out_shape=jax.ShapeDtypeStruct((N_ASSIGN, D_MODEL), F32),
        compiler_params=_cparams("arbitrary"),
        name="moe_dispatch",
    )(*where, h)


def _experts_kernel(tile_ref, expert_ref, lo_ref, hi_ref, first_ref, xs_ref, wg_ref, wu_ref, wd_ref, o_ref):
    p = pl.program_id(0)
    lo, hi = lo_ref[p], hi_ref[p]

    @pl.when(first_ref[p] == 1)
    def _():
        o_ref[...] = jnp.zeros_like(o_ref)

    @pl.when(hi > lo)
    def _():
        x = xs_ref[...].astype(BF16)
        hg = _dot(x, wg_ref[...].astype(BF16))
        hu = _dot(x, wu_ref[...].astype(BF16))
        row = tile_ref[p] * MOE_TILE + lax.broadcasted_iota(jnp.int32, (MOE_TILE, 1), 0)
        act = jnp.where((row >= lo) & (row < hi), _silu(hg) * hu, 0.0)
        o_ref[...] += _bdot(act, wd_ref[...])


def _experts(tables, xs, wg, wu, wd, layer):
    weight = lambda shape: pl.BlockSpec((None, None) + shape, lambda p, tile, expert, lo, hi, first: (layer, expert[p], 0, 0))
    rows = pl.BlockSpec((MOE_TILE, D_MODEL), lambda p, tile, expert, lo, hi, first: (tile[p], 0))
    return pl.pallas_call(
        _experts_kernel,
        grid_spec=pltpu.PrefetchScalarGridSpec(
            num_scalar_prefetch=5,
            grid=(MOE_SEGMENTS,),
            in_specs=[rows, weight((D_MODEL, MOE_FF)), weight((D_MODEL, MOE_FF)), weight((MOE_FF, D_MODEL))],
            out_specs=rows),
        out_shape=jax.ShapeDtypeStruct((N_ASSIGN, D_MODEL), F32),
        compiler_params=_cparams("arbitrary"),
        name="moe_experts",
    )(*tables, xs, wg, wu, wd)


def _combine_kernel(expert_ref, rank_ref, start_ref, ys_ref, w_ref, x_ref, mod_ref, gf_ref, *rest,
                    final, tile0, n_tiles, n_proj):
    n_in = 2 + n_proj if n_proj else 0
    proj_in, (o_ref, *proj_out), (buf, sem) = rest[:n_in], rest[n_in:-2], rest[-2:]
    i = pl.program_id(0)
    slot = i % 2

    def gather(tile, into):
        base = (tile0 + tile) * COMBINE_ROWS

        def start(t, carry):
            for k in range(2):
                a = 2 * (base + t) + k
                _row_copy(ys_ref, start_ref[expert_ref[a]] + rank_ref[a], buf.at[into, k], t, sem.at[into]).start()
            return carry

        lax.fori_loop(0, COMBINE_ROWS, start, 0, unroll=4)

    @pl.when(i == 0)
    def _():
        gather(0, 0)

    @pl.when(i + 1 < n_tiles)
    def _():
        gather(i + 1, 1 - slot)

    def wait(t, carry):
        for k in range(2):
            _row_copy(ys_ref, 0, buf.at[slot, k], 0, sem.at[slot]).wait()
        return carry

    lax.fori_loop(0, COMBINE_ROWS, wait, 0, unroll=4)
    w = w_ref[...]
    y = x_ref[...] + mod_ref[...] * (w[:, 0:1] * buf[slot, 0] + w[:, 1:2] * buf[slot, 1])
    if final:
        y = _rms(y, gf_ref[...])
    o_ref[...] = y
    if n_proj:
        g_ref, modin_ref, *w_refs = proj_in
        modin = modin_ref[...]
        hb = (_rms(y, g_ref[...]) * (1.0 + modin[:, D_MODEL:2 * D_MODEL]) + modin[:, :D_MODEL]).astype(BF16)
        for w_mat, y_ref in zip(w_refs, proj_out):
            y_ref[...] = _dot(hb, w_mat[...])


def _combine(where, ys, wts, x, mod4, layer, norm_final, final, tile0, n_tiles, proj=None):
    tm = COMBINE_ROWS
    rows = lambda width: pl.BlockSpec((tm, width), lambda i, e, r, s: (i, 0))
    in_specs = [pl.BlockSpec(memory_space=pl.ANY),
                pl.BlockSpec((tm, LANES), lambda i, e, r, s: (tile0 + i, 0)),
                pl.BlockSpec((tm, D_MODEL), lambda i, e, r, s: (tile0 + i, 0)),
                pl.BlockSpec((None, None, 1, D_MODEL), lambda i, e, r, s: (layer, _mod_index(tile0 + i, tm), 0, 5)),
                pl.BlockSpec((1, D_MODEL), lambda i, e, r, s: (0, 0))]
    args = [*where, ys, wts, x, mod4, norm_final.reshape(1, D_MODEL)]
    out_specs = [rows(D_MODEL)]
    out_shape = [jax.ShapeDtypeStruct((n_tiles * tm, D_MODEL), F32)]
    weights = []
    if proj is not None:
        gain, weights = proj
        in_specs += [pl.BlockSpec((1, D_MODEL), lambda i, e, r, s: (0, 0)),
                     pl.BlockSpec((None, None, 1, 2 * D_MODEL),
                                  lambda i, e, r, s: (layer + 1, _mod_index(tile0 + i, tm), 0, 0))]
        in_specs += [pl.BlockSpec(w.shape, lambda i, e, r, s: (0, 0)) for w in weights]
        args += [gain.reshape(1, D_MODEL), mod4, *weights]
        out_specs += [rows(w.shape[1]) for w in weights]
        out_shape += [jax.ShapeDtypeStruct((n_tiles * tm, w.shape[1]), F32) for w in weights]
    return pl.pallas_call(
        functools.partial(_combine_kernel, final=final, tile0=tile0, n_tiles=n_tiles, n_proj=len(weights)),
        grid_spec=pltpu.PrefetchScalarGridSpec(
            num_scalar_prefetch=3,
            grid=(n_tiles,),
            in_specs=in_specs,
            out_specs=out_specs,
            scratch_shapes=[pltpu.VMEM((2, 2, tm, D_MODEL), F32), pltpu.SemaphoreType.DMA((2,))]),
        out_shape=out_shape,
        compiler_params=_cparams("arbitrary"),
        name="moe_combine",
    )(*args)


def _rope_tables():
    t = jnp.arange(DEC_SEQ)
    pos = jnp.stack([(t // GRID_W).astype(F32), (t % GRID_W).astype(F32)], axis=1)
    nf = MLA_ROPE // 4
    inv = ROPE_BASE ** (-jnp.arange(nf, dtype=F32) / nf)
    j = jnp.arange(MLA_ROPE)
    ang = pos[:, j // (2 * nf)] * inv[j % nf][None, :]
    sign = jnp.where((j % (2 * nf)) < nf, -1.0, 1.0)
    cos = jnp.pad(jnp.cos(ang), ((0, 0), (MLA_NOPE, LANES - MLA_NOPE - MLA_ROPE)), constant_values=1.0)
    sin = jnp.pad(jnp.sin(ang) * sign, ((0, 0), (MLA_NOPE, LANES - MLA_NOPE - MLA_ROPE)))
    cos = jnp.concatenate([jnp.ones((ROW_TILE, LANES), F32), cos], axis=0)
    sin = jnp.concatenate([jnp.zeros((ROW_TILE, LANES), F32), sin], axis=0)
    return cos, sin


def _pad_heads(w, n_heads, width, lo, hi):
    k = w.shape[0]
    w = w.reshape(k, n_heads, width)[:, :, lo:hi]
    w = jnp.pad(w, ((0, 0), (0, 0), (0, LANES - (hi - lo))))
    return w.reshape(k, n_heads * LANES)


def kernel(x_prompt, x_sample, c, cache_mla_kv, cache_mla_krope, state_mlstm_C, state_mlstm_n, state_mlstm_m, state_ssd, state_gdn, c_ctx, ada_w, ada_b, norm_mix, norm_ffn, w_in_even, ml_i_bias, ml_f_bias, ml_norm, mla_q_norm, mla_w_uq, mla_kv_norm, mla_w_ukv, w_out_even, w_in_odd, ssd_conv_w, ssd_conv_b, ssd_dt_bias, ssd_A_log, ssd_D, ssd_norm, gdn_conv_w, gdn_conv_b, gdn_dt_bias, gdn_A_log, gdn_norm, w_out_odd, moe_w_group, moe_b_group, moe_w_expert, moe_b_expert, moe_w_gate, moe_w_up, moe_w_down, norm_final):
    x = (x_prompt.reshape(N_PROMPT, D_MODEL), x_sample.reshape(N_SAMPLE, D_MODEL))
    cond = jnp.concatenate([c_ctx[None, :], c, jnp.zeros((N_COND - 1 - DEC_BATCH, D_MODEL), F32)], axis=0)
    mod4 = _ada(cond, ada_w, ada_b).reshape(DEPTH, N_COND, 1, 6 * D_MODEL)
    cos_tab, sin_tab = _rope_tables()

    def route_params(layer):
        w_route = jnp.concatenate([moe_w_expert[layer], moe_w_group[layer]], axis=1)
        w_route = jnp.pad(w_route, ((0, 0), (0, LANES - MOE_EXPERTS - MOE_GROUPS)))
        b_route = _lane_row(jnp.concatenate([moe_b_expert[layer], moe_b_group[layer]]))
        return norm_ffn[layer], w_route, b_route

    def moe_layer(mixed, layer, final, proj=None):
        x, h, ri, wts, cnt = mixed
        where, tables = _routing_tables(ri, cnt)
        ys = _experts(tables, _dispatch(where, h), moe_w_gate, moe_w_up, moe_w_down, layer)
        combine = functools.partial(_combine, where, ys, wts, x, mod4, layer, norm_final, final)
        if not final:
            return combine(0, N_TOK // COMBINE_ROWS, proj)
        p_tiles = N_PROMPT // COMBINE_ROWS
        return combine(0, p_tiles)[0], combine(p_tiles, N_SAMPLE // COMBINE_ROWS)[0]

    e = 0
    w = w_in_even[e]
    off = np.cumsum([0, 4 * ML_HEADS * ML_DK, 2 * ML_HEADS, 2 * ML_HEADS, MLA_Q_RANK, MLA_KV_RANK, MLA_ROPE])
    w_misc = jnp.concatenate([w[:, off[1]:off[3]],
                              jnp.zeros((D_MODEL, MLA_NOPE - 4 * ML_HEADS), F32),
                              w[:, off[5]:off[6]],
                              jnp.zeros((D_MODEL, LANES - MLA_NOPE - MLA_ROPE), F32)], axis=1)
    weights = [w[:, :off[1]].astype(BF16), w[:, off[3]:off[4]].astype(BF16),
               w[:, off[4]:off[5]].astype(BF16), w_misc.astype(BF16)]
    dq = MLA_NOPE + MLA_ROPE
    w_uq_pad = _pad_heads(mla_w_uq[e], MLA_HEADS, dq, 0, dq).astype(BF16)
    w_uk_pad = _pad_heads(mla_w_ukv[e], MLA_HEADS, MLA_NOPE + MLA_V, 0, MLA_NOPE).astype(BF16)
    w_uv = mla_w_ukv[e].reshape(MLA_KV_RANK, MLA_HEADS, MLA_NOPE + MLA_V)[:, :, MLA_NOPE:]
    w_uv = w_uv.reshape(MLA_KV_RANK, MLA_HEADS * MLA_V).astype(BF16)
    qkvo, misc0, q_cat, ckv_n, k_cat, v_all = _inproj(
        x, norm_mix[0], mod4, 0, weights,
        mla=(mla_q_norm[e], w_uq_pad, mla_kv_norm[e], w_uk_pad, w_uv, cos_tab, sin_tab))

    hm_p, st_c, st_n, st_m = _mlstm(qkvo, misc0, ml_i_bias[e], ml_f_bias[e], ml_norm[e], prompt=True)
    (hm_s,) = _mlstm(qkvo, misc0, ml_i_bias[e], ml_f_bias[e], ml_norm[e], prompt=False,
                     init=(state_mlstm_C[:, e], state_mlstm_n[:, e], state_mlstm_m[:, e]))

    cache_kpe = jnp.pad(cache_mla_krope[:, e].reshape(DEC_BATCH * PAST_LEN, MLA_ROPE),
                        ((0, 0), (MLA_NOPE, LANES - MLA_NOPE - MLA_ROPE)))
    k_cache, v_cache = _mla_kv_cache(cache_mla_kv[:, e].reshape(DEC_BATCH * PAST_LEN, MLA_KV_RANK), cache_kpe,
                                     w_uk_pad, w_uv)
    att_p = _attention(q_cat, k_cat, v_all, B=BATCH, Tq=SEQ, Tk=SEQ, tq=SEQ, q_row0=0, heads=MLA_HEADS)

    def with_cache(cache, new):
        width = new.shape[1]
        both = jnp.concatenate([cache.reshape(DEC_BATCH, PAST_LEN, width),
                                new[N_PROMPT:].reshape(DEC_BATCH, DEC_SEQ, width)], axis=1)
        return both.reshape(DEC_BATCH * (PAST_LEN + DEC_SEQ), width)

    att_s = _attention(q_cat, with_cache(k_cache, k_cat), with_cache(v_cache, v_all), B=DEC_BATCH, Tq=DEC_SEQ,
                       Tk=PAST_LEN + DEC_SEQ, tq=256, q_row0=N_PROMPT, heads=2)
    wo = w_out_even[e].astype(BF16)
    mixed = _outproj((hm_p, hm_s), (att_p, att_s), wo[:ML_HEADS * ML_DV], wo[ML_HEADS * ML_DV:], x, mod4, 0,
                     *route_params(0))
    oi = 0
    w = w_in_odd[oi]
    ssd_w = SSD_HEADS * SSD_P
    ssd_cc = ssd_w + 2 * SSD_GROUPS * SSD_N
    gdn_w = GDN_HEADS * GDN_DK
    off = np.cumsum([0, ssd_w, ssd_cc, 2 * SSD_HEADS, 3 * gdn_w, gdn_w, 2 * GDN_HEADS, 2 * GDN_HEADS])
    w_misc = jnp.concatenate([w[:, off[2]:off[3]], w[:, off[5]:off[7]],
                              jnp.zeros((D_MODEL, LANES - 2 * SSD_HEADS - 4 * GDN_HEADS), F32)], axis=1)
    weights = [w[:, off[0]:off[1]].astype(BF16), w[:, off[1]:off[2]].astype(BF16),
               w[:, off[3]:off[4]].astype(BF16), w[:, off[4]:off[5]].astype(BF16), w_misc.astype(BF16)]
    x, z_s, xbc, qkv_g, z_g, misc = moe_layer(mixed, 0, final=False, proj=(norm_mix[1], weights))

    ssd_args = (xbc, misc, ssd_conv_w[oi], ssd_conv_b[oi], ssd_dt_bias[oi], ssd_A_log[oi], ssd_D[oi])
    ys_p, st_ssd = _ssd(*ssd_args, prompt=True)
    (ys_s,) = _ssd(*ssd_args, prompt=False, init=state_ssd[:, oi])

    lo = 2 * SSD_HEADS
    gdn_dt_row = jnp.pad(gdn_dt_bias[oi].reshape(-1), (lo, LANES - lo - 2 * GDN_HEADS)).reshape(1, LANES)
    gdn_alog_row = jnp.pad(gdn_A_log[oi].reshape(-1), (lo, LANES - lo - 2 * GDN_HEADS)).reshape(1, LANES)
    gdn_args = (qkv_g, misc, z_g, gdn_conv_w[oi], gdn_conv_b[oi], gdn_dt_row, gdn_alog_row, gdn_norm[oi])
    og_p, st_gdn = _gdn(*gdn_args, prompt=True)
    (og_s,) = _gdn(*gdn_args, prompt=False, init=state_gdn[:, oi])

    wo = w_out_odd[oi].astype(BF16)
    mixed = _outproj((ys_p, ys_s), (og_p, og_s), wo[:ssd_w], wo[ssd_w:], x, mod4, 1, *route_params(1),
                     z=z_s, gnorm=ssd_norm[oi])
    y_p, y_s = moe_layer(mixed, 1, final=True)

    y_prompt = y_p.reshape(BATCH, SEQ, D_MODEL)
    y_sample = y_s.reshape(DEC_BATCH, DEC_SEQ, D_MODEL)
    new_mla_kv = ckv_n[:N_PROMPT].reshape(BATCH, 1, SEQ, MLA_KV_RANK)
    new_mla_krope = misc0[:N_PROMPT, MLA_NOPE:MLA_NOPE + MLA_ROPE].reshape(BATCH, 1, SEQ, MLA_ROPE)
    return (y_prompt, y_sample, new_mla_kv, new_mla_krope, st_c[:, None], st_n.reshape(BATCH, 1, 2, ML_HEADS, ML_DK),
            st_m.reshape(BATCH, 1, 2, ML_HEADS), st_ssd[:, None], st_gdn[:, None])
```

```python
import functools

import numpy as np
import jax
import jax.numpy as jnp
from jax import lax
from jax.experimental import pallas as pl
from jax.experimental.pallas import tpu as pltpu

F32 = jnp.float32
BF16 = jnp.bfloat16

D_MODEL = 1024
BATCH = 32
SEQ = 256
DEPTH = 2
DEC_BATCH = 2
DEC_SEQ = 2048
PAST_LEN = 256
GRID_W = 64
EPS = 1e-6
ML_HEADS = 4
ML_DK = 128
ML_DV = 128
MLA_HEADS = 8
MLA_Q_RANK = 384
MLA_KV_RANK = 256
MLA_NOPE = 64
MLA_ROPE = 32
MLA_V = 64
ROPE_BASE = 10000.0
SSD_HEADS = 8
SSD_P = 64
SSD_GROUPS = 2
SSD_N = 128
GDN_HEADS = 4
GDN_DK = 128
GDN_DV = 128
CONV_W = 5
MOE_GROUPS = 4
MOE_PER_GROUP = 8
MOE_EXPERTS = 32
MOE_FF = 256

N_PROMPT = BATCH * SEQ
N_SAMPLE = DEC_BATCH * DEC_SEQ
N_TOK = N_PROMPT + N_SAMPLE
N_COND = 8

LANES = 128
CHUNK = 64
ROW_TILE = 512
CONV_HALO = 8
CONV_ROWS = 256
VMEM_LIMIT = 56 * 1024 * 1024
PRE_CHUNKS = 8
GDN_PRE_CHUNKS = 16
STEP_CHUNKS = 4
GDN_PROMPT_SEQS_PER_STEP = 8
GDN_SAMPLE_SEQS_PER_STEP = 1

assert LANES == 2 * CHUNK and SSD_P == CHUNK

NEG_INF = float("-inf")


def _cparams(*sem):
    return pltpu.CompilerParams(dimension_semantics=sem, vmem_limit_bytes=VMEM_LIMIT)


def _dot(a, b):
    return jnp.dot(a, b, preferred_element_type=F32)


def _dot_nt(a, b):
    return lax.dot_general(a, b, (((1,), (1,)), ((), ())), preferred_element_type=F32)


def _dot_tn(a, b):
    return lax.dot_general(a, b, (((0,), (0,)), ((), ())), preferred_element_type=F32)


def _bdot(a, b):
    return _dot(a.astype(BF16), b.astype(BF16))


def _rms(x, g):
    return x * lax.rsqrt(jnp.mean(x * x, axis=-1, keepdims=True) + EPS) * g


def _softplus(x):
    return jnp.maximum(x, 0.0) + jnp.log1p(jnp.exp(-jnp.abs(x)))


def _silu(x):
    return x * jax.nn.sigmoid(x)


def _colsel(x, j):
    lane = lax.broadcasted_iota(jnp.int32, x.shape, 1)
    return jnp.sum(jnp.where(lane == j, x, 0.0), axis=1, keepdims=True)


def _lanes(x):
    return jnp.broadcast_to(x, (x.shape[0], LANES))


def _half_masks(rev_lo, rev_hi):
    t = lax.broadcasted_iota(jnp.int32, (CHUNK, LANES), 0)
    lane = lax.broadcasted_iota(jnp.int32, (CHUNK, LANES), 1)
    s = lane & (CHUNK - 1)
    hi = lane >= CHUNK

    def pick(fwd, bwd):
        if rev_lo == rev_hi:
            return bwd if rev_lo else fwd
        on_hi, on_lo = (bwd, fwd) if rev_hi else (fwd, bwd)
        return (hi & on_hi) | (jnp.logical_not(hi) & on_lo)

    return dict(hi=hi, t=t, s=s, eye=(s == t), incl=pick(s <= t, s >= t), incl_t=pick(t <= s, t >= s),
                strict=pick(s < t, s > t))


def _pack(hm, col_lo, col_hi):
    return jnp.where(hm["hi"], col_hi, col_lo)


def _rows(hm, cols):
    return jnp.sum(jnp.where(hm["eye"], cols, 0.0), axis=0, keepdims=True)


def _half_sums(hm, x):
    lo = jnp.sum(jnp.where(hm["hi"], 0.0, x), axis=1, keepdims=True)
    hi = jnp.sum(jnp.where(hm["hi"], x, 0.0), axis=1, keepdims=True)
    return lo, hi


def _half_cum(hm, col_lo, col_hi):
    cols = _pack(hm, col_lo, col_hi)
    cum_lo, cum_hi = _half_sums(hm, jnp.where(hm["incl"], _rows(hm, cols), 0.0))
    cum_rows = jnp.sum(jnp.where(hm["incl_t"], cols, 0.0), axis=0, keepdims=True)
    return cum_lo, cum_hi, cum_rows


def _blockdiag(hm, x):
    return jnp.concatenate([jnp.where(hm["hi"], 0.0, x).astype(BF16),
                            jnp.where(hm["hi"], x, 0.0).astype(BF16)], axis=0)


def _chunk_rows(c):
    if isinstance(c, int):
        return pl.ds(c * CHUNK, CHUNK)
    return pl.ds(pl.multiple_of(c * CHUNK, CHUNK), CHUNK)


def _block_loop(n, body, size):
    if n <= size:
        body(list(range(n)))
        return

    def block(blk, carry):
        body([blk * size + j for j in range(size)])
        return carry

    lax.fori_loop(0, n // size, block, 0)


def _mod_index(i, rows_per_tile):
    p_tiles = N_PROMPT // rows_per_tile
    s_tiles = DEC_SEQ // rows_per_tile
    return jnp.where(i < p_tiles, 0, 1 + (i - p_tiles) // s_tiles)


def _ada_kernel(c_ref, w_ref, b_ref, o_ref):
    c = c_ref[...]
    o_ref[...] = _bdot(_silu(c), w_ref[...]) + b_ref[...]


def _ada(cond, ada_w, ada_b):
    nb = 6
    return pl.pallas_call(
        _ada_kernel,
        grid=(DEPTH, nb),
        in_specs=[pl.BlockSpec((N_COND, D_MODEL), lambda l, j: (0, 0)),
                  pl.BlockSpec((None, D_MODEL, D_MODEL), lambda l, j: (l, 0, j)),
                  pl.BlockSpec((None, 1, D_MODEL), lambda l, j: (l, 0, j))],
        out_specs=pl.BlockSpec((None, N_COND, D_MODEL), lambda l, j: (l, 0, j)),
        out_shape=jax.ShapeDtypeStruct((DEPTH, N_COND, 6 * D_MODEL), F32),
        compiler_params=_cparams("arbitrary", "arbitrary"),
        name="ada",
    )(cond, ada_w, ada_b.reshape(DEPTH, 1, 6 * D_MODEL))


def _token_operand(x, tm):
    if not isinstance(x, tuple):
        return [pl.BlockSpec((tm, x.shape[1]), lambda i: (i, 0))], [x]
    pt = N_PROMPT // tm
    width = x[0].shape[1]
    return ([pl.BlockSpec((tm, width), lambda i: (jnp.minimum(i, pt - 1), 0)),
             pl.BlockSpec((tm, width), lambda i: (jnp.maximum(i - pt, 0), 0))], list(x))


def _take_tile(refs, split, tm):
    if not split:
        return refs.pop(0)[...]
    p_ref, s_ref = refs.pop(0), refs.pop(0)
    return jnp.where(pl.program_id(0) < N_PROMPT // tm, p_ref[...], s_ref[...])


def _inproj_kernel(*refs, n_out, split, tm, mla):
    refs = list(refs)
    x = _take_tile(refs, split, tm)
    g_ref, mod_ref = refs[:2]
    w_refs, rest = refs[2:2 + n_out], refs[2 + n_out:]
    mod = mod_ref[...]
    h = _rms(x, g_ref[...]) * (1.0 + mod[:, D_MODEL:2 * D_MODEL]) + mod[:, :D_MODEL]
    hb = h.astype(BF16)
    ys = [_dot(hb, w_ref[...]) for w_ref in w_refs]
    if not mla:
        for y, o_ref in zip(ys, rest):
            o_ref[...] = y
        return
    (gq_ref, wuq_ref, gkv_ref, wuk_ref, wuv_ref, cos_ref, sin_ref,
     qkvo_ref, misc_ref, q_ref, ckvn_ref, k_ref, v_ref) = rest
    y_qkvo, y_cq, y_ckv, y_misc = ys
    qkvo_ref[...] = y_qkvo
    misc_ref[...] = y_misc
    cos, sin = cos_ref[...], sin_ref[...]
    _q_tile(y_cq, gq_ref, wuq_ref, cos, sin, q_ref)
    c = _rms(y_ckv, gkv_ref[...])
    ckvn_ref[...] = c
    _kv_tile(c, _rope(_rope_lanes(y_misc), cos, sin), wuk_ref, wuv_ref, k_ref, v_ref)


def _inproj(x, gain, mod4, layer, weights, mla=None):
    n_out = len(weights)
    tm = ROW_TILE
    rows = lambda width: pl.BlockSpec((tm, width), lambda i: (i, 0))
    full = lambda a: pl.BlockSpec(a.shape, lambda i: (0, 0))
    in_specs, args = _token_operand(x, tm)
    in_specs += [pl.BlockSpec((1, D_MODEL), lambda i: (0, 0)),
                 pl.BlockSpec((None, None, 1, 2 * D_MODEL), lambda i: (layer, _mod_index(i, tm), 0, 0))]
    in_specs += [full(w) for w in weights]
    args += [gain.reshape(1, D_MODEL), mod4, *weights]
    out_specs = [rows(w.shape[1]) for w in weights]
    out_shape = [jax.ShapeDtypeStruct((N_TOK, w.shape[1]), F32) for w in weights]
    if mla is not None:
        gq, w_uq, gkv, w_uk, w_uv, cos_tab, sin_tab = mla
        tab = pl.BlockSpec((tm, LANES), lambda i: (_rope_block_index(i, tm), 0))
        in_specs += [pl.BlockSpec((1, MLA_Q_RANK), lambda i: (0, 0)), full(w_uq),
                     pl.BlockSpec((1, MLA_KV_RANK), lambda i: (0, 0)), full(w_uk), full(w_uv), tab, tab]
        args += [gq.reshape(1, MLA_Q_RANK), w_uq, gkv.reshape(1, MLA_KV_RANK), w_uk, w_uv, cos_tab, sin_tab]
        widths = [(weights[0].shape[1], F32), (LANES, F32), (MLA_HEADS * LANES, BF16), (MLA_KV_RANK, F32),
                  (MLA_HEADS * LANES, BF16), (MLA_HEADS * MLA_V, BF16)]
        out_specs = [rows(width) for width, _ in widths]
        out_shape = [jax.ShapeDtypeStruct((N_TOK, width), dtype) for width, dtype in widths]
    return pl.pallas_call(
        functools.partial(_inproj_kernel, n_out=n_out, split=isinstance(x, tuple), tm=tm, mla=mla is not None),
        grid=(N_TOK // tm,),
        in_specs=in_specs,
        out_specs=out_specs,
        out_shape=out_shape,
        compiler_params=_cparams("arbitrary"),
        name="inproj",
    )(*args)


def _mlstm_kernel(*refs, T, NS, has_init, emit_state):
    it = iter(refs)
    ib_ref, fb_ref = next(it), next(it)
    m0_ref = next(it) if has_init else None
    q_ref, k_ref, v_ref, o_ref, misc_ref, gn_ref = (next(it) for _ in range(6))
    c0_ref, n0_ref = (next(it), next(it)) if has_init else (None, None)
    hm_ref = next(it)
    cout_ref, nout_ref, mout_ref = (next(it), next(it), next(it)) if emit_state else (None, None, None)
    (hb_scr, c_scr, n_scr, m_scr, num_scr, st_scr, kv_scr, nl_scr, bl_scr, gm_scr) = (next(it) for _ in range(10))

    b = pl.program_id(0)
    h = pl.program_id(1)
    nc = T // CHUNK
    scale = ML_DK ** -0.5

    for s in range(NS):
        for d in range(2):
            if has_init:
                c_scr[2 * s + d] = c0_ref[s, d]
                n_scr[2 * s + d] = n0_ref[s, d]
                m_scr[2 * s + d] = jnp.full((1, LANES), m0_ref[b * NS + s, d, h], F32)
            else:
                c_scr[2 * s + d] = jnp.zeros((ML_DK, ML_DV), F32)
                n_scr[2 * s + d] = jnp.zeros((1, ML_DK), F32)
                m_scr[2 * s + d] = jnp.zeros((1, LANES), F32)

    hm = _half_masks(False, True)

    def pre_gates(c):
        rows = _chunk_rows(c)
        mi = misc_ref[rows, :]
        li = [_colsel(mi, d * ML_HEADS + h) + ib_ref[d, h] for d in range(2)]
        lf = [-_softplus(-(_colsel(mi, 2 * ML_HEADS + d * ML_HEADS + h) + fb_ref[d, h])) for d in range(2)]
        b_cols = [None, None]
        b_cols[0], b_cols[1], b_rows = _half_cum(hm, lf[0], lf[1])
        li_rows = _rows(hm, _pack(hm, li[0], li[1]))
        dm = jnp.where(hm["incl"], _pack(hm, b_cols[0], b_cols[1]) - b_rows + li_rows, NEG_INF)
        mloc = [jnp.max(jnp.where(hm["hi"], NEG_INF, dm), axis=1, keepdims=True),
                jnp.max(jnp.where(hm["hi"], dm, NEG_INF), axis=1, keepdims=True)]
        kc = k_ref[rows, :]
        kws = []
        for d in range(2):
            bl = jnp.sum(lf[d], axis=0, keepdims=True)
            g = bl - b_cols[d] + li[d]
            gmax = jnp.max(g, axis=0, keepdims=True)
            kw = jnp.exp(g - gmax) * kc
            kws.append(kw.astype(BF16))
            nl_scr[d, c] = jnp.sum(kw, axis=0, keepdims=True)
            bl_scr[d, c] = _lanes(bl)
            gm_scr[d, c] = _lanes(gmax)
            st_scr[d, 0, rows, :] = _lanes(mloc[d])
            st_scr[d, 1, rows, :] = _lanes(b_cols[d])
        return dict(c=c, rows=rows, kc=kc.astype(BF16), kw2=jnp.concatenate(kws, axis=1),
                    e2=jnp.exp(dm - _pack(hm, mloc[0], mloc[1])))

    def pre_block(chunks):
        chains = [pre_gates(c) for c in chunks]
        for ch in chains:
            qc = (q_ref[ch["rows"], :] * scale).astype(BF16)
            ch["vb"] = v_ref[ch["rows"], :].astype(BF16)
            ch["qk2"] = _dot_nt(qc, jnp.concatenate([ch["kc"], ch["kc"]], axis=0))
            kv2 = _dot_tn(ch["kw2"], ch["vb"])
            kv_scr[0, ch["c"]] = kv2[:ML_DK]
            kv_scr[1, ch["c"]] = kv2[ML_DK:]
        for ch in chains:
            s2 = ch["qk2"] * ch["e2"]
            ch["s2"] = s2.astype(BF16)
            dens = _half_sums(hm, s2)
            for d in range(2):
                st_scr[d, 2, ch["rows"], :] = _lanes(dens[d])
        for ch in chains:
            vb = ch["vb"]
            zero = jnp.zeros_like(vb)
            vbd = jnp.concatenate([jnp.concatenate([vb, zero], axis=1),
                                   jnp.concatenate([zero, vb], axis=1)], axis=0)
            num2 = _dot(ch["s2"], vbd)
            num_scr[0, ch["rows"], :] = num2[:, :ML_DV]
            num_scr[1, ch["rows"], :] = num2[:, ML_DV:]

    def step_block(chunks):
        states = [(m_scr[j], c_scr[j], n_scr[j]) for j in range(2 * NS)]
        pairs = []
        for c in chunks:
            for s in range(NS):
                pair = []
                for d in range(2):
                    cc = s * nc + (c if d == 0 else nc - 1 - c)
                    rows = _chunk_rows(cc)
                    m, c_st, n_st = states[2 * s + d]
                    mloc, b_col, den_loc = st_scr[d, 0, rows, :], st_scr[d, 1, rows, :], st_scr[d, 2, rows, :]
                    bl, gmax = bl_scr[d, cc], gm_scr[d, cc]
                    qc = q_ref[rows, :] * scale
                    inter = b_col + m
                    mq = jnp.maximum(inter, mloc)
                    a = jnp.exp(inter - mq)
                    f = jnp.exp(mloc - mq)
                    den = f * den_loc + a * jnp.sum(qc * n_st, axis=1, keepdims=True)
                    pair.append(dict(rows=rows, qc=qc.astype(BF16), c_st=c_st.astype(BF16), a=a, f=f,
                                     inv=1.0 / jnp.maximum(jnp.abs(den), jnp.exp(-mq))))
                    m_new = jnp.maximum(bl + m, gmax)
                    dec = jnp.exp(bl + m - m_new)
                    fk = jnp.exp(gmax - m_new)
                    states[2 * s + d] = (m_new, dec * c_st + fk * kv_scr[d, cc], dec * n_st + fk * nl_scr[d, cc])
                pairs.append(pair)
        for j in range(2 * NS):
            m_scr[j], c_scr[j], n_scr[j] = states[j]
        for pair in pairs:
            lhs = jnp.concatenate([pair[0]["qc"], pair[1]["qc"]], axis=0)
            rhs = jnp.concatenate([pair[0]["c_st"], pair[1]["c_st"]], axis=1)
            pair.append(_dot(lhs, rhs))
        for pair in pairs:
            res = pair[2]
            for d, dst in enumerate((hm_ref, hb_scr)):
                it_ = pair[d]
                qc_c = res[d * CHUNK:(d + 1) * CHUNK, d * ML_DV:(d + 1) * ML_DV]
                dst[it_["rows"], :] = (it_["f"] * num_scr[d, it_["rows"], :] + it_["a"] * qc_c) * it_["inv"]

    _block_loop(NS * nc, pre_block, PRE_CHUNKS)
    _block_loop(nc, step_block, STEP_CHUNKS)

    hs = hm_ref[...] + hb_scr[...]
    hm_ref[...] = _rms(hs, gn_ref[...]) * jax.nn.sigmoid(o_ref[...])
    if emit_state:
        for s in range(NS):
            for d in range(2):
                cout_ref[s, d] = c_scr[2 * s + d]
                nout_ref[s, d] = n_scr[2 * s + d]
                mout_ref[s, d] = m_scr[2 * s + d][:, 0:1]


def _seq_geometry(prompt, prompt_seqs_per_step=1, sample_seqs_per_step=1):
    if prompt:
        return SEQ, BATCH, prompt_seqs_per_step, 0
    return DEC_SEQ, DEC_BATCH, sample_seqs_per_step, N_PROMPT // (sample_seqs_per_step * DEC_SEQ)


def _mlstm(qkvo, misc, i_bias, f_bias, gnorm, *, prompt, init=None):
    T, B, NS, blk0 = _seq_geometry(prompt)
    R = NS * T
    nc = R // CHUNK
    has_init = init is not None
    smem = pl.BlockSpec(memory_space=pltpu.SMEM)

    def col(j):
        return pl.BlockSpec((R, LANES), lambda b, h: (blk0 + b, j * ML_HEADS + h))

    in_specs = [smem, smem]
    args = [i_bias, f_bias]
    if has_init:
        in_specs.append(smem)
        args.append(init[2])
    in_specs += [col(0), col(1), col(2), col(3),
                 pl.BlockSpec((R, LANES), lambda b, h: (blk0 + b, 0)),
                 pl.BlockSpec((None, 1, ML_DV), lambda b, h: (h, 0, 0))]
    args += [qkvo, qkvo, qkvo, qkvo, misc, gnorm.reshape(ML_HEADS, 1, ML_DV)]
    c_spec = pl.BlockSpec((NS, 2, None, ML_DK, ML_DV), lambda b, h: (b, 0, h, 0, 0))
    n_spec = pl.BlockSpec((NS, 2, None, 1, ML_DK), lambda b, h: (b, 0, h, 0, 0))
    if has_init:
        in_specs += [c_spec, n_spec]
        args += [init[0], init[1].reshape(B, 2, ML_HEADS, 1, ML_DK)]
    out_specs = [pl.BlockSpec((R, LANES), lambda b, h: (b, h))]
    out_shape = [jax.ShapeDtypeStruct((B * T, ML_HEADS * ML_DV), F32)]
    if prompt:
        out_specs += [c_spec, n_spec, pl.BlockSpec((NS, 2, None, 1, 1), lambda b, h: (b, 0, h, 0, 0))]
        out_shape += [jax.ShapeDtypeStruct((B, 2, ML_HEADS, ML_DK, ML_DV), F32),
                      jax.ShapeDtypeStruct((B, 2, ML_HEADS, 1, ML_DK), F32),
                      jax.ShapeDtypeStruct((B, 2, ML_HEADS, 1, 1), F32)]
    return pl.pallas_call(
        functools.partial(_mlstm_kernel, T=T, NS=NS, has_init=has_init, emit_state=prompt),
        grid=(B // NS, ML_HEADS),
        in_specs=in_specs,
        out_specs=out_specs,
        out_shape=out_shape,
        scratch_shapes=[pltpu.VMEM((R, ML_DV), F32),
                        pltpu.VMEM((2 * NS, ML_DK, ML_DV), F32),
                        pltpu.VMEM((2 * NS, 1, ML_DK), F32),
                        pltpu.VMEM((2 * NS, 1, LANES), F32),
                        pltpu.VMEM((2, R, ML_DV), F32),
                        pltpu.VMEM((2, 3, R, LANES), F32),
                        pltpu.VMEM((2, nc, ML_DK, ML_DV), F32),
                        pltpu.VMEM((2, nc, 1, ML_DK), F32),
                        pltpu.VMEM((2, nc, 1, LANES), F32),
                        pltpu.VMEM((2, nc, 1, LANES), F32)],
        compiler_params=_cparams("arbitrary", "arbitrary"),
        name="mlstm_prompt" if prompt else "mlstm_sample",
    )(*args)


def _rope(x, cos, sin_signed):
    lane = lax.broadcasted_iota(jnp.int32, x.shape, 1)
    first = (lane & 15) < 8
    partner = jnp.where(first, pltpu.roll(x, LANES - 8, axis=1), pltpu.roll(x, 8, axis=1))
    return x * cos + partner * sin_signed


def _q_tile(cq, g_ref, w_ref, cos, sin, q_ref):
    y = _bdot(_rms(cq, g_ref[...]), w_ref[...])
    for hd in range(MLA_HEADS):
        sl = slice(hd * LANES, (hd + 1) * LANES)
        q_ref[:, sl] = _rope(y[:, sl], cos, sin).astype(BF16)


def _rope_lanes(kp):
    lane = lax.broadcasted_iota(jnp.int32, kp.shape, 1)
    return jnp.where((lane >= MLA_NOPE) & (lane < MLA_NOPE + MLA_ROPE), kp, 0.0)


def _kv_tile(c, kp, wk_ref, wv_ref, k_ref, v_ref):
    kn = _bdot(c, wk_ref[...])
    for hd in range(MLA_HEADS):
        sl = slice(hd * LANES, (hd + 1) * LANES)
        k_ref[:, sl] = (kn[:, sl] + kp).astype(BF16)
    v_ref[...] = _bdot(c, wv_ref[...]).astype(BF16)


def _rope_block_index(i, tm):
    p_tiles = N_PROMPT // tm
    s_tiles = DEC_SEQ // tm
    return jnp.where(i < p_tiles, 0, 1 + (i - p_tiles) % s_tiles)


def _kv_cache_kernel(ckv_ref, kpe_ref, wk_ref, wv_ref, k_ref, v_ref):
    _kv_tile(ckv_ref[...], _rope_lanes(kpe_ref[...]), wk_ref, wv_ref, k_ref, v_ref)


def _mla_kv_cache(ckv, kpe128, w_uk_pad, w_uv):
    n = ckv.shape[0]
    tm = ROW_TILE
    row = lambda w: pl.BlockSpec((tm, w), lambda i: (i, 0))
    full = lambda a: pl.BlockSpec(a.shape, lambda i: (0, 0))
    in_specs = [row(MLA_KV_RANK), row(LANES), full(w_uk_pad), full(w_uv)]
    args = [ckv, kpe128, w_uk_pad, w_uv]
    out_specs = [row(MLA_HEADS * LANES), row(MLA_HEADS * MLA_V)]
    out_shape = [jax.ShapeDtypeStruct((n, MLA_HEADS * LANES), BF16),
                 jax.ShapeDtypeStruct((n, MLA_HEADS * MLA_V), BF16)]
    return pl.pallas_call(
        _kv_cache_kernel,
        grid=(n // tm,),
        in_specs=in_specs,
        out_specs=out_specs,
        out_shape=out_shape,
        compiler_params=_cparams("arbitrary"),
        name="mla_kv_cache",
    )(*args)


def _attn_kernel(q_ref, k_ref, v_ref, o_ref, *, heads):
    scale = (MLA_NOPE + MLA_ROPE) ** -0.5
    scores = [_dot_nt(q_ref[:, j * LANES:(j + 1) * LANES], k_ref[:, j * LANES:(j + 1) * LANES]) * scale
              for j in range(heads)]
    probs, sums = [], []
    for s in scores:
        p = jnp.exp(s - jnp.max(s, axis=1, keepdims=True))
        sums.append(jnp.sum(p, axis=1, keepdims=True))
        probs.append(p.astype(BF16))
    pvs = [_dot(probs[j], v_ref[:, j * MLA_V:(j + 1) * MLA_V]) for j in range(heads)]
    o_ref[...] = jnp.concatenate([pvs[j] / sums[j] for j in range(heads)], axis=1)


def _attention(q, k, v, *, B, Tq, Tk, tq, q_row0, heads):
    nq = Tq // tq
    blk0 = q_row0 // tq
    return pl.pallas_call(
        functools.partial(_attn_kernel, heads=heads),
        grid=(B, MLA_HEADS // heads, nq),
        in_specs=[pl.BlockSpec((tq, heads * LANES), lambda b, hp, i: (blk0 + b * nq + i, hp)),
                  pl.BlockSpec((Tk, heads * LANES), lambda b, hp, i: (b, hp)),
                  pl.BlockSpec((Tk, heads * MLA_V), lambda b, hp, i: (b, hp))],
        out_specs=pl.BlockSpec((tq, heads * MLA_V), lambda b, hp, i: (b * nq + i, hp)),
        out_shape=jax.ShapeDtypeStruct((B * Tq, MLA_HEADS * MLA_V), F32),
        compiler_params=_cparams("arbitrary", "arbitrary", "arbitrary"),
        name="attention",
    )(q, k, v)


def _outproj_kernel(*refs, odd, split_x, tm):
    refs = list(refs)
    a1 = _take_tile(refs, True, tm)
    a2 = _take_tile(refs, True, tm)
    x = _take_tile(refs, split_x, tm)
    if odd:
        z_ref, gn_ref = refs.pop(0), refs.pop(0)
        a1 = _rms(a1 * _silu(z_ref[...]), gn_ref[...])
    (w1_ref, w2_ref, g1_ref), route_in, o_ref, route_out = refs[:3], refs[3:8], refs[8], refs[9:]
    out = _bdot(a1, w1_ref[...]) + _bdot(a2, w2_ref[...])
    x1 = x + g1_ref[...] * out
    o_ref[...] = x1
    _route_tile(x1, *route_in, *route_out)


def _outproj(a1, a2, w1, w2, x, mod4, layer, gain_ffn, w_route, b_route, z=None, gnorm=None):
    odd = z is not None
    tm = ROW_TILE
    half = w1.shape[0]
    full = lambda a: pl.BlockSpec(a.shape, lambda i: (0, 0))
    rows = lambda width: pl.BlockSpec((tm, width), lambda i: (i, 0))
    in_specs, args = [], []
    for operand in (a1, a2, x):
        specs, arrays = _token_operand(operand, tm)
        in_specs += specs
        args += arrays
    if odd:
        in_specs += [rows(half), pl.BlockSpec((1, half), lambda i: (0, 0))]
        args += [z, gnorm.reshape(1, half)]
    w_hi = w_route.astype(BF16)
    w_lo = (w_route - w_hi.astype(F32)).astype(BF16)
    in_specs += [full(w1), full(w2),
                 pl.BlockSpec((None, None, 1, D_MODEL), lambda i: (layer, _mod_index(i, tm), 0, 2)),
                 pl.BlockSpec((1, D_MODEL), lambda i: (0, 0)),
                 pl.BlockSpec((None, None, 1, 3 * D_MODEL), lambda i: (layer, _mod_index(i, tm), 0, 1)),
                 full(w_hi), full(w_lo), pl.BlockSpec((1, LANES), lambda i: (0, 0))]
    args += [w1, w2, mod4, gain_ffn.reshape(1, D_MODEL), mod4, w_hi, w_lo, b_route]
    return pl.pallas_call(
        functools.partial(_outproj_kernel, odd=odd, split_x=isinstance(x, tuple), tm=tm),
        grid=(N_TOK // tm,),
        in_specs=in_specs,
        out_specs=[rows(D_MODEL), rows(D_MODEL), rows(LANES), rows(LANES), pl.BlockSpec((1, LANES), lambda i: (0, 0))],
        out_shape=[jax.ShapeDtypeStruct((N_TOK, D_MODEL), F32),
                   jax.ShapeDtypeStruct((N_TOK, D_MODEL), F32),
                   jax.ShapeDtypeStruct((N_TOK, LANES), jnp.int32),
                   jax.ShapeDtypeStruct((N_TOK, LANES), F32),
                   jax.ShapeDtypeStruct((1, LANES), F32)],
        scratch_shapes=[pltpu.VMEM((1, LANES), F32)],
        compiler_params=_cparams("arbitrary"),
        name="outproj_router",
    )(*args)


def _conv_silu(src_ref, dst_ref, pad_ref, w_ref, b_ref, T, NS):
    width = src_ref.shape[1]
    zeros = jnp.zeros((CONV_HALO, width), F32)
    pad_ref[pl.ds(0, CONV_HALO), :] = zeros
    pad_ref[pl.ds(CONV_HALO + T, CONV_HALO), :] = zeros
    w = w_ref[...]
    bias = b_ref[...]
    for s in range(NS):
        pad_ref[pl.ds(CONV_HALO, T), :] = src_ref[pl.ds(s * T, T), :]
        for r0 in range(0, T, CONV_ROWS):
            acc = bias
            for j in range(CONV_W):
                start = r0 + CONV_HALO + j - CONV_W // 2
                acc = acc + w[j:j + 1, :] * pad_ref[pl.ds(start, CONV_ROWS), :]
            dst_ref[pl.ds(s * T + r0, CONV_ROWS), :] = _silu(acc)


def _ssd_kernel(*refs, T, NS, has_init, emit_state):
    it = iter(refs)
    x_ref, b_ref, c_ref = next(it), next(it), next(it)
    wx_ref, wb_ref, wc_ref = next(it), next(it), next(it)
    bx_ref, bb_ref, bc_ref = next(it), next(it), next(it)
    misc_ref, dtb_ref, alog_ref, dskip_ref = next(it), next(it), next(it), next(it)
    h0_ref = next(it) if has_init else None
    y_ref = next(it)
    hout_ref = next(it) if emit_state else None
    (pad_scr, xc_scr, bc_scr, cc_scr, yb_scr, hs_scr, yi_scr, ea_scr, upd_scr, eal_scr) = (
        next(it) for _ in range(10))

    p = pl.program_id(1)
    nc = T // CHUNK

    _conv_silu(x_ref, xc_scr, pad_scr, wx_ref, bx_ref, T, NS)
    _conv_silu(b_ref, bc_scr, pad_scr, wb_ref, bb_ref, T, NS)
    _conv_silu(c_ref, cc_scr, pad_scr, wc_ref, bc_ref, T, NS)

    for s in range(NS):
        for d in range(2):
            if has_init:
                hs_scr[2 * s + d] = h0_ref[s, d].reshape(2 * SSD_P, SSD_N).T
            else:
                hs_scr[2 * s + d] = jnp.zeros((SSD_N, 2 * SSD_P), F32)

    dtb = dtb_ref[...]
    neg_a = -jnp.exp(alog_ref[...])
    hms = [_half_masks(False, False), _half_masks(True, True)]
    hi_row = lax.broadcasted_iota(jnp.int32, (1, LANES), 1) >= CHUNK

    def pre_block(chunks):
        chains = []
        for c in chunks:
            rows = _chunk_rows(c)
            bc = bc_scr[rows, :].astype(BF16)
            cb2 = _dot_nt(cc_scr[rows, :].astype(BF16), jnp.concatenate([bc, bc], axis=0))
            chains.append(dict(c=c, rows=rows, bc=bc, cb2=cb2))
        for ch in chains:
            rows = ch["rows"]
            xc = xc_scr[rows, :]
            dts = _softplus(misc_ref[rows, :] + dtb)
            adt = dts * neg_a
            ch["x2m"] = _blockdiag(hms[0], xc)
            ch["g2"], ch["xw"] = [], []
            for d in range(2):
                hm = hms[d]
                dt_h = [_colsel(dts, d * SSD_HEADS + 2 * p + hh) for hh in range(2)]
                a_h = [_colsel(adt, d * SSD_HEADS + 2 * p + hh) for hh in range(2)]
                cum0, cum1, cum_rows = _half_cum(hm, a_h[0], a_h[1])
                seg2 = jnp.exp(jnp.where(hm["incl"], _pack(hm, cum0, cum1) - cum_rows, NEG_INF))
                dt_rows = _rows(hm, _pack(hm, dt_h[0], dt_h[1]))
                ch["g2"].append((ch["cb2"] * seg2 * dt_rows).astype(BF16))
                al = [jnp.sum(a, axis=0, keepdims=True) for a in a_h]
                wgt = [jnp.exp(al[hh] - cum) * dt_h[hh] for hh, cum in enumerate((cum0, cum1))]
                ch["xw"].append((xc * _pack(hm, wgt[0], wgt[1])).astype(BF16))
                ea_scr[d, rows, :] = _pack(hm, jnp.exp(cum0), jnp.exp(cum1))
                eal_scr[d, ch["c"]] = jnp.where(hi_row, jnp.exp(al[1]), jnp.exp(al[0]))
        for ch in chains:
            for d in range(2):
                yi_scr[d, ch["rows"], :] = _dot(ch["g2"][d], ch["x2m"])
                upd_scr[d, ch["c"]] = _dot_tn(ch["bc"], ch["xw"][d])

    def step_block(chunks):
        states = [hs_scr[j] for j in range(2 * NS)]
        pairs = []
        for c in chunks:
            for s in range(NS):
                pair = []
                for d in range(2):
                    cc = s * nc + (c if d == 0 else nc - 1 - c)
                    pair.append(dict(rows=_chunk_rows(cc), hs=states[2 * s + d].astype(BF16)))
                    states[2 * s + d] = eal_scr[d, cc] * states[2 * s + d] + upd_scr[d, cc]
                pairs.append(pair)
        for j in range(2 * NS):
            hs_scr[j] = states[j]
        for pair in pairs:
            lhs = jnp.concatenate([cc_scr[pair[d]["rows"], :].astype(BF16) for d in range(2)], axis=0)
            rhs = jnp.concatenate([pair[d]["hs"] for d in range(2)], axis=1)
            pair.append(_dot(lhs, rhs))
        for pair in pairs:
            for d, dst in enumerate((y_ref, yb_scr)):
                rows = pair[d]["rows"]
                ch = pair[2][d * CHUNK:(d + 1) * CHUNK, d * LANES:(d + 1) * LANES]
                dst[rows, :] = yi_scr[d, rows, :] + ea_scr[d, rows, :] * ch

    _block_loop(NS * nc, pre_block, PRE_CHUNKS)
    _block_loop(nc, step_block, STEP_CHUNKS)

    y_ref[...] = y_ref[...] + yb_scr[...] + dskip_ref[...] * xc_scr[...]
    if emit_state:
        for s in range(NS):
            for d in range(2):
                hout_ref[s, d] = hs_scr[2 * s + d].T.reshape(2, SSD_P, SSD_N)


def _lane_row(v):
    return jnp.pad(v.astype(F32), (0, LANES - v.shape[0])).reshape(1, LANES)


def _ssd(xbc, misc, conv_w, conv_b, dt_bias, a_log, d_skip, *, prompt, init=None):
    T, B, NS, blk0 = _seq_geometry(prompt)
    R = NS * T
    nc = R // CHUNK
    has_init = init is not None
    n_pairs = SSD_HEADS // 2
    pairs_per_group = n_pairs // SSD_GROUPS
    xb = SSD_HEADS * SSD_P // LANES
    cb = xb + SSD_GROUPS * SSD_N // LANES
    colx = lambda b, p: (blk0 + b, p)
    colb = lambda b, p: (blk0 + b, xb + p // pairs_per_group)
    colc = lambda b, p: (blk0 + b, cb + p // pairs_per_group)
    wsel = lambda f: (lambda b, p: (0, f(b, p)[1]))
    row128 = pl.BlockSpec((1, LANES), lambda b, p: (0, 0))
    in_specs = [pl.BlockSpec((R, LANES), colx), pl.BlockSpec((R, LANES), colb), pl.BlockSpec((R, LANES), colc),
                pl.BlockSpec((CONV_W, LANES), wsel(colx)), pl.BlockSpec((CONV_W, LANES), wsel(colb)),
                pl.BlockSpec((CONV_W, LANES), wsel(colc)),
                pl.BlockSpec((1, LANES), wsel(colx)), pl.BlockSpec((1, LANES), wsel(colb)),
                pl.BlockSpec((1, LANES), wsel(colc)),
                pl.BlockSpec((R, LANES), lambda b, p: (blk0 + b, 0)), row128, row128,
                pl.BlockSpec((1, LANES), lambda b, p: (0, p))]
    cb2 = conv_b.reshape(1, -1)
    args = [xbc, xbc, xbc, conv_w, conv_w, conv_w, cb2, cb2, cb2, misc,
            _lane_row(dt_bias.reshape(-1)), _lane_row(a_log.reshape(-1)),
            jnp.repeat(d_skip, SSD_P).reshape(1, SSD_HEADS * SSD_P)]
    state_spec = pl.BlockSpec((NS, 2, 2, SSD_P, SSD_N), lambda b, p: (b, 0, p, 0, 0))
    if has_init:
        in_specs.append(state_spec)
        args.append(init)
    out_specs = [pl.BlockSpec((R, LANES), lambda b, p: (b, p))]
    out_shape = [jax.ShapeDtypeStruct((B * T, SSD_HEADS * SSD_P), F32)]
    if prompt:
        out_specs.append(state_spec)
        out_shape.append(jax.ShapeDtypeStruct((B, 2, SSD_HEADS, SSD_P, SSD_N), F32))
    return pl.pallas_call(
        functools.partial(_ssd_kernel, T=T, NS=NS, has_init=has_init, emit_state=prompt),
        grid=(B // NS, n_pairs),
        in_specs=in_specs,
        out_specs=out_specs,
        out_shape=out_shape,
        scratch_shapes=[pltpu.VMEM((T + 2 * CONV_HALO, LANES), F32)]
        + [pltpu.VMEM((R, LANES), F32) for _ in range(4)]
        + [pltpu.VMEM((2 * NS, SSD_N, 2 * SSD_P), F32),
           pltpu.VMEM((2, R, LANES), F32),
           pltpu.VMEM((2, R, LANES), F32),
           pltpu.VMEM((2, nc, SSD_N, 2 * SSD_P), F32),
           pltpu.VMEM((2, nc, 1, LANES), F32)],
        compiler_params=_cparams("arbitrary", "arbitrary"),
        name="ssd_prompt" if prompt else "ssd_sample",
    )(*args)


def _tri_inverse(hm, nmat, eye_f):
    levels = int(np.log2(CHUNK))

    def off(level):
        same_big = lax.shift_right_logical(hm["t"], level) == lax.shift_right_logical(hm["s"], level)
        same_small = lax.shift_right_logical(hm["t"], level - 1) == lax.shift_right_logical(hm["s"], level - 1)
        return jnp.where(same_big & jnp.logical_not(same_small), nmat, 0.0)

    state = dict(dinv=eye_f - off(1))

    def first(level):
        def run():
            state["t1"] = _dot(state["dinv"].astype(BF16), _blockdiag(hm, off(level)))
        return run

    def second():
        state["dinv"] = state["dinv"] - _dot(state["t1"].astype(BF16), _blockdiag(hm, state["dinv"]))

    stages = []
    for level in range(2, levels + 1):
        stages += [first(level), second]
    return state, stages


def _gdn_kernel(*refs, T, NS, has_init, emit_state):
    it = iter(refs)
    q_ref, k_ref, v_ref = next(it), next(it), next(it)
    wq_ref, wk_ref, wv_ref = next(it), next(it), next(it)
    bq_ref, bk_ref, bv_ref = next(it), next(it), next(it)
    misc_ref, dtb_ref, alog_ref, z_ref, gn_ref = (next(it) for _ in range(5))
    s0_ref = next(it) if has_init else None
    o_ref = next(it)
    sout_ref = next(it) if emit_state else None
    (pad_scr, qc_scr, kc_scr, vc_scr, ob_scr, s_scr,
     u0_scr, wq_scr, kcf_scr, p2_scr, egl_scr) = (next(it) for _ in range(11))

    h = pl.program_id(1)
    nc = T // CHUNK

    _conv_silu(q_ref, qc_scr, pad_scr, wq_ref, bq_ref, T, NS)
    _conv_silu(k_ref, kc_scr, pad_scr, wk_ref, bk_ref, T, NS)
    _conv_silu(v_ref, vc_scr, pad_scr, wv_ref, bv_ref, T, NS)
    for r0 in range(0, NS * T, CONV_ROWS):
        rows = pl.ds(r0, CONV_ROWS)
        q = qc_scr[rows, :]
        qc_scr[rows, :] = q * lax.rsqrt(jnp.sum(q * q, axis=-1, keepdims=True) + EPS) * (GDN_DK ** -0.5)
        k = kc_scr[rows, :]
        kc_scr[rows, :] = k * lax.rsqrt(jnp.sum(k * k, axis=-1, keepdims=True) + EPS)

    for s in range(NS):
        for d in range(2):
            if has_init:
                s_scr[2 * s + d] = s0_ref[s, d]
            else:
                s_scr[2 * s + d] = jnp.zeros((GDN_DK, GDN_DV), F32)

    dtb = dtb_ref[...]
    neg_a = -jnp.exp(alog_ref[...])
    hm = _half_masks(False, True)
    eye_f = hm["eye"].astype(F32)

    def pre_gates(c):
        rows = _chunk_rows(c)
        kc = kc_scr[rows, :]
        qc = qc_scr[rows, :]
        mi = misc_ref[rows, :]
        gdec = neg_a * _softplus(mi + dtb)
        g_src = [_colsel(gdec, 2 * SSD_HEADS + d * GDN_HEADS + h) for d in range(2)]
        beta = [jax.nn.sigmoid(_colsel(mi, 2 * SSD_HEADS + 2 * GDN_HEADS + d * GDN_HEADS + h)) for d in range(2)]
        gc = [None, None]
        gc[0], gc[1], gc_rows = _half_cum(hm, g_src[0], g_src[1])
        diff = _pack(hm, gc[0], gc[1]) - gc_rows
        beta_rows = _rows(hm, _pack(hm, beta[0], beta[1]))
        kb = kc.astype(BF16)
        ek = []
        for d in range(2):
            egc = jnp.exp(gc[d])
            gl = jnp.sum(g_src[d], axis=0, keepdims=True)
            ek.append((egc * kc).astype(BF16))
            wq_scr[d, c, CHUNK:, :] = (egc * qc).astype(BF16)
            kcf_scr[d, rows, :] = (jnp.exp(gl - gc[d]) * beta[d] * kc).astype(BF16)
            egl_scr[d, c] = _lanes(jnp.exp(gl))
        return dict(c=c, rows=rows, kb=kb, qb=qc.astype(BF16), ek=ek,
                    dec_s=jnp.exp(jnp.where(hm["strict"], diff, NEG_INF)) * beta_rows,
                    dec_i=jnp.exp(jnp.where(hm["incl"], diff, NEG_INF)) * beta_rows)

    def pre_block(chunks):
        chains = [pre_gates(c) for c in chunks]
        for ch in chains:
            kb = ch["kb"]
            ch["kq"] = _dot_nt(jnp.concatenate([kb, ch["qb"]], axis=0), jnp.concatenate([kb, kb], axis=0))
        for ch in chains:
            p2_scr[ch["rows"], :] = (ch["dec_i"] * ch["kq"][CHUNK:]).astype(BF16)
            ch["inv"], ch["stages"] = _tri_inverse(hm, ch["dec_s"] * ch["kq"][:CHUNK], eye_f)
        for i in range(len(chains[0]["stages"])):
            for ch in chains:
                ch["stages"][i]()
        for ch in chains:
            vb = vc_scr[ch["rows"], :].astype(BF16)
            zero = jnp.zeros_like(vb)
            rhs = jnp.concatenate([jnp.concatenate([vb, ch["ek"][0], zero, zero], axis=1),
                                   jnp.concatenate([zero, zero, vb, ch["ek"][1]], axis=1)], axis=0)
            uw = _dot(ch["inv"]["dinv"].astype(BF16), rhs)
            for d in range(2):
                u0_scr[d, ch["rows"], :] = uw[:, 2 * d * LANES:(2 * d + 1) * LANES]
                wq_scr[d, ch["c"], :CHUNK, :] = uw[:, (2 * d + 1) * LANES:(2 * d + 2) * LANES].astype(BF16)

    def step_block(chunks):
        states = [s_scr[j] for j in range(2 * NS)]
        for c in chunks:
            work = []
            for s in range(NS):
                cs = [s * nc + c, s * nc + nc - 1 - c]
                lhs = jnp.concatenate([wq_scr[d, cs[d]] for d in range(2)], axis=0)
                rhs = jnp.concatenate([states[2 * s + d].astype(BF16) for d in range(2)], axis=1)
                work.append(dict(s=s, cs=cs, rows=[_chunk_rows(cc) for cc in cs], a=_dot(lhs, rhs)))
            for w in work:
                rows, ub, w["qs"] = w["rows"], [], []
                for d in range(2):
                    blk = w["a"][2 * d * CHUNK:(2 * d + 2) * CHUNK, d * LANES:(d + 1) * LANES]
                    ub.append((u0_scr[d, rows[d], :] - blk[:CHUNK]).astype(BF16))
                    w["qs"].append(blk[CHUNK:])
                zero = jnp.zeros_like(ub[0])
                w["pu"] = _dot(jnp.concatenate([p2_scr[rows[d], :] for d in range(2)], axis=0),
                               jnp.concatenate([jnp.concatenate([ub[0], zero], axis=1),
                                                jnp.concatenate([zero, ub[1]], axis=1)], axis=0))
                w["ktu"] = _dot_tn(jnp.concatenate([kcf_scr[d, rows[d], :] for d in range(2)], axis=1),
                                   jnp.concatenate(ub, axis=1))
            for w in work:
                for d, dst in enumerate((o_ref, ob_scr)):
                    j = 2 * w["s"] + d
                    dst[w["rows"][d], :] = w["qs"][d] + w["pu"][d * CHUNK:(d + 1) * CHUNK, d * LANES:(d + 1) * LANES]
                    states[j] = (egl_scr[d, w["cs"][d]] * states[j]
                                 + w["ktu"][d * GDN_DK:(d + 1) * GDN_DK, d * GDN_DV:(d + 1) * GDN_DV])
        for j in range(2 * NS):
            s_scr[j] = states[j]

    _block_loop(NS * nc, pre_block, GDN_PRE_CHUNKS)
    _block_loop(nc, step_block, STEP_CHUNKS)

    for r0 in range(0, NS * T, CONV_ROWS):
        rows = pl.ds(r0, CONV_ROWS)
        og = o_ref[rows, :] + ob_scr[rows, :]
        o_ref[rows, :] = _rms(og, gn_ref[...]) * _silu(z_ref[rows, :])
    if emit_state:
        for s in range(NS):
            for d in range(2):
                sout_ref[s, d] = s_scr[2 * s + d]


def _gdn(qkv, misc, zg, conv_w, conv_b, dt_row, alog_row, gnorm, *, prompt, init=None):
    T, B, NS, blk0 = _seq_geometry(prompt, GDN_PROMPT_SEQS_PER_STEP, GDN_SAMPLE_SEQS_PER_STEP)
    R = NS * T
    nc = R // CHUNK
    has_init = init is not None
    col = lambda j: (lambda b, h: (blk0 + b, j * GDN_HEADS + h))
    wsel = lambda j: (lambda b, h: (0, j * GDN_HEADS + h))
    row128 = pl.BlockSpec((1, LANES), lambda b, h: (0, 0))
    in_specs = [pl.BlockSpec((R, LANES), col(j)) for j in range(3)]
    in_specs += [pl.BlockSpec((CONV_W, LANES), wsel(j)) for j in range(3)]
    in_specs += [pl.BlockSpec((1, LANES), wsel(j)) for j in range(3)]
    in_specs += [pl.BlockSpec((R, LANES), lambda b, h: (blk0 + b, 0)), row128, row128,
                 pl.BlockSpec((R, LANES), lambda b, h: (blk0 + b, h)), row128]
    cb2 = conv_b.reshape(1, -1)
    args = [qkv, qkv, qkv, conv_w, conv_w, conv_w, cb2, cb2, cb2, misc, dt_row, alog_row, zg,
            gnorm.reshape(1, GDN_DV)]
    state_spec = pl.BlockSpec((NS, 2, None, GDN_DK, GDN_DV), lambda b, h: (b, 0, h, 0, 0))
    if has_init:
        in_specs.append(state_spec)
        args.append(init)
    out_specs = [pl.BlockSpec((R, LANES), lambda b, h: (b, h))]
    out_shape = [jax.ShapeDtypeStruct((B * T, GDN_HEADS * GDN_DV), F32)]
    if prompt:
        out_specs.append(state_spec)
        out_shape.append(jax.ShapeDtypeStruct((B, 2, GDN_HEADS, GDN_DK, GDN_DV), F32))
    return pl.pallas_call(
        functools.partial(_gdn_kernel, T=T, NS=NS, has_init=has_init, emit_state=prompt),
        grid=(B // NS, GDN_HEADS),
        in_specs=in_specs,
        out_specs=out_specs,
        out_shape=out_shape,
        scratch_shapes=[pltpu.VMEM((T + 2 * CONV_HALO, LANES), F32)]
        + [pltpu.VMEM((R, LANES), F32) for _ in range(4)]
        + [pltpu.VMEM((2 * NS, GDN_DK, GDN_DV), F32),
           pltpu.VMEM((2, R, GDN_DV), F32),
           pltpu.VMEM((2, nc, 2 * CHUNK, GDN_DK), BF16),
           pltpu.VMEM((2, R, GDN_DK), BF16),
           pltpu.VMEM((R, LANES), BF16),
           pltpu.VMEM((2, nc, 1, LANES), F32)],
        compiler_params=_cparams("arbitrary", "arbitrary"),
        name="gdn_prompt" if prompt else "gdn_sample",
    )(*args)


def _route_tile(x, g_ref, mod_ref, whi_ref, wlo_ref, b_ref, h_ref, ri_ref, w_ref, cnt_ref, count_scr):
    @pl.when(pl.program_id(0) == 0)
    def _():
        count_scr[...] = jnp.zeros_like(count_scr)

    mod = mod_ref[...]
    h = _rms(x, g_ref[...]) * (1.0 + mod[:, D_MODEL:2 * D_MODEL]) + mod[:, :D_MODEL]
    hb = h.astype(BF16)
    h_ref[...] = h
    hl = (h - hb.astype(F32)).astype(BF16)
    whi = whi_ref[...]
    logit = _dot(hb, whi) + (_dot(hl, whi) + _dot(hb, wlo_ref[...])) + b_ref[...]
    lane = lax.broadcasted_iota(jnp.int32, logit.shape, 1)
    is_group = (lane >= MOE_EXPERTS) & (lane < MOE_EXPERTS + MOE_GROUPS)
    glog = jnp.where(is_group, logit, NEG_INF)
    gmax = jnp.max(glog, axis=1, keepdims=True)
    gsel = jnp.min(jnp.where(glog == gmax, lane, LANES), axis=1, keepdims=True) - MOE_EXPERTS
    gw = 1.0 / jnp.sum(jnp.exp(glog - gmax), axis=1, keepdims=True)
    lo = gsel * MOE_PER_GROUP
    in_group = (lane >= lo) & (lane < lo + MOE_PER_GROUP)
    elog = jnp.where(in_group, logit, NEG_INF)
    v1 = jnp.max(elog, axis=1, keepdims=True)
    i1 = jnp.min(jnp.where(elog == v1, lane, LANES), axis=1, keepdims=True)
    elog2 = jnp.where(lane == i1, NEG_INF, elog)
    v2 = jnp.max(elog2, axis=1, keepdims=True)
    i2 = jnp.min(jnp.where(elog2 == v2, lane, LANES), axis=1, keepdims=True)
    e2 = jnp.exp(v2 - v1)
    w1 = gw / (1.0 + e2)
    w2 = gw * e2 / (1.0 + e2)
    w_ref[...] = jnp.where(lane == 0, w1, jnp.where(lane == 1, w2, 0.0))
    tm = logit.shape[0]
    onehot = jnp.where((lane == i1) | (lane == i2), 1.0, 0.0)
    t_i = lax.broadcasted_iota(jnp.int32, (tm, tm), 0)
    s_i = lax.broadcasted_iota(jnp.int32, (tm, tm), 1)
    earlier = jnp.where(s_i < t_i, 1.0, 0.0).astype(BF16)
    before = _dot(earlier, onehot.astype(BF16)) + count_scr[...]
    r1 = jnp.sum(jnp.where(lane == i1, before, 0.0), axis=1, keepdims=True).astype(jnp.int32)
    r2 = jnp.sum(jnp.where(lane == i2, before, 0.0), axis=1, keepdims=True).astype(jnp.int32)
    ri_ref[...] = jnp.where(lane == 0, i1, jnp.where(lane == 1, i2, jnp.where(lane == 2, r1, jnp.where(lane == 3, r2, 0))))
    count_scr[...] += jnp.sum(onehot, axis=0, keepdims=True)
    cnt_ref[...] = count_scr[...]


N_ASSIGN = 2 * N_TOK
MOE_TILE = 512
MOE_TILES = N_ASSIGN // MOE_TILE
MOE_SEGMENTS = MOE_TILES + MOE_EXPERTS - 1
DISPATCH_ROWS = 1024
COMBINE_ROWS = 256


def _routing_tables(ri, cnt):
    counts = cnt[0, :MOE_EXPERTS].astype(jnp.int32)
    cum_end = jnp.cumsum(counts)
    cum_start = cum_end - counts
    where = (ri[:, 0:2].reshape(N_ASSIGN), ri[:, 2:4].reshape(N_ASSIGN), cum_start)
    tile_start = jnp.arange(MOE_TILES, dtype=jnp.int32) * MOE_TILE
    expert_start = cum_start[1:]
    tile_rank = jnp.arange(MOE_TILES, dtype=jnp.int32) + jnp.sum(
        (expert_start[None, :] < tile_start[:, None]).astype(jnp.int32), axis=1)
    expert_rank = jnp.arange(MOE_EXPERTS - 1, dtype=jnp.int32) + jnp.minimum(expert_start // MOE_TILE + 1, MOE_TILES)
    values = jnp.concatenate([tile_start, expert_start])
    ranks = jnp.concatenate([tile_rank, expert_rank])
    slot = jnp.arange(MOE_SEGMENTS, dtype=jnp.int32)
    cuts = jnp.sum(jnp.where(ranks[None, :] == slot[:, None], values[None, :], 0), axis=1)
    seg_lo = cuts
    seg_hi = jnp.concatenate([cuts[1:], jnp.full((1,), N_ASSIGN, jnp.int32)])
    seg_tile = jnp.minimum(seg_lo // MOE_TILE, MOE_TILES - 1)
    seg_expert = jnp.minimum(jnp.sum((cum_end[None, :] <= seg_lo[:, None]).astype(jnp.int32), axis=1),
                             MOE_EXPERTS - 1)
    seg_first = jnp.concatenate([jnp.ones((1,), jnp.int32), (seg_tile[1:] != seg_tile[:-1]).astype(jnp.int32)])
    return where, (seg_tile, seg_expert, seg_lo, seg_hi, seg_first)


def _row_copy(src_ref, src_row, dst_ref, dst_row, sem):
    return pltpu.make_async_copy(src_ref.at[pl.ds(src_row, 1), :], dst_ref.at[pl.ds(dst_row, 1), :], sem)


def _dispatch_kernel(expert_ref, rank_ref, start_ref, h_ref, xs_ref, sem):
    base = pl.program_id(0) * DISPATCH_ROWS

    def start(t, carry):
        for k in range(2):
            a = 2 * (base + t) + k
            _row_copy(h_ref, t, xs_ref, start_ref[expert_ref[a]] + rank_ref[a], sem).start()
        return carry

    def wait(t, carry):
        for k in range(2):
            _row_copy(h_ref, 0, xs_ref, 0, sem).wait()
        return carry

    lax.fori_loop(0, DISPATCH_ROWS, start, 0, unroll=4)
    lax.fori_loop(0, DISPATCH_ROWS, wait, 0, unroll=4)


def _dispatch(where, h):
    tm = DISPATCH_ROWS
    return pl.pallas_call(
        _dispatch_kernel,
        grid_spec=pltpu.PrefetchScalarGridSpec(
            num_scalar_prefetch=3,
            grid=(N_TOK // tm,),
            in_specs=[pl.BlockSpec((tm, D_MODEL), lambda i, e, r, s: (i, 0))],
            out_specs=pl.BlockSpec(memory_space=pl.ANY),
            scratch_shapes=[pltpu.SemaphoreType.DMA(())]),
        out_shape=jax.ShapeDtypeStruct((N_ASSIGN, D_MODEL), F32),
        compiler_params=_cparams("arbitrary"),
        name="moe_dispatch",
    )(*where, h)


def _experts_kernel(tile_ref, expert_ref, lo_ref, hi_ref, first_ref, xs_ref, wg_ref, wu_ref, wd_ref, o_ref):
    p = pl.program_id(0)
    lo, hi = lo_ref[p], hi_ref[p]

    @pl.when(first_ref[p] == 1)
    def _():
        o_ref[...] = jnp.zeros_like(o_ref)

    @pl.when(hi > lo)
    def _():
        x = xs_ref[...].astype(BF16)
        hg = _dot(x, wg_ref[...].astype(BF16))
        hu = _dot(x, wu_ref[...].astype(BF16))
        row = tile_ref[p] * MOE_TILE + lax.broadcasted_iota(jnp.int32, (MOE_TILE, 1), 0)
        act = jnp.where((row >= lo) & (row < hi), _silu(hg) * hu, 0.0)
        o_ref[...] += _bdot(act, wd_ref[...])


def _experts(tables, xs, wg, wu, wd, layer):
    weight = lambda shape: pl.BlockSpec((None, None) + shape, lambda p, tile, expert, lo, hi, first: (layer, expert[p], 0, 0))
    rows = pl.BlockSpec((MOE_TILE, D_MODEL), lambda p, tile, expert, lo, hi, first: (tile[p], 0))
    return pl.pallas_call(
        _experts_kernel,
        grid_spec=pltpu.PrefetchScalarGridSpec(
            num_scalar_prefetch=5,
            grid=(MOE_SEGMENTS,),
            in_specs=[rows, weight((D_MODEL, MOE_FF)), weight((D_MODEL, MOE_FF)), weight((MOE_FF, D_MODEL))],
            out_specs=rows),
        out_shape=jax.ShapeDtypeStruct((N_ASSIGN, D_MODEL), F32),
        compiler_params=_cparams("arbitrary"),
        name="moe_experts",
    )(*tables, xs, wg, wu, wd)


def _combine_kernel(expert_ref, rank_ref, start_ref, ys_ref, w_ref, x_ref, mod_ref, gf_ref, o_ref, buf, sem, *,
                    final, tile0, n_tiles):
    i = pl.program_id(0)
    slot = i % 2

    def gather(tile, into):
        base = (tile0 + tile) * COMBINE_ROWS

        def start(t, carry):
            for k in range(2):
                a = 2 * (base + t) + k
                _row_copy(ys_ref, start_ref[expert_ref[a]] + rank_ref[a], buf.at[into, k], t, sem.at[into]).start()
            return carry

        lax.fori_loop(0, COMBINE_ROWS, start, 0, unroll=4)

    @pl.when(i == 0)
    def _():
        gather(0, 0)

    @pl.when(i + 1 < n_tiles)
    def _():
        gather(i + 1, 1 - slot)

    def wait(t, carry):
        for k in range(2):
            _row_copy(ys_ref, 0, buf.at[slot, k], 0, sem.at[slot]).wait()
        return carry

    lax.fori_loop(0, COMBINE_ROWS, wait, 0, unroll=4)
    w = w_ref[...]
    y = x_ref[...] + mod_ref[...] * (w[:, 0:1] * buf[slot, 0] + w[:, 1:2] * buf[slot, 1])
    if final:
        y = _rms(y, gf_ref[...])
    o_ref[...] = y


def _combine(where, ys, wts, x, mod4, layer, norm_final, final, tile0, n_tiles):
    tm = COMBINE_ROWS
    return pl.pallas_call(
        functools.partial(_combine_kernel, final=final, tile0=tile0, n_tiles=n_tiles),
        grid_spec=pltpu.PrefetchScalarGridSpec(
            num_scalar_prefetch=3,
            grid=(n_tiles,),
            in_specs=[pl.BlockSpec(memory_space=pl.ANY),
                      pl.BlockSpec((tm, LANES), lambda i, e, r, s: (tile0 + i, 0)),
                      pl.BlockSpec((tm, D_MODEL), lambda i, e, r, s: (tile0 + i, 0)),
                      pl.BlockSpec((None, None, 1, D_MODEL),
                                   lambda i, e, r, s: (layer, _mod_index(tile0 + i, tm), 0, 5)),
                      pl.BlockSpec((1, D_MODEL), lambda i, e, r, s: (0, 0))],
            out_specs=pl.BlockSpec((tm, D_MODEL), lambda i, e, r, s: (i, 0)),
            scratch_shapes=[pltpu.VMEM((2, 2, tm, D_MODEL), F32), pltpu.SemaphoreType.DMA((2,))]),
        out_shape=jax.ShapeDtypeStruct((n_tiles * tm, D_MODEL), F32),
        compiler_params=_cparams("arbitrary"),
        name="moe_combine",
    )(*where, ys, wts, x, mod4, norm_final.reshape(1, D_MODEL))


def _rope_tables():
    t = jnp.arange(DEC_SEQ)
    pos = jnp.stack([(t // GRID_W).astype(F32), (t % GRID_W).astype(F32)], axis=1)
    nf = MLA_ROPE // 4
    inv = ROPE_BASE ** (-jnp.arange(nf, dtype=F32) / nf)
    j = jnp.arange(MLA_ROPE)
    ang = pos[:, j // (2 * nf)] * inv[j % nf][None, :]
    sign = jnp.where((j % (2 * nf)) < nf, -1.0, 1.0)
    cos = jnp.pad(jnp.cos(ang), ((0, 0), (MLA_NOPE, LANES - MLA_NOPE - MLA_ROPE)), constant_values=1.0)
    sin = jnp.pad(jnp.sin(ang) * sign, ((0, 0), (MLA_NOPE, LANES - MLA_NOPE - MLA_ROPE)))
    cos = jnp.concatenate([jnp.ones((ROW_TILE, LANES), F32), cos], axis=0)
    sin = jnp.concatenate([jnp.zeros((ROW_TILE, LANES), F32), sin], axis=0)
    return cos, sin


def _pad_heads(w, n_heads, width, lo, hi):
    k = w.shape[0]
    w = w.reshape(k, n_heads, width)[:, :, lo:hi]
    w = jnp.pad(w, ((0, 0), (0, 0), (0, LANES - (hi - lo))))
    return w.reshape(k, n_heads * LANES)


def kernel(x_prompt, x_sample, c, cache_mla_kv, cache_mla_krope, state_mlstm_C, state_mlstm_n, state_mlstm_m, state_ssd, state_gdn, c_ctx, ada_w, ada_b, norm_mix, norm_ffn, w_in_even, ml_i_bias, ml_f_bias, ml_norm, mla_q_norm, mla_w_uq, mla_kv_norm, mla_w_ukv, w_out_even, w_in_odd, ssd_conv_w, ssd_conv_b, ssd_dt_bias, ssd_A_log, ssd_D, ssd_norm, gdn_conv_w, gdn_conv_b, gdn_dt_bias, gdn_A_log, gdn_norm, w_out_odd, moe_w_group, moe_b_group, moe_w_expert, moe_b_expert, moe_w_gate, moe_w_up, moe_w_down, norm_final):
    x = (x_prompt.reshape(N_PROMPT, D_MODEL), x_sample.reshape(N_SAMPLE, D_MODEL))
    cond = jnp.concatenate([c_ctx[None, :], c, jnp.zeros((N_COND - 1 - DEC_BATCH, D_MODEL), F32)], axis=0)
    mod4 = _ada(cond, ada_w, ada_b).reshape(DEPTH, N_COND, 1, 6 * D_MODEL)
    cos_tab, sin_tab = _rope_tables()

    def route_params(layer):
        w_route = jnp.concatenate([moe_w_expert[layer], moe_w_group[layer]], axis=1)
        w_route = jnp.pad(w_route, ((0, 0), (0, LANES - MOE_EXPERTS - MOE_GROUPS)))
        b_route = _lane_row(jnp.concatenate([moe_b_expert[layer], moe_b_group[layer]]))
        return norm_ffn[layer], w_route, b_route

    def moe_layer(mixed, layer, final):
        x, h, ri, wts, cnt = mixed
        where, tables = _routing_tables(ri, cnt)
        ys = _experts(tables, _dispatch(where, h), moe_w_gate, moe_w_up, moe_w_down, layer)
        combine = functools.partial(_combine, where, ys, wts, x, mod4, layer, norm_final, final)
        if not final:
            return combine(0, N_TOK // COMBINE_ROWS)
        p_tiles = N_PROMPT // COMBINE_ROWS
        return combine(0, p_tiles), combine(p_tiles, N_SAMPLE // COMBINE_ROWS)

    e = 0
    w = w_in_even[e]
    off = np.cumsum([0, 4 * ML_HEADS * ML_DK, 2 * ML_HEADS, 2 * ML_HEADS, MLA_Q_RANK, MLA_KV_RANK, MLA_ROPE])
    w_misc = jnp.concatenate([w[:, off[1]:off[3]],
                              jnp.zeros((D_MODEL, MLA_NOPE - 4 * ML_HEADS), F32),
                              w[:, off[5]:off[6]],
                              jnp.zeros((D_MODEL, LANES - MLA_NOPE - MLA_ROPE), F32)], axis=1)
    weights = [w[:, :off[1]].astype(BF16), w[:, off[3]:off[4]].astype(BF16),
               w[:, off[4]:off[5]].astype(BF16), w_misc.astype(BF16)]
    dq = MLA_NOPE + MLA_ROPE
    w_uq_pad = _pad_heads(mla_w_uq[e], MLA_HEADS, dq, 0, dq).astype(BF16)
    w_uk_pad = _pad_heads(mla_w_ukv[e], MLA_HEADS, MLA_NOPE + MLA_V, 0, MLA_NOPE).astype(BF16)
    w_uv = mla_w_ukv[e].reshape(MLA_KV_RANK, MLA_HEADS, MLA_NOPE + MLA_V)[:, :, MLA_NOPE:]
    w_uv = w_uv.reshape(MLA_KV_RANK, MLA_HEADS * MLA_V).astype(BF16)
    qkvo, misc0, q_cat, ckv_n, k_cat, v_all = _inproj(
        x, norm_mix[0], mod4, 0, weights,
        mla=(mla_q_norm[e], w_uq_pad, mla_kv_norm[e], w_uk_pad, w_uv, cos_tab, sin_tab))

    hm_p, st_c, st_n, st_m = _mlstm(qkvo, misc0, ml_i_bias[e], ml_f_bias[e], ml_norm[e], prompt=True)
    (hm_s,) = _mlstm(qkvo, misc0, ml_i_bias[e], ml_f_bias[e], ml_norm[e], prompt=False,
                     init=(state_mlstm_C[:, e], state_mlstm_n[:, e], state_mlstm_m[:, e]))

    cache_kpe = jnp.pad(cache_mla_krope[:, e].reshape(DEC_BATCH * PAST_LEN, MLA_ROPE),
                        ((0, 0), (MLA_NOPE, LANES - MLA_NOPE - MLA_ROPE)))
    k_cache, v_cache = _mla_kv_cache(cache_mla_kv[:, e].reshape(DEC_BATCH * PAST_LEN, MLA_KV_RANK), cache_kpe,
                                     w_uk_pad, w_uv)
    att_p = _attention(q_cat, k_cat, v_all, B=BATCH, Tq=SEQ, Tk=SEQ, tq=SEQ, q_row0=0, heads=MLA_HEADS)

    def with_cache(cache, new):
        width = new.shape[1]
        both = jnp.concatenate([cache.reshape(DEC_BATCH, PAST_LEN, width),
                                new[N_PROMPT:].reshape(DEC_BATCH, DEC_SEQ, width)], axis=1)
        return both.reshape(DEC_BATCH * (PAST_LEN + DEC_SEQ), width)

    att_s = _attention(q_cat, with_cache(k_cache, k_cat), with_cache(v_cache, v_all), B=DEC_BATCH, Tq=DEC_SEQ,
                       Tk=PAST_LEN + DEC_SEQ, tq=256, q_row0=N_PROMPT, heads=2)
    wo = w_out_even[e].astype(BF16)
    mixed = _outproj((hm_p, hm_s), (att_p, att_s), wo[:ML_HEADS * ML_DV], wo[ML_HEADS * ML_DV:], x, mod4, 0,
                     *route_params(0))
    x = moe_layer(mixed, 0, final=False)

    oi = 0
    w = w_in_odd[oi]
    ssd_w = SSD_HEADS * SSD_P
    ssd_cc = ssd_w + 2 * SSD_GROUPS * SSD_N
    gdn_w = GDN_HEADS * GDN_DK
    off = np.cumsum([0, ssd_w, ssd_cc, 2 * SSD_HEADS, 3 * gdn_w, gdn_w, 2 * GDN_HEADS, 2 * GDN_HEADS])
    w_misc = jnp.concatenate([w[:, off[2]:off[3]], w[:, off[5]:off[7]],
                              jnp.zeros((D_MODEL, LANES - 2 * SSD_HEADS - 4 * GDN_HEADS), F32)], axis=1)
    weights = [w[:, off[0]:off[1]].astype(BF16), w[:, off[1]:off[2]].astype(BF16),
               w[:, off[3]:off[4]].astype(BF16), w[:, off[4]:off[5]].astype(BF16), w_misc.astype(BF16)]
    z_s, xbc, qkv_g, z_g, misc = _inproj(x, norm_mix[1], mod4, 1, weights)

    ssd_args = (xbc, misc, ssd_conv_w[oi], ssd_conv_b[oi], ssd_dt_bias[oi], ssd_A_log[oi], ssd_D[oi])
    ys_p, st_ssd = _ssd(*ssd_args, prompt=True)
    (ys_s,) = _ssd(*ssd_args, prompt=False, init=state_ssd[:, oi])

    lo = 2 * SSD_HEADS
    gdn_dt_row = jnp.pad(gdn_dt_bias[oi].reshape(-1), (lo, LANES - lo - 2 * GDN_HEADS)).reshape(1, LANES)
    gdn_alog_row = jnp.pad(gdn_A_log[oi].reshape(-1), (lo, LANES - lo - 2 * GDN_HEADS)).reshape(1, LANES)
    gdn_args = (qkv_g, misc, z_g, gdn_conv_w[oi], gdn_conv_b[oi], gdn_dt_row, gdn_alog_row, gdn_norm[oi])
    og_p, st_gdn = _gdn(*gdn_args, prompt=True)
    (og_s,) = _gdn(*gdn_args, prompt=False, init=state_gdn[:, oi])

    wo = w_out_odd[oi].astype(BF16)
    mixed = _outproj((ys_p, ys_s), (og_p, og_s), wo[:ssd_w], wo[ssd_w:], x, mod4, 1, *route_params(1),
                     z=z_s, gnorm=ssd_norm[oi])
    y_p, y_s = moe_layer(mixed, 1, final=True)

    y_prompt = y_p.reshape(BATCH, SEQ, D_MODEL)
    y_sample = y_s.reshape(DEC_BATCH, DEC_SEQ, D_MODEL)
    new_mla_kv = ckv_n[:N_PROMPT].reshape(BATCH, 1, SEQ, MLA_KV_RANK)
    new_mla_krope = misc0[:N_PROMPT, MLA_NOPE:MLA_NOPE + MLA_ROPE].reshape(BATCH, 1, SEQ, MLA_ROPE)
    return (y_prompt, y_sample, new_mla_kv, new_mla_krope, st_c[:, None], st_n.reshape(BATCH, 1, 2, ML_HEADS, ML_DK),
            st_m.reshape(BATCH, 1, 2, ML_HEADS), st_ssd[:, None], st_gdn[:, None])
```
